```python
import jax, jax.numpy as jnp
from jax import lax
import numpy as np

D_MODEL = 1024
BATCH = 8
SEQ = 4096
DEPTH = 1
DEC_BATCH = 32
DEC_SEQ = 16
PAST_LEN = 1024

CHUNK = 64
N_HEADS_A = 16
HEAD_DIM = 64
D_ATT = N_HEADS_A * HEAD_DIM
D_RNN = D_MODEL
N_RNN_BLOCKS = 16
RNN_BLOCK = D_RNN // N_RNN_BLOCKS
CONV_WIDTH = 4
RGLRU_C = 8.0
N_GROUPS = 4
EXPERTS_PER_GROUP = 8
N_EXPERTS = N_GROUPS * EXPERTS_PER_GROUP
TOP_K = 2
D_EXPERT = 512
Q_BLOCK = 128
MOE_BLOCK = 128
FORGET_BIAS_INIT = 2.0
EPS = 1e-6
D_IN = 3 * D_ATT + N_HEADS_A + 2 * D_RNN
SPLITS = [D_ATT, 2 * D_ATT, 3 * D_ATT, 3 * D_ATT + N_HEADS_A, 3 * D_ATT + N_HEADS_A + D_RNN]

kernel_name = 'fox_rglru_hmoe_stream_step'


def rms_norm(x, g):
    xf = x.astype(jnp.float32)
    y = xf * lax.rsqrt(jnp.mean(xf * xf, axis=-1, keepdims=True) + EPS)
    return (y * g.astype(jnp.float32)).astype(x.dtype)


def modulate(h, shift, scale):
    return h * (1 + scale[:, None, :]) + shift[:, None, :]


def fox_attention(q, k_all, v_all, logf_all, past_len):
    B, T, H, dh = q.shape
    L = k_all.shape[1]
    F = jnp.cumsum(logf_all.astype(jnp.float32), axis=1)
    Fk = jnp.transpose(F, (0, 2, 1))
    Fq = Fk[:, :, past_len:]
    qb = min(T, Q_BLOCK)
    nb = T // qb
    k_pos = jnp.arange(L)
    scale = HEAD_DIM ** -0.5
    q_blocks = q.reshape(B, nb, qb, H, dh).transpose(1, 0, 2, 3, 4)
    fq_blocks = Fq.reshape(B, H, nb, qb).transpose(2, 0, 1, 3)
    pos_blocks = (past_len + jnp.arange(T)).reshape(nb, qb)

    def one_block(args):
        qblk, fq, qpos = args
        s = jnp.einsum('bqhd,bkhd->bhqk', qblk, k_all, preferred_element_type=jnp.float32) * scale
        s = s + (fq[..., :, None] - Fk[:, :, None, :])
        s = jnp.where(k_pos[None, :] <= qpos[:, None], s, -jnp.inf)
        p = jax.nn.softmax(s, axis=-1)
        return jnp.einsum('bhqk,bkhd->bqhd', p.astype(v_all.dtype), v_all)

    o = lax.map(one_block, (q_blocks, fq_blocks, pos_blocks))
    return o.transpose(1, 0, 2, 3, 4).reshape(B, T, H * dh)


def linear_scan(a, b, h0):
    b = b.at[:, 0].add(a[:, 0] * h0)

    def comb(l, r):
        return (l[0] * r[0], r[0] * l[1] + r[1])

    _, h = lax.associative_scan(comb, (a, b), axis=1)
    return h


def rglru_branch(xb, gb, conv_buf, h0, p):
    B, T, _ = xb.shape
    xp = jnp.concatenate([conv_buf.astype(xb.dtype), xb], axis=1)
    xc = p['conv_b'] + sum(xp[:, k:k + T] * p['conv_w'][k] for k in range(CONV_WIDTH))
    new_buf = xp[:, -(CONV_WIDTH - 1):]
    xr = xc.reshape(B, T, N_RNN_BLOCKS, RNN_BLOCK)
    r = jax.nn.sigmoid(jnp.einsum('btnc,ncd->btnd', xr, p['w_r']).reshape(B, T, D_RNN) + p['b_r'])
    i = jax.nn.sigmoid(jnp.einsum('btnc,ncd->btnd', xr, p['w_i']).reshape(B, T, D_RNN) + p['b_i'])
    log_a = -RGLRU_C * r.astype(jnp.float32) * jax.nn.softplus(-p['rglru_lambda'].astype(jnp.float32))
    a = jnp.exp(log_a)
    mult = jnp.sqrt(-jnp.expm1(2.0 * log_a))
    b = mult * (i * xc).astype(jnp.float32)
    h = linear_scan(a, b, h0.astype(jnp.float32))
    y = h.astype(xb.dtype) * jax.nn.gelu(gb)
    return y, new_buf, h[:, -1]


def grouped_experts(xf, eid, wts, w1, w3, w2):
    N, D = xf.shape
    A = N * TOP_K
    flat_e = eid.reshape(-1)
    flat_w = wts.reshape(-1)
    order = jnp.argsort(flat_e)
    e_sorted = flat_e[order]
    tok_sorted = order // TOP_K
    counts = jnp.bincount(flat_e, length=N_EXPERTS)
    padded = (counts + MOE_BLOCK - 1) // MOE_BLOCK * MOE_BLOCK
    pad_end = jnp.cumsum(padded)
    pad_start = pad_end - padded
    start = jnp.cumsum(counts) - counts
    dest = pad_start[e_sorted] + jnp.arange(A) - start[e_sorted]
    n_blocks = -(-A // MOE_BLOCK) + N_EXPERTS
    rows = jnp.full((n_blocks * MOE_BLOCK,), N, jnp.int32).at[dest].set(tok_sorted.astype(jnp.int32))
    block_e = jnp.minimum(jnp.searchsorted(pad_end, jnp.arange(n_blocks) * MOE_BLOCK, side='right'), N_EXPERTS - 1)
    x_pad = jnp.concatenate([xf, jnp.zeros((1, D), xf.dtype)], axis=0)
    xb = x_pad[rows].reshape(n_blocks, MOE_BLOCK, D)

    def expert_block(args):
        xblk, e = args
        return (jax.nn.silu(xblk @ w1[e]) * (xblk @ w3[e])) @ w2[e]

    yb = lax.map(expert_block, (xb, block_e)).reshape(-1, D)
    contrib = yb[dest] * flat_w[order][:, None].astype(yb.dtype)
    return jnp.zeros((N, D), yb.dtype).at[tok_sorted].add(contrib)


def hier_moe(h, p):
    B, T, D = h.shape
    xf = h.reshape(-1, D)
    N = xf.shape[0]
    g_prob = jax.nn.softmax((xf @ p['w_rg'] + p['b_rg']).astype(jnp.float32), axis=-1)
    p_g, g_idx = lax.top_k(g_prob, 1)
    e_logits = (xf @ p['w_re'] + p['b_re']).astype(jnp.float32).reshape(N, N_GROUPS, EXPERTS_PER_GROUP)
    e_in = jnp.take_along_axis(e_logits, g_idx[:, :, None], axis=1)[:, 0]
    e_prob = jax.nn.softmax(e_in, axis=-1)
    top_p, top_i = lax.top_k(e_prob, TOP_K)
    wts = p_g * top_p / jnp.sum(top_p, axis=-1, keepdims=True)
    eid = g_idx * EXPERTS_PER_GROUP + top_i
    y = grouped_experts(xf, eid, wts, p['w1'], p['w3'], p['w2'])
    return y.reshape(B, T, D).astype(h.dtype)


def layer(x, c, k_past, v_past, logf_past, conv_buf, h0, p):
    B, T, _ = x.shape
    P = k_past.shape[1]
    mod = jax.nn.silu(c) @ p['w_mod'] + p['b_mod']
    sh1, sc1, g1, sh2, sc2, g2 = jnp.split(mod, 6, axis=-1)
    h = modulate(rms_norm(x, p['norm1']), sh1, sc1)
    q, k, v, fl, xb, gb = jnp.split(h @ p['w_in'], SPLITS, axis=-1)
    logf = jax.nn.log_sigmoid((fl + p['b_f']).astype(jnp.float32))
    q = q.reshape(B, T, N_HEADS_A, HEAD_DIM)
    k = k.reshape(B, T, N_HEADS_A, HEAD_DIM)
    v = v.reshape(B, T, N_HEADS_A, HEAD_DIM)
    k_all = jnp.concatenate([k_past.astype(k.dtype), k], axis=1)
    v_all = jnp.concatenate([v_past.astype(v.dtype), v], axis=1)
    logf_all = jnp.concatenate([logf_past.astype(jnp.float32), logf], axis=1)
    o_a = fox_attention(q, k_all, v_all, logf_all, P)
    o_b, new_buf, h_last = rglru_branch(xb, gb, conv_buf, h0, p)
    g_a, g_b = jnp.split(jax.nn.sigmoid(h @ p['w_gate'] + p['b_gate']), 2, axis=-1)
    mix = g_a * (o_a @ p['w_pa']) + g_b * (o_b @ p['w_pb'])
    x = x + g1[:, None, :] * (mix @ p['w_out'])
    h2 = modulate(rms_norm(x, p['norm2']), sh2, sc2)
    x = x + g2[:, None, :] * hier_moe(h2, p)
    return x, (k, v, logf.astype(x.dtype), new_buf, h_last.astype(x.dtype))


def setup_inputs(seed: int = 0) -> dict:
    key = jax.random.key(seed)
    ks = jax.random.split(key, 40)
    L = DEPTH

    def nrm(i, shape, s):
        return s * jax.random.normal(ks[i], shape, jnp.float32)

    u = jax.random.uniform(ks[20], (L, D_RNN), jnp.float32, 0.9, 0.999)
    sg = u ** (1.0 / RGLRU_C)
    lam = jnp.log(sg) - jnp.log1p(-sg)
    return {
        'x_prompt': nrm(0, (BATCH, SEQ, D_MODEL), 1.0),
        'x_sample': nrm(1, (DEC_BATCH, DEC_SEQ, D_MODEL), 1.0),
        'cache_k': nrm(2, (L, DEC_BATCH, PAST_LEN, N_HEADS_A, HEAD_DIM), 1.0),
        'cache_v': nrm(3, (L, DEC_BATCH, PAST_LEN, N_HEADS_A, HEAD_DIM), 1.0),
        'cache_logf': jax.nn.log_sigmoid(FORGET_BIAS_INIT + nrm(4, (L, DEC_BATCH, PAST_LEN, N_HEADS_A), 0.5)),
        'state_conv': nrm(5, (L, DEC_BATCH, CONV_WIDTH - 1, D_RNN), 1.0),
        'state_h': nrm(6, (L, DEC_BATCH, D_RNN), 0.5),
        'c_prompt': nrm(7, (BATCH, D_MODEL), 1.0),
        'c_sample': nrm(8, (DEC_BATCH, D_MODEL), 1.0),
        'w_mod': nrm(9, (L, D_MODEL, 6 * D_MODEL), 0.5 * D_MODEL ** -0.5),
        'b_mod': nrm(10, (L, 6 * D_MODEL), 0.02),
        'norm1': 1.0 + nrm(11, (L, D_MODEL), 0.05),
        'w_in': nrm(12, (L, D_MODEL, D_IN), D_MODEL ** -0.5),
        'b_f': FORGET_BIAS_INIT + nrm(13, (L, N_HEADS_A), 0.1),
        'conv_w': nrm(14, (L, CONV_WIDTH, D_RNN), CONV_WIDTH ** -0.5),
        'conv_b': nrm(15, (L, D_RNN), 0.02),
        'w_r': nrm(16, (L, N_RNN_BLOCKS, RNN_BLOCK, RNN_BLOCK), RNN_BLOCK ** -0.5),
        'b_r': nrm(17, (L, D_RNN), 0.02),
        'w_i': nrm(18, (L, N_RNN_BLOCKS, RNN_BLOCK, RNN_BLOCK), RNN_BLOCK ** -0.5),
        'b_i': nrm(19, (L, D_RNN), 0.02),
        'rglru_lambda': lam,
        'w_gate': nrm(21, (L, D_MODEL, 2 * D_MODEL), D_MODEL ** -0.5),
        'b_gate': nrm(22, (L, 2 * D_MODEL), 0.02),
        'w_pa': nrm(23, (L, D_ATT, D_MODEL), D_ATT ** -0.5),
        'w_pb': nrm(24, (L, D_RNN, D_MODEL), D_RNN ** -0.5),
        'w_out': nrm(25, (L, D_MODEL, D_MODEL), D_MODEL ** -0.5),
        'norm2': 1.0 + nrm(26, (L, D_MODEL), 0.05),
        'w_rg': nrm(27, (L, D_MODEL, N_GROUPS), D_MODEL ** -0.5),
        'b_rg': nrm(28, (L, N_GROUPS), 0.01),
        'w_re': nrm(29, (L, D_MODEL, N_EXPERTS), D_MODEL ** -0.5),
        'b_re': nrm(30, (L, N_EXPERTS), 0.01),
        'w1': nrm(31, (L, N_EXPERTS, D_MODEL, D_EXPERT), D_MODEL ** -0.5),
        'w3': nrm(32, (L, N_EXPERTS, D_MODEL, D_EXPERT), D_MODEL ** -0.5),
        'w2': nrm(33, (L, N_EXPERTS, D_EXPERT, D_MODEL), D_EXPERT ** -0.5),
        'norm_f': 1.0 + nrm(34, (D_MODEL,), 0.05),
    }


def reference(x_prompt, x_sample, cache_k, cache_v, cache_logf, state_conv, state_h, c_prompt, c_sample,
              w_mod, b_mod, norm1, w_in, b_f, conv_w, conv_b, w_r, b_r, w_i, b_i, rglru_lambda,
              w_gate, b_gate, w_pa, w_pb, w_out, norm2, w_rg, b_rg, w_re, b_re, w1, w3, w2, norm_f):
    yp, ys = x_prompt, x_sample
    Bp = x_prompt.shape[0]
    dt = x_prompt.dtype
    outs_p, outs_s = [], []
    for l in range(DEPTH):
        p = {'w_mod': w_mod[l], 'b_mod': b_mod[l], 'norm1': norm1[l], 'w_in': w_in[l], 'b_f': b_f[l],
             'conv_w': conv_w[l], 'conv_b': conv_b[l], 'w_r': w_r[l], 'b_r': b_r[l], 'w_i': w_i[l], 'b_i': b_i[l],
             'rglru_lambda': rglru_lambda[l], 'w_gate': w_gate[l], 'b_gate': b_gate[l], 'w_pa': w_pa[l],
             'w_pb': w_pb[l], 'w_out': w_out[l], 'norm2': norm2[l], 'w_rg': w_rg[l], 'b_rg': b_rg[l],
             'w_re': w_re[l], 'b_re': b_re[l], 'w1': w1[l], 'w3': w3[l], 'w2': w2[l]}
        yp, sp = layer(yp, c_prompt,
                       jnp.zeros((Bp, 0, N_HEADS_A, HEAD_DIM), dt), jnp.zeros((Bp, 0, N_HEADS_A, HEAD_DIM), dt),
                       jnp.zeros((Bp, 0, N_HEADS_A), jnp.float32), jnp.zeros((Bp, CONV_WIDTH - 1, D_RNN), dt),
                       jnp.zeros((Bp, D_RNN), jnp.float32), p)
        ys, ss = layer(ys, c_sample, cache_k[l], cache_v[l], cache_logf[l], state_conv[l], state_h[l], p)
        outs_p.append(sp)
        outs_s.append(ss)
    y_prompt = rms_norm(yp, norm_f)
    y_sample = rms_norm(ys, norm_f)
    k_prompt = jnp.stack([s[0] for s in outs_p])
    v_prompt = jnp.stack([s[1] for s in outs_p])
    logf_prompt = jnp.stack([s[2] for s in outs_p])
    conv_prompt = jnp.stack([s[3] for s in outs_p])
    h_prompt = jnp.stack([s[4] for s in outs_p])
    k_sample = jnp.stack([s[0] for s in outs_s])
    v_sample = jnp.stack([s[1] for s in outs_s])
    logf_sample = jnp.stack([s[2] for s in outs_s])
    conv_sample = jnp.stack([s[3] for s in outs_s])
    h_sample = jnp.stack([s[4] for s in outs_s])
    return (y_prompt, y_sample, k_prompt, v_prompt, logf_prompt, conv_prompt, h_prompt,
            k_sample, v_sample, logf_sample, conv_sample, h_sample)
```

```python
import functools

import jax
import jax.numpy as jnp
from jax import lax
from jax.experimental import pallas as pl
from jax.experimental.pallas import tpu as pltpu

F32 = jnp.float32
BF16 = jnp.bfloat16

D_MODEL = 1024
N_HEADS = 16
HEAD_DIM = 64
HEAD_PAIR = 2 * HEAD_DIM
N_PAIRS = N_HEADS // 2
N_RNN_BLOCKS = 16
CONV_WIDTH = 4
RGLRU_C = 8.0
N_GROUPS = 4
EXPERTS_PER_GROUP = 8
N_EXPERTS = N_GROUPS * EXPERTS_PER_GROUP
TOP_K = 2
D_EXPERT = 512
EPS = 1e-6

LANES = 128
SUBLANES = 8
ROW_TILE = 512
ATTN_BLOCK = 512
RNN_TILE = 256
MOE_BLOCK = 256
GATE_TILE = 256
VMEM_LIMIT = 56 * 1024 * 1024
NEG_BIG = -1e30


def _cparams(sem):
    return pltpu.CompilerParams(dimension_semantics=sem, vmem_limit_bytes=VMEM_LIMIT)


def _resident(shape):
    nd = len(shape)
    return pl.BlockSpec(shape, lambda *_: (0,) * nd, pipeline_mode=pl.Buffered(1))


def _split3(x):
    p1 = x.astype(BF16)
    r1 = x - p1.astype(F32)
    p2 = r1.astype(BF16)
    p3 = (r1 - p2.astype(F32)).astype(BF16)
    return p1, p2, p3


def _dot(a, b):
    return jnp.dot(a, b, preferred_element_type=F32)


def _dot_nt(a, b):
    return lax.dot_general(a, b, (((1,), (1,)), ((), ())), preferred_element_type=F32)


def _sigmoid(x):
    return 1.0 / (1.0 + jnp.exp(-x))


def _gelu_tanh(x):
    return 0.5 * x * (1.0 + jnp.tanh(0.7978845608028654 * (x + 0.044715 * x * x * x)))


def _rms(x, g):
    ms = jnp.mean(x * x, axis=-1, keepdims=True)
    return x * lax.rsqrt(ms + EPS) * g


def _div_p2(x, n):
    assert n & (n - 1) == 0
    return lax.shift_right_logical(x, jnp.int32(n.bit_length() - 1))


def _mod_p2(x, n):
    assert n & (n - 1) == 0
    return x & (n - 1)


def _upper_tri(n):
    r = lax.broadcasted_iota(jnp.int32, (n, n), 0)
    c = lax.broadcasted_iota(jnp.int32, (n, n), 1)
    return jnp.where(r <= c, 1.0, 0.0).astype(BF16)


def _cumsum_lanes(x, u):
    p1, p2, p3 = _split3(x)
    return _dot(p1, u) + _dot(p2, u) + _dot(p3, u)


def _mod_kernel(c_ref, w_ref, b_ref, o_ref):
    c = c_ref[...]
    s = c * _sigmoid(c)
    s1 = s.astype(BF16)
    s2 = (s - s1.astype(F32)).astype(BF16)
    w = w_ref[...]
    w1 = w.astype(BF16)
    w2 = (w - w1.astype(F32)).astype(BF16)
    o_ref[...] = _dot(s1, w1) + _dot(s1, w2) + _dot(s2, w1) + b_ref[...]


def _mod(c, w, b):
    n = c.shape[0]
    cols = w.shape[1]
    return pl.pallas_call(
        _mod_kernel,
        grid=(cols // D_MODEL,),
        in_specs=[pl.BlockSpec((n, D_MODEL), lambda j: (0, 0)),
                  pl.BlockSpec((D_MODEL, D_MODEL), lambda j: (0, j)),
                  pl.BlockSpec((1, D_MODEL), lambda j: (0, j))],
        out_specs=pl.BlockSpec((n, D_MODEL), lambda j: (0, j)),
        out_shape=jax.ShapeDtypeStruct((n, cols), F32),
        compiler_params=_cparams(("arbitrary",)),
        name="mod",
    )(c, w, b.reshape(1, cols))


def _in_kernel(x_ref, sh_ref, sc_ref, n1_ref, w_ref, wfl_ref, wflt_ref, bg_ref, bf_ref, bft_ref,
               q_ref, k_ref, v_ref, xb_ref, gb_ref, ga_ref, gg_ref, lf_ref, lft_ref, ft_ref,
               carry_ref, *, tiles_per_seq):
    x = x_ref[...]
    h = (_rms(x, n1_ref[...]) * (1.0 + sc_ref[...]) + sh_ref[...]).astype(BF16)

    def mm(c):
        return _dot(h, w_ref[:, c * D_MODEL:(c + 1) * D_MODEL])

    q_ref[...] = mm(0).astype(BF16)
    k_ref[...] = mm(1)
    v_ref[...] = mm(2)
    xb_ref[...] = mm(3).astype(BF16)
    gb_ref[...] = mm(4).astype(BF16)
    ga_ref[...] = _sigmoid(mm(5) + bg_ref[:, :D_MODEL]).astype(BF16)
    gg_ref[...] = _sigmoid(mm(6) + bg_ref[:, D_MODEL:]).astype(BF16)

    def log_sigmoid(z):
        return jnp.minimum(z, 0.0) - jnp.log(1.0 + jnp.exp(-jnp.abs(z)))

    fl = _dot(h, wfl_ref[...])[:, :N_HEADS] + bf_ref[...]
    lf_ref[...] = log_sigmoid(fl)
    flt = _dot_nt(wflt_ref[...], h)[:N_HEADS, :] + bft_ref[...]
    lft = log_sigmoid(flt)
    lft_ref[...] = lft

    @pl.when(pl.program_id(0) % tiles_per_seq == 0)
    def _():
        carry_ref[...] = jnp.zeros_like(carry_ref)

    tm = x.shape[0]
    ft = _cumsum_lanes(lft, _upper_tri(tm)) + carry_ref[:, 0:1]
    ft_ref[...] = ft
    carry_ref[...] = jnp.broadcast_to(ft[:, tm - 1:tm], carry_ref.shape)


def _in_proj(x2, mod3, norm1, wcat, wfl, wflt, b_gate, b_f, *, tiles_per_mod, tiles_per_seq):
    rows = x2.shape[0]
    tm = ROW_TILE
    mr = mod3.shape[1]
    nt = rows // tm
    row_blk = lambda dt: pl.BlockSpec((tm, D_MODEL), lambda i: (i, 0))
    outs = [jax.ShapeDtypeStruct((rows, D_MODEL), BF16),
            jax.ShapeDtypeStruct((rows, D_MODEL), F32),
            jax.ShapeDtypeStruct((rows, D_MODEL), F32),
            jax.ShapeDtypeStruct((rows, D_MODEL), BF16),
            jax.ShapeDtypeStruct((rows, D_MODEL), BF16),
            jax.ShapeDtypeStruct((rows, D_MODEL), BF16),
            jax.ShapeDtypeStruct((rows, D_MODEL), BF16),
            jax.ShapeDtypeStruct((rows, N_HEADS), F32),
            jax.ShapeDtypeStruct((N_HEADS, rows), F32),
            jax.ShapeDtypeStruct((N_HEADS, rows), F32)]
    return pl.pallas_call(
        functools.partial(_in_kernel, tiles_per_seq=tiles_per_seq),
        grid=(nt,),
        in_specs=[pl.BlockSpec((tm, D_MODEL), lambda i: (i, 0)),
                  pl.BlockSpec((None, mr, D_MODEL), lambda i: (i // tiles_per_mod, 0, 0)),
                  pl.BlockSpec((None, mr, D_MODEL), lambda i: (i // tiles_per_mod, 0, 1)),
                  _resident((1, D_MODEL)),
                  _resident(wcat.shape),
                  _resident(wfl.shape),
                  _resident(wflt.shape),
                  _resident((1, 2 * D_MODEL)),
                  _resident((1, N_HEADS)),
                  _resident((N_HEADS, 1))],
        out_specs=[row_blk(None)] * 7 + [pl.BlockSpec((tm, N_HEADS), lambda i: (i, 0)),
                                          pl.BlockSpec((N_HEADS, tm), lambda i: (0, i)),
                                          pl.BlockSpec((N_HEADS, tm), lambda i: (0, i))],
        out_shape=outs,
        scratch_shapes=[pltpu.VMEM((N_HEADS, LANES), F32)],
        compiler_params=_cparams(("arbitrary",)),
        name="in_proj",
    )(x2, mod3, mod3, norm1.reshape(1, D_MODEL), wcat, wfl, wflt,
      b_gate.reshape(1, 2 * D_MODEL), b_f.reshape(1, N_HEADS), b_f.reshape(N_HEADS, 1))


def _attn_kernel(q_ref, k_ref, v_ref, ft_ref, o_ref, kb_ref, vb_ref, m_ref, l_ref, acc_ref):
    qi = pl.program_id(2)
    bq = q_ref.shape[0]
    bk = ATTN_BLOCK

    @pl.when(qi == 0)
    def _():
        kb_ref[...] = k_ref[...].astype(BF16)
        vb_ref[...] = v_ref[...].astype(BF16)

    q = q_ref[...]
    lane = lax.broadcasted_iota(jnp.int32, q.shape, 1)
    first = lane < HEAD_DIM
    zero = jnp.zeros_like(q)
    qm = (jnp.where(first, q, zero), jnp.where(first, zero, q))

    m_ref[...] = jnp.full(m_ref.shape, NEG_BIG, F32)
    l_ref[...] = jnp.zeros(l_ref.shape, F32)
    acc_ref[...] = jnp.zeros(acc_ref.shape, F32)

    def chunk(j, masked):
        start = pl.multiple_of(j * bk, bk)
        kc = kb_ref[pl.ds(start, bk), :]
        vc = vb_ref[pl.ds(start, bk), :]
        fk = ft_ref[j]
        for hh in range(2):
            s = _dot_nt(qm[hh], kc) - fk[hh:hh + 1, :]
            if masked:
                r = lax.broadcasted_iota(jnp.int32, s.shape, 0)
                c = lax.broadcasted_iota(jnp.int32, s.shape, 1)
                s = jnp.where(c <= r, s, NEG_BIG)
            m_prev = m_ref[hh]
            m_new = jnp.maximum(m_prev, jnp.max(s, axis=-1, keepdims=True))
            alpha = jnp.exp(m_prev - m_new)
            p = jnp.exp(s - m_new)
            l_ref[hh] = alpha * l_ref[hh] + jnp.sum(p, axis=-1, keepdims=True)
            acc_ref[hh] = alpha * acc_ref[hh] + _dot(p.astype(BF16), vc)
            m_ref[hh] = m_new

    def body(j, carry):
        chunk(j, False)
        return carry

    lax.fori_loop(0, qi, body, 0)
    chunk(qi, True)

    oa = acc_ref[0] / l_ref[0]
    ob = acc_ref[1] / l_ref[1]
    o_ref[...] = jnp.where(first, oa, ob).astype(BF16)


def _attn_prompt(q3, k3, v3, ft5):
    B, T, _ = q3.shape
    bq = ATTN_BLOCK
    nq = T // bq
    return pl.pallas_call(
        _attn_kernel,
        grid=(B, N_PAIRS, nq),
        in_specs=[pl.BlockSpec((None, bq, HEAD_PAIR), lambda b, p, i: (b, i, p)),
                  pl.BlockSpec((None, T, HEAD_PAIR), lambda b, p, i: (b, 0, p)),
                  pl.BlockSpec((None, T, HEAD_PAIR), lambda b, p, i: (b, 0, p)),
                  pl.BlockSpec((None, None, nq, 2, bq), lambda b, p, i: (b, p, 0, 0, 0))],
        out_specs=pl.BlockSpec((None, bq, HEAD_PAIR), lambda b, p, i: (b, i, p)),
        out_shape=jax.ShapeDtypeStruct((B, T, D_MODEL), BF16),
        scratch_shapes=[pltpu.VMEM((T, HEAD_PAIR), BF16),
                        pltpu.VMEM((T, HEAD_PAIR), BF16),
                        pltpu.VMEM((2, bq, 1), F32),
                        pltpu.VMEM((2, bq, 1), F32),
                        pltpu.VMEM((2, bq, HEAD_PAIR), F32)],
        compiler_params=_cparams(("arbitrary", "arbitrary", "arbitrary")),
        name="attn_prompt",
    )(q3, k3, v3, ft5)


def _attn_sample_kernel(q_ref, kn_ref, vn_ref, kc_ref, vc_ref, lfp_ref, lfn_ref, o_ref):
    tq = q_ref.shape[0]
    past = kc_ref.shape[0]
    nrow = N_HEADS * tq
    row = lax.broadcasted_iota(jnp.int32, (nrow, D_MODEL), 0)
    col = lax.broadcasted_iota(jnp.int32, (nrow, D_MODEL), 1)
    own = _div_p2(row, tq) == _div_p2(col, HEAD_DIM)

    rr = lax.broadcasted_iota(jnp.int32, (nrow, tq), 0)
    rc = lax.broadcasted_iota(jnp.int32, (nrow, tq), 1)
    rep = jnp.where(_mod_p2(rr, tq) == rc, 1.0, 0.0).astype(BF16)
    qx = jnp.where(own, _dot(rep, q_ref[...]), 0.0).astype(BF16)

    fp = _cumsum_lanes(lfp_ref[...], _upper_tri(past))
    fn = _cumsum_lanes(lfn_ref[...], _upper_tri(tq)) + fp[:, past - 1:past]
    hr = lax.broadcasted_iota(jnp.int32, (nrow, N_HEADS), 0)
    hc = lax.broadcasted_iota(jnp.int32, (nrow, N_HEADS), 1)
    hsel = jnp.where(_div_p2(hr, tq) == hc, 1.0, 0.0).astype(BF16)

    def expand(f):
        p1, p2, p3 = _split3(f)
        return _dot(hsel, p1) + _dot(hsel, p2) + _dot(hsel, p3)

    s_p = _dot_nt(qx, kc_ref[...].astype(BF16)) - expand(fp)
    s_n = _dot_nt(qx, kn_ref[...].astype(BF16)) - expand(fn)
    nr = lax.broadcasted_iota(jnp.int32, (nrow, tq), 0)
    nc = lax.broadcasted_iota(jnp.int32, (nrow, tq), 1)
    s_n = jnp.where(nc <= _mod_p2(nr, tq), s_n, NEG_BIG)
    m = jnp.maximum(jnp.max(s_p, axis=-1, keepdims=True), jnp.max(s_n, axis=-1, keepdims=True))
    p_p = jnp.exp(s_p - m)
    p_n = jnp.exp(s_n - m)
    l = jnp.sum(p_p, axis=-1, keepdims=True) + jnp.sum(p_n, axis=-1, keepdims=True)
    o_full = _dot(p_p.astype(BF16), vc_ref[...].astype(BF16)) + _dot(p_n.astype(BF16), vn_ref[...].astype(BF16))
    o_own = jnp.where(own, o_full / l, 0.0).astype(BF16)
    cr = lax.broadcasted_iota(jnp.int32, (tq, nrow), 0)
    cc = lax.broadcasted_iota(jnp.int32, (tq, nrow), 1)
    col_sel = jnp.where(_mod_p2(cc, tq) == cr, 1.0, 0.0).astype(BF16)
    o_ref[...] = _dot(col_sel, o_own).astype(BF16)


def _attn_sample(q3, kn3, vn3, kc3, vc3, lfp3, lfn3):
    B, tq, _ = q3.shape
    past = kc3.shape[1]
    blk = lambda n, w: pl.BlockSpec((None, n, w), lambda b: (b, 0, 0))
    return pl.pallas_call(
        _attn_sample_kernel,
        grid=(B,),
        in_specs=[blk(tq, D_MODEL), blk(tq, D_MODEL), blk(tq, D_MODEL),
                  blk(past, D_MODEL), blk(past, D_MODEL), blk(N_HEADS, past), blk(N_HEADS, tq)],
        out_specs=blk(tq, D_MODEL),
        out_shape=jax.ShapeDtypeStruct((B, tq, D_MODEL), BF16),
        compiler_params=_cparams(("arbitrary",)),
        name="attn_sample",
    )(q3, kn3, vn3, kc3, vc3, lfp3, lfn3)


def _rnn_kernel(xb_ref, gb_ref, c0_ref, h0_ref, cw_ref, cb_ref, wr_ref, wi_ref, br_ref, bi_ref, lam_ref,
                o_ref, cout_ref, hout_ref, xp_ref, hc_ref):
    ti = pl.program_id(1)
    tt = xb_ref.shape[0]
    pad = SUBLANES

    @pl.when(ti == 0)
    def _():
        xp_ref[0:pad, :] = c0_ref[...]
        hc_ref[...] = h0_ref[...]

    xp_ref[pad:pad + tt, :] = xb_ref[...].astype(F32)
    xc = cb_ref[...] + xp_ref[pad:pad + tt, :] * cw_ref[CONV_WIDTH - 1:CONV_WIDTH, :]
    for k in range(CONV_WIDTH - 1):
        back = CONV_WIDTH - 1 - k
        xc = xc + xp_ref[pad - back:pad - back + tt, :] * cw_ref[k:k + 1, :]
    tail = xp_ref[tt:tt + pad, :]
    xp_ref[0:pad, :] = tail

    xcb = xc.astype(BF16)
    ng = D_MODEL // GATE_TILE
    r = jnp.concatenate([_dot(xcb[:, g * GATE_TILE:(g + 1) * GATE_TILE], wr_ref[g]) for g in range(ng)], axis=-1)
    i = jnp.concatenate([_dot(xcb[:, g * GATE_TILE:(g + 1) * GATE_TILE], wi_ref[g]) for g in range(ng)], axis=-1)
    r = _sigmoid(r + br_ref[...])
    i = _sigmoid(i + bi_ref[...])
    lam = lam_ref[...]
    softplus_neg = jnp.maximum(-lam, 0.0) + jnp.log(1.0 + jnp.exp(-jnp.abs(lam)))
    log_a = (-RGLRU_C) * r * softplus_neg
    a = jnp.exp(log_a)
    th = jnp.tanh(log_a)
    mult = jnp.sqrt(-2.0 * th / (1.0 - th))
    b = mult * (i * xc)

    ng8 = tt // SUBLANES
    a3 = a.reshape(ng8, SUBLANES, D_MODEL)
    b3 = b.reshape(ng8, SUBLANES, D_MODEL)
    sub = lax.broadcasted_iota(jnp.int32, a3.shape, 1)
    for s in (1, 2, 4):
        keep = sub >= s
        a_sh = jnp.where(keep, pltpu.roll(a3, s, 1), 1.0)
        b_sh = jnp.where(keep, pltpu.roll(b3, s, 1), 0.0)
        b3 = a3 * b_sh + b3
        a3 = a3 * a_sh
    hc = hc_ref[...]
    hs = []
    for g in range(ng8):
        hg = b3[g] + a3[g] * hc
        hs.append(hg)
        hc = hg[SUBLANES - 1:SUBLANES, :]
    hc_ref[...] = hc
    h = jnp.concatenate(hs, axis=0)

    o_ref[...] = (h * _gelu_tanh(gb_ref[...].astype(F32))).astype(BF16)

    @pl.when(ti == pl.num_programs(1) - 1)
    def _():
        cout_ref[...] = tail
        hout_ref[...] = hc


def _rnn(xb3, gb3, conv0, h0, conv_w, conv_b, wr4, wi4, b_r, b_i, lam):
    B, T, _ = xb3.shape
    tt = min(RNN_TILE, T)
    vec = lambda a: a.reshape(1, D_MODEL)
    seq = pl.BlockSpec((None, tt, D_MODEL), lambda b, t: (b, t, 0))
    st8 = pl.BlockSpec((None, SUBLANES, D_MODEL), lambda b, t: (b, 0, 0))
    st1 = pl.BlockSpec((None, 1, D_MODEL), lambda b, t: (b, 0, 0))
    return pl.pallas_call(
        _rnn_kernel,
        grid=(B, T // tt),
        in_specs=[seq, seq, st8, st1,
                  _resident((CONV_WIDTH, D_MODEL)), _resident((1, D_MODEL)),
                  _resident(wr4.shape), _resident(wi4.shape),
                  _resident((1, D_MODEL)), _resident((1, D_MODEL)), _resident((1, D_MODEL))],
        out_specs=[seq, st8, st1],
        out_shape=[jax.ShapeDtypeStruct((B, T, D_MODEL), BF16),
                   jax.ShapeDtypeStruct((B, SUBLANES, D_MODEL), F32),
                   jax.ShapeDtypeStruct((B, 1, D_MODEL), F32)],
        scratch_shapes=[pltpu.VMEM((tt + SUBLANES, D_MODEL), F32),
                        pltpu.VMEM((1, D_MODEL), F32)],
        compiler_params=_cparams(("arbitrary", "arbitrary")),
        name="rnn",
    )(xb3, gb3, conv0, h0, conv_w, vec(conv_b), wr4, wi4, vec(b_r), vec(b_i), vec(lam))


def _post_kernel(x_ref, oa_ref, ob_ref, ga_ref, gg_ref, g1_ref, sh_ref, sc_ref, n2_ref,
                 wpa_ref, wpb_ref, wo_ref, wr1_ref, wr2_ref, br_ref,
                 x1_ref, h2_ref, rt_ref):
    mix = (ga_ref[...].astype(F32) * _dot(oa_ref[...], wpa_ref[...])
           + gg_ref[...].astype(F32) * _dot(ob_ref[...], wpb_ref[...]))
    x1 = x_ref[...] + g1_ref[...] * _dot(mix.astype(BF16), wo_ref[...])
    x1_ref[...] = x1
    h2 = _rms(x1, n2_ref[...]) * (1.0 + sc_ref[...]) + sh_ref[...]
    h2a = h2.astype(BF16)
    h2_ref[...] = h2a
    h2b = (h2 - h2a.astype(F32)).astype(BF16)
    lg = _dot(h2a, wr1_ref[...]) + _dot(h2a, wr2_ref[...]) + _dot(h2b, wr1_ref[...]) + br_ref[...]

    lane = lax.broadcasted_iota(jnp.int32, lg.shape, 1).astype(F32)
    big = float(LANES)
    is_g = lane < N_GROUPS
    gl = jnp.where(is_g, lg, NEG_BIG)
    mg = jnp.max(gl, axis=-1, keepdims=True)
    sg = jnp.sum(jnp.where(is_g, jnp.exp(gl - mg), 0.0), axis=-1, keepdims=True)
    pg = 1.0 / sg
    gidx = jnp.min(jnp.where(is_g & (gl == mg), lane, big), axis=-1, keepdims=True)
    lo = N_GROUPS + EXPERTS_PER_GROUP * gidx
    ing = (lane >= lo) & (lane < lo + EXPERTS_PER_GROUP)
    el = jnp.where(ing, lg, NEG_BIG)
    m1 = jnp.max(el, axis=-1, keepdims=True)
    se = jnp.sum(jnp.where(ing, jnp.exp(el - m1), 0.0), axis=-1, keepdims=True)
    i1 = jnp.min(jnp.where(ing & (el == m1), lane, big), axis=-1, keepdims=True)
    el2 = jnp.where(lane == i1, NEG_BIG, el)
    m2 = jnp.max(el2, axis=-1, keepdims=True)
    i2 = jnp.min(jnp.where(ing & (el2 == m2) & (lane != i1), lane, big), axis=-1, keepdims=True)
    p1 = 1.0 / se
    p2 = jnp.exp(m2 - m1) / se
    den = p1 + p2
    w1 = pg * p1 / den
    w2 = pg * p2 / den
    out = jnp.where(lane == 0.0, i1 - N_GROUPS,
                    jnp.where(lane == 1.0, i2 - N_GROUPS,
                              jnp.where(lane == 2.0, w1, jnp.where(lane == 3.0, w2, 0.0))))
    rt_ref[...] = out[:, :SUBLANES]


def _post(x2, oa2, ob2, ga2, gg2, mod3, norm2, wpa, wpb, wo, wr1, wr2, br, *, tiles_per_mod):
    rows = x2.shape[0]
    tm = ROW_TILE
    mr = mod3.shape[1]
    row = pl.BlockSpec((tm, D_MODEL), lambda i: (i, 0))
    modc = lambda c: pl.BlockSpec((None, mr, D_MODEL), lambda i: (i // tiles_per_mod, 0, c))
    return pl.pallas_call(
        _post_kernel,
        grid=(rows // tm,),
        in_specs=[row, row, row, row, row, modc(2), modc(3), modc(4), _resident((1, D_MODEL)),
                  _resident(wpa.shape), _resident(wpb.shape), _resident(wo.shape),
                  _resident(wr1.shape), _resident(wr2.shape), _resident((1, LANES))],
        out_specs=[row, row, pl.BlockSpec((tm, SUBLANES), lambda i: (i, 0))],
        out_shape=[jax.ShapeDtypeStruct((rows, D_MODEL), F32),
                   jax.ShapeDtypeStruct((rows, D_MODEL), BF16),
                   jax.ShapeDtypeStruct((rows, SUBLANES), F32)],
        compiler_params=_cparams(("arbitrary",)),
        name="post",
    )(x2, oa2, ob2, ga2, gg2, mod3, mod3, mod3, norm2.reshape(1, D_MODEL), wpa, wpb, wo, wr1, wr2, br)


def _moe_kernel(be_ref, nb_ref, x_ref, rw_ref, w1_ref, w3_ref, w2_ref, o_ref):
    @pl.when(pl.program_id(0) < nb_ref[0])
    def _():
        x = x_ref[...]
        a = _dot(x, w1_ref[...])
        g = a * _sigmoid(a) * _dot(x, w3_ref[...])
        o_ref[...] = (_dot(g.astype(BF16), w2_ref[...]) * rw_ref[...]).astype(BF16)

    @pl.when(pl.program_id(0) >= nb_ref[0])
    def _():
        o_ref[...] = jnp.zeros_like(o_ref)


def _moe(xs, roww, block_e, n_used, w1, w3, w2):
    n_blocks = block_e.shape[0]
    bm = MOE_BLOCK
    wspec = lambda s: pl.BlockSpec((None,) + s, lambda i, be, nb: (be[i], 0, 0))
    gs = pltpu.PrefetchScalarGridSpec(
        num_scalar_prefetch=2,
        grid=(n_blocks,),
        in_specs=[pl.BlockSpec((bm, D_MODEL), lambda i, be, nb: (i, 0)),
                  pl.BlockSpec((bm, 1), lambda i, be, nb: (i, 0)),
                  wspec((D_MODEL, D_EXPERT)), wspec((D_MODEL, D_EXPERT)), wspec((D_EXPERT, D_MODEL))],
        out_specs=pl.BlockSpec((bm, D_MODEL), lambda i, be, nb: (i, 0)),
    )
    return pl.pallas_call(
        _moe_kernel,
        grid_spec=gs,
        out_shape=jax.ShapeDtypeStruct((n_blocks * bm, D_MODEL), BF16),
        compiler_params=_cparams(("arbitrary",)),
        name="moe",
    )(block_e, n_used, xs, roww, w1, w3, w2)


def _final_kernel(x1_ref, ya_ref, yb_ref, g2_ref, nf_ref, o_ref):
    y = x1_ref[...] + g2_ref[...] * (ya_ref[...].astype(F32) + yb_ref[...].astype(F32))
    o_ref[...] = _rms(y, nf_ref[...])


def _final(x1, ya, yb, mod3, norm_f, *, tiles_per_mod):
    rows = x1.shape[0]
    tm = ROW_TILE
    mr = mod3.shape[1]
    row = pl.BlockSpec((tm, D_MODEL), lambda i: (i, 0))
    return pl.pallas_call(
        _final_kernel,
        grid=(rows // tm,),
        in_specs=[row, row, row,
                  pl.BlockSpec((None, mr, D_MODEL), lambda i: (i // tiles_per_mod, 0, 5)),
                  _resident((1, D_MODEL))],
        out_specs=row,
        out_shape=jax.ShapeDtypeStruct((rows, D_MODEL), F32),
        compiler_params=_cparams(("arbitrary",)),
        name="final",
    )(x1, ya, yb, mod3, norm_f.reshape(1, D_MODEL))


def _routing_plan(eid, wts, n_tokens):
    bm = MOE_BLOCK
    n_assign = n_tokens * TOP_K
    n_blocks = -(-n_assign // bm) + N_EXPERTS
    flat_e = eid.reshape(-1)
    onehot = (flat_e[:, None] == jnp.arange(N_EXPERTS, dtype=jnp.int32)[None, :]).astype(jnp.int32)
    rank = jnp.take_along_axis(jnp.cumsum(onehot, axis=0), flat_e[:, None], axis=1)[:, 0] - 1
    counts = jnp.sum(onehot, axis=0)
    padded = (counts + bm - 1) // bm * bm
    pad_end = jnp.cumsum(padded)
    pad_start = pad_end - padded
    dest = (pad_start[flat_e] + rank).astype(jnp.int32)
    rows = jnp.full((n_blocks * bm,), n_tokens, jnp.int32).at[dest].set(
        jnp.arange(n_assign, dtype=jnp.int32) // TOP_K)
    roww = jnp.zeros((n_blocks * bm,), F32).at[dest].set(wts.reshape(-1))
    block_e = jnp.minimum(jnp.searchsorted(pad_end, jnp.arange(n_blocks, dtype=jnp.int32) * bm, side='right'),
                          N_EXPERTS - 1).astype(jnp.int32)
    n_used = (pad_end[-1] // bm).astype(jnp.int32).reshape(1)
    return rows, roww.reshape(-1, 1), block_e, n_used, dest.reshape(n_tokens, TOP_K)


def _layer(xp, xs, c_all, cache_k, cache_v, cache_logf, state_conv, state_h, p, norm_f):
    Bp, Tp, _ = xp.shape
    Bs, Ts, _ = xs.shape
    past = cache_k.shape[1]
    np_rows, ns_rows = Bp * Tp, Bs * Ts
    assert Tp % ROW_TILE == 0 and ns_rows == ROW_TILE and Tp % ATTN_BLOCK == 0

    mod = _mod(c_all, p['w_mod'], p['b_mod'])
    mod_p = mod[:Bp].reshape(Bp, 1, 6 * D_MODEL)
    mod_s = jnp.repeat(mod[Bp:], Ts, axis=0).reshape(1, ns_rows, 6 * D_MODEL)

    w_in = p['w_in']
    d3 = 3 * D_MODEL
    wcat = jnp.concatenate([w_in[:, :D_MODEL] * (HEAD_DIM ** -0.5), w_in[:, D_MODEL:d3],
                            w_in[:, d3 + N_HEADS:], p['w_gate']], axis=1).astype(BF16)
    wfl = jnp.pad(w_in[:, d3:d3 + N_HEADS], ((0, 0), (0, LANES - N_HEADS))).astype(BF16)
    wflt = wfl.T
    eye = jnp.eye(GATE_TILE // (D_MODEL // N_RNN_BLOCKS), dtype=F32)

    def gate_tiles(w):
        per = GATE_TILE // w.shape[1]
        w4 = w.reshape(N_RNN_BLOCKS // per, per, w.shape[1], w.shape[2])
        return jnp.einsum('gpcd,pq->gpcqd', w4, eye).reshape(N_RNN_BLOCKS // per, GATE_TILE, GATE_TILE).astype(BF16)

    wr4, wi4 = gate_tiles(p['w_r']), gate_tiles(p['w_i'])
    wpa, wpb, wo = p['w_pa'].astype(BF16), p['w_pb'].astype(BF16), p['w_out'].astype(BF16)
    wr = jnp.pad(jnp.concatenate([p['w_rg'], p['w_re']], axis=1), ((0, 0), (0, LANES - N_GROUPS - N_EXPERTS)))
    wr1 = wr.astype(BF16)
    wr2 = (wr - wr1.astype(F32)).astype(BF16)
    br = jnp.pad(jnp.concatenate([p['b_rg'], p['b_re']]), (0, LANES - N_GROUPS - N_EXPERTS)).reshape(1, LANES)
    w1, w3, w2 = p['w1'].astype(BF16), p['w3'].astype(BF16), p['w2'].astype(BF16)

    tiles_p = Tp // ROW_TILE
    in_p = _in_proj(xp.reshape(np_rows, D_MODEL), mod_p, p['norm1'], wcat, wfl, wflt, p['b_gate'], p['b_f'],
                    tiles_per_mod=tiles_p, tiles_per_seq=tiles_p)
    in_s = _in_proj(xs.reshape(ns_rows, D_MODEL), mod_s, p['norm1'], wcat, wfl, wflt, p['b_gate'], p['b_f'],
                    tiles_per_mod=1, tiles_per_seq=1)
    q_p, k_p, v_p, xb_p, gb_p, ga_p, gg_p, lf_p, _, ft_p = in_p
    q_s, k_s, v_s, xb_s, gb_s, ga_s, gg_s, lf_s, lft_s, _ = in_s
    r3 = lambda a, B, T: a.reshape(B, T, D_MODEL)

    nq = Tp // ATTN_BLOCK
    ft5 = ft_p.reshape(N_PAIRS, 2, Bp, nq, ATTN_BLOCK).transpose(2, 0, 3, 1, 4)
    oa_p = _attn_prompt(r3(q_p, Bp, Tp), r3(k_p, Bp, Tp), r3(v_p, Bp, Tp), ft5)

    lfp3 = jnp.transpose(cache_logf.astype(F32), (0, 2, 1))
    lfn3 = lft_s.reshape(N_HEADS, Bs, Ts).transpose(1, 0, 2)
    oa_s = _attn_sample(r3(q_s, Bs, Ts), r3(k_s, Bs, Ts), r3(v_s, Bs, Ts),
                        cache_k.reshape(Bs, past, D_MODEL), cache_v.reshape(Bs, past, D_MODEL), lfp3, lfn3)

    zc = jnp.zeros((Bp, SUBLANES, D_MODEL), F32)
    zh = jnp.zeros((Bp, 1, D_MODEL), F32)
    sc = jnp.pad(state_conv.astype(F32), ((0, 0), (SUBLANES - (CONV_WIDTH - 1), 0), (0, 0)))
    rnn_w = (p['conv_w'], p['conv_b'], wr4, wi4, p['b_r'], p['b_i'], p['rglru_lambda'])
    ob_p, conv_p, hl_p = _rnn(r3(xb_p, Bp, Tp), r3(gb_p, Bp, Tp), zc, zh, *rnn_w)
    ob_s, conv_s, hl_s = _rnn(r3(xb_s, Bs, Ts), r3(gb_s, Bs, Ts), sc, state_h.astype(F32).reshape(Bs, 1, D_MODEL),
                              *rnn_w)

    post_w = (p['norm2'], wpa, wpb, wo, wr1, wr2, br)
    x1_p, h2_p, rt_p = _post(xp.reshape(np_rows, D_MODEL), oa_p.reshape(np_rows, D_MODEL),
                             ob_p.reshape(np_rows, D_MODEL), ga_p, gg_p, mod_p, *post_w, tiles_per_mod=tiles_p)
    x1_s, h2_s, rt_s = _post(xs.reshape(ns_rows, D_MODEL), oa_s.reshape(ns_rows, D_MODEL),
                             ob_s.reshape(ns_rows, D_MODEL), ga_s, gg_s, mod_s, *post_w, tiles_per_mod=1)

    n_tok = np_rows + ns_rows
    rt = jnp.concatenate([rt_p, rt_s], axis=0)
    eid = rt[:, :TOP_K].astype(jnp.int32)
    wts = rt[:, TOP_K:2 * TOP_K]
    rows, roww, block_e, n_used, dest = _routing_plan(eid, wts, n_tok)
    h2_all = jnp.concatenate([h2_p, h2_s, jnp.zeros((1, D_MODEL), BF16)], axis=0)
    yb = _moe(jnp.take(h2_all, rows, axis=0), roww, block_e, n_used, w1, w3, w2)
    ya0 = jnp.take(yb, dest[:, 0], axis=0)
    ya1 = jnp.take(yb, dest[:, 1], axis=0)

    y_p = _final(x1_p, ya0[:np_rows], ya1[:np_rows], mod_p, norm_f, tiles_per_mod=tiles_p)
    y_s = _final(x1_s, ya0[np_rows:], ya1[np_rows:], mod_s, norm_f, tiles_per_mod=1)

    st = CONV_WIDTH - 1
    state_p = (k_p.reshape(Bp, Tp, N_HEADS, HEAD_DIM), v_p.reshape(Bp, Tp, N_HEADS, HEAD_DIM),
               lf_p.reshape(Bp, Tp, N_HEADS), conv_p[:, SUBLANES - st:], hl_p.reshape(Bp, D_MODEL))
    state_s = (k_s.reshape(Bs, Ts, N_HEADS, HEAD_DIM), v_s.reshape(Bs, Ts, N_HEADS, HEAD_DIM),
               lf_s.reshape(Bs, Ts, N_HEADS), conv_s[:, SUBLANES - st:], hl_s.reshape(Bs, D_MODEL))
    return y_p.reshape(Bp, Tp, D_MODEL), y_s.reshape(Bs, Ts, D_MODEL), state_p, state_s


def kernel(x_prompt, x_sample, cache_k, cache_v, cache_logf, state_conv, state_h, c_prompt, c_sample, w_mod, b_mod, norm1, w_in, b_f, conv_w, conv_b, w_r, b_r, w_i, b_i, rglru_lambda, w_gate, b_gate, w_pa, w_pb, w_out, norm2, w_rg, b_rg, w_re, b_re, w1, w3, w2, norm_f):
    assert w_mod.shape[0] == 1, "single-layer trunk: the final norm is fused into the layer's last stage"
    names = ('w_mod', 'b_mod', 'norm1', 'w_in', 'b_f', 'conv_w', 'conv_b', 'w_r', 'b_r', 'w_i', 'b_i',
             'rglru_lambda', 'w_gate', 'b_gate', 'w_pa', 'w_pb', 'w_out', 'norm2', 'w_rg', 'b_rg', 'w_re', 'b_re',
             'w1', 'w3', 'w2')
    vals = (w_mod, b_mod, norm1, w_in, b_f, conv_w, conv_b, w_r, b_r, w_i, b_i, rglru_lambda, w_gate, b_gate,
            w_pa, w_pb, w_out, norm2, w_rg, b_rg, w_re, b_re, w1, w3, w2)
    p = {n: v[0] for n, v in zip(names, vals)}
    c_all = jnp.concatenate([c_prompt, c_sample], axis=0)
    y_p, y_s, sp, ss = _layer(x_prompt, x_sample, c_all, cache_k[0], cache_v[0], cache_logf[0],
                              state_conv[0], state_h[0], p, norm_f)
    lead = lambda a: a[None]
    return (y_p, y_s, lead(sp[0]), lead(sp[1]), lead(sp[2]), lead(sp[3]), lead(sp[4]),
            lead(ss[0]), lead(ss[1]), lead(ss[2]), lead(ss[3]), lead(ss[4]))
```

```python
import functools

import jax
import jax.numpy as jnp
from jax import lax
from jax.experimental import pallas as pl
from jax.experimental.pallas import tpu as pltpu

F32 = jnp.float32
BF16 = jnp.bfloat16

D_MODEL = 1024
N_HEADS = 16
HEAD_DIM = 64
HEAD_PAIR = 2 * HEAD_DIM
N_PAIRS = N_HEADS // 2
N_RNN_BLOCKS = 16
CONV_WIDTH = 4
RGLRU_C = 8.0
N_GROUPS = 4
EXPERTS_PER_GROUP = 8
N_EXPERTS = N_GROUPS * EXPERTS_PER_GROUP
TOP_K = 2
D_EXPERT = 512
EPS = 1e-6

LANES = 128
SUBLANES = 8
ROW_TILE = 512
ATTN_BLOCK = 512
RNN_TILE = 256
MOE_BLOCK = 256
GATE_TILE = 256
VMEM_LIMIT = 56 * 1024 * 1024
NEG_BIG = -1e30
LOG2E = 1.4426950408889634


def _cparams(sem):
    return pltpu.CompilerParams(dimension_semantics=sem, vmem_limit_bytes=VMEM_LIMIT)


def _resident(shape):
    nd = len(shape)
    return pl.BlockSpec(shape, lambda *_: (0,) * nd, pipeline_mode=pl.Buffered(1))


def _split3(x):
    p1 = x.astype(BF16)
    r1 = x - p1.astype(F32)
    p2 = r1.astype(BF16)
    p3 = (r1 - p2.astype(F32)).astype(BF16)
    return p1, p2, p3


def _dot(a, b):
    return jnp.dot(a, b, preferred_element_type=F32)


def _dot_nt(a, b):
    return lax.dot_general(a, b, (((1,), (1,)), ((), ())), preferred_element_type=F32)


def _sigmoid(x):
    return 1.0 / (1.0 + jnp.exp(-x))


def _gelu_tanh(x):
    return 0.5 * x * (1.0 + jnp.tanh(0.7978845608028654 * (x + 0.044715 * x * x * x)))


def _rms(x, g):
    ms = jnp.mean(x * x, axis=-1, keepdims=True)
    return x * lax.rsqrt(ms + EPS) * g


def _div_p2(x, n):
    assert n & (n - 1) == 0
    return lax.shift_right_logical(x, jnp.int32(n.bit_length() - 1))


def _mod_p2(x, n):
    assert n & (n - 1) == 0
    return x & (n - 1)


def _upper_tri(n):
    r = lax.broadcasted_iota(jnp.int32, (n, n), 0)
    c = lax.broadcasted_iota(jnp.int32, (n, n), 1)
    return jnp.where(r <= c, 1.0, 0.0).astype(BF16)


def _cumsum_lanes(x, u):
    p1, p2, p3 = _split3(x)
    return _dot(p1, u) + _dot(p2, u) + _dot(p3, u)


def _mod_kernel(c_ref, w_ref, b_ref, o_ref):
    c = c_ref[...]
    s = c * _sigmoid(c)
    s1 = s.astype(BF16)
    s2 = (s - s1.astype(F32)).astype(BF16)
    w = w_ref[...]
    w1 = w.astype(BF16)
    w2 = (w - w1.astype(F32)).astype(BF16)
    o_ref[...] = _dot(s1, w1) + _dot(s1, w2) + _dot(s2, w1) + b_ref[...]


def _mod(c, w, b):
    n = c.shape[0]
    cols = w.shape[1]
    return pl.pallas_call(
        _mod_kernel,
        grid=(cols // D_MODEL,),
        in_specs=[pl.BlockSpec((n, D_MODEL), lambda j: (0, 0)),
                  pl.BlockSpec((D_MODEL, D_MODEL), lambda j: (0, j)),
                  pl.BlockSpec((1, D_MODEL), lambda j: (0, j))],
        out_specs=pl.BlockSpec((n, D_MODEL), lambda j: (0, j)),
        out_shape=jax.ShapeDtypeStruct((n, cols), F32),
        compiler_params=_cparams(("arbitrary",)),
        name="mod",
    )(c, w, b.reshape(1, cols))


def _in_kernel(x_ref, sh_ref, sc_ref, n1_ref, w_ref, wfl_ref, wflt_ref, bg_ref, bf_ref, bft_ref,
               q_ref, k_ref, v_ref, xb_ref, gb_ref, ga_ref, gg_ref, lf_ref, lft_ref, fc_ref,
               carry_ref, *, tiles_per_seq):
    x = x_ref[...]
    h = (_rms(x, n1_ref[...]) * (1.0 + sc_ref[...]) + sh_ref[...]).astype(BF16)

    def mm(c):
        return _dot(h, w_ref[:, c * D_MODEL:(c + 1) * D_MODEL])

    q_ref[...] = mm(0).astype(BF16)
    k_ref[...] = mm(1)
    v_ref[...] = mm(2)
    xb_ref[...] = mm(3).astype(BF16)
    gb_ref[...] = mm(4).astype(BF16)
    ga_ref[...] = _sigmoid(mm(5) + bg_ref[:, :D_MODEL]).astype(BF16)
    gg_ref[...] = _sigmoid(mm(6) + bg_ref[:, D_MODEL:]).astype(BF16)

    def log_sigmoid(z):
        return jnp.minimum(z, 0.0) - jnp.log(1.0 + jnp.exp(-jnp.abs(z)))

    lf = log_sigmoid(_dot(h, wfl_ref[...]) + bf_ref[...])
    lf_ref[...] = lf[:, :N_HEADS]
    lft_ref[...] = log_sigmoid(_dot_nt(wflt_ref[...], h)[:N_HEADS, :] + bft_ref[...])

    @pl.when(pl.program_id(0) % tiles_per_seq == 0)
    def _():
        carry_ref[...] = jnp.zeros_like(carry_ref)

    tm = x.shape[0]
    r = lax.broadcasted_iota(jnp.int32, (tm, tm), 0)
    c = lax.broadcasted_iota(jnp.int32, (tm, tm), 1)
    lower = jnp.where(c <= r, 1.0, 0.0).astype(BF16)
    p1, p2, p3 = _split3(lf)
    fc = _dot(lower, p1) + _dot(lower, p2) + _dot(lower, p3) + carry_ref[...]
    fc_ref[...] = fc[:, :N_HEADS]
    carry_ref[...] = fc[tm - 1:tm, :]


def _in_proj(x2, mod3, norm1, wcat, wfl, wflt, b_gate, b_f, *, tiles_per_mod, tiles_per_seq):
    rows = x2.shape[0]
    tm = ROW_TILE
    mr = mod3.shape[1]
    nt = rows // tm
    row_blk = lambda dt: pl.BlockSpec((tm, D_MODEL), lambda i: (i, 0))
    outs = [jax.ShapeDtypeStruct((rows, D_MODEL), BF16),
            jax.ShapeDtypeStruct((rows, D_MODEL), F32),
            jax.ShapeDtypeStruct((rows, D_MODEL), F32),
            jax.ShapeDtypeStruct((rows, D_MODEL), BF16),
            jax.ShapeDtypeStruct((rows, D_MODEL), BF16),
            jax.ShapeDtypeStruct((rows, D_MODEL), BF16),
            jax.ShapeDtypeStruct((rows, D_MODEL), BF16),
            jax.ShapeDtypeStruct((rows, N_HEADS), F32),
            jax.ShapeDtypeStruct((N_HEADS, rows), F32),
            jax.ShapeDtypeStruct((rows, N_HEADS), F32)]
    return pl.pallas_call(
        functools.partial(_in_kernel, tiles_per_seq=tiles_per_seq),
        grid=(nt,),
        in_specs=[pl.BlockSpec((tm, D_MODEL), lambda i: (i, 0)),
                  pl.BlockSpec((None, mr, D_MODEL), lambda i: (i // tiles_per_mod, 0, 0)),
                  pl.BlockSpec((None, mr, D_MODEL), lambda i: (i // tiles_per_mod, 0, 1)),
                  _resident((1, D_MODEL)),
                  _resident(wcat.shape),
                  _resident(wfl.shape),
                  _resident(wflt.shape),
                  _resident((1, 2 * D_MODEL)),
                  _resident((1, LANES)),
                  _resident((N_HEADS, 1))],
        out_specs=[row_blk(None)] * 7 + [pl.BlockSpec((tm, N_HEADS), lambda i: (i, 0)),
                                          pl.BlockSpec((N_HEADS, tm), lambda i: (0, i)),
                                          pl.BlockSpec((tm, N_HEADS), lambda i: (i, 0))],
        out_shape=outs,
        scratch_shapes=[pltpu.VMEM((1, LANES), F32)],
        compiler_params=_cparams(("arbitrary",)),
        name="in_proj",
    )(x2, mod3, mod3, norm1.reshape(1, D_MODEL), wcat, wfl, wflt,
      b_gate.reshape(1, 2 * D_MODEL), jnp.pad(b_f, (0, LANES - N_HEADS)).reshape(1, LANES),
      b_f.reshape(N_HEADS, 1))


def _attn_kernel(q_ref, k_ref, v_ref, f_ref, o_ref, kb_ref, vt_ref, fb_ref, qt_ref, m_ref, l_ref, acc_ref):
    qi = pl.program_id(2)
    bq = q_ref.shape[0]
    bk = ATTN_BLOCK
    nk = k_ref.shape[0] // bk

    @pl.when(qi == 0)
    def _():
        kb_ref[...] = k_ref[...].astype(BF16)
        for c in range(nk):
            vt_ref[c] = v_ref[c * bk:(c + 1) * bk, :].T.astype(BF16)
        f = f_ref[...] * LOG2E
        for hh in range(2):
            fb_ref[hh] = jnp.broadcast_to(f[:, hh:hh + 1], fb_ref.shape[1:])

    qt = q_ref[...].astype(F32).T
    feat = lax.broadcasted_iota(jnp.int32, qt.shape, 0)
    qt_ref[0] = jnp.where(feat < HEAD_DIM, qt, 0.0).astype(BF16)
    qt_ref[1] = jnp.where(feat < HEAD_DIM, 0.0, qt).astype(BF16)

    m_ref[...] = jnp.full(m_ref.shape, NEG_BIG, F32)
    l_ref[...] = jnp.zeros(l_ref.shape, F32)
    acc_ref[...] = jnp.zeros(acc_ref.shape, F32)

    def chunk(j, masked):
        start = pl.multiple_of(j * bk, bk)
        kc = kb_ref[pl.ds(start, bk), :]
        vt = vt_ref[j]
        for hh in range(2):
            fb = fb_ref[hh, pl.ds(start, bk), :]
            s = _dot(kc, qt_ref[hh]) - jnp.concatenate([fb] * (bq // LANES), axis=1)
            if masked:
                key = lax.broadcasted_iota(jnp.int32, s.shape, 0)
                qry = lax.broadcasted_iota(jnp.int32, s.shape, 1)
                s = jnp.where(key <= qry, s, NEG_BIG)
            m_prev = m_ref[hh]
            m_new = jnp.maximum(m_prev, jnp.max(s, axis=0, keepdims=True))
            alpha = jnp.exp2(m_prev - m_new)
            p = jnp.exp2(s - m_new)
            l_ref[hh] = alpha * l_ref[hh] + jnp.sum(p, axis=0, keepdims=True)
            pv = _dot(vt[hh * HEAD_DIM:(hh + 1) * HEAD_DIM, :], p.astype(BF16))
            acc_ref[hh] = alpha * acc_ref[hh] + pv
            m_ref[hh] = m_new

    def body(j, carry):
        chunk(j, False)
        return carry

    lax.fori_loop(0, qi, body, 0)
    chunk(qi, True)

    ot = jnp.concatenate([acc_ref[0] / l_ref[0], acc_ref[1] / l_ref[1]], axis=0)
    o_ref[...] = ot.T.astype(BF16)


def _attn_prompt(q3, k3, v3, fc4):
    B, T, _ = q3.shape
    bq = ATTN_BLOCK
    nq = T // bq
    return pl.pallas_call(
        _attn_kernel,
        grid=(B, N_PAIRS, nq),
        in_specs=[pl.BlockSpec((None, bq, HEAD_PAIR), lambda b, p, i: (b, i, p)),
                  pl.BlockSpec((None, T, HEAD_PAIR), lambda b, p, i: (b, 0, p)),
                  pl.BlockSpec((None, T, HEAD_PAIR), lambda b, p, i: (b, 0, p)),
                  pl.BlockSpec((None, None, T, 2), lambda b, p, i: (b, p, 0, 0))],
        out_specs=pl.BlockSpec((None, bq, HEAD_PAIR), lambda b, p, i: (b, i, p)),
        out_shape=jax.ShapeDtypeStruct((B, T, D_MODEL), BF16),
        scratch_shapes=[pltpu.VMEM((T, HEAD_PAIR), BF16),
                        pltpu.VMEM((nq, HEAD_PAIR, bq), BF16),
                        pltpu.VMEM((2, T, LANES), F32),
                        pltpu.VMEM((2, HEAD_PAIR, bq), BF16),
                        pltpu.VMEM((2, 1, bq), F32),
                        pltpu.VMEM((2, 1, bq), F32),
                        pltpu.VMEM((2, HEAD_DIM, bq), F32)],
        compiler_params=_cparams(("arbitrary", "arbitrary", "arbitrary")),
        name="attn_prompt",
    )(q3, k3, v3, fc4)


def _attn_sample_kernel(q_ref, kn_ref, vn_ref, kc_ref, vc_ref, lfp_ref, lfn_ref, o_ref):
    tq = q_ref.shape[0]
    past = kc_ref.shape[0]
    nrow = N_HEADS * tq
    row = lax.broadcasted_iota(jnp.int32, (nrow, D_MODEL), 0)
    col = lax.broadcasted_iota(jnp.int32, (nrow, D_MODEL), 1)
    own = _div_p2(row, tq) == _div_p2(col, HEAD_DIM)

    rr = lax.broadcasted_iota(jnp.int32, (nrow, tq), 0)
    rc = lax.broadcasted_iota(jnp.int32, (nrow, tq), 1)
    rep = jnp.where(_mod_p2(rr, tq) == rc, 1.0, 0.0).astype(BF16)
    qx = jnp.where(own, _dot(rep, q_ref[...]), 0.0).astype(BF16)

    fp = _cumsum_lanes(lfp_ref[...] * LOG2E, _upper_tri(past))
    fn = _cumsum_lanes(lfn_ref[...] * LOG2E, _upper_tri(tq)) + fp[:, past - 1:past]
    hr = lax.broadcasted_iota(jnp.int32, (nrow, N_HEADS), 0)
    hc = lax.broadcasted_iota(jnp.int32, (nrow, N_HEADS), 1)
    hsel = jnp.where(_div_p2(hr, tq) == hc, 1.0, 0.0).astype(BF16)

    def expand(f):
        p1, p2, p3 = _split3(f)
        return _dot(hsel, p1) + _dot(hsel, p2) + _dot(hsel, p3)

    s_p = _dot_nt(qx, kc_ref[...].astype(BF16)) - expand(fp)
    s_n = _dot_nt(qx, kn_ref[...].astype(BF16)) - expand(fn)
    nr = lax.broadcasted_iota(jnp.int32, (nrow, tq), 0)
    nc = lax.broadcasted_iota(jnp.int32, (nrow, tq), 1)
    s_n = jnp.where(nc <= _mod_p2(nr, tq), s_n, NEG_BIG)
    m = jnp.maximum(jnp.max(s_p, axis=-1, keepdims=True), jnp.max(s_n, axis=-1, keepdims=True))
    p_p = jnp.exp2(s_p - m)
    p_n = jnp.exp2(s_n - m)
    l = jnp.sum(p_p, axis=-1, keepdims=True) + jnp.sum(p_n, axis=-1, keepdims=True)
    o_full = _dot(p_p.astype(BF16), vc_ref[...].astype(BF16)) + _dot(p_n.astype(BF16), vn_ref[...].astype(BF16))
    o_own = jnp.where(own, o_full / l, 0.0).astype(BF16)
    cr = lax.broadcasted_iota(jnp.int32, (tq, nrow), 0)
    cc = lax.broadcasted_iota(jnp.int32, (tq, nrow), 1)
    col_sel = jnp.where(_mod_p2(cc, tq) == cr, 1.0, 0.0).astype(BF16)
    o_ref[...] = _dot(col_sel, o_own).astype(BF16)


def _attn_sample(q3, kn3, vn3, kc3, vc3, lfp3, lfn3):
    B, tq, _ = q3.shape
    past = kc3.shape[1]
    blk = lambda n, w: pl.BlockSpec((None, n, w), lambda b: (b, 0, 0))
    return pl.pallas_call(
        _attn_sample_kernel,
        grid=(B,),
        in_specs=[blk(tq, D_MODEL), blk(tq, D_MODEL), blk(tq, D_MODEL),
                  blk(past, D_MODEL), blk(past, D_MODEL), blk(N_HEADS, past), blk(N_HEADS, tq)],
        out_specs=blk(tq, D_MODEL),
        out_shape=jax.ShapeDtypeStruct((B, tq, D_MODEL), BF16),
        compiler_params=_cparams(("arbitrary",)),
        name="attn_sample",
    )(q3, kn3, vn3, kc3, vc3, lfp3, lfn3)


def _rnn_kernel(xb_ref, gb_ref, c0_ref, h0_ref, cw_ref, cb_ref, wr_ref, wi_ref, br_ref, bi_ref, lam_ref,
                o_ref, cout_ref, hout_ref, xp_ref, hc_ref):
    ti = pl.program_id(1)
    tt = xb_ref.shape[0]
    pad = SUBLANES

    @pl.when(ti == 0)
    def _():
        xp_ref[0:pad, :] = c0_ref[...]
        hc_ref[...] = h0_ref[...]

    xp_ref[pad:pad + tt, :] = xb_ref[...].astype(F32)
    xc = cb_ref[...] + xp_ref[pad:pad + tt, :] * cw_ref[CONV_WIDTH - 1:CONV_WIDTH, :]
    for k in range(CONV_WIDTH - 1):
        back = CONV_WIDTH - 1 - k
        xc = xc + xp_ref[pad - back:pad - back + tt, :] * cw_ref[k:k + 1, :]
    tail = xp_ref[tt:tt + pad, :]
    xp_ref[0:pad, :] = tail

    xcb = xc.astype(BF16)
    ng = D_MODEL // GATE_TILE
    r = jnp.concatenate([_dot(xcb[:, g * GATE_TILE:(g + 1) * GATE_TILE], wr_ref[g]) for g in range(ng)], axis=-1)
    i = jnp.concatenate([_dot(xcb[:, g * GATE_TILE:(g + 1) * GATE_TILE], wi_ref[g]) for g in range(ng)], axis=-1)
    r = _sigmoid(r + br_ref[...])
    i = _sigmoid(i + bi_ref[...])
    lam = lam_ref[...]
    softplus_neg = jnp.maximum(-lam, 0.0) + jnp.log(1.0 + jnp.exp(-jnp.abs(lam)))
    log_a = (-RGLRU_C) * r * softplus_neg
    a = jnp.exp(log_a)
    th = jnp.tanh(log_a)
    mult = jnp.sqrt(-2.0 * th / (1.0 - th))
    b = mult * (i * xc)

    ng8 = tt // SUBLANES
    a3 = a.reshape(ng8, SUBLANES, D_MODEL)
    b3 = b.reshape(ng8, SUBLANES, D_MODEL)
    sub = lax.broadcasted_iota(jnp.int32, a3.shape, 1)
    for s in (1, 2, 4):
        keep = sub >= s
        a_sh = jnp.where(keep, pltpu.roll(a3, s, 1), 1.0)
        b_sh = jnp.where(keep, pltpu.roll(b3, s, 1), 0.0)
        b3 = a3 * b_sh + b3
        a3 = a3 * a_sh
    hc = hc_ref[...]
    hs = []
    for g in range(ng8):
        hg = b3[g] + a3[g] * hc
        hs.append(hg)
        hc = hg[SUBLANES - 1:SUBLANES, :]
    hc_ref[...] = hc
    h = jnp.concatenate(hs, axis=0)

    o_ref[...] = (h * _gelu_tanh(gb_ref[...].astype(F32))).astype(BF16)

    @pl.when(ti == pl.num_programs(1) - 1)
    def _():
        cout_ref[...] = tail
        hout_ref[...] = hc


def _rnn(xb3, gb3, conv0, h0, conv_w, conv_b, wr4, wi4, b_r, b_i, lam):
    B, T, _ = xb3.shape
    tt = min(RNN_TILE, T)
    vec = lambda a: a.reshape(1, D_MODEL)
    seq = pl.BlockSpec((None, tt, D_MODEL), lambda b, t: (b, t, 0))
    st8 = pl.BlockSpec((None, SUBLANES, D_MODEL), lambda b, t: (b, 0, 0))
    st1 = pl.BlockSpec((None, 1, D_MODEL), lambda b, t: (b, 0, 0))
    return pl.pallas_call(
        _rnn_kernel,
        grid=(B, T // tt),
        in_specs=[seq, seq, st8, st1,
                  _resident((CONV_WIDTH, D_MODEL)), _resident((1, D_MODEL)),
                  _resident(wr4.shape), _resident(wi4.shape),
                  _resident((1, D_MODEL)), _resident((1, D_MODEL)), _resident((1, D_MODEL))],
        out_specs=[seq, st8, st1],
        out_shape=[jax.ShapeDtypeStruct((B, T, D_MODEL), BF16),
                   jax.ShapeDtypeStruct((B, SUBLANES, D_MODEL), F32),
                   jax.ShapeDtypeStruct((B, 1, D_MODEL), F32)],
        scratch_shapes=[pltpu.VMEM((tt + SUBLANES, D_MODEL), F32),
                        pltpu.VMEM((1, D_MODEL), F32)],
        compiler_params=_cparams(("arbitrary", "arbitrary")),
        name="rnn",
    )(xb3, gb3, conv0, h0, conv_w, vec(conv_b), wr4, wi4, vec(b_r), vec(b_i), vec(lam))


def _post_kernel(x_ref, oa_ref, ob_ref, ga_ref, gg_ref, g1_ref, sh_ref, sc_ref, n2_ref,
                 wpa_ref, wpb_ref, wo_ref, wr1_ref, wr2_ref, br_ref,
                 x1_ref, h2_ref, rt_ref):
    mix = (ga_ref[...].astype(F32) * _dot(oa_ref[...], wpa_ref[...])
           + gg_ref[...].astype(F32) * _dot(ob_ref[...], wpb_ref[...]))
    x1 = x_ref[...] + g1_ref[...] * _dot(mix.astype(BF16), wo_ref[...])
    x1_ref[...] = x1
    h2 = _rms(x1, n2_ref[...]) * (1.0 + sc_ref[...]) + sh_ref[...]
    h2a = h2.astype(BF16)
    h2_ref[...] = h2a
    h2b = (h2 - h2a.astype(F32)).astype(BF16)
    lg = _dot(h2a, wr1_ref[...]) + _dot(h2a, wr2_ref[...]) + _dot(h2b, wr1_ref[...]) + br_ref[...]

    lane = lax.broadcasted_iota(jnp.int32, lg.shape, 1).astype(F32)
    big = float(LANES)
    is_g = lane < N_GROUPS
    gl = jnp.where(is_g, lg, NEG_BIG)
    mg = jnp.max(gl, axis=-1, keepdims=True)
    sg = jnp.sum(jnp.where(is_g, jnp.exp(gl - mg), 0.0), axis=-1, keepdims=True)
    pg = 1.0 / sg
    gidx = jnp.min(jnp.where(is_g & (gl == mg), lane, big), axis=-1, keepdims=True)
    lo = N_GROUPS + EXPERTS_PER_GROUP * gidx
    ing = (lane >= lo) & (lane < lo + EXPERTS_PER_GROUP)
    el = jnp.where(ing, lg, NEG_BIG)
    m1 = jnp.max(el, axis=-1, keepdims=True)
    se = jnp.sum(jnp.where(ing, jnp.exp(el - m1), 0.0), axis=-1, keepdims=True)
    i1 = jnp.min(jnp.where(ing & (el == m1), lane, big), axis=-1, keepdims=True)
    el2 = jnp.where(lane == i1, NEG_BIG, el)
    m2 = jnp.max(el2, axis=-1, keepdims=True)
    i2 = jnp.min(jnp.where(ing & (el2 == m2) & (lane != i1), lane, big), axis=-1, keepdims=True)
    p1 = 1.0 / se
    p2 = jnp.exp(m2 - m1) / se
    den = p1 + p2
    w1 = pg * p1 / den
    w2 = pg * p2 / den
    out = jnp.where(lane == 0.0, i1 - N_GROUPS,
                    jnp.where(lane == 1.0, i2 - N_GROUPS,
                              jnp.where(lane == 2.0, w1, jnp.where(lane == 3.0, w2, 0.0))))
    rt_ref[...] = out[:, :SUBLANES]


def _post(x2, oa2, ob2, ga2, gg2, mod3, norm2, wpa, wpb, wo, wr1, wr2, br, *, tiles_per_mod):
    rows = x2.shape[0]
    tm = ROW_TILE
    mr = mod3.shape[1]
    row = pl.BlockSpec((tm, D_MODEL), lambda i: (i, 0))
    modc = lambda c: pl.BlockSpec((None, mr, D_MODEL), lambda i: (i // tiles_per_mod, 0, c))
    return pl.pallas_call(
        _post_kernel,
        grid=(rows // tm,),
        in_specs=[row, row, row, row, row, modc(2), modc(3), modc(4), _resident((1, D_MODEL)),
                  _resident(wpa.shape), _resident(wpb.shape), _resident(wo.shape),
                  _resident(wr1.shape), _resident(wr2.shape), _resident((1, LANES))],
        out_specs=[row, row, pl.BlockSpec((tm, SUBLANES), lambda i: (i, 0))],
        out_shape=[jax.ShapeDtypeStruct((rows, D_MODEL), F32),
                   jax.ShapeDtypeStruct((rows, D_MODEL), BF16),
                   jax.ShapeDtypeStruct((rows, SUBLANES), F32)],
        compiler_params=_cparams(("arbitrary",)),
        name="post",
    )(x2, oa2, ob2, ga2, gg2, mod3, mod3, mod3, norm2.reshape(1, D_MODEL), wpa, wpb, wo, wr1, wr2, br)


def _moe_kernel(be_ref, nb_ref, x_ref, rw_ref, w1_ref, w3_ref, w2_ref, o_ref):
    @pl.when(pl.program_id(0) < nb_ref[0])
    def _():
        x = x_ref[...]
        a = _dot(x, w1_ref[...])
        g = a * _sigmoid(a) * _dot(x, w3_ref[...])
        o_ref[...] = (_dot(g.astype(BF16), w2_ref[...]) * rw_ref[...]).astype(BF16)

    @pl.when(pl.program_id(0) >= nb_ref[0])
    def _():
        o_ref[...] = jnp.zeros_like(o_ref)


def _moe(xs, roww, block_e, n_used, w1, w3, w2):
    n_blocks = block_e.shape[0]
    bm = MOE_BLOCK
    wspec = lambda s: pl.BlockSpec((None,) + s, lambda i, be, nb: (be[i], 0, 0))
    gs = pltpu.PrefetchScalarGridSpec(
        num_scalar_prefetch=2,
        grid=(n_blocks,),
        in_specs=[pl.BlockSpec((bm, D_MODEL), lambda i, be, nb: (i, 0)),
                  pl.BlockSpec((bm, 1), lambda i, be, nb: (i, 0)),
                  wspec((D_MODEL, D_EXPERT)), wspec((D_MODEL, D_EXPERT)), wspec((D_EXPERT, D_MODEL))],
        out_specs=pl.BlockSpec((bm, D_MODEL), lambda i, be, nb: (i, 0)),
    )
    return pl.pallas_call(
        _moe_kernel,
        grid_spec=gs,
        out_shape=jax.ShapeDtypeStruct((n_blocks * bm, D_MODEL), BF16),
        compiler_params=_cparams(("arbitrary",)),
        name="moe",
    )(block_e, n_used, xs, roww, w1, w3, w2)


def _final_kernel(x1_ref, ya_ref, yb_ref, g2_ref, nf_ref, o_ref):
    y = x1_ref[...] + g2_ref[...] * (ya_ref[...].astype(F32) + yb_ref[...].astype(F32))
    o_ref[...] = _rms(y, nf_ref[...])


def _final(x1, ya, yb, mod3, norm_f, *, tiles_per_mod):
    rows = x1.shape[0]
    tm = ROW_TILE
    mr = mod3.shape[1]
    row = pl.BlockSpec((tm, D_MODEL), lambda i: (i, 0))
    return pl.pallas_call(
        _final_kernel,
        grid=(rows // tm,),
        in_specs=[row, row, row,
                  pl.BlockSpec((None, mr, D_MODEL), lambda i: (i // tiles_per_mod, 0, 5)),
                  _resident((1, D_MODEL))],
        out_specs=row,
        out_shape=jax.ShapeDtypeStruct((rows, D_MODEL), F32),
        compiler_params=_cparams(("arbitrary",)),
        name="final",
    )(x1, ya, yb, mod3, norm_f.reshape(1, D_MODEL))


def _routing_plan(eid, wts, n_tokens):
    bm = MOE_BLOCK
    n_assign = n_tokens * TOP_K
    n_blocks = -(-n_assign // bm) + N_EXPERTS
    flat_e = eid.reshape(-1)
    onehot = (flat_e[:, None] == jnp.arange(N_EXPERTS, dtype=jnp.int32)[None, :]).astype(jnp.int32)
    rank = jnp.take_along_axis(jnp.cumsum(onehot, axis=0), flat_e[:, None], axis=1)[:, 0] - 1
    counts = jnp.sum(onehot, axis=0)
    padded = (counts + bm - 1) // bm * bm
    pad_end = jnp.cumsum(padded)
    pad_start = pad_end - padded
    dest = (pad_start[flat_e] + rank).astype(jnp.int32)
    rows = jnp.full((n_blocks * bm,), n_tokens, jnp.int32).at[dest].set(
        jnp.arange(n_assign, dtype=jnp.int32) // TOP_K)
    roww = jnp.zeros((n_blocks * bm,), F32).at[dest].set(wts.reshape(-1))
    block_e = jnp.minimum(jnp.searchsorted(pad_end, jnp.arange(n_blocks, dtype=jnp.int32) * bm, side='right'),
                          N_EXPERTS - 1).astype(jnp.int32)
    n_used = (pad_end[-1] // bm).astype(jnp.int32).reshape(1)
    return rows, roww.reshape(-1, 1), block_e, n_used, dest.reshape(n_tokens, TOP_K)


def _layer(xp, xs, c_all, cache_k, cache_v, cache_logf, state_conv, state_h, p, norm_f):
    Bp, Tp, _ = xp.shape
    Bs, Ts, _ = xs.shape
    past = cache_k.shape[1]
    np_rows, ns_rows = Bp * Tp, Bs * Ts
    assert Tp % ROW_TILE == 0 and ns_rows == ROW_TILE and Tp % ATTN_BLOCK == 0

    mod = _mod(c_all, p['w_mod'], p['b_mod'])
    mod_p = mod[:Bp].reshape(Bp, 1, 6 * D_MODEL)
    mod_s = jnp.repeat(mod[Bp:], Ts, axis=0).reshape(1, ns_rows, 6 * D_MODEL)

    w_in = p['w_in']
    d3 = 3 * D_MODEL
    wcat = jnp.concatenate([w_in[:, :D_MODEL] * (LOG2E * HEAD_DIM ** -0.5), w_in[:, D_MODEL:d3],
                            w_in[:, d3 + N_HEADS:], p['w_gate']], axis=1).astype(BF16)
    wfl = jnp.pad(w_in[:, d3:d3 + N_HEADS], ((0, 0), (0, LANES - N_HEADS))).astype(BF16)
    wflt = wfl.T
    eye = jnp.eye(GATE_TILE // (D_MODEL // N_RNN_BLOCKS), dtype=F32)

    def gate_tiles(w):
        per = GATE_TILE // w.shape[1]
        w4 = w.reshape(N_RNN_BLOCKS // per, per, w.shape[1], w.shape[2])
        return jnp.einsum('gpcd,pq->gpcqd', w4, eye).reshape(N_RNN_BLOCKS // per, GATE_TILE, GATE_TILE).astype(BF16)

    wr4, wi4 = gate_tiles(p['w_r']), gate_tiles(p['w_i'])
    wpa, wpb, wo = p['w_pa'].astype(BF16), p['w_pb'].astype(BF16), p['w_out'].astype(BF16)
    wr = jnp.pad(jnp.concatenate([p['w_rg'], p['w_re']], axis=1), ((0, 0), (0, LANES - N_GROUPS - N_EXPERTS)))
    wr1 = wr.astype(BF16)
    wr2 = (wr - wr1.astype(F32)).astype(BF16)
    br = jnp.pad(jnp.concatenate([p['b_rg'], p['b_re']]), (0, LANES - N_GROUPS - N_EXPERTS)).reshape(1, LANES)
    w1, w3, w2 = p['w1'].astype(BF16), p['w3'].astype(BF16), p['w2'].astype(BF16)

    tiles_p = Tp // ROW_TILE
    in_p = _in_proj(xp.reshape(np_rows, D_MODEL), mod_p, p['norm1'], wcat, wfl, wflt, p['b_gate'], p['b_f'],
                    tiles_per_mod=tiles_p, tiles_per_seq=tiles_p)
    in_s = _in_proj(xs.reshape(ns_rows, D_MODEL), mod_s, p['norm1'], wcat, wfl, wflt, p['b_gate'], p['b_f'],
                    tiles_per_mod=1, tiles_per_seq=1)
    q_p, k_p, v_p, xb_p, gb_p, ga_p, gg_p, lf_p, _, fc_p = in_p
    q_s, k_s, v_s, xb_s, gb_s, ga_s, gg_s, lf_s, lft_s, _ = in_s
    r3 = lambda a, B, T: a.reshape(B, T, D_MODEL)

    fc4 = fc_p.reshape(Bp, Tp, N_PAIRS, 2).transpose(0, 2, 1, 3)
    oa_p = _attn_prompt(r3(q_p, Bp, Tp), r3(k_p, Bp, Tp), r3(v_p, Bp, Tp), fc4)

    lfp3 = jnp.transpose(cache_logf.astype(F32), (0, 2, 1))
    lfn3 = lft_s.reshape(N_HEADS, Bs, Ts).transpose(1, 0, 2)
    oa_s = _attn_sample(r3(q_s, Bs, Ts), r3(k_s, Bs, Ts), r3(v_s, Bs, Ts),
                        cache_k.reshape(Bs, past, D_MODEL), cache_v.reshape(Bs, past, D_MODEL), lfp3, lfn3)

    zc = jnp.zeros((Bp, SUBLANES, D_MODEL), F32)
    zh = jnp.zeros((Bp, 1, D_MODEL), F32)
    sc = jnp.pad(state_conv.astype(F32), ((0, 0), (SUBLANES - (CONV_WIDTH - 1), 0), (0, 0)))
    rnn_w = (p['conv_w'], p['conv_b'], wr4, wi4, p['b_r'], p['b_i'], p['rglru_lambda'])
    ob_p, conv_p, hl_p = _rnn(r3(xb_p, Bp, Tp), r3(gb_p, Bp, Tp), zc, zh, *rnn_w)
    ob_s, conv_s, hl_s = _rnn(r3(xb_s, Bs, Ts), r3(gb_s, Bs, Ts), sc, state_h.astype(F32).reshape(Bs, 1, D_MODEL),
                              *rnn_w)

    post_w = (p['norm2'], wpa, wpb, wo, wr1, wr2, br)
    x1_p, h2_p, rt_p = _post(xp.reshape(np_rows, D_MODEL), oa_p.reshape(np_rows, D_MODEL),
                             ob_p.reshape(np_rows, D_MODEL), ga_p, gg_p, mod_p, *post_w, tiles_per_mod=tiles_p)
    x1_s, h2_s, rt_s = _post(xs.reshape(ns_rows, D_MODEL), oa_s.reshape(ns_rows, D_MODEL),
                             ob_s.reshape(ns_rows, D_MODEL), ga_s, gg_s, mod_s, *post_w, tiles_per_mod=1)

    n_tok = np_rows + ns_rows
    rt = jnp.concatenate([rt_p, rt_s], axis=0)
    eid = rt[:, :TOP_K].astype(jnp.int32)
    wts = rt[:, TOP_K:2 * TOP_K]
    rows, roww, block_e, n_used, dest = _routing_plan(eid, wts, n_tok)
    h2_all = jnp.concatenate([h2_p, h2_s, jnp.zeros((1, D_MODEL), BF16)], axis=0)
    yb = _moe(jnp.take(h2_all, rows, axis=0), roww, block_e, n_used, w1, w3, w2)
    ya0 = jnp.take(yb, dest[:, 0], axis=0)
    ya1 = jnp.take(yb, dest[:, 1], axis=0)

    y_p = _final(x1_p, ya0[:np_rows], ya1[:np_rows], mod_p, norm_f, tiles_per_mod=tiles_p)
    y_s = _final(x1_s, ya0[np_rows:], ya1[np_rows:], mod_s, norm_f, tiles_per_mod=1)

    st = CONV_WIDTH - 1
    state_p = (k_p.reshape(Bp, Tp, N_HEADS, HEAD_DIM), v_p.reshape(Bp, Tp, N_HEADS, HEAD_DIM),
               lf_p.reshape(Bp, Tp, N_HEADS), conv_p[:, SUBLANES - st:], hl_p.reshape(Bp, D_MODEL))
    state_s = (k_s.reshape(Bs, Ts, N_HEADS, HEAD_DIM), v_s.reshape(Bs, Ts, N_HEADS, HEAD_DIM),
               lf_s.reshape(Bs, Ts, N_HEADS), conv_s[:, SUBLANES - st:], hl_s.reshape(Bs, D_MODEL))
    return y_p.reshape(Bp, Tp, D_MODEL), y_s.reshape(Bs, Ts, D_MODEL), state_p, state_s


def kernel(x_prompt, x_sample, cache_k, cache_v, cache_logf, state_conv, state_h, c_prompt, c_sample, w_mod, b_mod, norm1, w_in, b_f, conv_w, conv_b, w_r, b_r, w_i, b_i, rglru_lambda, w_gate, b_gate, w_pa, w_pb, w_out, norm2, w_rg, b_rg, w_re, b_re, w1, w3, w2, norm_f):
    assert w_mod.shape[0] == 1, "single-layer trunk: the final norm is fused into the layer's last stage"
    names = ('w_mod', 'b_mod', 'norm1', 'w_in', 'b_f', 'conv_w', 'conv_b', 'w_r', 'b_r', 'w_i', 'b_i',
             'rglru_lambda', 'w_gate', 'b_gate', 'w_pa', 'w_pb', 'w_out', 'norm2', 'w_rg', 'b_rg', 'w_re', 'b_re',
             'w1', 'w3', 'w2')
    vals = (w_mod, b_mod, norm1, w_in, b_f, conv_w, conv_b, w_r, b_r, w_i, b_i, rglru_lambda, w_gate, b_gate,
            w_pa, w_pb, w_out, norm2, w_rg, b_rg, w_re, b_re, w1, w3, w2)
    p = {n: v[0] for n, v in zip(names, vals)}
    c_all = jnp.concatenate([c_prompt, c_sample], axis=0)
    y_p, y_s, sp, ss = _layer(x_prompt, x_sample, c_all, cache_k[0], cache_v[0], cache_logf[0],
                              state_conv[0], state_h[0], p, norm_f)
    lead = lambda a: a[None]
    return (y_p, y_s, lead(sp[0]), lead(sp[1]), lead(sp[2]), lead(sp[3]), lead(sp[4]),
            lead(ss[0]), lead(ss[1]), lead(ss[2]), lead(ss[3]), lead(ss[4]))
```

```python
import functools

import jax
import jax.numpy as jnp
from jax import lax
from jax.experimental import pallas as pl
from jax.experimental.pallas import tpu as pltpu

F32 = jnp.float32
BF16 = jnp.bfloat16

D_MODEL = 1024
N_HEADS = 16
HEAD_DIM = 64
HEAD_PAIR = 2 * HEAD_DIM
N_PAIRS = N_HEADS // 2
N_RNN_BLOCKS = 16
CONV_WIDTH = 4
RGLRU_C = 8.0
N_GROUPS = 4
EXPERTS_PER_GROUP = 8
N_EXPERTS = N_GROUPS * EXPERTS_PER_GROUP
TOP_K = 2
D_EXPERT = 512
EPS = 1e-6

LANES = 128
SUBLANES = 8
ROW_TILE = 512
IN_TILE = 256
ATTN_BLOCK = 512
ATTN_VROWS = HEAD_DIM + 16
RNN_TILE = 256
MOE_BLOCK = 256
GATE_TILE = 256
VMEM_LIMIT = 56 * 1024 * 1024
NEG_BIG = -1e30
LOG2E = 1.4426950408889634


def _cparams(sem):
    return pltpu.CompilerParams(dimension_semantics=sem, vmem_limit_bytes=VMEM_LIMIT)


def _resident(shape):
    nd = len(shape)
    return pl.BlockSpec(shape, lambda *_: (0,) * nd, pipeline_mode=pl.Buffered(1))


def _split3(x):
    p1 = x.astype(BF16)
    r1 = x - p1.astype(F32)
    p2 = r1.astype(BF16)
    p3 = (r1 - p2.astype(F32)).astype(BF16)
    return p1, p2, p3


def _dot(a, b):
    return jnp.dot(a, b, preferred_element_type=F32)


def _dot_nt(a, b):
    return lax.dot_general(a, b, (((1,), (1,)), ((), ())), preferred_element_type=F32)


def _sigmoid(x):
    return 1.0 / (1.0 + jnp.exp(-x))


def _gelu_tanh(x):
    return 0.5 * x * (1.0 + jnp.tanh(0.7978845608028654 * (x + 0.044715 * x * x * x)))


def _rms(x, g):
    ms = jnp.mean(x * x, axis=-1, keepdims=True)
    return x * lax.rsqrt(ms + EPS) * g


def _div_p2(x, n):
    assert n & (n - 1) == 0
    return lax.shift_right_logical(x, jnp.int32(n.bit_length() - 1))


def _mod_p2(x, n):
    assert n & (n - 1) == 0
    return x & (n - 1)


def _upper_tri(n):
    r = lax.broadcasted_iota(jnp.int32, (n, n), 0)
    c = lax.broadcasted_iota(jnp.int32, (n, n), 1)
    return jnp.where(r <= c, 1.0, 0.0).astype(BF16)


def _cumsum_lanes(x, u):
    p1, p2, p3 = _split3(x)
    return _dot(p1, u) + _dot(p2, u) + _dot(p3, u)


def _mod_kernel(c_ref, w_ref, b_ref, o_ref):
    c = c_ref[...]
    s = c * _sigmoid(c)
    s1 = s.astype(BF16)
    s2 = (s - s1.astype(F32)).astype(BF16)
    w = w_ref[...]
    w1 = w.astype(BF16)
    w2 = (w - w1.astype(F32)).astype(BF16)
    o_ref[...] = _dot(s1, w1) + _dot(s1, w2) + _dot(s2, w1) + b_ref[...]


def _mod(c, w, b):
    n = c.shape[0]
    cols = w.shape[1]
    return pl.pallas_call(
        _mod_kernel,
        grid=(cols // D_MODEL,),
        in_specs=[pl.BlockSpec((n, D_MODEL), lambda j: (0, 0)),
                  pl.BlockSpec((D_MODEL, D_MODEL), lambda j: (0, j)),
                  pl.BlockSpec((1, D_MODEL), lambda j: (0, j))],
        out_specs=pl.BlockSpec((n, D_MODEL), lambda j: (0, j)),
        out_shape=jax.ShapeDtypeStruct((n, cols), F32),
        compiler_params=_cparams(("arbitrary",)),
        name="mod",
    )(c, w, b.reshape(1, cols))


def _in_kernel(x_ref, sh_ref, sc_ref, n1_ref, w_ref, wfl_ref, wflt_ref, bg_ref, bf_ref, bft_ref,
               q_ref, kb_ref, vb_ref, k5_ref, v5_ref, xb_ref, gb_ref, ga_ref, gg_ref, lf_ref, lft_ref, fc_ref,
               carry_ref, *, tiles_per_seq):
    x = x_ref[...]
    h = (_rms(x, n1_ref[...]) * (1.0 + sc_ref[...]) + sh_ref[...]).astype(BF16)

    def mm(c):
        return _dot(h, w_ref[:, c * D_MODEL:(c + 1) * D_MODEL])

    def head_major(y):
        slabs = []
        for p in range(N_PAIRS):
            pair = y[:, p * HEAD_PAIR:(p + 1) * HEAD_PAIR]
            slabs += [pair, pltpu.roll(pair, HEAD_DIM, 1)]
        return jnp.swapaxes(jnp.stack(slabs, axis=0), 0, 1)[:, :, :HEAD_DIM]

    q_ref[...] = mm(0).astype(BF16)
    k = mm(1)
    kb_ref[...] = k.astype(BF16)
    k5_ref[...] = head_major(k)
    v = mm(2)
    vb_ref[...] = v.astype(BF16)
    v5_ref[...] = head_major(v)
    xb_ref[...] = mm(3).astype(BF16)
    gb_ref[...] = mm(4).astype(BF16)
    ga_ref[...] = _sigmoid(mm(5) + bg_ref[:, :D_MODEL]).astype(BF16)
    gg_ref[...] = _sigmoid(mm(6) + bg_ref[:, D_MODEL:]).astype(BF16)

    def log_sigmoid(z):
        return jnp.minimum(z, 0.0) - jnp.log(1.0 + jnp.exp(-jnp.abs(z)))

    lf = log_sigmoid(_dot(h, wfl_ref[...]) + bf_ref[...])
    lf_ref[...] = lf[:, :N_HEADS]
    lft_ref[...] = log_sigmoid(_dot_nt(wflt_ref[...], h)[:N_HEADS, :] + bft_ref[...])

    @pl.when(pl.program_id(0) % tiles_per_seq == 0)
    def _():
        carry_ref[...] = jnp.zeros_like(carry_ref)

    tm = x.shape[0]
    r = lax.broadcasted_iota(jnp.int32, (tm, tm), 0)
    c = lax.broadcasted_iota(jnp.int32, (tm, tm), 1)
    lower = jnp.where(c <= r, 1.0, 0.0).astype(BF16)
    p1, p2, p3 = _split3(lf)
    fc = _dot(lower, p1) + _dot(lower, p2) + _dot(lower, p3) + carry_ref[...]
    fc_ref[...] = fc[:, :N_HEADS]
    carry_ref[...] = fc[tm - 1:tm, :]


def _in_proj(x2, mod3, norm1, wcat, wfl, wflt, b_gate, b_f, *, tiles_per_mod, tiles_per_seq):
    rows = x2.shape[0]
    tm = IN_TILE
    nt = rows // tm
    row_blk = pl.BlockSpec((tm, D_MODEL), lambda i: (i, 0))
    head_blk = pl.BlockSpec((tm, N_HEADS, HEAD_DIM), lambda i: (i, 0, 0))
    if mod3.shape[1] == 1:
        modc = lambda c: pl.BlockSpec((None, 1, D_MODEL), lambda i: (i // tiles_per_mod, 0, c))
    else:
        modc = lambda c: pl.BlockSpec((None, tm, D_MODEL), lambda i: (0, i, c))
    outs = [jax.ShapeDtypeStruct((rows, D_MODEL), BF16),
            jax.ShapeDtypeStruct((rows, D_MODEL), BF16),
            jax.ShapeDtypeStruct((rows, D_MODEL), BF16),
            jax.ShapeDtypeStruct((rows, N_HEADS, HEAD_DIM), F32),
            jax.ShapeDtypeStruct((rows, N_HEADS, HEAD_DIM), F32),
            jax.ShapeDtypeStruct((rows, D_MODEL), BF16),
            jax.ShapeDtypeStruct((rows, D_MODEL), BF16),
            jax.ShapeDtypeStruct((rows, D_MODEL), BF16),
            jax.ShapeDtypeStruct((rows, D_MODEL), BF16),
            jax.ShapeDtypeStruct((rows, N_HEADS), F32),
            jax.ShapeDtypeStruct((N_HEADS, rows), F32),
            jax.ShapeDtypeStruct((rows, N_HEADS), F32)]
    return pl.pallas_call(
        functools.partial(_in_kernel, tiles_per_seq=tiles_per_seq),
        grid=(nt,),
        in_specs=[row_blk, modc(0), modc(1),
                  _resident((1, D_MODEL)),
                  _resident(wcat.shape),
                  _resident(wfl.shape),
                  _resident(wflt.shape),
                  _resident((1, 2 * D_MODEL)),
                  _resident((1, LANES)),
                  _resident((N_HEADS, 1))],
        out_specs=[row_blk] * 3 + [head_blk] * 2 + [row_blk] * 4
                  + [pl.BlockSpec((tm, N_HEADS), lambda i: (i, 0)),
                     pl.BlockSpec((N_HEADS, tm), lambda i: (0, i)),
                     pl.BlockSpec((tm, N_HEADS), lambda i: (i, 0))],
        out_shape=outs,
        scratch_shapes=[pltpu.VMEM((1, LANES), F32)],
        compiler_params=_cparams(("arbitrary",)),
        name="in_proj",
    )(x2, mod3, mod3, norm1.reshape(1, D_MODEL), wcat, wfl, wflt,
      b_gate.reshape(1, 2 * D_MODEL), jnp.pad(b_f, (0, LANES - N_HEADS)).reshape(1, LANES),
      b_f.reshape(N_HEADS, 1))


def _attn_kernel(q_ref, k_ref, v_ref, f_ref, o_ref, kb_ref, vt_ref, qt_ref, s0_ref, m_ref, acc_ref):
    qi = pl.program_id(2)
    bq = q_ref.shape[0]
    bk = ATTN_BLOCK
    nk = k_ref.shape[0] // bk
    own_lanes = (lambda i: i < HEAD_DIM, lambda i: i >= HEAD_DIM)
    bias_at = (HEAD_DIM, 0)

    @pl.when(qi == 0)
    def _():
        lane = lax.broadcasted_iota(jnp.int32, (bk, HEAD_PAIR), 1)
        ones_rows = jnp.where(lax.broadcasted_iota(jnp.int32, (ATTN_VROWS - HEAD_DIM, bk), 0) == 0, 1.0, 0.0)
        for c in range(nk):
            blk = slice(c * bk, (c + 1) * bk)
            k = k_ref[blk, :].astype(F32)
            f = f_ref[blk, :] * LOG2E
            for hh in range(2):
                slab = jnp.zeros(k.shape, F32)
                for n, piece in enumerate(_split3(f[:, hh:hh + 1])):
                    slab = jnp.where(lane == bias_at[hh] + n, piece.astype(F32), slab)
                kb_ref[hh, blk, :] = jnp.where(own_lanes[hh](lane), k, slab).astype(BF16)
            vt = v_ref[blk, :].astype(F32).T
            for hh in range(2):
                vt_ref[c, hh, 0:HEAD_DIM] = vt[hh * HEAD_DIM:(hh + 1) * HEAD_DIM].astype(BF16)
                vt_ref[c, hh, HEAD_DIM:ATTN_VROWS] = ones_rows.astype(BF16)

    qt = q_ref[...].astype(F32).T
    feat = lax.broadcasted_iota(jnp.int32, qt.shape, 0)
    for hh in range(2):
        is_bias = (feat >= bias_at[hh]) & (feat < bias_at[hh] + 3)
        qt_ref[hh] = jnp.where(own_lanes[hh](feat), qt, jnp.where(is_bias, -1.0, 0.0)).astype(BF16)

    m_ref[...] = jnp.full(m_ref.shape, NEG_BIG, F32)
    acc_ref[...] = jnp.zeros(acc_ref.shape, F32)

    def scores(hh, j):
        return _dot(kb_ref[hh, pl.ds(pl.multiple_of(j * bk, bk), bk), :], qt_ref[hh])

    def softmax_pv(hh, j, s, masked):
        if masked:
            key = lax.broadcasted_iota(jnp.int32, s.shape, 0)
            qry = lax.broadcasted_iota(jnp.int32, s.shape, 1)
            s = jnp.where(key <= qry, s, NEG_BIG)
        m_prev = m_ref[hh]
        m_new = jnp.maximum(m_prev, jnp.max(s, axis=0, keepdims=True))
        p = jnp.exp2(s - m_new).astype(BF16)
        acc_ref[hh] = jnp.exp2(m_prev - m_new) * acc_ref[hh] + _dot(vt_ref[j, hh], p)
        m_ref[hh] = m_new

    def chunk(j, masked):
        s1 = scores(1, j)
        softmax_pv(0, j, s0_ref[...], masked)
        if not masked:
            s0_ref[...] = scores(0, j + 1)
        softmax_pv(1, j, s1, masked)

    def body(j, carry):
        chunk(j, False)
        return carry

    s0_ref[...] = scores(0, 0)
    lax.fori_loop(0, qi, body, 0)
    chunk(qi, True)

    heads = [acc_ref[hh, 0:HEAD_DIM] / acc_ref[hh, HEAD_DIM:HEAD_DIM + 1] for hh in range(2)]
    o_ref[...] = jnp.concatenate(heads, axis=0).T.astype(BF16)


def _attn_prompt(q3, k3, v3, fc4):
    B, T, _ = q3.shape
    bq = ATTN_BLOCK
    nq = T // bq
    return pl.pallas_call(
        _attn_kernel,
        grid=(B, N_PAIRS, nq),
        in_specs=[pl.BlockSpec((None, bq, HEAD_PAIR), lambda b, p, i: (b, i, p)),
                  pl.BlockSpec((None, T, HEAD_PAIR), lambda b, p, i: (b, 0, p)),
                  pl.BlockSpec((None, T, HEAD_PAIR), lambda b, p, i: (b, 0, p)),
                  pl.BlockSpec((None, None, T, 2), lambda b, p, i: (b, p, 0, 0))],
        out_specs=pl.BlockSpec((None, bq, HEAD_PAIR), lambda b, p, i: (b, i, p)),
        out_shape=jax.ShapeDtypeStruct((B, T, D_MODEL), BF16),
        scratch_shapes=[pltpu.VMEM((2, T, HEAD_PAIR), BF16),
                        pltpu.VMEM((nq, 2, ATTN_VROWS, bq), BF16),
                        pltpu.VMEM((2, HEAD_PAIR, bq), BF16),
                        pltpu.VMEM((ATTN_BLOCK, bq), F32),
                        pltpu.VMEM((2, 1, bq), F32),
                        pltpu.VMEM((2, ATTN_VROWS, bq), F32)],
        compiler_params=_cparams(("arbitrary", "arbitrary", "arbitrary")),
        name="attn_prompt",
    )(q3, k3, v3, fc4)


def _attn_sample_kernel(q_ref, kn_ref, vn_ref, kc_ref, vc_ref, lfp_ref, lfn_ref, o_ref):
    tq = q_ref.shape[0]
    past = kc_ref.shape[0]
    nrow = N_HEADS * tq
    row = lax.broadcasted_iota(jnp.int32, (nrow, D_MODEL), 0)
    col = lax.broadcasted_iota(jnp.int32, (nrow, D_MODEL), 1)
    own = _div_p2(row, tq) == _div_p2(col, HEAD_DIM)

    rr = lax.broadcasted_iota(jnp.int32, (nrow, tq), 0)
    rc = lax.broadcasted_iota(jnp.int32, (nrow, tq), 1)
    rep = jnp.where(_mod_p2(rr, tq) == rc, 1.0, 0.0).astype(BF16)
    qx = jnp.where(own, _dot(rep, q_ref[...]), 0.0).astype(BF16)

    fp = _cumsum_lanes(lfp_ref[...] * LOG2E, _upper_tri(past))
    fn = _cumsum_lanes(lfn_ref[...] * LOG2E, _upper_tri(tq)) + fp[:, past - 1:past]
    hr = lax.broadcasted_iota(jnp.int32, (nrow, N_HEADS), 0)
    hc = lax.broadcasted_iota(jnp.int32, (nrow, N_HEADS), 1)
    hsel = jnp.where(_div_p2(hr, tq) == hc, 1.0, 0.0).astype(BF16)

    def expand(f):
        p1, p2, p3 = _split3(f)
        return _dot(hsel, p1) + _dot(hsel, p2) + _dot(hsel, p3)

    s_p = _dot_nt(qx, kc_ref[...].astype(BF16)) - expand(fp)
    s_n = _dot_nt(qx, kn_ref[...]) - expand(fn)
    nr = lax.broadcasted_iota(jnp.int32, (nrow, tq), 0)
    nc = lax.broadcasted_iota(jnp.int32, (nrow, tq), 1)
    s_n = jnp.where(nc <= _mod_p2(nr, tq), s_n, NEG_BIG)
    m = jnp.maximum(jnp.max(s_p, axis=-1, keepdims=True), jnp.max(s_n, axis=-1, keepdims=True))
    p_p = jnp.exp2(s_p - m)
    p_n = jnp.exp2(s_n - m)
    l = jnp.sum(p_p, axis=-1, keepdims=True) + jnp.sum(p_n, axis=-1, keepdims=True)
    o_full = _dot(p_p.astype(BF16), vc_ref[...].astype(BF16)) + _dot(p_n.astype(BF16), vn_ref[...])
    o_own = jnp.where(own, o_full / l, 0.0).astype(BF16)
    cr = lax.broadcasted_iota(jnp.int32, (tq, nrow), 0)
    cc = lax.broadcasted_iota(jnp.int32, (tq, nrow), 1)
    col_sel = jnp.where(_mod_p2(cc, tq) == cr, 1.0, 0.0).astype(BF16)
    o_ref[...] = _dot(col_sel, o_own).astype(BF16)


def _attn_sample(q3, kn3, vn3, kc3, vc3, lfp3, lfn3):
    B, tq, _ = q3.shape
    past = kc3.shape[1]
    blk = lambda n, w: pl.BlockSpec((None, n, w), lambda b: (b, 0, 0))
    return pl.pallas_call(
        _attn_sample_kernel,
        grid=(B,),
        in_specs=[blk(tq, D_MODEL), blk(tq, D_MODEL), blk(tq, D_MODEL),
                  blk(past, D_MODEL), blk(past, D_MODEL), blk(N_HEADS, past), blk(N_HEADS, tq)],
        out_specs=blk(tq, D_MODEL),
        out_shape=jax.ShapeDtypeStruct((B, tq, D_MODEL), BF16),
        compiler_params=_cparams(("arbitrary",)),
        name="attn_sample",
    )(q3, kn3, vn3, kc3, vc3, lfp3, lfn3)


def _rnn_kernel(xb_ref, gb_ref, c0_ref, h0_ref, cw_ref, cb_ref, wr_ref, wi_ref, br_ref, bi_ref, lam_ref,
                o_ref, cout_ref, hout_ref, xp_ref, hc_ref):
    ti = pl.program_id(1)
    tt = xb_ref.shape[0]
    pad = SUBLANES

    @pl.when(ti == 0)
    def _():
        xp_ref[0:pad, :] = c0_ref[...]
        hc_ref[...] = h0_ref[...]

    xp_ref[pad:pad + tt, :] = xb_ref[...].astype(F32)
    xc = cb_ref[...] + xp_ref[pad:pad + tt, :] * cw_ref[CONV_WIDTH - 1:CONV_WIDTH, :]
    for k in range(CONV_WIDTH - 1):
        back = CONV_WIDTH - 1 - k
        xc = xc + xp_ref[pad - back:pad - back + tt, :] * cw_ref[k:k + 1, :]
    tail = xp_ref[tt:tt + pad, :]
    xp_ref[0:pad, :] = tail

    xcb = xc.astype(BF16)
    ng = D_MODEL // GATE_TILE
    r = jnp.concatenate([_dot(xcb[:, g * GATE_TILE:(g + 1) * GATE_TILE], wr_ref[g]) for g in range(ng)], axis=-1)
    i = jnp.concatenate([_dot(xcb[:, g * GATE_TILE:(g + 1) * GATE_TILE], wi_ref[g]) for g in range(ng)], axis=-1)
    r = _sigmoid(r + br_ref[...])
    i = _sigmoid(i + bi_ref[...])
    lam = lam_ref[...]
    softplus_neg = jnp.maximum(-lam, 0.0) + jnp.log(1.0 + jnp.exp(-jnp.abs(lam)))
    log_a = (-RGLRU_C) * r * softplus_neg
    a = jnp.exp(log_a)
    th = jnp.tanh(log_a)
    mult = jnp.sqrt(-2.0 * th / (1.0 - th))
    b = mult * (i * xc)

    ng8 = tt // SUBLANES
    a3 = a.reshape(ng8, SUBLANES, D_MODEL)
    b3 = b.reshape(ng8, SUBLANES, D_MODEL)
    sub = lax.broadcasted_iota(jnp.int32, a3.shape, 1)
    for s in (1, 2, 4):
        keep = sub >= s
        a_sh = jnp.where(keep, pltpu.roll(a3, s, 1), 1.0)
        b_sh = jnp.where(keep, pltpu.roll(b3, s, 1), 0.0)
        b3 = a3 * b_sh + b3
        a3 = a3 * a_sh
    hc = hc_ref[...]
    hs = []
    for g in range(ng8):
        hg = b3[g] + a3[g] * hc
        hs.append(hg)
        hc = hg[SUBLANES - 1:SUBLANES, :]
    hc_ref[...] = hc
    h = jnp.concatenate(hs, axis=0)

    o_ref[...] = (h * _gelu_tanh(gb_ref[...].astype(F32))).astype(BF16)

    @pl.when(ti == pl.num_programs(1) - 1)
    def _():
        cout_ref[...] = tail
        hout_ref[...] = hc


def _rnn(xb3, gb3, conv0, h0, conv_w, conv_b, wr4, wi4, b_r, b_i, lam):
    B, T, _ = xb3.shape
    tt = min(RNN_TILE, T)
    vec = lambda a: a.reshape(1, D_MODEL)
    seq = pl.BlockSpec((None, tt, D_MODEL), lambda b, t: (b, t, 0))
    st8 = pl.BlockSpec((None, SUBLANES, D_MODEL), lambda b, t: (b, 0, 0))
    st1 = pl.BlockSpec((None, 1, D_MODEL), lambda b, t: (b, 0, 0))
    return pl.pallas_call(
        _rnn_kernel,
        grid=(B, T // tt),
        in_specs=[seq, seq, st8, st1,
                  _resident((CONV_WIDTH, D_MODEL)), _resident((1, D_MODEL)),
                  _resident(wr4.shape), _resident(wi4.shape),
                  _resident((1, D_MODEL)), _resident((1, D_MODEL)), _resident((1, D_MODEL))],
        out_specs=[seq, st8, st1],
        out_shape=[jax.ShapeDtypeStruct((B, T, D_MODEL), BF16),
                   jax.ShapeDtypeStruct((B, SUBLANES, D_MODEL), F32),
                   jax.ShapeDtypeStruct((B, 1, D_MODEL), F32)],
        scratch_shapes=[pltpu.VMEM((tt + SUBLANES, D_MODEL), F32),
                        pltpu.VMEM((1, D_MODEL), F32)],
        compiler_params=_cparams(("arbitrary", "arbitrary")),
        name="rnn",
    )(xb3, gb3, conv0, h0, conv_w, vec(conv_b), wr4, wi4, vec(b_r), vec(b_i), vec(lam))


def _post_kernel(x_ref, oa_ref, ob_ref, ga_ref, gg_ref, g1_ref, sh_ref, sc_ref, n2_ref,
                 wpa_ref, wpb_ref, wo_ref, wr1_ref, wr2_ref, br_ref,
                 x1_ref, h2_ref, rt_ref):
    mix = (ga_ref[...].astype(F32) * _dot(oa_ref[...], wpa_ref[...])
           + gg_ref[...].astype(F32) * _dot(ob_ref[...], wpb_ref[...]))
    x1 = x_ref[...] + g1_ref[...] * _dot(mix.astype(BF16), wo_ref[...])
    x1_ref[...] = x1
    h2 = _rms(x1, n2_ref[...]) * (1.0 + sc_ref[...]) + sh_ref[...]
    h2a = h2.astype(BF16)
    h2_ref[...] = h2a
    h2b = (h2 - h2a.astype(F32)).astype(BF16)
    lg = _dot(h2a, wr1_ref[...]) + _dot(h2a, wr2_ref[...]) + _dot(h2b, wr1_ref[...]) + br_ref[...]

    lane = lax.broadcasted_iota(jnp.int32, lg.shape, 1).astype(F32)
    big = float(LANES)
    is_g = lane < N_GROUPS
    gl = jnp.where(is_g, lg, NEG_BIG)
    mg = jnp.max(gl, axis=-1, keepdims=True)
    sg = jnp.sum(jnp.where(is_g, jnp.exp(gl - mg), 0.0), axis=-1, keepdims=True)
    pg = 1.0 / sg
    gidx = jnp.min(jnp.where(is_g & (gl == mg), lane, big), axis=-1, keepdims=True)
    lo = N_GROUPS + EXPERTS_PER_GROUP * gidx
    ing = (lane >= lo) & (lane < lo + EXPERTS_PER_GROUP)
    el = jnp.where(ing, lg, NEG_BIG)
    m1 = jnp.max(el, axis=-1, keepdims=True)
    se = jnp.sum(jnp.where(ing, jnp.exp(el - m1), 0.0), axis=-1, keepdims=True)
    i1 = jnp.min(jnp.where(ing & (el == m1), lane, big), axis=-1, keepdims=True)
    el2 = jnp.where(lane == i1, NEG_BIG, el)
    m2 = jnp.max(el2, axis=-1, keepdims=True)
    i2 = jnp.min(jnp.where(ing & (el2 == m2) & (lane != i1), lane, big), axis=-1, keepdims=True)
    p1 = 1.0 / se
    p2 = jnp.exp(m2 - m1) / se
    den = p1 + p2
    w1 = pg * p1 / den
    w2 = pg * p2 / den
    out = jnp.where(lane == 0.0, i1 - N_GROUPS,
                    jnp.where(lane == 1.0, i2 - N_GROUPS,
                              jnp.where(lane == 2.0, w1, jnp.where(lane == 3.0, w2, 0.0))))
    rt_ref[...] = out[:, :SUBLANES]


def _post(x2, oa2, ob2, ga2, gg2, mod3, norm2, wpa, wpb, wo, wr1, wr2, br, *, tiles_per_mod):
    rows = x2.shape[0]
    tm = ROW_TILE
    mr = mod3.shape[1]
    row = pl.BlockSpec((tm, D_MODEL), lambda i: (i, 0))
    modc = lambda c: pl.BlockSpec((None, mr, D_MODEL), lambda i: (i // tiles_per_mod, 0, c))
    return pl.pallas_call(
        _post_kernel,
        grid=(rows // tm,),
        in_specs=[row, row, row, row, row, modc(2), modc(3), modc(4), _resident((1, D_MODEL)),
                  _resident(wpa.shape), _resident(wpb.shape), _resident(wo.shape),
                  _resident(wr1.shape), _resident(wr2.shape), _resident((1, LANES))],
        out_specs=[row, row, pl.BlockSpec((tm, SUBLANES), lambda i: (i, 0))],
        out_shape=[jax.ShapeDtypeStruct((rows, D_MODEL), F32),
                   jax.ShapeDtypeStruct((rows, D_MODEL), BF16),
                   jax.ShapeDtypeStruct((rows, SUBLANES), F32)],
        compiler_params=_cparams(("arbitrary",)),
        name="post",
    )(x2, oa2, ob2, ga2, gg2, mod3, mod3, mod3, norm2.reshape(1, D_MODEL), wpa, wpb, wo, wr1, wr2, br)


def _moe_kernel(be_ref, nb_ref, x_ref, w1_ref, w3_ref, w2_ref, o_ref, w1b_ref, w3b_ref, w2b_ref):
    i = pl.program_id(0)

    @pl.when((i == 0) | (be_ref[i] != be_ref[jnp.maximum(i - 1, 0)]))
    def _():
        w1b_ref[...] = w1_ref[...].astype(BF16)
        w3b_ref[...] = w3_ref[...].astype(BF16)
        w2b_ref[...] = w2_ref[...].astype(BF16)

    @pl.when(i < nb_ref[0])
    def _():
        x = x_ref[...]
        a = _dot(x, w1b_ref[...])
        g = a * _sigmoid(a) * _dot(x, w3b_ref[...])
        o_ref[...] = _dot(g.astype(BF16), w2b_ref[...]).astype(BF16)

    @pl.when(i >= nb_ref[0])
    def _():
        o_ref[...] = jnp.zeros_like(o_ref)


def _moe(xs, block_e, n_used, w1, w3, w2):
    n_blocks = block_e.shape[0]
    bm = MOE_BLOCK
    wspec = lambda s: pl.BlockSpec((None,) + s, lambda i, be, nb: (be[i], 0, 0))
    gs = pltpu.PrefetchScalarGridSpec(
        num_scalar_prefetch=2,
        grid=(n_blocks,),
        in_specs=[pl.BlockSpec((bm, D_MODEL), lambda i, be, nb: (i, 0)),
                  wspec((D_MODEL, D_EXPERT)), wspec((D_MODEL, D_EXPERT)), wspec((D_EXPERT, D_MODEL))],
        out_specs=pl.BlockSpec((bm, D_MODEL), lambda i, be, nb: (i, 0)),
        scratch_shapes=[pltpu.VMEM((D_MODEL, D_EXPERT), BF16),
                        pltpu.VMEM((D_MODEL, D_EXPERT), BF16),
                        pltpu.VMEM((D_EXPERT, D_MODEL), BF16)],
    )
    return pl.pallas_call(
        _moe_kernel,
        grid_spec=gs,
        out_shape=jax.ShapeDtypeStruct((n_blocks * bm, D_MODEL), BF16),
        compiler_params=_cparams(("arbitrary",)),
        name="moe",
    )(block_e, n_used, xs, w1, w3, w2)


def _final_kernel(x1_ref, ya_ref, yb_ref, rt_ref, g2_ref, nf_ref, o_ref):
    rt = rt_ref[...]
    moe = rt[:, TOP_K:TOP_K + 1] * ya_ref[...].astype(F32) + rt[:, TOP_K + 1:TOP_K + 2] * yb_ref[...].astype(F32)
    o_ref[...] = _rms(x1_ref[...] + g2_ref[...] * moe, nf_ref[...])


def _final(x1, ya, yb, rt, mod3, norm_f, *, tiles_per_mod, tile_offset):
    rows = x1.shape[0]
    tm = ROW_TILE
    mr = mod3.shape[1]
    row = pl.BlockSpec((tm, D_MODEL), lambda i: (i, 0))
    tok = lambda w: pl.BlockSpec((tm, w), lambda i: (i + tile_offset, 0))
    return pl.pallas_call(
        _final_kernel,
        grid=(rows // tm,),
        in_specs=[row, tok(D_MODEL), tok(D_MODEL), tok(SUBLANES),
                  pl.BlockSpec((None, mr, D_MODEL), lambda i: (i // tiles_per_mod, 0, 5)),
                  _resident((1, D_MODEL))],
        out_specs=row,
        out_shape=jax.ShapeDtypeStruct((rows, D_MODEL), F32),
        compiler_params=_cparams(("arbitrary",)),
        name="final",
    )(x1, ya, yb, rt, mod3, norm_f.reshape(1, D_MODEL))


def _routing_plan(eid, n_tokens):
    bm = MOE_BLOCK
    n_assign = n_tokens * TOP_K
    n_blocks = -(-n_assign // bm) + N_EXPERTS
    flat_e = eid.reshape(-1)
    onehot = (flat_e[:, None] == jnp.arange(N_EXPERTS, dtype=jnp.int32)[None, :]).astype(jnp.int32)
    csum = jnp.cumsum(onehot, axis=0)
    counts = csum[-1]
    padded = (counts + bm - 1) // bm * bm
    pad_end = jnp.cumsum(padded)
    pad_start = pad_end - padded
    dest = jnp.sum(onehot * (csum - 1 + pad_start[None, :]), axis=1).astype(jnp.int32)
    rows = jnp.full((n_blocks * bm,), n_tokens, jnp.int32).at[dest].set(
        jnp.arange(n_assign, dtype=jnp.int32) // TOP_K, unique_indices=True, mode='promise_in_bounds')
    block_start = jnp.arange(n_blocks, dtype=jnp.int32) * bm
    block_e = jnp.minimum(jnp.sum((pad_end[None, :] <= block_start[:, None]).astype(jnp.int32), axis=1),
                          N_EXPERTS - 1).astype(jnp.int32)
    n_used = (pad_end[-1] // bm).astype(jnp.int32).reshape(1)
    return rows, block_e, n_used, dest.reshape(n_tokens, TOP_K)


def _layer(xp, xs, c_all, cache_k, cache_v, cache_logf, state_conv, state_h, p, norm_f):
    Bp, Tp, _ = xp.shape
    Bs, Ts, _ = xs.shape
    past = cache_k.shape[1]
    np_rows, ns_rows = Bp * Tp, Bs * Ts
    assert Tp % ROW_TILE == 0 and ns_rows == ROW_TILE and Tp % ATTN_BLOCK == 0

    mod = _mod(c_all, p['w_mod'], p['b_mod'])
    mod_p = mod[:Bp].reshape(Bp, 1, 6 * D_MODEL)
    mod_s = jnp.repeat(mod[Bp:], Ts, axis=0).reshape(1, ns_rows, 6 * D_MODEL)

    w_in = p['w_in']
    d3 = 3 * D_MODEL
    wcat = jnp.concatenate([w_in[:, :D_MODEL] * (LOG2E * HEAD_DIM ** -0.5), w_in[:, D_MODEL:d3],
                            w_in[:, d3 + N_HEADS:], p['w_gate']], axis=1).astype(BF16)
    wfl = jnp.pad(w_in[:, d3:d3 + N_HEADS], ((0, 0), (0, LANES - N_HEADS))).astype(BF16)
    wflt = wfl.T
    eye = jnp.eye(GATE_TILE // (D_MODEL // N_RNN_BLOCKS), dtype=F32)

    def gate_tiles(w):
        per = GATE_TILE // w.shape[1]
        w4 = w.reshape(N_RNN_BLOCKS // per, per, w.shape[1], w.shape[2])
        return jnp.einsum('gpcd,pq->gpcqd', w4, eye).reshape(N_RNN_BLOCKS // per, GATE_TILE, GATE_TILE).astype(BF16)

    wr4, wi4 = gate_tiles(p['w_r']), gate_tiles(p['w_i'])
    wpa, wpb, wo = p['w_pa'].astype(BF16), p['w_pb'].astype(BF16), p['w_out'].astype(BF16)
    wr = jnp.pad(jnp.concatenate([p['w_rg'], p['w_re']], axis=1), ((0, 0), (0, LANES - N_GROUPS - N_EXPERTS)))
    wr1 = wr.astype(BF16)
    wr2 = (wr - wr1.astype(F32)).astype(BF16)
    br = jnp.pad(jnp.concatenate([p['b_rg'], p['b_re']]), (0, LANES - N_GROUPS - N_EXPERTS)).reshape(1, LANES)

    tiles_p = Tp // ROW_TILE
    in_p = _in_proj(xp.reshape(np_rows, D_MODEL), mod_p, p['norm1'], wcat, wfl, wflt, p['b_gate'], p['b_f'],
                    tiles_per_mod=Tp // IN_TILE, tiles_per_seq=Tp // IN_TILE)
    in_s = _in_proj(xs.reshape(ns_rows, D_MODEL), mod_s, p['norm1'], wcat, wfl, wflt, p['b_gate'], p['b_f'],
                    tiles_per_mod=1, tiles_per_seq=1)
    q_p, kb_p, vb_p, k_p, v_p, xb_p, gb_p, ga_p, gg_p, lf_p, _, fc_p = in_p
    q_s, kb_s, vb_s, k_s, v_s, xb_s, gb_s, ga_s, gg_s, lf_s, lft_s, _ = in_s
    r3 = lambda a, B, T: a.reshape(B, T, D_MODEL)

    fc4 = fc_p.reshape(Bp, Tp, N_PAIRS, 2).transpose(0, 2, 1, 3)
    oa_p = _attn_prompt(r3(q_p, Bp, Tp), r3(kb_p, Bp, Tp), r3(vb_p, Bp, Tp), fc4)

    lfp3 = jnp.transpose(cache_logf.astype(F32), (0, 2, 1))
    lfn3 = lft_s.reshape(N_HEADS, Bs, Ts).transpose(1, 0, 2)
    oa_s = _attn_sample(r3(q_s, Bs, Ts), r3(kb_s, Bs, Ts), r3(vb_s, Bs, Ts),
                        cache_k.reshape(Bs, past, D_MODEL), cache_v.reshape(Bs, past, D_MODEL), lfp3, lfn3)

    zc = jnp.zeros((Bp, SUBLANES, D_MODEL), F32)
    zh = jnp.zeros((Bp, 1, D_MODEL), F32)
    sc = jnp.pad(state_conv.astype(F32), ((0, 0), (SUBLANES - (CONV_WIDTH - 1), 0), (0, 0)))
    rnn_w = (p['conv_w'], p['conv_b'], wr4, wi4, p['b_r'], p['b_i'], p['rglru_lambda'])
    ob_p, conv_p, hl_p = _rnn(r3(xb_p, Bp, Tp), r3(gb_p, Bp, Tp), zc, zh, *rnn_w)
    ob_s, conv_s, hl_s = _rnn(r3(xb_s, Bs, Ts), r3(gb_s, Bs, Ts), sc, state_h.astype(F32).reshape(Bs, 1, D_MODEL),
                              *rnn_w)

    post_w = (p['norm2'], wpa, wpb, wo, wr1, wr2, br)
    x1_p, h2_p, rt_p = _post(xp.reshape(np_rows, D_MODEL), oa_p.reshape(np_rows, D_MODEL),
                             ob_p.reshape(np_rows, D_MODEL), ga_p, gg_p, mod_p, *post_w, tiles_per_mod=tiles_p)
    x1_s, h2_s, rt_s = _post(xs.reshape(ns_rows, D_MODEL), oa_s.reshape(ns_rows, D_MODEL),
                             ob_s.reshape(ns_rows, D_MODEL), ga_s, gg_s, mod_s, *post_w, tiles_per_mod=1)

    n_tok = np_rows + ns_rows
    rt = jnp.concatenate([rt_p, rt_s], axis=0)
    rows, block_e, n_used, dest = _routing_plan(rt[:, :TOP_K].astype(jnp.int32), n_tok)
    h2_all = jnp.concatenate([h2_p, h2_s, jnp.zeros((1, D_MODEL), BF16)], axis=0)
    take = lambda a, idx: a.at[idx].get(mode='promise_in_bounds')
    yb = _moe(take(h2_all, rows), block_e, n_used, p['w1'], p['w3'], p['w2'])
    ya0 = take(yb, dest[:, 0])
    ya1 = take(yb, dest[:, 1])

    y_p = _final(x1_p, ya0, ya1, rt, mod_p, norm_f, tiles_per_mod=tiles_p, tile_offset=0)
    y_s = _final(x1_s, ya0, ya1, rt, mod_s, norm_f, tiles_per_mod=1, tile_offset=np_rows // ROW_TILE)

    st = CONV_WIDTH - 1
    state_p = (k_p.reshape(Bp, Tp, N_HEADS, HEAD_DIM), v_p.reshape(Bp, Tp, N_HEADS, HEAD_DIM),
               lf_p.reshape(Bp, Tp, N_HEADS), conv_p[:, SUBLANES - st:], hl_p.reshape(Bp, D_MODEL))
    state_s = (k_s.reshape(Bs, Ts, N_HEADS, HEAD_DIM), v_s.reshape(Bs, Ts, N_HEADS, HEAD_DIM),
               lf_s.reshape(Bs, Ts, N_HEADS), conv_s[:, SUBLANES - st:], hl_s.reshape(Bs, D_MODEL))
    return y_p.reshape(Bp, Tp, D_MODEL), y_s.reshape(Bs, Ts, D_MODEL), state_p, state_s


def kernel(x_prompt, x_sample, cache_k, cache_v, cache_logf, state_conv, state_h, c_prompt, c_sample, w_mod, b_mod, norm1, w_in, b_f, conv_w, conv_b, w_r, b_r, w_i, b_i, rglru_lambda, w_gate, b_gate, w_pa, w_pb, w_out, norm2, w_rg, b_rg, w_re, b_re, w1, w3, w2, norm_f):
    assert w_mod.shape[0] == 1, "single-layer trunk: the final norm is fused into the layer's last stage"
    names = ('w_mod', 'b_mod', 'norm1', 'w_in', 'b_f', 'conv_w', 'conv_b', 'w_r', 'b_r', 'w_i', 'b_i',
             'rglru_lambda', 'w_gate', 'b_gate', 'w_pa', 'w_pb', 'w_out', 'norm2', 'w_rg', 'b_rg', 'w_re', 'b_re',
             'w1', 'w3', 'w2')
    vals = (w_mod, b_mod, norm1, w_in, b_f, conv_w, conv_b, w_r, b_r, w_i, b_i, rglru_lambda, w_gate, b_gate,
            w_pa, w_pb, w_out, norm2, w_rg, b_rg, w_re, b_re, w1, w3, w2)
    p = {n: v[0] for n, v in zip(names, vals)}
    c_all = jnp.concatenate([c_prompt, c_sample], axis=0)
    y_p, y_s, sp, ss = _layer(x_prompt, x_sample, c_all, cache_k[0], cache_v[0], cache_logf[0],
                              state_conv[0], state_h[0], p, norm_f)
    lead = lambda a: a[None]
    return (y_p, y_s, lead(sp[0]), lead(sp[1]), lead(sp[2]), lead(sp[3]), lead(sp[4]),
            lead(ss[0]), lead(ss[1]), lead(ss[2]), lead(ss[3]), lead(ss[4]))
```

```python
import functools

import jax
import jax.numpy as jnp
from jax import lax
from jax.experimental import pallas as pl
from jax.experimental.pallas import tpu as pltpu

F32 = jnp.float32
BF16 = jnp.bfloat16

D_MODEL = 1024
N_HEADS = 16
HEAD_DIM = 64
HEAD_PAIR = 2 * HEAD_DIM
N_PAIRS = N_HEADS // 2
N_RNN_BLOCKS = 16
CONV_WIDTH = 4
RGLRU_C = 8.0
N_GROUPS = 4
EXPERTS_PER_GROUP = 8
N_EXPERTS = N_GROUPS * EXPERTS_PER_GROUP
TOP_K = 2
D_EXPERT = 512
EPS = 1e-6

LANES = 128
SUBLANES = 8
ROW_TILE = 512
IN_TILE = 256
ATTN_BLOCK = 512
ATTN_VROWS = HEAD_DIM + 16
RNN_TILE = 256
MOE_BLOCK = 256
GATE_TILE = 256
VMEM_LIMIT = 56 * 1024 * 1024
NEG_BIG = -1e30
LOG2E = 1.4426950408889634


def _cparams(sem):
    return pltpu.CompilerParams(dimension_semantics=sem, vmem_limit_bytes=VMEM_LIMIT)


def _resident(shape):
    nd = len(shape)
    return pl.BlockSpec(shape, lambda *_: (0,) * nd, pipeline_mode=pl.Buffered(1))


def _split3(x):
    p1 = x.astype(BF16)
    r1 = x - p1.astype(F32)
    p2 = r1.astype(BF16)
    p3 = (r1 - p2.astype(F32)).astype(BF16)
    return p1, p2, p3


def _dot(a, b):
    return jnp.dot(a, b, preferred_element_type=F32)


def _dot_nt(a, b):
    return lax.dot_general(a, b, (((1,), (1,)), ((), ())), preferred_element_type=F32)


def _sigmoid(x):
    return 0.5 * jnp.tanh(0.5 * x) + 0.5


def _gelu_tanh(x):
    return 0.5 * x * (1.0 + jnp.tanh(0.7978845608028654 * (x + 0.044715 * x * x * x)))


def _rms(x, g):
    ms = jnp.mean(x * x, axis=-1, keepdims=True)
    return x * lax.rsqrt(ms + EPS) * g


def _pack_pairs(x):
    n = x.shape[1] // 2
    hi = pltpu.bitcast(x[:, :n].astype(BF16).astype(F32), jnp.uint32)
    lo = pltpu.bitcast(x[:, n:].astype(BF16).astype(F32), jnp.uint32)
    return hi | lax.shift_right_logical(lo, jnp.uint32(16))


def _unpack_pairs(w):
    hi = pltpu.bitcast(w & jnp.uint32(0xFFFF0000), F32)
    lo = pltpu.bitcast(lax.shift_left(w, jnp.uint32(16)), F32)
    return jnp.concatenate([hi, lo], axis=1).astype(BF16)


def _div_p2(x, n):
    assert n & (n - 1) == 0
    return lax.shift_right_logical(x, jnp.int32(n.bit_length() - 1))


def _mod_p2(x, n):
    assert n & (n - 1) == 0
    return x & (n - 1)


def _upper_tri(n):
    r = lax.broadcasted_iota(jnp.int32, (n, n), 0)
    c = lax.broadcasted_iota(jnp.int32, (n, n), 1)
    return jnp.where(r <= c, 1.0, 0.0).astype(BF16)


def _cumsum_lanes(x, u):
    p1, p2, p3 = _split3(x)
    return _dot(p1, u) + _dot(p2, u) + _dot(p3, u)


def _mod_kernel(c_ref, w_ref, b_ref, o_ref):
    c = c_ref[...]
    s = c * _sigmoid(c)
    s1 = s.astype(BF16)
    s2 = (s - s1.astype(F32)).astype(BF16)
    w = w_ref[...]
    w1 = w.astype(BF16)
    w2 = (w - w1.astype(F32)).astype(BF16)
    o_ref[...] = _dot(s1, w1) + _dot(s1, w2) + _dot(s2, w1) + b_ref[...]


def _mod(c, w, b):
    n = c.shape[0]
    cols = w.shape[1]
    return pl.pallas_call(
        _mod_kernel,
        grid=(cols // D_MODEL,),
        in_specs=[pl.BlockSpec((n, D_MODEL), lambda j: (0, 0)),
                  pl.BlockSpec((D_MODEL, D_MODEL), lambda j: (0, j)),
                  pl.BlockSpec((1, D_MODEL), lambda j: (0, j))],
        out_specs=pl.BlockSpec((n, D_MODEL), lambda j: (0, j)),
        out_shape=jax.ShapeDtypeStruct((n, cols), F32),
        compiler_params=_cparams(("arbitrary",)),
        name="mod",
    )(c, w, b.reshape(1, cols))


def _in_kernel(x_ref, sh_ref, sc_ref, n1_ref, w_ref, wfl_ref, wflt_ref, bg_ref, bf_ref, bft_ref,
               q_ref, kb_ref, vb_ref, k5_ref, v5_ref, xb_ref, gb_ref, ga_ref, gg_ref, lf_ref, lft_ref, fc_ref,
               carry_ref, *, tiles_per_seq):
    x = x_ref[...]
    h = (_rms(x, n1_ref[...]) * (1.0 + sc_ref[...]) + sh_ref[...]).astype(BF16)

    def mm(c):
        return _dot(h, w_ref[:, c * D_MODEL:(c + 1) * D_MODEL])

    def head_major(y):
        slabs = []
        for p in range(N_PAIRS):
            pair = y[:, p * HEAD_PAIR:(p + 1) * HEAD_PAIR]
            slabs += [pair, pltpu.roll(pair, HEAD_DIM, 1)]
        return jnp.swapaxes(jnp.stack(slabs, axis=0), 0, 1)[:, :, :HEAD_DIM]

    q_ref[...] = mm(0).astype(BF16)
    k = mm(1)
    kb_ref[...] = k.astype(BF16)
    k5_ref[...] = head_major(k)
    v = mm(2)
    vb_ref[...] = v.astype(BF16)
    v5_ref[...] = head_major(v)
    xb_ref[...] = mm(3).astype(BF16)
    gb_ref[...] = mm(4).astype(BF16)
    ga_ref[...] = _sigmoid(mm(5) + bg_ref[:, :D_MODEL]).astype(BF16)
    gg_ref[...] = _sigmoid(mm(6) + bg_ref[:, D_MODEL:]).astype(BF16)

    def log_sigmoid(z):
        return jnp.minimum(z, 0.0) - jnp.log(1.0 + jnp.exp(-jnp.abs(z)))

    lf = log_sigmoid(_dot(h, wfl_ref[...]) + bf_ref[...])
    lf_ref[...] = lf[:, :N_HEADS]
    lft_ref[...] = log_sigmoid(_dot_nt(wflt_ref[...], h)[:N_HEADS, :] + bft_ref[...])

    @pl.when(pl.program_id(0) % tiles_per_seq == 0)
    def _():
        carry_ref[...] = jnp.zeros_like(carry_ref)

    tm = x.shape[0]
    r = lax.broadcasted_iota(jnp.int32, (tm, tm), 0)
    c = lax.broadcasted_iota(jnp.int32, (tm, tm), 1)
    lower = jnp.where(c <= r, 1.0, 0.0).astype(BF16)
    p1, p2, p3 = _split3(lf)
    fc = _dot(lower, p1) + _dot(lower, p2) + _dot(lower, p3) + carry_ref[...]
    fc_ref[...] = fc[:, :N_HEADS]
    carry_ref[...] = fc[tm - 1:tm, :]


def _in_proj(x2, mod3, norm1, wcat, wfl, wflt, b_gate, b_f, *, tiles_per_mod, tiles_per_seq):
    rows = x2.shape[0]
    tm = IN_TILE
    nt = rows // tm
    row_blk = pl.BlockSpec((tm, D_MODEL), lambda i: (i, 0))
    head_blk = pl.BlockSpec((tm, N_HEADS, HEAD_DIM), lambda i: (i, 0, 0))
    if mod3.shape[1] == 1:
        modc = lambda c: pl.BlockSpec((None, 1, D_MODEL), lambda i: (i // tiles_per_mod, 0, c))
    else:
        modc = lambda c: pl.BlockSpec((None, tm, D_MODEL), lambda i: (0, i, c))
    outs = [jax.ShapeDtypeStruct((rows, D_MODEL), BF16),
            jax.ShapeDtypeStruct((rows, D_MODEL), BF16),
            jax.ShapeDtypeStruct((rows, D_MODEL), BF16),
            jax.ShapeDtypeStruct((rows, N_HEADS, HEAD_DIM), F32),
            jax.ShapeDtypeStruct((rows, N_HEADS, HEAD_DIM), F32),
            jax.ShapeDtypeStruct((rows, D_MODEL), BF16),
            jax.ShapeDtypeStruct((rows, D_MODEL), BF16),
            jax.ShapeDtypeStruct((rows, D_MODEL), BF16),
            jax.ShapeDtypeStruct((rows, D_MODEL), BF16),
            jax.ShapeDtypeStruct((rows, N_HEADS), F32),
            jax.ShapeDtypeStruct((N_HEADS, rows), F32),
            jax.ShapeDtypeStruct((rows, N_HEADS), F32)]
    return pl.pallas_call(
        functools.partial(_in_kernel, tiles_per_seq=tiles_per_seq),
        grid=(nt,),
        in_specs=[row_blk, modc(0), modc(1),
                  _resident((1, D_MODEL)),
                  _resident(wcat.shape),
                  _resident(wfl.shape),
                  _resident(wflt.shape),
                  _resident((1, 2 * D_MODEL)),
                  _resident((1, LANES)),
                  _resident((N_HEADS, 1))],
        out_specs=[row_blk] * 3 + [head_blk] * 2 + [row_blk] * 4
                  + [pl.BlockSpec((tm, N_HEADS), lambda i: (i, 0)),
                     pl.BlockSpec((N_HEADS, tm), lambda i: (0, i)),
                     pl.BlockSpec((tm, N_HEADS), lambda i: (i, 0))],
        out_shape=outs,
        scratch_shapes=[pltpu.VMEM((1, LANES), F32)],
        compiler_params=_cparams(("arbitrary",)),
        name="in_proj",
    )(x2, mod3, mod3, norm1.reshape(1, D_MODEL), wcat, wfl, wflt,
      b_gate.reshape(1, 2 * D_MODEL), jnp.pad(b_f, (0, LANES - N_HEADS)).reshape(1, LANES),
      b_f.reshape(N_HEADS, 1))


def _attn_kernel(q_ref, k_ref, v_ref, f_ref, o_ref, kb_ref, vt_ref, qt_ref, s0_ref, m_ref, acc_ref):
    qi = pl.program_id(2)
    bq = q_ref.shape[0]
    bk = ATTN_BLOCK
    nk = k_ref.shape[0] // bk
    own_lanes = (lambda i: i < HEAD_DIM, lambda i: i >= HEAD_DIM)
    bias_at = (HEAD_DIM, 0)

    @pl.when(qi == 0)
    def _():
        lane = lax.broadcasted_iota(jnp.int32, (bk, HEAD_PAIR), 1)
        ones_rows = jnp.where(lax.broadcasted_iota(jnp.int32, (ATTN_VROWS - HEAD_DIM, bk), 0) == 0, 1.0, 0.0)
        for c in range(nk):
            blk = slice(c * bk, (c + 1) * bk)
            k = k_ref[blk, :].astype(F32)
            f = f_ref[blk, :] * LOG2E
            for hh in range(2):
                slab = jnp.zeros(k.shape, F32)
                for n, piece in enumerate(_split3(f[:, hh:hh + 1])):
                    slab = jnp.where(lane == bias_at[hh] + n, piece.astype(F32), slab)
                kb_ref[hh, blk, :] = jnp.where(own_lanes[hh](lane), k, slab).astype(BF16)
            vt = v_ref[blk, :].astype(F32).T
            for hh in range(2):
                vt_ref[c, hh, 0:HEAD_DIM] = vt[hh * HEAD_DIM:(hh + 1) * HEAD_DIM].astype(BF16)
                vt_ref[c, hh, HEAD_DIM:ATTN_VROWS] = ones_rows.astype(BF16)

    qt = q_ref[...].astype(F32).T
    feat = lax.broadcasted_iota(jnp.int32, qt.shape, 0)
    for hh in range(2):
        is_bias = (feat >= bias_at[hh]) & (feat < bias_at[hh] + 3)
        qt_ref[hh] = jnp.where(own_lanes[hh](feat), qt, jnp.where(is_bias, -1.0, 0.0)).astype(BF16)

    m_ref[...] = jnp.full(m_ref.shape, NEG_BIG, F32)
    acc_ref[...] = jnp.zeros(acc_ref.shape, F32)

    def scores(hh, j):
        return _dot(kb_ref[hh, pl.ds(pl.multiple_of(j * bk, bk), bk), :], qt_ref[hh])

    def softmax_pv(hh, j, s, masked):
        if masked:
            key = lax.broadcasted_iota(jnp.int32, s.shape, 0)
            qry = lax.broadcasted_iota(jnp.int32, s.shape, 1)
            s = jnp.where(key <= qry, s, NEG_BIG)
        m_prev = m_ref[hh]
        m_new = jnp.maximum(m_prev, jnp.max(s, axis=0, keepdims=True))
        p = jnp.exp2(s - m_new).astype(BF16)
        acc_ref[hh] = jnp.exp2(m_prev - m_new) * acc_ref[hh] + _dot(vt_ref[j, hh], p)
        m_ref[hh] = m_new

    def chunk(j, masked):
        s1 = scores(1, j)
        softmax_pv(0, j, s0_ref[...], masked)
        if not masked:
            s0_ref[...] = scores(0, j + 1)
        softmax_pv(1, j, s1, masked)

    def body(j, carry):
        chunk(j, False)
        return carry

    s0_ref[...] = scores(0, 0)
    lax.fori_loop(0, qi, body, 0)
    chunk(qi, True)

    heads = [acc_ref[hh, 0:HEAD_DIM] / acc_ref[hh, HEAD_DIM:HEAD_DIM + 1] for hh in range(2)]
    o_ref[...] = jnp.concatenate(heads, axis=0).T.astype(BF16)


def _attn_prompt(q3, k3, v3, fc4):
    B, T, _ = q3.shape
    bq = ATTN_BLOCK
    nq = T // bq
    return pl.pallas_call(
        _attn_kernel,
        grid=(B, N_PAIRS, nq),
        in_specs=[pl.BlockSpec((None, bq, HEAD_PAIR), lambda b, p, i: (b, i, p)),
                  pl.BlockSpec((None, T, HEAD_PAIR), lambda b, p, i: (b, 0, p)),
                  pl.BlockSpec((None, T, HEAD_PAIR), lambda b, p, i: (b, 0, p)),
                  pl.BlockSpec((None, None, T, 2), lambda b, p, i: (b, p, 0, 0))],
        out_specs=pl.BlockSpec((None, bq, HEAD_PAIR), lambda b, p, i: (b, i, p)),
        out_shape=jax.ShapeDtypeStruct((B, T, D_MODEL), BF16),
        scratch_shapes=[pltpu.VMEM((2, T, HEAD_PAIR), BF16),
                        pltpu.VMEM((nq, 2, ATTN_VROWS, bq), BF16),
                        pltpu.VMEM((2, HEAD_PAIR, bq), BF16),
                        pltpu.VMEM((ATTN_BLOCK, bq), F32),
                        pltpu.VMEM((2, 1, bq), F32),
                        pltpu.VMEM((2, ATTN_VROWS, bq), F32)],
        compiler_params=_cparams(("arbitrary", "arbitrary", "arbitrary")),
        name="attn_prompt",
    )(q3, k3, v3, fc4)


def _attn_sample_kernel(q_ref, kn_ref, vn_ref, kc_ref, vc_ref, lfp_ref, lfn_ref, o_ref):
    tq = q_ref.shape[0]
    past = kc_ref.shape[0]
    nrow = N_HEADS * tq
    row = lax.broadcasted_iota(jnp.int32, (nrow, D_MODEL), 0)
    col = lax.broadcasted_iota(jnp.int32, (nrow, D_MODEL), 1)
    own = _div_p2(row, tq) == _div_p2(col, HEAD_DIM)

    rr = lax.broadcasted_iota(jnp.int32, (nrow, tq), 0)
    rc = lax.broadcasted_iota(jnp.int32, (nrow, tq), 1)
    rep = jnp.where(_mod_p2(rr, tq) == rc, 1.0, 0.0).astype(BF16)
    qx = jnp.where(own, _dot(rep, q_ref[...]), 0.0).astype(BF16)

    fp = _cumsum_lanes(lfp_ref[...] * LOG2E, _upper_tri(past))
    fn = _cumsum_lanes(lfn_ref[...] * LOG2E, _upper_tri(tq)) + fp[:, past - 1:past]
    hr = lax.broadcasted_iota(jnp.int32, (nrow, N_HEADS), 0)
    hc = lax.broadcasted_iota(jnp.int32, (nrow, N_HEADS), 1)
    hsel = jnp.where(_div_p2(hr, tq) == hc, 1.0, 0.0).astype(BF16)

    def expand(f):
        p1, p2, p3 = _split3(f)
        return _dot(hsel, p1) + _dot(hsel, p2) + _dot(hsel, p3)

    s_p = _dot_nt(qx, kc_ref[...].astype(BF16)) - expand(fp)
    s_n = _dot_nt(qx, kn_ref[...]) - expand(fn)
    nr = lax.broadcasted_iota(jnp.int32, (nrow, tq), 0)
    nc = lax.broadcasted_iota(jnp.int32, (nrow, tq), 1)
    s_n = jnp.where(nc <= _mod_p2(nr, tq), s_n, NEG_BIG)
    m = jnp.maximum(jnp.max(s_p, axis=-1, keepdims=True), jnp.max(s_n, axis=-1, keepdims=True))
    p_p = jnp.exp2(s_p - m)
    p_n = jnp.exp2(s_n - m)
    l = jnp.sum(p_p, axis=-1, keepdims=True) + jnp.sum(p_n, axis=-1, keepdims=True)
    o_full = _dot(p_p.astype(BF16), vc_ref[...].astype(BF16)) + _dot(p_n.astype(BF16), vn_ref[...])
    o_own = jnp.where(own, o_full / l, 0.0).astype(BF16)
    cr = lax.broadcasted_iota(jnp.int32, (tq, nrow), 0)
    cc = lax.broadcasted_iota(jnp.int32, (tq, nrow), 1)
    col_sel = jnp.where(_mod_p2(cc, tq) == cr, 1.0, 0.0).astype(BF16)
    o_ref[...] = _dot(col_sel, o_own).astype(BF16)


def _attn_sample(q3, kn3, vn3, kc3, vc3, lfp3, lfn3):
    B, tq, _ = q3.shape
    past = kc3.shape[1]
    blk = lambda n, w: pl.BlockSpec((None, n, w), lambda b: (b, 0, 0))
    return pl.pallas_call(
        _attn_sample_kernel,
        grid=(B,),
        in_specs=[blk(tq, D_MODEL), blk(tq, D_MODEL), blk(tq, D_MODEL),
                  blk(past, D_MODEL), blk(past, D_MODEL), blk(N_HEADS, past), blk(N_HEADS, tq)],
        out_specs=blk(tq, D_MODEL),
        out_shape=jax.ShapeDtypeStruct((B, tq, D_MODEL), BF16),
        compiler_params=_cparams(("arbitrary",)),
        name="attn_sample",
    )(q3, kn3, vn3, kc3, vc3, lfp3, lfn3)


def _rnn_kernel(xb_ref, gb_ref, c0_ref, h0_ref, cw_ref, cb_ref, wr_ref, wi_ref, br_ref, bi_ref, lam_ref,
                o_ref, cout_ref, hout_ref, xp_ref, hc_ref):
    ti = pl.program_id(1)
    tt = xb_ref.shape[0]
    pad = SUBLANES

    @pl.when(ti == 0)
    def _():
        xp_ref[0:pad, :] = c0_ref[...]
        hc_ref[...] = h0_ref[...]

    xp_ref[pad:pad + tt, :] = xb_ref[...].astype(F32)
    xc = cb_ref[...] + xp_ref[pad:pad + tt, :] * cw_ref[CONV_WIDTH - 1:CONV_WIDTH, :]
    for k in range(CONV_WIDTH - 1):
        back = CONV_WIDTH - 1 - k
        xc = xc + xp_ref[pad - back:pad - back + tt, :] * cw_ref[k:k + 1, :]
    tail = xp_ref[tt:tt + pad, :]
    xp_ref[0:pad, :] = tail

    xcb = xc.astype(BF16)
    ng = D_MODEL // GATE_TILE
    r = jnp.concatenate([_dot(xcb[:, g * GATE_TILE:(g + 1) * GATE_TILE], wr_ref[g]) for g in range(ng)], axis=-1)
    i = jnp.concatenate([_dot(xcb[:, g * GATE_TILE:(g + 1) * GATE_TILE], wi_ref[g]) for g in range(ng)], axis=-1)
    r = _sigmoid(r + br_ref[...])
    i = _sigmoid(i + bi_ref[...])
    lam = lam_ref[...]
    softplus_neg = jnp.maximum(-lam, 0.0) + jnp.log(1.0 + jnp.exp(-jnp.abs(lam)))
    log_a = (-RGLRU_C) * r * softplus_neg
    a = jnp.exp(log_a)
    th = jnp.tanh(log_a)
    mult = jnp.sqrt(-2.0 * th / (1.0 - th))
    b = mult * (i * xc)

    ng8 = tt // SUBLANES
    a3 = a.reshape(ng8, SUBLANES, D_MODEL)
    b3 = b.reshape(ng8, SUBLANES, D_MODEL)
    sub = lax.broadcasted_iota(jnp.int32, a3.shape, 1)
    for s in (1, 2, 4):
        keep = sub >= s
        a_sh = jnp.where(keep, pltpu.roll(a3, s, 1), 1.0)
        b_sh = jnp.where(keep, pltpu.roll(b3, s, 1), 0.0)
        b3 = a3 * b_sh + b3
        a3 = a3 * a_sh
    hc = hc_ref[...]
    hs = []
    for g in range(ng8):
        hg = b3[g] + a3[g] * hc
        hs.append(hg)
        hc = hg[SUBLANES - 1:SUBLANES, :]
    hc_ref[...] = hc
    h = jnp.concatenate(hs, axis=0)

    o_ref[...] = (h * _gelu_tanh(gb_ref[...].astype(F32))).astype(BF16)

    @pl.when(ti == pl.num_programs(1) - 1)
    def _():
        cout_ref[...] = tail
        hout_ref[...] = hc


def _rnn(xb3, gb3, conv0, h0, conv_w, conv_b, wr4, wi4, b_r, b_i, lam):
    B, T, _ = xb3.shape
    tt = min(RNN_TILE, T)
    vec = lambda a: a.reshape(1, D_MODEL)
    seq = pl.BlockSpec((None, tt, D_MODEL), lambda b, t: (b, t, 0))
    st8 = pl.BlockSpec((None, SUBLANES, D_MODEL), lambda b, t: (b, 0, 0))
    st1 = pl.BlockSpec((None, 1, D_MODEL), lambda b, t: (b, 0, 0))
    return pl.pallas_call(
        _rnn_kernel,
        grid=(B, T // tt),
        in_specs=[seq, seq, st8, st1,
                  _resident((CONV_WIDTH, D_MODEL)), _resident((1, D_MODEL)),
                  _resident(wr4.shape), _resident(wi4.shape),
                  _resident((1, D_MODEL)), _resident((1, D_MODEL)), _resident((1, D_MODEL))],
        out_specs=[seq, st8, st1],
        out_shape=[jax.ShapeDtypeStruct((B, T, D_MODEL), BF16),
                   jax.ShapeDtypeStruct((B, SUBLANES, D_MODEL), F32),
                   jax.ShapeDtypeStruct((B, 1, D_MODEL), F32)],
        scratch_shapes=[pltpu.VMEM((tt + SUBLANES, D_MODEL), F32),
                        pltpu.VMEM((1, D_MODEL), F32)],
        compiler_params=_cparams(("arbitrary", "arbitrary")),
        name="rnn",
    )(xb3, gb3, conv0, h0, conv_w, vec(conv_b), wr4, wi4, vec(b_r), vec(b_i), vec(lam))


def _post_kernel(x_ref, oa_ref, ob_ref, ga_ref, gg_ref, g1_ref, sh_ref, sc_ref, n2_ref,
                 wpa_ref, wpb_ref, wo_ref, wr1_ref, wr2_ref, br_ref,
                 x1_ref, h2_ref, rt_ref):
    mix = (ga_ref[...].astype(F32) * _dot(oa_ref[...], wpa_ref[...])
           + gg_ref[...].astype(F32) * _dot(ob_ref[...], wpb_ref[...]))
    x1 = x_ref[...] + g1_ref[...] * _dot(mix.astype(BF16), wo_ref[...])
    x1_ref[...] = x1
    h2 = _rms(x1, n2_ref[...]) * (1.0 + sc_ref[...]) + sh_ref[...]
    h2_ref[...] = _pack_pairs(h2)
    h2a = h2.astype(BF16)
    h2b = (h2 - h2a.astype(F32)).astype(BF16)
    lg = _dot(h2a, wr1_ref[...]) + _dot(h2a, wr2_ref[...]) + _dot(h2b, wr1_ref[...]) + br_ref[...]

    lane = lax.broadcasted_iota(jnp.int32, lg.shape, 1).astype(F32)
    big = float(LANES)
    is_g = lane < N_GROUPS
    gl = jnp.where(is_g, lg, NEG_BIG)
    mg = jnp.max(gl, axis=-1, keepdims=True)
    sg = jnp.sum(jnp.where(is_g, jnp.exp(gl - mg), 0.0), axis=-1, keepdims=True)
    pg = 1.0 / sg
    gidx = jnp.min(jnp.where(is_g & (gl == mg), lane, big), axis=-1, keepdims=True)
    lo = N_GROUPS + EXPERTS_PER_GROUP * gidx
    ing = (lane >= lo) & (lane < lo + EXPERTS_PER_GROUP)
    el = jnp.where(ing, lg, NEG_BIG)
    m1 = jnp.max(el, axis=-1, keepdims=True)
    se = jnp.sum(jnp.where(ing, jnp.exp(el - m1), 0.0), axis=-1, keepdims=True)
    i1 = jnp.min(jnp.where(ing & (el == m1), lane, big), axis=-1, keepdims=True)
    el2 = jnp.where(lane == i1, NEG_BIG, el)
    m2 = jnp.max(el2, axis=-1, keepdims=True)
    i2 = jnp.min(jnp.where(ing & (el2 == m2) & (lane != i1), lane, big), axis=-1, keepdims=True)
    p1 = 1.0 / se
    p2 = jnp.exp(m2 - m1) / se
    den = p1 + p2
    w1 = pg * p1 / den
    w2 = pg * p2 / den
    rt_ref[...] = jnp.where(lane == 0.0, i1 - N_GROUPS,
                            jnp.where(lane == 1.0, i2 - N_GROUPS,
                                      jnp.where(lane == 2.0, w1, jnp.where(lane == 3.0, w2, 0.0))))


def _post(x2, oa2, ob2, ga2, gg2, mod3, norm2, wpa, wpb, wo, wr1, wr2, br, *, tiles_per_mod):
    rows = x2.shape[0]
    tm = ROW_TILE
    mr = mod3.shape[1]
    row = pl.BlockSpec((tm, D_MODEL), lambda i: (i, 0))
    modc = lambda c: pl.BlockSpec((None, mr, D_MODEL), lambda i: (i // tiles_per_mod, 0, c))
    return pl.pallas_call(
        _post_kernel,
        grid=(rows // tm,),
        in_specs=[row, row, row, row, row, modc(2), modc(3), modc(4), _resident((1, D_MODEL)),
                  _resident(wpa.shape), _resident(wpb.shape), _resident(wo.shape),
                  _resident(wr1.shape), _resident(wr2.shape), _resident((1, LANES))],
        out_specs=[row, pl.BlockSpec((tm, D_MODEL // 2), lambda i: (i, 0)), pl.BlockSpec((tm, LANES), lambda i: (i, 0))],
        out_shape=[jax.ShapeDtypeStruct((rows, D_MODEL), F32),
                   jax.ShapeDtypeStruct((rows, D_MODEL // 2), jnp.uint32),
                   jax.ShapeDtypeStruct((rows, LANES), F32)],
        compiler_params=_cparams(("arbitrary",)),
        name="post",
    )(x2, oa2, ob2, ga2, gg2, mod3, mod3, mod3, norm2.reshape(1, D_MODEL), wpa, wpb, wo, wr1, wr2, br)


def _dispatch_kernel(dest_ref, x_ref, slots_in_ref, slots_ref, sem):
    del slots_in_ref
    tm = x_ref.shape[0]

    def row_copy(r, slot):
        return pltpu.make_async_copy(x_ref.at[pl.ds(r, 1)], slots_ref.at[pl.ds(slot, 1)], sem)

    def issue(r, carry):
        for k in range(TOP_K):
            row_copy(r, dest_ref[TOP_K * r + k]).start()
        return carry

    lax.fori_loop(0, tm, issue, 0, unroll=8)

    for _ in range(TOP_K):
        pltpu.make_async_copy(x_ref, slots_ref.at[pl.ds(0, tm)], sem).wait()


def _dispatch(x, dest, slots, *, tile_offset):
    rows, width = x.shape
    tm = ROW_TILE
    return pl.pallas_call(
        _dispatch_kernel,
        grid=(rows // tm,),
        in_specs=[pl.BlockSpec((tm * TOP_K,), lambda i: (i + tile_offset,), memory_space=pltpu.SMEM),
                  pl.BlockSpec((tm, width), lambda i: (i, 0)),
                  pl.BlockSpec(memory_space=pl.ANY)],
        out_specs=pl.BlockSpec(memory_space=pl.ANY),
        out_shape=jax.ShapeDtypeStruct(slots.shape, slots.dtype),
        scratch_shapes=[pltpu.SemaphoreType.DMA(())],
        input_output_aliases={2: 0},
        compiler_params=_cparams(("arbitrary",)),
        name="dispatch",
    )(dest, x, slots)


def _moe_kernel(be_ref, nb_ref, x_ref, w1_ref, w3_ref, w2_ref, o_ref, w1b_ref, w3b_ref, w2b_ref):
    i = pl.program_id(0)

    @pl.when((i == 0) | (be_ref[i] != be_ref[jnp.maximum(i - 1, 0)]))
    def _():
        w1b_ref[...] = w1_ref[...].astype(BF16)
        w3b_ref[...] = w3_ref[...].astype(BF16)
        w2b_ref[...] = w2_ref[...].astype(BF16)

    @pl.when(i < nb_ref[0])
    def _():
        x = _unpack_pairs(x_ref[...])
        a = _dot(x, w1b_ref[...])
        g = a * _sigmoid(a) * _dot(x, w3b_ref[...])
        o_ref[...] = _dot(g.astype(BF16), w2b_ref[...]).astype(BF16)

    @pl.when(i >= nb_ref[0])
    def _():
        o_ref[...] = jnp.zeros_like(o_ref)


def _moe(xs, block_e, n_used, w1, w3, w2):
    n_blocks = block_e.shape[0]
    bm = MOE_BLOCK
    wspec = lambda s: pl.BlockSpec((None,) + s, lambda i, be, nb: (be[i], 0, 0))
    gs = pltpu.PrefetchScalarGridSpec(
        num_scalar_prefetch=2,
        grid=(n_blocks,),
        in_specs=[pl.BlockSpec((bm, D_MODEL // 2), lambda i, be, nb: (i, 0)),
                  wspec((D_MODEL, D_EXPERT)), wspec((D_MODEL, D_EXPERT)), wspec((D_EXPERT, D_MODEL))],
        out_specs=pl.BlockSpec((bm, D_MODEL), lambda i, be, nb: (i, 0)),
        scratch_shapes=[pltpu.VMEM((D_MODEL, D_EXPERT), BF16),
                        pltpu.VMEM((D_MODEL, D_EXPERT), BF16),
                        pltpu.VMEM((D_EXPERT, D_MODEL), BF16)],
    )
    return pl.pallas_call(
        _moe_kernel,
        grid_spec=gs,
        out_shape=jax.ShapeDtypeStruct((n_blocks * bm, D_MODEL), BF16),
        compiler_params=_cparams(("arbitrary",)),
        name="moe",
    )(block_e, n_used, xs, w1, w3, w2)


def _final_kernel(x1_ref, ya_ref, yb_ref, rt_ref, g2_ref, nf_ref, o_ref):
    rt = rt_ref[...]
    moe = rt[:, TOP_K:TOP_K + 1] * ya_ref[...].astype(F32) + rt[:, TOP_K + 1:TOP_K + 2] * yb_ref[...].astype(F32)
    o_ref[...] = _rms(x1_ref[...] + g2_ref[...] * moe, nf_ref[...])


def _final(x1, ya, yb, rt, mod3, norm_f, *, tiles_per_mod, tile_offset):
    rows = x1.shape[0]
    tm = ROW_TILE
    mr = mod3.shape[1]
    row = pl.BlockSpec((tm, D_MODEL), lambda i: (i, 0))
    tok = lambda w: pl.BlockSpec((tm, w), lambda i: (i + tile_offset, 0))
    return pl.pallas_call(
        _final_kernel,
        grid=(rows // tm,),
        in_specs=[row, tok(D_MODEL), tok(D_MODEL), tok(LANES),
                  pl.BlockSpec((None, mr, D_MODEL), lambda i: (i // tiles_per_mod, 0, 5)),
                  _resident((1, D_MODEL))],
        out_specs=row,
        out_shape=jax.ShapeDtypeStruct((rows, D_MODEL), F32),
        compiler_params=_cparams(("arbitrary",)),
        name="final",
    )(x1, ya, yb, rt, mod3, norm_f.reshape(1, D_MODEL))


def _routing_plan(eid, n_tokens):
    bm = MOE_BLOCK
    n_assign = n_tokens * TOP_K
    n_blocks = -(-n_assign // bm) + N_EXPERTS
    flat_e = eid.reshape(-1)
    onehot = (flat_e[:, None] == jnp.arange(N_EXPERTS, dtype=jnp.int32)[None, :]).astype(jnp.int32)
    csum = jnp.cumsum(onehot, axis=0)
    counts = csum[-1]
    padded = (counts + bm - 1) // bm * bm
    pad_end = jnp.cumsum(padded)
    pad_start = pad_end - padded
    dest = jnp.sum(onehot * (csum - 1 + pad_start[None, :]), axis=1).astype(jnp.int32)
    block_start = jnp.arange(n_blocks, dtype=jnp.int32) * bm
    block_e = jnp.minimum(jnp.sum((pad_end[None, :] <= block_start[:, None]).astype(jnp.int32), axis=1),
                          N_EXPERTS - 1).astype(jnp.int32)
    n_used = (pad_end[-1] // bm).astype(jnp.int32).reshape(1)
    return block_e, n_used, dest.reshape(n_tokens, TOP_K)


def _layer(xp, xs, c_all, cache_k, cache_v, cache_logf, state_conv, state_h, p, norm_f):
    Bp, Tp, _ = xp.shape
    Bs, Ts, _ = xs.shape
    past = cache_k.shape[1]
    np_rows, ns_rows = Bp * Tp, Bs * Ts
    assert Tp % ROW_TILE == 0 and ns_rows == ROW_TILE and Tp % ATTN_BLOCK == 0

    mod = _mod(c_all, p['w_mod'], p['b_mod'])
    mod_p = mod[:Bp].reshape(Bp, 1, 6 * D_MODEL)
    mod_s = jnp.repeat(mod[Bp:], Ts, axis=0).reshape(1, ns_rows, 6 * D_MODEL)

    w_in = p['w_in']
    d3 = 3 * D_MODEL
    wcat = jnp.concatenate([w_in[:, :D_MODEL] * (LOG2E * HEAD_DIM ** -0.5), w_in[:, D_MODEL:d3],
                            w_in[:, d3 + N_HEADS:], p['w_gate']], axis=1).astype(BF16)
    wfl = jnp.pad(w_in[:, d3:d3 + N_HEADS], ((0, 0), (0, LANES - N_HEADS))).astype(BF16)
    wflt = wfl.T
    eye = jnp.eye(GATE_TILE // (D_MODEL // N_RNN_BLOCKS), dtype=F32)

    def gate_tiles(w):
        per = GATE_TILE // w.shape[1]
        w4 = w.reshape(N_RNN_BLOCKS // per, per, w.shape[1], w.shape[2])
        return jnp.einsum('gpcd,pq->gpcqd', w4, eye).reshape(N_RNN_BLOCKS // per, GATE_TILE, GATE_TILE).astype(BF16)

    wr4, wi4 = gate_tiles(p['w_r']), gate_tiles(p['w_i'])
    wpa, wpb, wo = p['w_pa'].astype(BF16), p['w_pb'].astype(BF16), p['w_out'].astype(BF16)
    wr = jnp.pad(jnp.concatenate([p['w_rg'], p['w_re']], axis=1), ((0, 0), (0, LANES - N_GROUPS - N_EXPERTS)))
    wr1 = wr.astype(BF16)
    wr2 = (wr - wr1.astype(F32)).astype(BF16)
    br = jnp.pad(jnp.concatenate([p['b_rg'], p['b_re']]), (0, LANES - N_GROUPS - N_EXPERTS)).reshape(1, LANES)

    tiles_p = Tp // ROW_TILE
    in_p = _in_proj(xp.reshape(np_rows, D_MODEL), mod_p, p['norm1'], wcat, wfl, wflt, p['b_gate'], p['b_f'],
                    tiles_per_mod=Tp // IN_TILE, tiles_per_seq=Tp // IN_TILE)
    in_s = _in_proj(xs.reshape(ns_rows, D_MODEL), mod_s, p['norm1'], wcat, wfl, wflt, p['b_gate'], p['b_f'],
                    tiles_per_mod=1, tiles_per_seq=1)
    q_p, kb_p, vb_p, k_p, v_p, xb_p, gb_p, ga_p, gg_p, lf_p, _, fc_p = in_p
    q_s, kb_s, vb_s, k_s, v_s, xb_s, gb_s, ga_s, gg_s, lf_s, lft_s, _ = in_s
    r3 = lambda a, B, T: a.reshape(B, T, D_MODEL)

    fc4 = fc_p.reshape(Bp, Tp, N_PAIRS, 2).transpose(0, 2, 1, 3)
    oa_p = _attn_prompt(r3(q_p, Bp, Tp), r3(kb_p, Bp, Tp), r3(vb_p, Bp, Tp), fc4)

    lfp3 = jnp.transpose(cache_logf.astype(F32), (0, 2, 1))
    lfn3 = lft_s.reshape(N_HEADS, Bs, Ts).transpose(1, 0, 2)
    oa_s = _attn_sample(r3(q_s, Bs, Ts), r3(kb_s, Bs, Ts), r3(vb_s, Bs, Ts),
                        cache_k.reshape(Bs, past, D_MODEL), cache_v.reshape(Bs, past, D_MODEL), lfp3, lfn3)

    zc = jnp.zeros((Bp, SUBLANES, D_MODEL), F32)
    zh = jnp.zeros((Bp, 1, D_MODEL), F32)
    sc = jnp.pad(state_conv.astype(F32), ((0, 0), (SUBLANES - (CONV_WIDTH - 1), 0), (0, 0)))
    rnn_w = (p['conv_w'], p['conv_b'], wr4, wi4, p['b_r'], p['b_i'], p['rglru_lambda'])
    ob_p, conv_p, hl_p = _rnn(r3(xb_p, Bp, Tp), r3(gb_p, Bp, Tp), zc, zh, *rnn_w)
    ob_s, conv_s, hl_s = _rnn(r3(xb_s, Bs, Ts), r3(gb_s, Bs, Ts), sc, state_h.astype(F32).reshape(Bs, 1, D_MODEL),
                              *rnn_w)

    post_w = (p['norm2'], wpa, wpb, wo, wr1, wr2, br)
    x1_p, h2_p, rt_p = _post(xp.reshape(np_rows, D_MODEL), oa_p.reshape(np_rows, D_MODEL),
                             ob_p.reshape(np_rows, D_MODEL), ga_p, gg_p, mod_p, *post_w, tiles_per_mod=tiles_p)
    x1_s, h2_s, rt_s = _post(xs.reshape(ns_rows, D_MODEL), oa_s.reshape(ns_rows, D_MODEL),
                             ob_s.reshape(ns_rows, D_MODEL), ga_s, gg_s, mod_s, *post_w, tiles_per_mod=1)

    n_tok = np_rows + ns_rows
    rt = jnp.concatenate([rt_p, rt_s], axis=0)
    block_e, n_used, dest = _routing_plan(rt[:, :TOP_K].astype(jnp.int32), n_tok)
    slots = jnp.zeros((block_e.shape[0] * MOE_BLOCK, D_MODEL // 2), jnp.uint32)
    slots = _dispatch(h2_p, dest.reshape(-1), slots, tile_offset=0)
    slots = _dispatch(h2_s, dest.reshape(-1), slots, tile_offset=np_rows // ROW_TILE)
    take = lambda a, idx: a.at[idx].get(mode='promise_in_bounds')
    yb = _moe(slots, block_e, n_used, p['w1'], p['w3'], p['w2'])
    ya0 = take(yb, dest[:, 0])
    ya1 = take(yb, dest[:, 1])

    y_p = _final(x1_p, ya0, ya1, rt, mod_p, norm_f, tiles_per_mod=tiles_p, tile_offset=0)
    y_s = _final(x1_s, ya0, ya1, rt, mod_s, norm_f, tiles_per_mod=1, tile_offset=np_rows // ROW_TILE)

    st = CONV_WIDTH - 1
    state_p = (k_p.reshape(Bp, Tp, N_HEADS, HEAD_DIM), v_p.reshape(Bp, Tp, N_HEADS, HEAD_DIM),
               lf_p.reshape(Bp, Tp, N_HEADS), conv_p[:, SUBLANES - st:], hl_p.reshape(Bp, D_MODEL))
    state_s = (k_s.reshape(Bs, Ts, N_HEADS, HEAD_DIM), v_s.reshape(Bs, Ts, N_HEADS, HEAD_DIM),
               lf_s.reshape(Bs, Ts, N_HEADS), conv_s[:, SUBLANES - st:], hl_s.reshape(Bs, D_MODEL))
    return y_p.reshape(Bp, Tp, D_MODEL), y_s.reshape(Bs, Ts, D_MODEL), state_p, state_s


def kernel(x_prompt, x_sample, cache_k, cache_v, cache_logf, state_conv, state_h, c_prompt, c_sample, w_mod, b_mod, norm1, w_in, b_f, conv_w, conv_b, w_r, b_r, w_i, b_i, rglru_lambda, w_gate, b_gate, w_pa, w_pb, w_out, norm2, w_rg, b_rg, w_re, b_re, w1, w3, w2, norm_f):
    assert w_mod.shape[0] == 1, "single-layer trunk: the final norm is fused into the layer's last stage"
    names = ('w_mod', 'b_mod', 'norm1', 'w_in', 'b_f', 'conv_w', 'conv_b', 'w_r', 'b_r', 'w_i', 'b_i',
             'rglru_lambda', 'w_gate', 'b_gate', 'w_pa', 'w_pb', 'w_out', 'norm2', 'w_rg', 'b_rg', 'w_re', 'b_re',
             'w1', 'w3', 'w2')
    vals = (w_mod, b_mod, norm1, w_in, b_f, conv_w, conv_b, w_r, b_r, w_i, b_i, rglru_lambda, w_gate, b_gate,
            w_pa, w_pb, w_out, norm2, w_rg, b_rg, w_re, b_re, w1, w3, w2)
    p = {n: v[0] for n, v in zip(names, vals)}
    c_all = jnp.concatenate([c_prompt, c_sample], axis=0)
    y_p, y_s, sp, ss = _layer(x_prompt, x_sample, c_all, cache_k[0], cache_v[0], cache_logf[0],
                              state_conv[0], state_h[0], p, norm_f)
    lead = lambda a: a[None]
    return (y_p, y_s, lead(sp[0]), lead(sp[1]), lead(sp[2]), lead(sp[3]), lead(sp[4]),
            lead(ss[0]), lead(ss[1]), lead(ss[2]), lead(ss[3]), lead(ss[4]))
```

```python
import functools

import jax
import jax.numpy as jnp
from jax import lax
from jax.experimental import pallas as pl
from jax.experimental.pallas import tpu as pltpu

F32 = jnp.float32
BF16 = jnp.bfloat16

D_MODEL = 1024
N_HEADS = 16
HEAD_DIM = 64
HEAD_PAIR = 2 * HEAD_DIM
N_PAIRS = N_HEADS // 2
N_RNN_BLOCKS = 16
CONV_WIDTH = 4
RGLRU_C = 8.0
N_GROUPS = 4
EXPERTS_PER_GROUP = 8
N_EXPERTS = N_GROUPS * EXPERTS_PER_GROUP
TOP_K = 2
D_EXPERT = 512
EPS = 1e-6

LANES = 128
SUBLANES = 8
ROW_TILE = 512
FINAL_TILE = 1024
IN_TILE = 256
ATTN_BLOCK = 512
ATTN_QBLOCK = 1024
ATTN_VROWS = HEAD_DIM + 16
RNN_TILE = 256
MOE_BLOCK = 256
GATE_TILE = 256
VMEM_LIMIT = 56 * 1024 * 1024
NEG_BIG = -1e30
LOG2E = 1.4426950408889634


def _cparams(sem):
    return pltpu.CompilerParams(dimension_semantics=sem, vmem_limit_bytes=VMEM_LIMIT)


def _resident(shape):
    nd = len(shape)
    return pl.BlockSpec(shape, lambda *_: (0,) * nd, pipeline_mode=pl.Buffered(1))


def _split3(x):
    p1 = x.astype(BF16)
    r1 = x - p1.astype(F32)
    p2 = r1.astype(BF16)
    p3 = (r1 - p2.astype(F32)).astype(BF16)
    return p1, p2, p3


def _dot(a, b):
    return jnp.dot(a, b, preferred_element_type=F32)


def _dot_nt(a, b):
    return lax.dot_general(a, b, (((1,), (1,)), ((), ())), preferred_element_type=F32)


def _sigmoid(x):
    return 0.5 * jnp.tanh(0.5 * x) + 0.5


def _gelu_tanh(x):
    return 0.5 * x * (1.0 + jnp.tanh(0.7978845608028654 * (x + 0.044715 * x * x * x)))


def _rms(x, g):
    ms = jnp.mean(x * x, axis=-1, keepdims=True)
    return x * lax.rsqrt(ms + EPS) * g


def _pack_pairs(x):
    n = x.shape[1] // 2
    hi = pltpu.bitcast(x[:, :n].astype(BF16).astype(F32), jnp.uint32)
    lo = pltpu.bitcast(x[:, n:].astype(BF16).astype(F32), jnp.uint32)
    return hi | lax.shift_right_logical(lo, jnp.uint32(16))


def _unpack_pairs(w):
    hi = pltpu.bitcast(w & jnp.uint32(0xFFFF0000), F32)
    lo = pltpu.bitcast(lax.shift_left(w, jnp.uint32(16)), F32)
    return jnp.concatenate([hi, lo], axis=1)


def _div_p2(x, n):
    assert n & (n - 1) == 0
    return lax.shift_right_logical(x, jnp.int32(n.bit_length() - 1))


def _mod_p2(x, n):
    assert n & (n - 1) == 0
    return x & (n - 1)


def _upper_tri(n):
    r = lax.broadcasted_iota(jnp.int32, (n, n), 0)
    c = lax.broadcasted_iota(jnp.int32, (n, n), 1)
    return jnp.where(r <= c, 1.0, 0.0).astype(BF16)


def _cumsum_lanes(x, u):
    p1, p2, p3 = _split3(x)
    return _dot(p1, u) + _dot(p2, u) + _dot(p3, u)


def _mod_kernel(c_ref, w_ref, b_ref, o_ref):
    c = c_ref[...]
    s = c * _sigmoid(c)
    s1 = s.astype(BF16)
    s2 = (s - s1.astype(F32)).astype(BF16)
    w = w_ref[...]
    w1 = w.astype(BF16)
    w2 = (w - w1.astype(F32)).astype(BF16)
    o_ref[...] = _dot(s1, w1) + _dot(s1, w2) + _dot(s2, w1) + b_ref[...]


def _mod(c, w, b):
    n = c.shape[0]
    cols = w.shape[1]
    return pl.pallas_call(
        _mod_kernel,
        grid=(cols // D_MODEL,),
        in_specs=[pl.BlockSpec((n, D_MODEL), lambda j: (0, 0)),
                  pl.BlockSpec((D_MODEL, D_MODEL), lambda j: (0, j)),
                  pl.BlockSpec((1, D_MODEL), lambda j: (0, j))],
        out_specs=pl.BlockSpec((n, D_MODEL), lambda j: (0, j)),
        out_shape=jax.ShapeDtypeStruct((n, cols), F32),
        compiler_params=_cparams(("arbitrary",)),
        name="mod",
    )(c, w, b.reshape(1, cols))


def _in_kernel(x_ref, sh_ref, sc_ref, n1_ref, w_ref, wfl_ref, wflt_ref, bg_ref, bf_ref, bft_ref,
               q_ref, kb_ref, vb_ref, k5_ref, v5_ref, xb_ref, gb_ref, ga_ref, gg_ref, lf_ref, lft_ref, fc_ref,
               carry_ref, *, tiles_per_seq):
    x = x_ref[...]
    h = (_rms(x, n1_ref[...]) * (1.0 + sc_ref[...]) + sh_ref[...]).astype(BF16)

    def mm(c):
        return _dot(h, w_ref[:, c * D_MODEL:(c + 1) * D_MODEL])

    def head_major(y):
        slabs = []
        for p in range(N_PAIRS):
            pair = y[:, p * HEAD_PAIR:(p + 1) * HEAD_PAIR]
            slabs += [pair, pltpu.roll(pair, HEAD_DIM, 1)]
        return jnp.swapaxes(jnp.stack(slabs, axis=0), 0, 1)[:, :, :HEAD_DIM]

    q_ref[...] = mm(0).astype(BF16)
    k = mm(1)
    kb_ref[...] = k.astype(BF16)
    k5_ref[...] = head_major(k)
    v = mm(2)
    vb_ref[...] = v.astype(BF16)
    v5_ref[...] = head_major(v)
    xb_ref[...] = mm(3).astype(BF16)
    gb_ref[...] = mm(4).astype(BF16)
    ga_ref[...] = _sigmoid(mm(5) + bg_ref[:, :D_MODEL]).astype(BF16)
    gg_ref[...] = _sigmoid(mm(6) + bg_ref[:, D_MODEL:]).astype(BF16)

    def log_sigmoid(z):
        return jnp.minimum(z, 0.0) - jnp.log(1.0 + jnp.exp(-jnp.abs(z)))

    lf = log_sigmoid(_dot(h, wfl_ref[...]) + bf_ref[...])
    lf_ref[...] = lf[:, :N_HEADS]
    lft_ref[...] = log_sigmoid(_dot_nt(wflt_ref[...], h)[:N_HEADS, :] + bft_ref[...])

    @pl.when(pl.program_id(0) % tiles_per_seq == 0)
    def _():
        carry_ref[...] = jnp.zeros_like(carry_ref)

    tm = x.shape[0]
    r = lax.broadcasted_iota(jnp.int32, (tm, tm), 0)
    c = lax.broadcasted_iota(jnp.int32, (tm, tm), 1)
    lower = jnp.where(c <= r, 1.0, 0.0).astype(BF16)
    p1, p2, p3 = _split3(lf)
    fc = _dot(lower, p1) + _dot(lower, p2) + _dot(lower, p3) + carry_ref[...]
    fc_ref[...] = fc[:, :N_HEADS]
    carry_ref[...] = fc[tm - 1:tm, :]


def _in_proj(x2, mod3, norm1, wcat, wfl, wflt, b_gate, b_f, *, tiles_per_mod, tiles_per_seq):
    rows = x2.shape[0]
    tm = IN_TILE
    nt = rows // tm
    row_blk = pl.BlockSpec((tm, D_MODEL), lambda i: (i, 0))
    head_blk = pl.BlockSpec((tm, N_HEADS, HEAD_DIM), lambda i: (i, 0, 0))
    if mod3.shape[1] == 1:
        modc = lambda c: pl.BlockSpec((None, 1, D_MODEL), lambda i: (i // tiles_per_mod, 0, c))
    else:
        modc = lambda c: pl.BlockSpec((None, tm, D_MODEL), lambda i: (0, i, c))
    outs = [jax.ShapeDtypeStruct((rows, D_MODEL), BF16),
            jax.ShapeDtypeStruct((rows, D_MODEL), BF16),
            jax.ShapeDtypeStruct((rows, D_MODEL), BF16),
            jax.ShapeDtypeStruct((rows, N_HEADS, HEAD_DIM), F32),
            jax.ShapeDtypeStruct((rows, N_HEADS, HEAD_DIM), F32),
            jax.ShapeDtypeStruct((rows, D_MODEL), BF16),
            jax.ShapeDtypeStruct((rows, D_MODEL), BF16),
            jax.ShapeDtypeStruct((rows, D_MODEL), BF16),
            jax.ShapeDtypeStruct((rows, D_MODEL), BF16),
            jax.ShapeDtypeStruct((rows, N_HEADS), F32),
            jax.ShapeDtypeStruct((N_HEADS, rows), F32),
            jax.ShapeDtypeStruct((rows, N_HEADS), F32)]
    return pl.pallas_call(
        functools.partial(_in_kernel, tiles_per_seq=tiles_per_seq),
        grid=(nt,),
        in_specs=[row_blk, modc(0), modc(1),
                  _resident((1, D_MODEL)),
                  _resident(wcat.shape),
                  _resident(wfl.shape),
                  _resident(wflt.shape),
                  _resident((1, 2 * D_MODEL)),
                  _resident((1, LANES)),
                  _resident((N_HEADS, 1))],
        out_specs=[row_blk] * 3 + [head_blk] * 2 + [row_blk] * 4
                  + [pl.BlockSpec((tm, N_HEADS), lambda i: (i, 0)),
                     pl.BlockSpec((N_HEADS, tm), lambda i: (0, i)),
                     pl.BlockSpec((tm, N_HEADS), lambda i: (i, 0))],
        out_shape=outs,
        scratch_shapes=[pltpu.VMEM((1, LANES), F32)],
        compiler_params=_cparams(("arbitrary",)),
        name="in_proj",
    )(x2, mod3, mod3, norm1.reshape(1, D_MODEL), wcat, wfl, wflt,
      b_gate.reshape(1, 2 * D_MODEL), jnp.pad(b_f, (0, LANES - N_HEADS)).reshape(1, LANES),
      b_f.reshape(N_HEADS, 1))


def _attn_kernel(q_ref, k_ref, v_ref, f_ref, o_ref, kb_ref, vt_ref, qt_ref, s0_ref, m_ref, acc_ref):
    qi = pl.program_id(2)
    bq = q_ref.shape[0]
    bk = ATTN_BLOCK
    nk = k_ref.shape[0] // bk
    own_lanes = (lambda i: i < HEAD_DIM, lambda i: i >= HEAD_DIM)
    bias_at = (HEAD_DIM, 0)

    @pl.when(qi == 0)
    def _():
        lane = lax.broadcasted_iota(jnp.int32, (bk, HEAD_PAIR), 1)
        ones_rows = jnp.where(lax.broadcasted_iota(jnp.int32, (ATTN_VROWS - HEAD_DIM, bk), 0) == 0, 1.0, 0.0)
        for c in range(nk):
            blk = slice(c * bk, (c + 1) * bk)
            k = k_ref[blk, :].astype(F32)
            f = f_ref[blk, :] * LOG2E
            head = lax.broadcasted_iota(jnp.int32, f.shape, 1)
            for hh in range(2):
                mine = head == 2 * pl.program_id(1) + hh
                f_head = jnp.sum(jnp.where(mine, f, 0.0), axis=1, keepdims=True)
                slab = jnp.zeros(k.shape, F32)
                for n, piece in enumerate(_split3(f_head)):
                    slab = jnp.where(lane == bias_at[hh] + n, piece.astype(F32), slab)
                kb_ref[hh, blk, :] = jnp.where(own_lanes[hh](lane), k, slab).astype(BF16)
            vt = v_ref[blk, :].astype(F32).T
            for hh in range(2):
                vt_ref[c, hh, 0:HEAD_DIM] = vt[hh * HEAD_DIM:(hh + 1) * HEAD_DIM].astype(BF16)
                vt_ref[c, hh, HEAD_DIM:ATTN_VROWS] = ones_rows.astype(BF16)

    qt = q_ref[...].astype(F32).T
    feat = lax.broadcasted_iota(jnp.int32, qt.shape, 0)
    for hh in range(2):
        is_bias = (feat >= bias_at[hh]) & (feat < bias_at[hh] + 3)
        qt_ref[hh] = jnp.where(own_lanes[hh](feat), qt, jnp.where(is_bias, -1.0, 0.0)).astype(BF16)

    m_ref[...] = jnp.full(m_ref.shape, NEG_BIG, F32)
    acc_ref[...] = jnp.zeros(acc_ref.shape, F32)

    nsub = bq // bk

    def scores(hh, u, j):
        keys = kb_ref[hh, pl.ds(pl.multiple_of(j * bk, bk), bk), :]
        return _dot(keys, qt_ref[hh, :, u * bk:(u + 1) * bk])

    def softmax_pv(hh, u, j, s, masked):
        cols = slice(u * bk, (u + 1) * bk)
        if masked:
            key = lax.broadcasted_iota(jnp.int32, s.shape, 0)
            qry = lax.broadcasted_iota(jnp.int32, s.shape, 1)
            s = jnp.where(key <= qry, s, NEG_BIG)
        m_prev = m_ref[hh, :, cols]
        m_new = jnp.maximum(m_prev, jnp.max(s, axis=0, keepdims=True))
        p = jnp.exp2(s - m_new).astype(BF16)
        acc_ref[hh, :, cols] = jnp.exp2(m_prev - m_new) * acc_ref[hh, :, cols] + _dot(vt_ref[j, hh], p)
        m_ref[hh, :, cols] = m_new

    def run(items, first_of_next):
        s = s0_ref[...]
        for n, (hh, u, j, masked) in enumerate(items):
            if n + 1 < len(items):
                nxt = scores(*items[n + 1][:3])
            elif first_of_next is not None:
                s0_ref[...] = scores(*first_of_next)
            softmax_pv(hh, u, j, s, masked)
            s = nxt if n + 1 < len(items) else None

    def body(j, carry):
        run([(hh, u, j, False) for u in range(nsub) for hh in range(2)], (0, 0, j + 1))
        return carry

    s0_ref[...] = scores(0, 0, 0)
    lax.fori_loop(0, nsub * qi, body, 0)
    for d in range(nsub):
        j = nsub * qi + d
        items = [(hh, u, j, u == d) for u in range(d, nsub) for hh in range(2)]
        run(items, (0, d + 1, j + 1) if d + 1 < nsub else None)

    heads = [acc_ref[hh, 0:HEAD_DIM] / acc_ref[hh, HEAD_DIM:HEAD_DIM + 1] for hh in range(2)]
    o_ref[...] = jnp.concatenate(heads, axis=0).T.astype(BF16)


def _attn_prompt(q3, k3, v3, fc3):
    B, T, _ = q3.shape
    bq, bk = ATTN_QBLOCK, ATTN_BLOCK
    nq, nk = T // bq, T // bk
    return pl.pallas_call(
        _attn_kernel,
        grid=(B, N_PAIRS, nq),
        in_specs=[pl.BlockSpec((None, bq, HEAD_PAIR), lambda b, p, i: (b, i, p)),
                  pl.BlockSpec((None, T, HEAD_PAIR), lambda b, p, i: (b, 0, p)),
                  pl.BlockSpec((None, T, HEAD_PAIR), lambda b, p, i: (b, 0, p)),
                  pl.BlockSpec((None, T, N_HEADS), lambda b, p, i: (b, 0, 0))],
        out_specs=pl.BlockSpec((None, bq, HEAD_PAIR), lambda b, p, i: (b, i, p)),
        out_shape=jax.ShapeDtypeStruct((B, T, D_MODEL), BF16),
        scratch_shapes=[pltpu.VMEM((2, T, HEAD_PAIR), BF16),
                        pltpu.VMEM((nk, 2, ATTN_VROWS, bk), BF16),
                        pltpu.VMEM((2, HEAD_PAIR, bq), BF16),
                        pltpu.VMEM((bk, bk), F32),
                        pltpu.VMEM((2, 1, bq), F32),
                        pltpu.VMEM((2, ATTN_VROWS, bq), F32)],
        compiler_params=_cparams(("arbitrary", "arbitrary", "arbitrary")),
        name="attn_prompt",
    )(q3, k3, v3, fc3)


def _attn_sample_kernel(q_ref, kn_ref, vn_ref, kc_ref, vc_ref, lfp_ref, lfn_ref, o_ref):
    tq = q_ref.shape[0]
    past = kc_ref.shape[0]
    nrow = N_HEADS * tq
    row = lax.broadcasted_iota(jnp.int32, (nrow, D_MODEL), 0)
    col = lax.broadcasted_iota(jnp.int32, (nrow, D_MODEL), 1)
    own = _div_p2(row, tq) == _div_p2(col, HEAD_DIM)

    rr = lax.broadcasted_iota(jnp.int32, (nrow, tq), 0)
    rc = lax.broadcasted_iota(jnp.int32, (nrow, tq), 1)
    rep = jnp.where(_mod_p2(rr, tq) == rc, 1.0, 0.0).astype(BF16)
    qx = jnp.where(own, _dot(rep, q_ref[...]), 0.0).astype(BF16)

    fp = _cumsum_lanes(lfp_ref[...] * LOG2E, _upper_tri(past))
    fn = _cumsum_lanes(lfn_ref[...] * LOG2E, _upper_tri(tq)) + fp[:, past - 1:past]
    hr = lax.broadcasted_iota(jnp.int32, (nrow, N_HEADS), 0)
    hc = lax.broadcasted_iota(jnp.int32, (nrow, N_HEADS), 1)
    hsel = jnp.where(_div_p2(hr, tq) == hc, 1.0, 0.0).astype(BF16)

    def expand(f):
        p1, p2, p3 = _split3(f)
        return _dot(hsel, p1) + _dot(hsel, p2) + _dot(hsel, p3)

    s_p = _dot_nt(qx, kc_ref[...]) - expand(fp)
    s_n = _dot_nt(qx, kn_ref[...]) - expand(fn)
    nr = lax.broadcasted_iota(jnp.int32, (nrow, tq), 0)
    nc = lax.broadcasted_iota(jnp.int32, (nrow, tq), 1)
    s_n = jnp.where(nc <= _mod_p2(nr, tq), s_n, NEG_BIG)
    m = jnp.maximum(jnp.max(s_p, axis=-1, keepdims=True), jnp.max(s_n, axis=-1, keepdims=True))
    p_p = jnp.exp2(s_p - m)
    p_n = jnp.exp2(s_n - m)
    l = jnp.sum(p_p, axis=-1, keepdims=True) + jnp.sum(p_n, axis=-1, keepdims=True)
    o_full = _dot(p_p.astype(BF16), vc_ref[...]) + _dot(p_n.astype(BF16), vn_ref[...])
    o_own = jnp.where(own, o_full / l, 0.0).astype(BF16)
    cr = lax.broadcasted_iota(jnp.int32, (tq, nrow), 0)
    cc = lax.broadcasted_iota(jnp.int32, (tq, nrow), 1)
    col_sel = jnp.where(_mod_p2(cc, tq) == cr, 1.0, 0.0).astype(BF16)
    o_ref[...] = _dot(col_sel, o_own).astype(BF16)


def _attn_sample(q3, kn3, vn3, kc3, vc3, lfp3, lfn3):
    B, tq, _ = q3.shape
    past = kc3.shape[1]
    blk = lambda n, w: pl.BlockSpec((None, n, w), lambda b: (b, 0, 0))
    return pl.pallas_call(
        _attn_sample_kernel,
        grid=(B,),
        in_specs=[blk(tq, D_MODEL), blk(tq, D_MODEL), blk(tq, D_MODEL),
                  blk(past, D_MODEL), blk(past, D_MODEL), blk(N_HEADS, past), blk(N_HEADS, tq)],
        out_specs=blk(tq, D_MODEL),
        out_shape=jax.ShapeDtypeStruct((B, tq, D_MODEL), BF16),
        compiler_params=_cparams(("arbitrary",)),
        name="attn_sample",
    )(q3, kn3, vn3, kc3, vc3, lfp3, lfn3)


def _rnn_kernel(xb_ref, gb_ref, c0_ref, h0_ref, cw_ref, cb_ref, wr_ref, wi_ref, br_ref, bi_ref, lam_ref,
                o_ref, cout_ref, hout_ref, xp_ref, hc_ref):
    ti = pl.program_id(1)
    tt = xb_ref.shape[0]
    pad = SUBLANES

    @pl.when(ti == 0)
    def _():
        xp_ref[0:pad, :] = c0_ref[...]
        hc_ref[...] = h0_ref[...]

    xp_ref[pad:pad + tt, :] = xb_ref[...].astype(F32)
    xc = cb_ref[...] + xp_ref[pad:pad + tt, :] * cw_ref[CONV_WIDTH - 1:CONV_WIDTH, :]
    for k in range(CONV_WIDTH - 1):
        back = CONV_WIDTH - 1 - k
        xc = xc + xp_ref[pad - back:pad - back + tt, :] * cw_ref[k:k + 1, :]
    tail = xp_ref[tt:tt + pad, :]
    xp_ref[0:pad, :] = tail

    xcb = xc.astype(BF16)
    ng = D_MODEL // GATE_TILE
    r = jnp.concatenate([_dot(xcb[:, g * GATE_TILE:(g + 1) * GATE_TILE], wr_ref[g]) for g in range(ng)], axis=-1)
    i = jnp.concatenate([_dot(xcb[:, g * GATE_TILE:(g + 1) * GATE_TILE], wi_ref[g]) for g in range(ng)], axis=-1)
    r = _sigmoid(r + br_ref[...])
    i = _sigmoid(i + bi_ref[...])
    lam = lam_ref[...]
    softplus_neg = jnp.maximum(-lam, 0.0) + jnp.log(1.0 + jnp.exp(-jnp.abs(lam)))
    log_a = (-RGLRU_C) * r * softplus_neg
    a = jnp.exp(log_a)
    th = jnp.tanh(log_a)
    mult = jnp.sqrt(-2.0 * th / (1.0 - th))
    b = mult * (i * xc)

    ng8 = tt // SUBLANES
    a3 = a.reshape(ng8, SUBLANES, D_MODEL)
    b3 = b.reshape(ng8, SUBLANES, D_MODEL)
    sub = lax.broadcasted_iota(jnp.int32, a3.shape, 1)
    for s in (1, 2, 4):
        keep = sub >= s
        a_sh = jnp.where(keep, pltpu.roll(a3, s, 1), 1.0)
        b_sh = jnp.where(keep, pltpu.roll(b3, s, 1), 0.0)
        b3 = a3 * b_sh + b3
        a3 = a3 * a_sh
    hc = hc_ref[...]
    hs = []
    for g in range(ng8):
        hg = b3[g] + a3[g] * hc
        hs.append(hg)
        hc = hg[SUBLANES - 1:SUBLANES, :]
    hc_ref[...] = hc
    h = jnp.concatenate(hs, axis=0)

    o_ref[...] = (h * _gelu_tanh(gb_ref[...].astype(F32))).astype(BF16)

    @pl.when(ti == pl.num_programs(1) - 1)
    def _():
        cout_ref[...] = tail
        hout_ref[...] = hc


def _rnn(xb3, gb3, conv0, h0, conv_w, conv_b, wr4, wi4, b_r, b_i, lam):
    B, T, _ = xb3.shape
    tt = min(RNN_TILE, T)
    vec = lambda a: a.reshape(1, D_MODEL)
    seq = pl.BlockSpec((None, tt, D_MODEL), lambda b, t: (b, t, 0))
    st8 = pl.BlockSpec((None, SUBLANES, D_MODEL), lambda b, t: (b, 0, 0))
    st1 = pl.BlockSpec((None, 1, D_MODEL), lambda b, t: (b, 0, 0))
    return pl.pallas_call(
        _rnn_kernel,
        grid=(B, T // tt),
        in_specs=[seq, seq, st8, st1,
                  _resident((CONV_WIDTH, D_MODEL)), _resident((1, D_MODEL)),
                  _resident(wr4.shape), _resident(wi4.shape),
                  _resident((1, D_MODEL)), _resident((1, D_MODEL)), _resident((1, D_MODEL))],
        out_specs=[seq, st8, st1],
        out_shape=[jax.ShapeDtypeStruct((B, T, D_MODEL), BF16),
                   jax.ShapeDtypeStruct((B, SUBLANES, D_MODEL), F32),
                   jax.ShapeDtypeStruct((B, 1, D_MODEL), F32)],
        scratch_shapes=[pltpu.VMEM((tt + SUBLANES, D_MODEL), F32),
                        pltpu.VMEM((1, D_MODEL), F32)],
        compiler_params=_cparams(("arbitrary", "arbitrary")),
        name="rnn",
    )(xb3, gb3, conv0, h0, conv_w, vec(conv_b), wr4, wi4, vec(b_r), vec(b_i), vec(lam))


def _post_kernel(x_ref, oa_ref, ob_ref, ga_ref, gg_ref, g1_ref, sh_ref, sc_ref, n2_ref,
                 wpa_ref, wpb_ref, wo_ref, wr1_ref, wr2_ref, br_ref,
                 x1_ref, h2_ref, rt_ref):
    mix = (ga_ref[...].astype(F32) * _dot(oa_ref[...], wpa_ref[...])
           + gg_ref[...].astype(F32) * _dot(ob_ref[...], wpb_ref[...]))
    x1 = x_ref[...] + g1_ref[...] * _dot(mix.astype(BF16), wo_ref[...])
    x1_ref[...] = x1
    h2 = _rms(x1, n2_ref[...]) * (1.0 + sc_ref[...]) + sh_ref[...]
    h2_ref[...] = _pack_pairs(h2)
    h2a = h2.astype(BF16)
    h2b = (h2 - h2a.astype(F32)).astype(BF16)
    lg = _dot(h2a, wr1_ref[...]) + _dot(h2a, wr2_ref[...]) + _dot(h2b, wr1_ref[...]) + br_ref[...]

    lane = lax.broadcasted_iota(jnp.int32, lg.shape, 1).astype(F32)
    big = float(LANES)
    is_g = lane < N_GROUPS
    gl = jnp.where(is_g, lg, NEG_BIG)
    mg = jnp.max(gl, axis=-1, keepdims=True)
    sg = jnp.sum(jnp.where(is_g, jnp.exp(gl - mg), 0.0), axis=-1, keepdims=True)
    pg = 1.0 / sg
    gidx = jnp.min(jnp.where(is_g & (gl == mg), lane, big), axis=-1, keepdims=True)
    lo = N_GROUPS + EXPERTS_PER_GROUP * gidx
    ing = (lane >= lo) & (lane < lo + EXPERTS_PER_GROUP)
    el = jnp.where(ing, lg, NEG_BIG)
    m1 = jnp.max(el, axis=-1, keepdims=True)
    se = jnp.sum(jnp.where(ing, jnp.exp(el - m1), 0.0), axis=-1, keepdims=True)
    i1 = jnp.min(jnp.where(ing & (el == m1), lane, big), axis=-1, keepdims=True)
    el2 = jnp.where(lane == i1, NEG_BIG, el)
    m2 = jnp.max(el2, axis=-1, keepdims=True)
    i2 = jnp.min(jnp.where(ing & (el2 == m2) & (lane != i1), lane, big), axis=-1, keepdims=True)
    p1 = 1.0 / se
    p2 = jnp.exp(m2 - m1) / se
    den = p1 + p2
    w1 = pg * p1 / den
    w2 = pg * p2 / den
    rt_ref[...] = jnp.where(lane == 0.0, i1 - N_GROUPS,
                            jnp.where(lane == 1.0, i2 - N_GROUPS,
                                      jnp.where(lane == 2.0, w1, jnp.where(lane == 3.0, w2, 0.0))))


def _post(x2, oa2, ob2, ga2, gg2, mod3, norm2, wpa, wpb, wo, wr1, wr2, br, *, tiles_per_mod):
    rows = x2.shape[0]
    tm = ROW_TILE
    mr = mod3.shape[1]
    row = pl.BlockSpec((tm, D_MODEL), lambda i: (i, 0))
    modc = lambda c: pl.BlockSpec((None, mr, D_MODEL), lambda i: (i // tiles_per_mod, 0, c))
    return pl.pallas_call(
        _post_kernel,
        grid=(rows // tm,),
        in_specs=[row, row, row, row, row, modc(2), modc(3), modc(4), _resident((1, D_MODEL)),
                  _resident(wpa.shape), _resident(wpb.shape), _resident(wo.shape),
                  _resident(wr1.shape), _resident(wr2.shape), _resident((1, LANES))],
        out_specs=[row, pl.BlockSpec((tm, D_MODEL // 2), lambda i: (i, 0)), pl.BlockSpec((tm, LANES), lambda i: (i, 0))],
        out_shape=[jax.ShapeDtypeStruct((rows, D_MODEL), F32),
                   jax.ShapeDtypeStruct((rows, D_MODEL // 2), jnp.uint32),
                   jax.ShapeDtypeStruct((rows, LANES), F32)],
        compiler_params=_cparams(("arbitrary",)),
        name="post",
    )(x2, oa2, ob2, ga2, gg2, mod3, mod3, mod3, norm2.reshape(1, D_MODEL), wpa, wpb, wo, wr1, wr2, br)


def _dispatch_kernel(dest_ref, x_ref, slots_in_ref, slots_ref, sem):
    del slots_in_ref
    tm = x_ref.shape[0]

    def row_copy(r, slot):
        return pltpu.make_async_copy(x_ref.at[pl.ds(r, 1)], slots_ref.at[pl.ds(slot, 1)], sem)

    def issue(r, carry):
        for k in range(TOP_K):
            row_copy(r, dest_ref[TOP_K * r + k]).start()
        return carry

    lax.fori_loop(0, tm, issue, 0, unroll=8)

    for _ in range(TOP_K):
        pltpu.make_async_copy(x_ref, slots_ref.at[pl.ds(0, tm)], sem).wait()


def _dispatch(x, dest, slots, *, tile_offset):
    rows, width = x.shape
    tm = ROW_TILE
    return pl.pallas_call(
        _dispatch_kernel,
        grid=(rows // tm,),
        in_specs=[pl.BlockSpec((tm * TOP_K,), lambda i: (i + tile_offset,), memory_space=pltpu.SMEM),
                  pl.BlockSpec((tm, width), lambda i: (i, 0)),
                  pl.BlockSpec(memory_space=pl.ANY)],
        out_specs=pl.BlockSpec(memory_space=pl.ANY),
        out_shape=jax.ShapeDtypeStruct(slots.shape, slots.dtype),
        scratch_shapes=[pltpu.SemaphoreType.DMA(())],
        input_output_aliases={2: 0},
        compiler_params=_cparams(("arbitrary",)),
        name="dispatch",
    )(dest, x, slots)


def _moe_kernel(be_ref, nb_ref, x_ref, w1_ref, w3_ref, w2_ref, o_ref, w1b_ref, w3b_ref, w2b_ref):
    i = pl.program_id(0)

    @pl.when((i == 0) | (be_ref[i] != be_ref[jnp.maximum(i - 1, 0)]))
    def _():
        w1b_ref[...] = w1_ref[...].astype(BF16)
        w3b_ref[...] = w3_ref[...].astype(BF16)
        w2b_ref[...] = w2_ref[...].astype(BF16)

    @pl.when(i < nb_ref[0])
    def _():
        x = _unpack_pairs(x_ref[...]).astype(BF16)
        a = _dot(x, w1b_ref[...])
        g = a * _sigmoid(a) * _dot(x, w3b_ref[...])
        o_ref[...] = _pack_pairs(_dot(g.astype(BF16), w2b_ref[...]))

    @pl.when(i >= nb_ref[0])
    def _():
        o_ref[...] = jnp.zeros_like(o_ref)


def _moe(xs, block_e, n_used, w1, w3, w2):
    n_blocks = block_e.shape[0]
    bm = MOE_BLOCK
    wspec = lambda s: pl.BlockSpec((None,) + s, lambda i, be, nb: (be[i], 0, 0))
    gs = pltpu.PrefetchScalarGridSpec(
        num_scalar_prefetch=2,
        grid=(n_blocks,),
        in_specs=[pl.BlockSpec((bm, D_MODEL // 2), lambda i, be, nb: (i, 0)),
                  wspec((D_MODEL, D_EXPERT)), wspec((D_MODEL, D_EXPERT)), wspec((D_EXPERT, D_MODEL))],
        out_specs=pl.BlockSpec((bm, D_MODEL // 2), lambda i, be, nb: (i, 0)),
        scratch_shapes=[pltpu.VMEM((D_MODEL, D_EXPERT), BF16),
                        pltpu.VMEM((D_MODEL, D_EXPERT), BF16),
                        pltpu.VMEM((D_EXPERT, D_MODEL), BF16)],
    )
    return pl.pallas_call(
        _moe_kernel,
        grid_spec=gs,
        out_shape=jax.ShapeDtypeStruct((n_blocks * bm, D_MODEL // 2), jnp.uint32),
        compiler_params=_cparams(("arbitrary",)),
        name="moe",
    )(block_e, n_used, xs, w1, w3, w2)


def _final_kernel(x1_ref, ya_ref, yb_ref, rt_ref, g2_ref, nf_ref, o_ref):
    rt = rt_ref[...]
    moe = (rt[:, TOP_K:TOP_K + 1] * _unpack_pairs(ya_ref[...])
           + rt[:, TOP_K + 1:TOP_K + 2] * _unpack_pairs(yb_ref[...]))
    o_ref[...] = _rms(x1_ref[...] + g2_ref[...] * moe, nf_ref[...])


def _final(x1, ya, yb, rt, mod3, norm_f, *, rows_per_mod, row_offset):
    rows = x1.shape[0]
    tm = min(FINAL_TILE, rows)
    assert row_offset % tm == 0 and rows_per_mod % tm == 0
    tile_offset, tiles_per_mod = row_offset // tm, rows_per_mod // tm
    mr = mod3.shape[1]
    row = pl.BlockSpec((tm, D_MODEL), lambda i: (i, 0))
    tok = lambda w: pl.BlockSpec((tm, w), lambda i: (i + tile_offset, 0))
    return pl.pallas_call(
        _final_kernel,
        grid=(rows // tm,),
        in_specs=[row, tok(D_MODEL // 2), tok(D_MODEL // 2), tok(LANES),
                  pl.BlockSpec((None, mr, D_MODEL), lambda i: (i // tiles_per_mod, 0, 5)),
                  _resident((1, D_MODEL))],
        out_specs=row,
        out_shape=jax.ShapeDtypeStruct((rows, D_MODEL), F32),
        compiler_params=_cparams(("arbitrary",)),
        name="final",
    )(x1, ya, yb, rt, mod3, norm_f.reshape(1, D_MODEL))


def _routing_plan(eid, n_tokens):
    bm = MOE_BLOCK
    n_assign = n_tokens * TOP_K
    n_blocks = -(-n_assign // bm) + N_EXPERTS
    flat_e = eid.reshape(-1)
    onehot = (flat_e[:, None] == jnp.arange(N_EXPERTS, dtype=jnp.int32)[None, :]).astype(jnp.int32)
    csum = jnp.cumsum(onehot, axis=0)
    counts = csum[-1]
    padded = (counts + bm - 1) // bm * bm
    pad_end = jnp.cumsum(padded)
    pad_start = pad_end - padded
    dest = jnp.sum(onehot * (csum - 1 + pad_start[None, :]), axis=1).astype(jnp.int32)
    block_start = jnp.arange(n_blocks, dtype=jnp.int32) * bm
    block_e = jnp.minimum(jnp.sum((pad_end[None, :] <= block_start[:, None]).astype(jnp.int32), axis=1),
                          N_EXPERTS - 1).astype(jnp.int32)
    n_used = (pad_end[-1] // bm).astype(jnp.int32).reshape(1)
    return block_e, n_used, dest.reshape(n_tokens, TOP_K)


def _layer(xp, xs, c_all, cache_k, cache_v, cache_logf, state_conv, state_h, p, norm_f):
    Bp, Tp, _ = xp.shape
    Bs, Ts, _ = xs.shape
    past = cache_k.shape[1]
    np_rows, ns_rows = Bp * Tp, Bs * Ts
    assert Tp % ROW_TILE == 0 and ns_rows == ROW_TILE and Tp % ATTN_BLOCK == 0

    mod = _mod(c_all, p['w_mod'], p['b_mod'])
    mod_p = mod[:Bp].reshape(Bp, 1, 6 * D_MODEL)
    mod_s = jnp.repeat(mod[Bp:], Ts, axis=0).reshape(1, ns_rows, 6 * D_MODEL)

    w_in = p['w_in']
    d3 = 3 * D_MODEL
    wcat = jnp.concatenate([w_in[:, :D_MODEL] * (LOG2E * HEAD_DIM ** -0.5), w_in[:, D_MODEL:d3],
                            w_in[:, d3 + N_HEADS:], p['w_gate']], axis=1).astype(BF16)
    wfl = jnp.pad(w_in[:, d3:d3 + N_HEADS], ((0, 0), (0, LANES - N_HEADS))).astype(BF16)
    wflt = wfl.T
    eye = jnp.eye(GATE_TILE // (D_MODEL // N_RNN_BLOCKS), dtype=F32)

    def gate_tiles(w):
        per = GATE_TILE // w.shape[1]
        w4 = w.reshape(N_RNN_BLOCKS // per, per, w.shape[1], w.shape[2])
        return jnp.einsum('gpcd,pq->gpcqd', w4, eye).reshape(N_RNN_BLOCKS // per, GATE_TILE, GATE_TILE).astype(BF16)

    wr4, wi4 = gate_tiles(p['w_r']), gate_tiles(p['w_i'])
    wpa, wpb, wo = p['w_pa'].astype(BF16), p['w_pb'].astype(BF16), p['w_out'].astype(BF16)
    wr = jnp.pad(jnp.concatenate([p['w_rg'], p['w_re']], axis=1), ((0, 0), (0, LANES - N_GROUPS - N_EXPERTS)))
    wr1 = wr.astype(BF16)
    wr2 = (wr - wr1.astype(F32)).astype(BF16)
    br = jnp.pad(jnp.concatenate([p['b_rg'], p['b_re']]), (0, LANES - N_GROUPS - N_EXPERTS)).reshape(1, LANES)

    tiles_p = Tp // ROW_TILE
    in_p = _in_proj(xp.reshape(np_rows, D_MODEL), mod_p, p['norm1'], wcat, wfl, wflt, p['b_gate'], p['b_f'],
                    tiles_per_mod=Tp // IN_TILE, tiles_per_seq=Tp // IN_TILE)
    in_s = _in_proj(xs.reshape(ns_rows, D_MODEL), mod_s, p['norm1'], wcat, wfl, wflt, p['b_gate'], p['b_f'],
                    tiles_per_mod=1, tiles_per_seq=1)
    q_p, kb_p, vb_p, k_p, v_p, xb_p, gb_p, ga_p, gg_p, lf_p, _, fc_p = in_p
    q_s, kb_s, vb_s, k_s, v_s, xb_s, gb_s, ga_s, gg_s, lf_s, lft_s, _ = in_s
    r3 = lambda a, B, T: a.reshape(B, T, D_MODEL)

    oa_p = _attn_prompt(r3(q_p, Bp, Tp), r3(kb_p, Bp, Tp), r3(vb_p, Bp, Tp), fc_p.reshape(Bp, Tp, N_HEADS))

    lfp3 = jnp.transpose(cache_logf.astype(F32), (0, 2, 1))
    lfn3 = lft_s.reshape(N_HEADS, Bs, Ts).transpose(1, 0, 2)
    oa_s = _attn_sample(r3(q_s, Bs, Ts), r3(kb_s, Bs, Ts), r3(vb_s, Bs, Ts),
                        cache_k.astype(BF16).reshape(Bs, past, D_MODEL),
                        cache_v.astype(BF16).reshape(Bs, past, D_MODEL), lfp3, lfn3)

    zc = jnp.zeros((Bp, SUBLANES, D_MODEL), F32)
    zh = jnp.zeros((Bp, 1, D_MODEL), F32)
    sc = jnp.pad(state_conv.astype(F32), ((0, 0), (SUBLANES - (CONV_WIDTH - 1), 0), (0, 0)))
    rnn_w = (p['conv_w'], p['conv_b'], wr4, wi4, p['b_r'], p['b_i'], p['rglru_lambda'])
    ob_p, conv_p, hl_p = _rnn(r3(xb_p, Bp, Tp), r3(gb_p, Bp, Tp), zc, zh, *rnn_w)
    ob_s, conv_s, hl_s = _rnn(r3(xb_s, Bs, Ts), r3(gb_s, Bs, Ts), sc, state_h.astype(F32).reshape(Bs, 1, D_MODEL),
                              *rnn_w)

    post_w = (p['norm2'], wpa, wpb, wo, wr1, wr2, br)
    x1_p, h2_p, rt_p = _post(xp.reshape(np_rows, D_MODEL), oa_p.reshape(np_rows, D_MODEL),
                             ob_p.reshape(np_rows, D_MODEL), ga_p, gg_p, mod_p, *post_w, tiles_per_mod=tiles_p)
    x1_s, h2_s, rt_s = _post(xs.reshape(ns_rows, D_MODEL), oa_s.reshape(ns_rows, D_MODEL),
                             ob_s.reshape(ns_rows, D_MODEL), ga_s, gg_s, mod_s, *post_w, tiles_per_mod=1)

    n_tok = np_rows + ns_rows
    rt = jnp.concatenate([rt_p, rt_s], axis=0)
    block_e, n_used, dest = _routing_plan(rt[:, :TOP_K].astype(jnp.int32), n_tok)
    slots = jnp.zeros((block_e.shape[0] * MOE_BLOCK, D_MODEL // 2), jnp.uint32)
    slots = _dispatch(h2_p, dest.reshape(-1), slots, tile_offset=0)
    slots = _dispatch(h2_s, dest.reshape(-1), slots, tile_offset=np_rows // ROW_TILE)
    take = lambda a, idx: a.at[idx].get(mode='promise_in_bounds')
    yb = _moe(slots, block_e, n_used, p['w1'], p['w3'], p['w2'])
    ya0 = take(yb, dest[:, 0])
    ya1 = take(yb, dest[:, 1])

    y_p = _final(x1_p, ya0, ya1, rt, mod_p, norm_f, rows_per_mod=Tp, row_offset=0)
    y_s = _final(x1_s, ya0, ya1, rt, mod_s, norm_f, rows_per_mod=ns_rows, row_offset=np_rows)

    st = CONV_WIDTH - 1
    state_p = (k_p.reshape(Bp, Tp, N_HEADS, HEAD_DIM), v_p.reshape(Bp, Tp, N_HEADS, HEAD_DIM),
               lf_p.reshape(Bp, Tp, N_HEADS), conv_p[:, SUBLANES - st:], hl_p.reshape(Bp, D_MODEL))
    state_s = (k_s.reshape(Bs, Ts, N_HEADS, HEAD_DIM), v_s.reshape(Bs, Ts, N_HEADS, HEAD_DIM),
               lf_s.reshape(Bs, Ts, N_HEADS), conv_s[:, SUBLANES - st:], hl_s.reshape(Bs, D_MODEL))
    return y_p.reshape(Bp, Tp, D_MODEL), y_s.reshape(Bs, Ts, D_MODEL), state_p, state_s


def kernel(x_prompt, x_sample, cache_k, cache_v, cache_logf, state_conv, state_h, c_prompt, c_sample, w_mod, b_mod, norm1, w_in, b_f, conv_w, conv_b, w_r, b_r, w_i, b_i, rglru_lambda, w_gate, b_gate, w_pa, w_pb, w_out, norm2, w_rg, b_rg, w_re, b_re, w1, w3, w2, norm_f):
    assert w_mod.shape[0] == 1, "single-layer trunk: the final norm is fused into the layer's last stage"
    names = ('w_mod', 'b_mod', 'norm1', 'w_in', 'b_f', 'conv_w', 'conv_b', 'w_r', 'b_r', 'w_i', 'b_i',
             'rglru_lambda', 'w_gate', 'b_gate', 'w_pa', 'w_pb', 'w_out', 'norm2', 'w_rg', 'b_rg', 'w_re', 'b_re',
             'w1', 'w3', 'w2')
    vals = (w_mod, b_mod, norm1, w_in, b_f, conv_w, conv_b, w_r, b_r, w_i, b_i, rglru_lambda, w_gate, b_gate,
            w_pa, w_pb, w_out, norm2, w_rg, b_rg, w_re, b_re, w1, w3, w2)
    p = {n: v[0] for n, v in zip(names, vals)}
    c_all = jnp.concatenate([c_prompt, c_sample], axis=0)
    y_p, y_s, sp, ss = _layer(x_prompt, x_sample, c_all, cache_k[0], cache_v[0], cache_logf[0],
                              state_conv[0], state_h[0], p, norm_f)
    lead = lambda a: a[None]
    return (y_p, y_s, lead(sp[0]), lead(sp[1]), lead(sp[2]), lead(sp[3]), lead(sp[4]),
            lead(ss[0]), lead(ss[1]), lead(ss[2]), lead(ss[3]), lead(ss[4]))
```

```python
import functools

import jax
import jax.numpy as jnp
from jax import lax
from jax.experimental import pallas as pl
from jax.experimental.pallas import tpu as pltpu

F32 = jnp.float32
BF16 = jnp.bfloat16

D_MODEL = 1024
N_HEADS = 16
HEAD_DIM = 64
HEAD_PAIR = 2 * HEAD_DIM
N_PAIRS = N_HEADS // 2
N_RNN_BLOCKS = 16
CONV_WIDTH = 4
RGLRU_C = 8.0
N_GROUPS = 4
EXPERTS_PER_GROUP = 8
N_EXPERTS = N_GROUPS * EXPERTS_PER_GROUP
TOP_K = 2
D_EXPERT = 512
EPS = 1e-6

LANES = 128
SUBLANES = 8
ROW_TILE = 512
FINAL_TILE = 1024
IN_TILE = 256
ATTN_BLOCK = 512
ATTN_QBLOCK = 1024
ATTN_VROWS = HEAD_DIM + 16
RNN_TILE = 256
MIX_TILE = 512
MOE_BLOCK = 256
GATE_TILE = 256
VMEM_LIMIT = 56 * 1024 * 1024
NEG_BIG = -1e30
LOG2E = 1.4426950408889634


def _cparams(sem):
    return pltpu.CompilerParams(dimension_semantics=sem, vmem_limit_bytes=VMEM_LIMIT)


def _resident(shape):
    nd = len(shape)
    return pl.BlockSpec(shape, lambda *_: (0,) * nd, pipeline_mode=pl.Buffered(1))


def _split3(x):
    p1 = x.astype(BF16)
    r1 = x - p1.astype(F32)
    p2 = r1.astype(BF16)
    p3 = (r1 - p2.astype(F32)).astype(BF16)
    return p1, p2, p3


def _dot(a, b):
    return jnp.dot(a, b, preferred_element_type=F32)


def _dot_nt(a, b):
    return lax.dot_general(a, b, (((1,), (1,)), ((), ())), preferred_element_type=F32)


def _sigmoid(x):
    return 0.5 * jnp.tanh(0.5 * x) + 0.5


def _gelu_tanh(x):
    return 0.5 * x * (1.0 + jnp.tanh(0.7978845608028654 * (x + 0.044715 * x * x * x)))


def _rms(x, g):
    ms = jnp.mean(x * x, axis=-1, keepdims=True)
    return x * lax.rsqrt(ms + EPS) * g


def _pack_pairs(x):
    n = x.shape[1] // 2
    hi = pltpu.bitcast(x[:, :n].astype(BF16).astype(F32), jnp.uint32)
    lo = pltpu.bitcast(x[:, n:].astype(BF16).astype(F32), jnp.uint32)
    return hi | lax.shift_right_logical(lo, jnp.uint32(16))


def _unpack_pairs(w):
    hi = pltpu.bitcast(w & jnp.uint32(0xFFFF0000), F32)
    lo = pltpu.bitcast(lax.shift_left(w, jnp.uint32(16)), F32)
    return jnp.concatenate([hi, lo], axis=1)


def _div_p2(x, n):
    assert n & (n - 1) == 0
    return lax.shift_right_logical(x, jnp.int32(n.bit_length() - 1))


def _mod_p2(x, n):
    assert n & (n - 1) == 0
    return x & (n - 1)


def _upper_tri(n):
    r = lax.broadcasted_iota(jnp.int32, (n, n), 0)
    c = lax.broadcasted_iota(jnp.int32, (n, n), 1)
    return jnp.where(r <= c, 1.0, 0.0).astype(BF16)


def _cumsum_lanes(x, u):
    p1, p2, p3 = _split3(x)
    return _dot(p1, u) + _dot(p2, u) + _dot(p3, u)


def _mod_kernel(c_ref, w_ref, b_ref, o_ref):
    c = c_ref[...]
    s = c * _sigmoid(c)
    s1 = s.astype(BF16)
    s2 = (s - s1.astype(F32)).astype(BF16)
    w = w_ref[...]
    w1 = w.astype(BF16)
    w2 = (w - w1.astype(F32)).astype(BF16)
    o_ref[...] = _dot(s1, w1) + _dot(s1, w2) + _dot(s2, w1) + b_ref[...]


def _mod(c, w, b):
    n = c.shape[0]
    cols = w.shape[1]
    return pl.pallas_call(
        _mod_kernel,
        grid=(cols // D_MODEL,),
        in_specs=[pl.BlockSpec((n, D_MODEL), lambda j: (0, 0)),
                  pl.BlockSpec((D_MODEL, D_MODEL), lambda j: (0, j)),
                  pl.BlockSpec((1, D_MODEL), lambda j: (0, j))],
        out_specs=pl.BlockSpec((n, D_MODEL), lambda j: (0, j)),
        out_shape=jax.ShapeDtypeStruct((n, cols), F32),
        compiler_params=_cparams(("arbitrary",)),
        name="mod",
    )(c, w, b.reshape(1, cols))


def _in_kernel(x_ref, sh_ref, sc_ref, n1_ref, w_ref, wfl_ref, wflt_ref, bg_ref, bf_ref, bft_ref, *refs,
               tiles_per_seq, with_mixer):
    q_ref, kb_ref, vb_ref, k5_ref, v5_ref = refs[:5]
    lf_ref, lft_ref, fc_ref, carry_ref = refs[-4:]
    x = x_ref[...]
    h = (_rms(x, n1_ref[...]) * (1.0 + sc_ref[...]) + sh_ref[...]).astype(BF16)

    def mm(c):
        return _dot(h, w_ref[:, c * D_MODEL:(c + 1) * D_MODEL])

    def head_major(y):
        slabs = []
        for p in range(N_PAIRS):
            pair = y[:, p * HEAD_PAIR:(p + 1) * HEAD_PAIR]
            slabs += [pair, pltpu.roll(pair, HEAD_DIM, 1)]
        return jnp.swapaxes(jnp.stack(slabs, axis=0), 0, 1)[:, :, :HEAD_DIM]

    q_ref[...] = mm(0).astype(BF16)
    k = mm(1)
    kb_ref[...] = k.astype(BF16)
    k5_ref[...] = head_major(k)
    v = mm(2)
    vb_ref[...] = v.astype(BF16)
    v5_ref[...] = head_major(v)
    if with_mixer:
        xb_ref, gb_ref, ga_ref, gg_ref = refs[5:9]
        xb_ref[...] = mm(3).astype(BF16)
        gb_ref[...] = mm(4).astype(BF16)
        ga_ref[...] = _sigmoid(mm(5) + bg_ref[:, :D_MODEL]).astype(BF16)
        gg_ref[...] = _sigmoid(mm(6) + bg_ref[:, D_MODEL:]).astype(BF16)

    def log_sigmoid(z):
        return jnp.minimum(z, 0.0) - jnp.log(1.0 + jnp.exp(-jnp.abs(z)))

    lf = log_sigmoid(_dot(h, wfl_ref[...]) + bf_ref[...])
    lf_ref[...] = lf[:, :N_HEADS]
    lft_ref[...] = log_sigmoid(_dot_nt(wflt_ref[...], h)[:N_HEADS, :] + bft_ref[...])

    @pl.when(pl.program_id(0) % tiles_per_seq == 0)
    def _():
        carry_ref[...] = jnp.zeros_like(carry_ref)

    tm = x.shape[0]
    r = lax.broadcasted_iota(jnp.int32, (tm, tm), 0)
    c = lax.broadcasted_iota(jnp.int32, (tm, tm), 1)
    lower = jnp.where(c <= r, 1.0, 0.0).astype(BF16)
    p1, p2, p3 = _split3(lf)
    fc = _dot(lower, p1) + _dot(lower, p2) + _dot(lower, p3) + carry_ref[...]
    fc_ref[...] = fc[:, :N_HEADS]
    carry_ref[...] = fc[tm - 1:tm, :]


def _in_proj(x2, mod3, norm1, wcat, wfl, wflt, b_gate, b_f, *, tiles_per_mod, tiles_per_seq, with_mixer):
    rows = x2.shape[0]
    tm = IN_TILE
    nt = rows // tm
    row_blk = pl.BlockSpec((tm, D_MODEL), lambda i: (i, 0))
    head_blk = pl.BlockSpec((tm, N_HEADS, HEAD_DIM), lambda i: (i, 0, 0))
    if mod3.shape[1] == 1:
        modc = lambda c: pl.BlockSpec((None, 1, D_MODEL), lambda i: (i // tiles_per_mod, 0, c))
    else:
        modc = lambda c: pl.BlockSpec((None, tm, D_MODEL), lambda i: (0, i, c))
    outs = [jax.ShapeDtypeStruct((rows, D_MODEL), BF16),
            jax.ShapeDtypeStruct((rows, D_MODEL), BF16),
            jax.ShapeDtypeStruct((rows, D_MODEL), BF16),
            jax.ShapeDtypeStruct((rows, N_HEADS, HEAD_DIM), F32),
            jax.ShapeDtypeStruct((rows, N_HEADS, HEAD_DIM), F32),
            *([jax.ShapeDtypeStruct((rows, D_MODEL), BF16)] * (4 if with_mixer else 0)),
            jax.ShapeDtypeStruct((rows, N_HEADS), F32),
            jax.ShapeDtypeStruct((N_HEADS, rows), F32),
            jax.ShapeDtypeStruct((rows, N_HEADS), F32)]
    return pl.pallas_call(
        functools.partial(_in_kernel, tiles_per_seq=tiles_per_seq, with_mixer=with_mixer),
        grid=(nt,),
        in_specs=[row_blk, modc(0), modc(1),
                  _resident((1, D_MODEL)),
                  _resident(wcat.shape),
                  _resident(wfl.shape),
                  _resident(wflt.shape),
                  _resident((1, 2 * D_MODEL)),
                  _resident((1, LANES)),
                  _resident((N_HEADS, 1))],
        out_specs=[row_blk] * 3 + [head_blk] * 2 + [row_blk] * (4 if with_mixer else 0)
                  + [pl.BlockSpec((tm, N_HEADS), lambda i: (i, 0)),
                     pl.BlockSpec((N_HEADS, tm), lambda i: (0, i)),
                     pl.BlockSpec((tm, N_HEADS), lambda i: (i, 0))],
        out_shape=outs,
        scratch_shapes=[pltpu.VMEM((1, LANES), F32)],
        compiler_params=_cparams(("arbitrary",)),
        name="in_proj",
    )(x2, mod3, mod3, norm1.reshape(1, D_MODEL), wcat, wfl, wflt,
      b_gate.reshape(1, 2 * D_MODEL), jnp.pad(b_f, (0, LANES - N_HEADS)).reshape(1, LANES),
      b_f.reshape(N_HEADS, 1))


def _attn_kernel(q_ref, k_ref, v_ref, f_ref, o_ref, kb_ref, vt_ref, qt_ref, s0_ref, m_ref, acc_ref):
    qi = pl.program_id(2)
    bq = q_ref.shape[0]
    bk = ATTN_BLOCK
    nk = k_ref.shape[0] // bk
    own_lanes = (lambda i: i < HEAD_DIM, lambda i: i >= HEAD_DIM)
    bias_at = (HEAD_DIM, 0)

    @pl.when(qi == 0)
    def _():
        lane = lax.broadcasted_iota(jnp.int32, (bk, HEAD_PAIR), 1)
        ones_rows = jnp.where(lax.broadcasted_iota(jnp.int32, (ATTN_VROWS - HEAD_DIM, bk), 0) == 0, 1.0, 0.0)
        for c in range(nk):
            blk = slice(c * bk, (c + 1) * bk)
            k = k_ref[blk, :].astype(F32)
            f = f_ref[blk, :] * LOG2E
            head = lax.broadcasted_iota(jnp.int32, f.shape, 1)
            for hh in range(2):
                mine = head == 2 * pl.program_id(1) + hh
                f_head = jnp.sum(jnp.where(mine, f, 0.0), axis=1, keepdims=True)
                slab = jnp.zeros(k.shape, F32)
                for n, piece in enumerate(_split3(f_head)):
                    slab = jnp.where(lane == bias_at[hh] + n, piece.astype(F32), slab)
                kb_ref[hh, blk, :] = jnp.where(own_lanes[hh](lane), k, slab).astype(BF16)
            vt = v_ref[blk, :].astype(F32).T
            for hh in range(2):
                vt_ref[c, hh, 0:HEAD_DIM] = vt[hh * HEAD_DIM:(hh + 1) * HEAD_DIM].astype(BF16)
                vt_ref[c, hh, HEAD_DIM:ATTN_VROWS] = ones_rows.astype(BF16)

    qt = q_ref[...].astype(F32).T
    feat = lax.broadcasted_iota(jnp.int32, qt.shape, 0)
    for hh in range(2):
        is_bias = (feat >= bias_at[hh]) & (feat < bias_at[hh] + 3)
        qt_ref[hh] = jnp.where(own_lanes[hh](feat), qt, jnp.where(is_bias, -1.0, 0.0)).astype(BF16)

    m_ref[...] = jnp.full(m_ref.shape, NEG_BIG, F32)
    acc_ref[...] = jnp.zeros(acc_ref.shape, F32)

    nsub = bq // bk

    def scores(hh, u, j):
        keys = kb_ref[hh, pl.ds(pl.multiple_of(j * bk, bk), bk), :]
        return _dot(keys, qt_ref[hh, :, u * bk:(u + 1) * bk])

    def softmax_pv(hh, u, j, s, masked):
        cols = slice(u * bk, (u + 1) * bk)
        if masked:
            key = lax.broadcasted_iota(jnp.int32, s.shape, 0)
            qry = lax.broadcasted_iota(jnp.int32, s.shape, 1)
            s = jnp.where(key <= qry, s, NEG_BIG)
        m_prev = m_ref[hh, :, cols]
        m_new = jnp.maximum(m_prev, jnp.max(s, axis=0, keepdims=True))
        p = jnp.exp2(s - m_new).astype(BF16)
        acc_ref[hh, :, cols] = jnp.exp2(m_prev - m_new) * acc_ref[hh, :, cols] + _dot(vt_ref[j, hh], p)
        m_ref[hh, :, cols] = m_new

    def run(items, first_of_next):
        s = s0_ref[...]
        for n, (hh, u, j, masked) in enumerate(items):
            if n + 1 < len(items):
                nxt = scores(*items[n + 1][:3])
            elif first_of_next is not None:
                s0_ref[...] = scores(*first_of_next)
            softmax_pv(hh, u, j, s, masked)
            s = nxt if n + 1 < len(items) else None

    def body(j, carry):
        run([(hh, u, j, False) for u in range(nsub) for hh in range(2)], (0, 0, j + 1))
        return carry

    s0_ref[...] = scores(0, 0, 0)
    lax.fori_loop(0, nsub * qi, body, 0)
    for d in range(nsub):
        j = nsub * qi + d
        items = [(hh, u, j, u == d) for u in range(d, nsub) for hh in range(2)]
        run(items, (0, d + 1, j + 1) if d + 1 < nsub else None)

    heads = [acc_ref[hh, 0:HEAD_DIM] / acc_ref[hh, HEAD_DIM:HEAD_DIM + 1] for hh in range(2)]
    o_ref[...] = jnp.concatenate(heads, axis=0).T.astype(BF16)


def _attn_prompt(q3, k3, v3, fc3):
    B, T, _ = q3.shape
    bq, bk = ATTN_QBLOCK, ATTN_BLOCK
    nq, nk = T // bq, T // bk
    return pl.pallas_call(
        _attn_kernel,
        grid=(B, N_PAIRS, nq),
        in_specs=[pl.BlockSpec((None, bq, HEAD_PAIR), lambda b, p, i: (b, i, p)),
                  pl.BlockSpec((None, T, HEAD_PAIR), lambda b, p, i: (b, 0, p)),
                  pl.BlockSpec((None, T, HEAD_PAIR), lambda b, p, i: (b, 0, p)),
                  pl.BlockSpec((None, T, N_HEADS), lambda b, p, i: (b, 0, 0))],
        out_specs=pl.BlockSpec((None, bq, HEAD_PAIR), lambda b, p, i: (b, i, p)),
        out_shape=jax.ShapeDtypeStruct((B, T, D_MODEL), BF16),
        scratch_shapes=[pltpu.VMEM((2, T, HEAD_PAIR), BF16),
                        pltpu.VMEM((nk, 2, ATTN_VROWS, bk), BF16),
                        pltpu.VMEM((2, HEAD_PAIR, bq), BF16),
                        pltpu.VMEM((bk, bk), F32),
                        pltpu.VMEM((2, 1, bq), F32),
                        pltpu.VMEM((2, ATTN_VROWS, bq), F32)],
        compiler_params=_cparams(("arbitrary", "arbitrary", "arbitrary")),
        name="attn_prompt",
    )(q3, k3, v3, fc3)


def _attn_sample_kernel(q_ref, kn_ref, vn_ref, kc_ref, vc_ref, lfp_ref, lfn_ref, o_ref):
    tq = q_ref.shape[0]
    past = kc_ref.shape[0]
    nrow = N_HEADS * tq
    row = lax.broadcasted_iota(jnp.int32, (nrow, D_MODEL), 0)
    col = lax.broadcasted_iota(jnp.int32, (nrow, D_MODEL), 1)
    own = _div_p2(row, tq) == _div_p2(col, HEAD_DIM)

    rr = lax.broadcasted_iota(jnp.int32, (nrow, tq), 0)
    rc = lax.broadcasted_iota(jnp.int32, (nrow, tq), 1)
    rep = jnp.where(_mod_p2(rr, tq) == rc, 1.0, 0.0).astype(BF16)
    qx = jnp.where(own, _dot(rep, q_ref[...]), 0.0).astype(BF16)

    fp = _cumsum_lanes(lfp_ref[...] * LOG2E, _upper_tri(past))
    fn = _cumsum_lanes(lfn_ref[...] * LOG2E, _upper_tri(tq)) + fp[:, past - 1:past]
    hr = lax.broadcasted_iota(jnp.int32, (nrow, N_HEADS), 0)
    hc = lax.broadcasted_iota(jnp.int32, (nrow, N_HEADS), 1)
    hsel = jnp.where(_div_p2(hr, tq) == hc, 1.0, 0.0).astype(BF16)

    def expand(f):
        p1, p2, p3 = _split3(f)
        return _dot(hsel, p1) + _dot(hsel, p2) + _dot(hsel, p3)

    s_p = _dot_nt(qx, kc_ref[...]) - expand(fp)
    s_n = _dot_nt(qx, kn_ref[...]) - expand(fn)
    nr = lax.broadcasted_iota(jnp.int32, (nrow, tq), 0)
    nc = lax.broadcasted_iota(jnp.int32, (nrow, tq), 1)
    s_n = jnp.where(nc <= _mod_p2(nr, tq), s_n, NEG_BIG)
    m = jnp.maximum(jnp.max(s_p, axis=-1, keepdims=True), jnp.max(s_n, axis=-1, keepdims=True))
    p_p = jnp.exp2(s_p - m)
    p_n = jnp.exp2(s_n - m)
    l = jnp.sum(p_p, axis=-1, keepdims=True) + jnp.sum(p_n, axis=-1, keepdims=True)
    o_full = _dot(p_p.astype(BF16), vc_ref[...]) + _dot(p_n.astype(BF16), vn_ref[...])
    o_own = jnp.where(own, o_full / l, 0.0).astype(BF16)
    cr = lax.broadcasted_iota(jnp.int32, (tq, nrow), 0)
    cc = lax.broadcasted_iota(jnp.int32, (tq, nrow), 1)
    col_sel = jnp.where(_mod_p2(cc, tq) == cr, 1.0, 0.0).astype(BF16)
    o_ref[...] = _dot(col_sel, o_own).astype(BF16)


def _attn_sample(q3, kn3, vn3, kc3, vc3, lfp3, lfn3):
    B, tq, _ = q3.shape
    past = kc3.shape[1]
    blk = lambda n, w: pl.BlockSpec((None, n, w), lambda b: (b, 0, 0))
    return pl.pallas_call(
        _attn_sample_kernel,
        grid=(B,),
        in_specs=[blk(tq, D_MODEL), blk(tq, D_MODEL), blk(tq, D_MODEL),
                  blk(past, D_MODEL), blk(past, D_MODEL), blk(N_HEADS, past), blk(N_HEADS, tq)],
        out_specs=blk(tq, D_MODEL),
        out_shape=jax.ShapeDtypeStruct((B, tq, D_MODEL), BF16),
        compiler_params=_cparams(("arbitrary",)),
        name="attn_sample",
    )(q3, kn3, vn3, kc3, vc3, lfp3, lfn3)


def _rglru_gates(xb, xp_ref, cw_ref, cb_ref, wr_ref, wi_ref):
    tt = xb.shape[0]
    pad = SUBLANES
    xp_ref[pad:pad + tt, :] = xb
    xc = cb_ref[...] + xb * cw_ref[CONV_WIDTH - 1:CONV_WIDTH, :]
    for k in range(CONV_WIDTH - 1):
        back = CONV_WIDTH - 1 - k
        xc = xc + xp_ref[pad - back:pad - back + tt, :] * cw_ref[k:k + 1, :]
    tail = xp_ref[tt:tt + pad, :]
    xp_ref[0:pad, :] = tail

    xcb = xc.astype(BF16)
    ng = D_MODEL // GATE_TILE
    r = jnp.concatenate([_dot(xcb[:, g * GATE_TILE:(g + 1) * GATE_TILE], wr_ref[g]) for g in range(ng)], axis=-1)
    i = jnp.concatenate([_dot(xcb[:, g * GATE_TILE:(g + 1) * GATE_TILE], wi_ref[g]) for g in range(ng)], axis=-1)
    return xc, r, i, tail


def _rglru_scan(xc, r, i, gb, hc_ref, br_ref, bi_ref, lam_ref):
    tt = xc.shape[0]
    r = _sigmoid(r + br_ref[...])
    i = _sigmoid(i + bi_ref[...])
    lam = lam_ref[...]
    softplus_neg = jnp.maximum(-lam, 0.0) + jnp.log(1.0 + jnp.exp(-jnp.abs(lam)))
    log_a = (-RGLRU_C) * r * softplus_neg
    a = jnp.exp(log_a)
    th = jnp.tanh(log_a)
    mult = jnp.sqrt(-2.0 * th / (1.0 - th))
    b = mult * (i * xc)

    ng8 = tt // SUBLANES
    a3 = a.reshape(ng8, SUBLANES, D_MODEL)
    b3 = b.reshape(ng8, SUBLANES, D_MODEL)
    sub = lax.broadcasted_iota(jnp.int32, a3.shape, 1)
    for s in (1, 2, 4):
        keep = sub >= s
        a_sh = jnp.where(keep, pltpu.roll(a3, s, 1), 1.0)
        b_sh = jnp.where(keep, pltpu.roll(b3, s, 1), 0.0)
        b3 = a3 * b_sh + b3
        a3 = a3 * a_sh
    hc = hc_ref[...]
    hs = []
    for g in range(ng8):
        hg = b3[g] + a3[g] * hc
        hs.append(hg)
        hc = hg[SUBLANES - 1:SUBLANES, :]
    hc_ref[...] = hc
    return jnp.concatenate(hs, axis=0) * _gelu_tanh(gb), hc


def _rnn_kernel(xb_ref, gb_ref, c0_ref, h0_ref, cw_ref, cb_ref, wr_ref, wi_ref, br_ref, bi_ref, lam_ref,
                o_ref, cout_ref, hout_ref, xp_ref, hc_ref):
    ti = pl.program_id(1)

    @pl.when(ti == 0)
    def _():
        xp_ref[0:SUBLANES, :] = c0_ref[...]
        hc_ref[...] = h0_ref[...]

    xc, r, i, tail = _rglru_gates(xb_ref[...].astype(F32), xp_ref, cw_ref, cb_ref, wr_ref, wi_ref)
    ob, hc = _rglru_scan(xc, r, i, gb_ref[...].astype(F32), hc_ref, br_ref, bi_ref, lam_ref)
    o_ref[...] = ob.astype(BF16)

    @pl.when(ti == pl.num_programs(1) - 1)
    def _():
        cout_ref[...] = tail
        hout_ref[...] = hc


def _rnn(xb3, gb3, conv0, h0, conv_w, conv_b, wr4, wi4, b_r, b_i, lam):
    B, T, _ = xb3.shape
    tt = min(RNN_TILE, T)
    vec = lambda a: a.reshape(1, D_MODEL)
    seq = pl.BlockSpec((None, tt, D_MODEL), lambda b, t: (b, t, 0))
    st8 = pl.BlockSpec((None, SUBLANES, D_MODEL), lambda b, t: (b, 0, 0))
    st1 = pl.BlockSpec((None, 1, D_MODEL), lambda b, t: (b, 0, 0))
    return pl.pallas_call(
        _rnn_kernel,
        grid=(B, T // tt),
        in_specs=[seq, seq, st8, st1,
                  _resident((CONV_WIDTH, D_MODEL)), _resident((1, D_MODEL)),
                  _resident(wr4.shape), _resident(wi4.shape),
                  _resident((1, D_MODEL)), _resident((1, D_MODEL)), _resident((1, D_MODEL))],
        out_specs=[seq, st8, st1],
        out_shape=[jax.ShapeDtypeStruct((B, T, D_MODEL), BF16),
                   jax.ShapeDtypeStruct((B, SUBLANES, D_MODEL), F32),
                   jax.ShapeDtypeStruct((B, 1, D_MODEL), F32)],
        scratch_shapes=[pltpu.VMEM((tt + SUBLANES, D_MODEL), F32),
                        pltpu.VMEM((1, D_MODEL), F32)],
        compiler_params=_cparams(("arbitrary", "arbitrary")),
        name="rnn",
    )(xb3, gb3, conv0, h0, conv_w, vec(conv_b), wr4, wi4, vec(b_r), vec(b_i), vec(lam))


def _merge_route_tile(x, pa, ob, ga, gg, g1, sh2, sc2, n2_ref, wpb_ref, wo_ref, wr1_ref, wr2_ref, br_ref):
    mix = ga * pa + gg * _dot(ob, wpb_ref[...])
    x1 = x + g1 * _dot(mix.astype(BF16), wo_ref[...])
    h2 = _rms(x1, n2_ref[...]) * (1.0 + sc2) + sh2
    h2a = h2.astype(BF16)
    h2b = (h2 - h2a.astype(F32)).astype(BF16)
    lg = _dot(h2a, wr1_ref[...]) + _dot(h2a, wr2_ref[...]) + _dot(h2b, wr1_ref[...]) + br_ref[...]

    lane = lax.broadcasted_iota(jnp.int32, lg.shape, 1).astype(F32)
    big = float(LANES)
    is_g = lane < N_GROUPS
    gl = jnp.where(is_g, lg, NEG_BIG)
    mg = jnp.max(gl, axis=-1, keepdims=True)
    sg = jnp.sum(jnp.where(is_g, jnp.exp(gl - mg), 0.0), axis=-1, keepdims=True)
    pg = 1.0 / sg
    gidx = jnp.min(jnp.where(is_g & (gl == mg), lane, big), axis=-1, keepdims=True)
    lo = N_GROUPS + EXPERTS_PER_GROUP * gidx
    ing = (lane >= lo) & (lane < lo + EXPERTS_PER_GROUP)
    el = jnp.where(ing, lg, NEG_BIG)
    m1 = jnp.max(el, axis=-1, keepdims=True)
    se = jnp.sum(jnp.where(ing, jnp.exp(el - m1), 0.0), axis=-1, keepdims=True)
    i1 = jnp.min(jnp.where(ing & (el == m1), lane, big), axis=-1, keepdims=True)
    el2 = jnp.where(lane == i1, NEG_BIG, el)
    m2 = jnp.max(el2, axis=-1, keepdims=True)
    i2 = jnp.min(jnp.where(ing & (el2 == m2) & (lane != i1), lane, big), axis=-1, keepdims=True)
    p1 = 1.0 / se
    p2 = jnp.exp(m2 - m1) / se
    den = p1 + p2
    w1 = pg * p1 / den
    w2 = pg * p2 / den
    rt = jnp.where(lane == 0.0, i1 - N_GROUPS,
                   jnp.where(lane == 1.0, i2 - N_GROUPS,
                             jnp.where(lane == 2.0, w1, jnp.where(lane == 3.0, w2, 0.0))))
    return x1, h2, rt


def _post_kernel(x_ref, oa_ref, ob_ref, ga_ref, gg_ref, g1_ref, sh_ref, sc_ref, n2_ref,
                 wpa_ref, wpb_ref, wo_ref, wr1_ref, wr2_ref, br_ref,
                 x1_ref, h2_ref, rt_ref):
    x1, h2, rt = _merge_route_tile(x_ref[...], _dot(oa_ref[...], wpa_ref[...]), ob_ref[...],
                                   ga_ref[...].astype(F32), gg_ref[...].astype(F32), g1_ref[...], sh_ref[...],
                                   sc_ref[...], n2_ref, wpb_ref, wo_ref, wr1_ref, wr2_ref, br_ref)
    x1_ref[...] = x1
    h2_ref[...] = _pack_pairs(h2)
    rt_ref[...] = rt


def _post(x2, oa2, ob2, ga2, gg2, mod3, norm2, wpa, wpb, wo, wr1, wr2, br, *, tiles_per_mod):
    rows = x2.shape[0]
    tm = ROW_TILE
    mr = mod3.shape[1]
    row = pl.BlockSpec((tm, D_MODEL), lambda i: (i, 0))
    modc = lambda c: pl.BlockSpec((None, mr, D_MODEL), lambda i: (i // tiles_per_mod, 0, c))
    return pl.pallas_call(
        _post_kernel,
        grid=(rows // tm,),
        in_specs=[row, row, row, row, row, modc(2), modc(3), modc(4), _resident((1, D_MODEL)),
                  _resident(wpa.shape), _resident(wpb.shape), _resident(wo.shape),
                  _resident(wr1.shape), _resident(wr2.shape), _resident((1, LANES))],
        out_specs=[row, pl.BlockSpec((tm, D_MODEL // 2), lambda i: (i, 0)), pl.BlockSpec((tm, LANES), lambda i: (i, 0))],
        out_shape=[jax.ShapeDtypeStruct((rows, D_MODEL), F32),
                   jax.ShapeDtypeStruct((rows, D_MODEL // 2), jnp.uint32),
                   jax.ShapeDtypeStruct((rows, LANES), F32)],
        compiler_params=_cparams(("arbitrary",)),
        name="post",
    )(x2, oa2, ob2, ga2, gg2, mod3, mod3, mod3, norm2.reshape(1, D_MODEL), wpa, wpb, wo, wr1, wr2, br)


def _mix_kernel(x_ref, oa_ref, sh1_ref, sc1_ref, g1_ref, sh2_ref, sc2_ref, n1_ref, n2_ref, w_ref, bg_ref,
                c0_ref, h0_ref, cw_ref, cb_ref, wr_ref, wi_ref, br_ref, bi_ref, lam_ref,
                wpa_ref, wpb_ref, wo_ref, wr1_ref, wr2_ref, brt_ref,
                x1_ref, h2_ref, rt_ref, cout_ref, hout_ref, xp_ref, hc_ref):
    ti = pl.program_id(1)

    @pl.when(ti == 0)
    def _():
        xp_ref[0:SUBLANES, :] = c0_ref[...]
        hc_ref[...] = h0_ref[...]

    def project(rows):
        x = x_ref[rows, :]
        h = (_rms(x, n1_ref[...]) * (1.0 + sc1_ref[...]) + sh1_ref[...]).astype(BF16)
        mm = lambda c: _dot(h, w_ref[:, c * D_MODEL:(c + 1) * D_MODEL])
        xc, r, i, tail = _rglru_gates(mm(0), xp_ref, cw_ref, cb_ref, wr_ref, wi_ref)
        return dict(x=x, xc=xc, r=r, i=i, tail=tail, gb=mm(1), ga=mm(2), gg=mm(3),
                    pa=_dot(oa_ref[rows, :], wpa_ref[...]))

    def recur(st):
        ob, hc = _rglru_scan(st['xc'], st['r'], st['i'], st['gb'], hc_ref, br_ref, bi_ref, lam_ref)
        return ob.astype(BF16), hc

    def merge(rows, st, ob):
        ga = _sigmoid(st['ga'] + bg_ref[:, :D_MODEL])
        gg = _sigmoid(st['gg'] + bg_ref[:, D_MODEL:])
        x1, h2, rt = _merge_route_tile(st['x'], st['pa'], ob, ga, gg, g1_ref[...], sh2_ref[...], sc2_ref[...],
                                       n2_ref, wpb_ref, wo_ref, wr1_ref, wr2_ref, brt_ref)
        x1_ref[rows, :] = x1
        h2_ref[rows, :] = _pack_pairs(h2)
        rt_ref[rows, :] = rt

    subs = [slice(s, s + RNN_TILE) for s in range(0, x_ref.shape[0], RNN_TILE)]
    states = [project(subs[0])]
    for n, rows in enumerate(subs):
        if n + 1 < len(subs):
            states.append(project(subs[n + 1]))
        ob, hc = recur(states[n])
        merge(rows, states[n], ob)

    @pl.when(ti == pl.num_programs(1) - 1)
    def _():
        cout_ref[...] = states[-1]['tail']
        hout_ref[...] = hc


def _mix(x3, oa3, mod3, conv0, h0, norm1, norm2, w_mix, b_gate, conv_w, conv_b, wr4, wi4, b_r, b_i, lam,
         wpa, wpb, wo, wr1, wr2, br):
    B, T, _ = x3.shape
    tt = MIX_TILE
    vec = lambda a: a.reshape(1, D_MODEL)
    seq = lambda w: pl.BlockSpec((None, tt, w), lambda b, t: (b, t, 0))
    modc = lambda c: pl.BlockSpec((None, 1, D_MODEL), lambda b, t: (b, 0, c))
    st8 = pl.BlockSpec((None, SUBLANES, D_MODEL), lambda b, t: (b, 0, 0))
    st1 = pl.BlockSpec((None, 1, D_MODEL), lambda b, t: (b, 0, 0))
    v1 = _resident((1, D_MODEL))
    return pl.pallas_call(
        _mix_kernel,
        grid=(B, T // tt),
        in_specs=[seq(D_MODEL), seq(D_MODEL), modc(0), modc(1), modc(2), modc(3), modc(4), v1, v1,
                  _resident(w_mix.shape), _resident((1, 2 * D_MODEL)), st8, st1,
                  _resident((CONV_WIDTH, D_MODEL)), v1, _resident(wr4.shape), _resident(wi4.shape), v1, v1, v1,
                  _resident(wpa.shape), _resident(wpb.shape), _resident(wo.shape),
                  _resident(wr1.shape), _resident(wr2.shape), _resident((1, LANES))],
        out_specs=[seq(D_MODEL), seq(D_MODEL // 2), seq(LANES), st8, st1],
        out_shape=[jax.ShapeDtypeStruct((B, T, D_MODEL), F32),
                   jax.ShapeDtypeStruct((B, T, D_MODEL // 2), jnp.uint32),
                   jax.ShapeDtypeStruct((B, T, LANES), F32),
                   jax.ShapeDtypeStruct((B, SUBLANES, D_MODEL), F32),
                   jax.ShapeDtypeStruct((B, 1, D_MODEL), F32)],
        scratch_shapes=[pltpu.VMEM((RNN_TILE + SUBLANES, D_MODEL), F32),
                        pltpu.VMEM((1, D_MODEL), F32)],
        compiler_params=_cparams(("arbitrary", "arbitrary")),
        name="mix",
    )(x3, oa3, mod3, mod3, mod3, mod3, mod3, vec(norm1), vec(norm2), w_mix, b_gate.reshape(1, 2 * D_MODEL),
      conv0, h0, conv_w, vec(conv_b), wr4, wi4, vec(b_r), vec(b_i), vec(lam), wpa, wpb, wo, wr1, wr2, br)


def _dispatch_kernel(dest_ref, x_ref, slots_in_ref, slots_ref, sem):
    del slots_in_ref
    tm = x_ref.shape[0]

    def row_copy(r, slot):
        return pltpu.make_async_copy(x_ref.at[pl.ds(r, 1)], slots_ref.at[pl.ds(slot, 1)], sem)

    def issue(r, carry):
        for k in range(TOP_K):
            row_copy(r, dest_ref[TOP_K * r + k]).start()
        return carry

    lax.fori_loop(0, tm, issue, 0, unroll=8)

    for _ in range(TOP_K):
        pltpu.make_async_copy(x_ref, slots_ref.at[pl.ds(0, tm)], sem).wait()


def _dispatch(x, dest, slots, *, tile_offset):
    rows, width = x.shape
    tm = ROW_TILE
    return pl.pallas_call(
        _dispatch_kernel,
        grid=(rows // tm,),
        in_specs=[pl.BlockSpec((tm * TOP_K,), lambda i: (i + tile_offset,), memory_space=pltpu.SMEM),
                  pl.BlockSpec((tm, width), lambda i: (i, 0)),
                  pl.BlockSpec(memory_space=pl.ANY)],
        out_specs=pl.BlockSpec(memory_space=pl.ANY),
        out_shape=jax.ShapeDtypeStruct(slots.shape, slots.dtype),
        scratch_shapes=[pltpu.SemaphoreType.DMA(())],
        input_output_aliases={2: 0},
        compiler_params=_cparams(("arbitrary",)),
        name="dispatch",
    )(dest, x, slots)


def _moe_kernel(be_ref, nb_ref, x_ref, w1_ref, w3_ref, w2_ref, o_ref, w1b_ref, w3b_ref, w2b_ref):
    i = pl.program_id(0)

    @pl.when((i == 0) | (be_ref[i] != be_ref[jnp.maximum(i - 1, 0)]))
    def _():
        w1b_ref[...] = w1_ref[...].astype(BF16)
        w3b_ref[...] = w3_ref[...].astype(BF16)
        w2b_ref[...] = w2_ref[...].astype(BF16)

    @pl.when(i < nb_ref[0])
    def _():
        x = _unpack_pairs(x_ref[...]).astype(BF16)
        a = _dot(x, w1b_ref[...])
        g = a * _sigmoid(a) * _dot(x, w3b_ref[...])
        o_ref[...] = _pack_pairs(_dot(g.astype(BF16), w2b_ref[...]))

    @pl.when(i >= nb_ref[0])
    def _():
        o_ref[...] = jnp.zeros_like(o_ref)


def _moe(xs, block_e, n_used, w1, w3, w2):
    n_blocks = block_e.shape[0]
    bm = MOE_BLOCK
    wspec = lambda s: pl.BlockSpec((None,) + s, lambda i, be, nb: (be[i], 0, 0))
    gs = pltpu.PrefetchScalarGridSpec(
        num_scalar_prefetch=2,
        grid=(n_blocks,),
        in_specs=[pl.BlockSpec((bm, D_MODEL // 2), lambda i, be, nb: (i, 0)),
                  wspec((D_MODEL, D_EXPERT)), wspec((D_MODEL, D_EXPERT)), wspec((D_EXPERT, D_MODEL))],
        out_specs=pl.BlockSpec((bm, D_MODEL // 2), lambda i, be, nb: (i, 0)),
        scratch_shapes=[pltpu.VMEM((D_MODEL, D_EXPERT), BF16),
                        pltpu.VMEM((D_MODEL, D_EXPERT), BF16),
                        pltpu.VMEM((D_EXPERT, D_MODEL), BF16)],
    )
    return pl.pallas_call(
        _moe_kernel,
        grid_spec=gs,
        out_shape=jax.ShapeDtypeStruct((n_blocks * bm, D_MODEL // 2), jnp.uint32),
        compiler_params=_cparams(("arbitrary",)),
        name="moe",
    )(block_e, n_used, xs, w1, w3, w2)


def _final_kernel(x1_ref, ya_ref, yb_ref, rt_ref, g2_ref, nf_ref, o_ref):
    rt = rt_ref[...]
    moe = (rt[:, TOP_K:TOP_K + 1] * _unpack_pairs(ya_ref[...])
           + rt[:, TOP_K + 1:TOP_K + 2] * _unpack_pairs(yb_ref[...]))
    o_ref[...] = _rms(x1_ref[...] + g2_ref[...] * moe, nf_ref[...])


def _final(x1, ya, yb, rt, mod3, norm_f, *, rows_per_mod, row_offset):
    rows = x1.shape[0]
    tm = min(FINAL_TILE, rows)
    assert row_offset % tm == 0 and rows_per_mod % tm == 0
    tile_offset, tiles_per_mod = row_offset // tm, rows_per_mod // tm
    mr = mod3.shape[1]
    row = pl.BlockSpec((tm, D_MODEL), lambda i: (i, 0))
    tok = lambda w: pl.BlockSpec((tm, w), lambda i: (i + tile_offset, 0))
    return pl.pallas_call(
        _final_kernel,
        grid=(rows // tm,),
        in_specs=[row, tok(D_MODEL // 2), tok(D_MODEL // 2), tok(LANES),
                  pl.BlockSpec((None, mr, D_MODEL), lambda i: (i // tiles_per_mod, 0, 5)),
                  _resident((1, D_MODEL))],
        out_specs=row,
        out_shape=jax.ShapeDtypeStruct((rows, D_MODEL), F32),
        compiler_params=_cparams(("arbitrary",)),
        name="final",
    )(x1, ya, yb, rt, mod3, norm_f.reshape(1, D_MODEL))


def _routing_plan(eid, n_tokens):
    bm = MOE_BLOCK
    n_assign = n_tokens * TOP_K
    n_blocks = -(-n_assign // bm) + N_EXPERTS
    flat_e = eid.reshape(-1)
    onehot = (flat_e[:, None] == jnp.arange(N_EXPERTS, dtype=jnp.int32)[None, :]).astype(jnp.int32)
    csum = jnp.cumsum(onehot, axis=0)
    counts = csum[-1]
    padded = (counts + bm - 1) // bm * bm
    pad_end = jnp.cumsum(padded)
    pad_start = pad_end - padded
    dest = jnp.sum(onehot * (csum - 1 + pad_start[None, :]), axis=1).astype(jnp.int32)
    block_start = jnp.arange(n_blocks, dtype=jnp.int32) * bm
    block_e = jnp.minimum(jnp.sum((pad_end[None, :] <= block_start[:, None]).astype(jnp.int32), axis=1),
                          N_EXPERTS - 1).astype(jnp.int32)
    n_used = (pad_end[-1] // bm).astype(jnp.int32).reshape(1)
    return block_e, n_used, dest.reshape(n_tokens, TOP_K)


def _layer(xp, xs, c_all, cache_k, cache_v, cache_logf, state_conv, state_h, p, norm_f):
    Bp, Tp, _ = xp.shape
    Bs, Ts, _ = xs.shape
    past = cache_k.shape[1]
    np_rows, ns_rows = Bp * Tp, Bs * Ts
    assert Tp % ROW_TILE == 0 and ns_rows == ROW_TILE and Tp % ATTN_BLOCK == 0

    mod = _mod(c_all, p['w_mod'], p['b_mod'])
    mod_p = mod[:Bp].reshape(Bp, 1, 6 * D_MODEL)
    mod_s = jnp.repeat(mod[Bp:], Ts, axis=0).reshape(1, ns_rows, 6 * D_MODEL)

    w_in = p['w_in']
    d3 = 3 * D_MODEL
    wcat = jnp.concatenate([w_in[:, :D_MODEL] * (LOG2E * HEAD_DIM ** -0.5), w_in[:, D_MODEL:d3],
                            w_in[:, d3 + N_HEADS:], p['w_gate']], axis=1).astype(BF16)
    wfl = jnp.pad(w_in[:, d3:d3 + N_HEADS], ((0, 0), (0, LANES - N_HEADS))).astype(BF16)
    wflt = wfl.T
    eye = jnp.eye(GATE_TILE // (D_MODEL // N_RNN_BLOCKS), dtype=F32)

    def gate_tiles(w):
        per = GATE_TILE // w.shape[1]
        w4 = w.reshape(N_RNN_BLOCKS // per, per, w.shape[1], w.shape[2])
        return jnp.einsum('gpcd,pq->gpcqd', w4, eye).reshape(N_RNN_BLOCKS // per, GATE_TILE, GATE_TILE).astype(BF16)

    wr4, wi4 = gate_tiles(p['w_r']), gate_tiles(p['w_i'])
    wpa, wpb, wo = p['w_pa'].astype(BF16), p['w_pb'].astype(BF16), p['w_out'].astype(BF16)
    wr = jnp.pad(jnp.concatenate([p['w_rg'], p['w_re']], axis=1), ((0, 0), (0, LANES - N_GROUPS - N_EXPERTS)))
    wr1 = wr.astype(BF16)
    wr2 = (wr - wr1.astype(F32)).astype(BF16)
    br = jnp.pad(jnp.concatenate([p['b_rg'], p['b_re']]), (0, LANES - N_GROUPS - N_EXPERTS)).reshape(1, LANES)

    in_p = _in_proj(xp.reshape(np_rows, D_MODEL), mod_p, p['norm1'], wcat[:, :d3], wfl, wflt, p['b_gate'], p['b_f'],
                    tiles_per_mod=Tp // IN_TILE, tiles_per_seq=Tp // IN_TILE, with_mixer=False)
    in_s = _in_proj(xs.reshape(ns_rows, D_MODEL), mod_s, p['norm1'], wcat, wfl, wflt, p['b_gate'], p['b_f'],
                    tiles_per_mod=1, tiles_per_seq=1, with_mixer=True)
    q_p, kb_p, vb_p, k_p, v_p, lf_p, _, fc_p = in_p
    q_s, kb_s, vb_s, k_s, v_s, xb_s, gb_s, ga_s, gg_s, lf_s, lft_s, _ = in_s
    r3 = lambda a, B, T: a.reshape(B, T, D_MODEL)

    oa_p = _attn_prompt(r3(q_p, Bp, Tp), r3(kb_p, Bp, Tp), r3(vb_p, Bp, Tp), fc_p.reshape(Bp, Tp, N_HEADS))

    lfp3 = jnp.transpose(cache_logf.astype(F32), (0, 2, 1))
    lfn3 = lft_s.reshape(N_HEADS, Bs, Ts).transpose(1, 0, 2)
    oa_s = _attn_sample(r3(q_s, Bs, Ts), r3(kb_s, Bs, Ts), r3(vb_s, Bs, Ts),
                        cache_k.astype(BF16).reshape(Bs, past, D_MODEL),
                        cache_v.astype(BF16).reshape(Bs, past, D_MODEL), lfp3, lfn3)

    zc = jnp.zeros((Bp, SUBLANES, D_MODEL), F32)
    zh = jnp.zeros((Bp, 1, D_MODEL), F32)
    rnn_w = (p['conv_w'], p['conv_b'], wr4, wi4, p['b_r'], p['b_i'], p['rglru_lambda'])
    post_w = (wpa, wpb, wo, wr1, wr2, br)
    x1_p, h2_p, rt_p, conv_p, hl_p = _mix(xp, oa_p, mod_p, zc, zh, p['norm1'], p['norm2'], wcat[:, d3:], p['b_gate'],
                                          *rnn_w, *post_w)
    x1_p, h2_p, rt_p = (a.reshape(np_rows, a.shape[-1]) for a in (x1_p, h2_p, rt_p))

    sc = jnp.pad(state_conv.astype(F32), ((0, 0), (SUBLANES - (CONV_WIDTH - 1), 0), (0, 0)))
    ob_s, conv_s, hl_s = _rnn(r3(xb_s, Bs, Ts), r3(gb_s, Bs, Ts), sc, state_h.astype(F32).reshape(Bs, 1, D_MODEL),
                              *rnn_w)
    x1_s, h2_s, rt_s = _post(xs.reshape(ns_rows, D_MODEL), oa_s.reshape(ns_rows, D_MODEL),
                             ob_s.reshape(ns_rows, D_MODEL), ga_s, gg_s, mod_s, p['norm2'], *post_w, tiles_per_mod=1)

    n_tok = np_rows + ns_rows
    rt = jnp.concatenate([rt_p, rt_s], axis=0)
    block_e, n_used, dest = _routing_plan(rt[:, :TOP_K].astype(jnp.int32), n_tok)
    slots = jnp.zeros((block_e.shape[0] * MOE_BLOCK, D_MODEL // 2), jnp.uint32)
    slots = _dispatch(h2_p, dest.reshape(-1), slots, tile_offset=0)
    slots = _dispatch(h2_s, dest.reshape(-1), slots, tile_offset=np_rows // ROW_TILE)
    take = lambda a, idx: a.at[idx].get(mode='promise_in_bounds')
    yb = _moe(slots, block_e, n_used, p['w1'], p['w3'], p['w2'])
    ya0 = take(yb, dest[:, 0])
    ya1 = take(yb, dest[:, 1])

    y_p = _final(x1_p, ya0, ya1, rt, mod_p, norm_f, rows_per_mod=Tp, row_offset=0)
    y_s = _final(x1_s, ya0, ya1, rt, mod_s, norm_f, rows_per_mod=ns_rows, row_offset=np_rows)

    st = CONV_WIDTH - 1
    state_p = (k_p.reshape(Bp, Tp, N_HEADS, HEAD_DIM), v_p.reshape(Bp, Tp, N_HEADS, HEAD_DIM),
               lf_p.reshape(Bp, Tp, N_HEADS), conv_p[:, SUBLANES - st:], hl_p.reshape(Bp, D_MODEL))
    state_s = (k_s.reshape(Bs, Ts, N_HEADS, HEAD_DIM), v_s.reshape(Bs, Ts, N_HEADS, HEAD_DIM),
               lf_s.reshape(Bs, Ts, N_HEADS), conv_s[:, SUBLANES - st:], hl_s.reshape(Bs, D_MODEL))
    return y_p.reshape(Bp, Tp, D_MODEL), y_s.reshape(Bs, Ts, D_MODEL), state_p, state_s


def kernel(x_prompt, x_sample, cache_k, cache_v, cache_logf, state_conv, state_h, c_prompt, c_sample, w_mod, b_mod, norm1, w_in, b_f, conv_w, conv_b, w_r, b_r, w_i, b_i, rglru_lambda, w_gate, b_gate, w_pa, w_pb, w_out, norm2, w_rg, b_rg, w_re, b_re, w1, w3, w2, norm_f):
    assert w_mod.shape[0] == 1, "single-layer trunk: the final norm is fused into the layer's last stage"
    names = ('w_mod', 'b_mod', 'norm1', 'w_in', 'b_f', 'conv_w', 'conv_b', 'w_r', 'b_r', 'w_i', 'b_i',
             'rglru_lambda', 'w_gate', 'b_gate', 'w_pa', 'w_pb', 'w_out', 'norm2', 'w_rg', 'b_rg', 'w_re', 'b_re',
             'w1', 'w3', 'w2')
    vals = (w_mod, b_mod, norm1, w_in, b_f, conv_w, conv_b, w_r, b_r, w_i, b_i, rglru_lambda, w_gate, b_gate,
            w_pa, w_pb, w_out, norm2, w_rg, b_rg, w_re, b_re, w1, w3, w2)
    p = {n: v[0] for n, v in zip(names, vals)}
    c_all = jnp.concatenate([c_prompt, c_sample], axis=0)
    y_p, y_s, sp, ss = _layer(x_prompt, x_sample, c_all, cache_k[0], cache_v[0], cache_logf[0],
                              state_conv[0], state_h[0], p, norm_f)
    lead = lambda a: a[None]
    return (y_p, y_s, lead(sp[0]), lead(sp[1]), lead(sp[2]), lead(sp[3]), lead(sp[4]),
            lead(ss[0]), lead(ss[1]), lead(ss[2]), lead(ss[3]), lead(ss[4]))
```

```python
import functools

import jax
import jax.numpy as jnp
from jax import lax
from jax.experimental import pallas as pl
from jax.experimental.pallas import tpu as pltpu

F32 = jnp.float32
BF16 = jnp.bfloat16

D_MODEL = 1024
N_HEADS = 16
HEAD_DIM = 64
HEAD_PAIR = 2 * HEAD_DIM
N_PAIRS = N_HEADS // 2
N_RNN_BLOCKS = 16
CONV_WIDTH = 4
RGLRU_C = 8.0
N_GROUPS = 4
EXPERTS_PER_GROUP = 8
N_EXPERTS = N_GROUPS * EXPERTS_PER_GROUP
TOP_K = 2
D_EXPERT = 512
EPS = 1e-6

LANES = 128
SUBLANES = 8
ROW_TILE = 512
FINAL_TILE = 1024
IN_TILE = 256
ATTN_BLOCK = 512
ATTN_QBLOCK = 1024
ATTN_VROWS = HEAD_DIM + 16
RNN_TILE = 256
MIX_TILE = 512
MOE_BLOCK = 512
MOE_SUB = 256
GATE_TILE = 256
VMEM_LIMIT = 56 * 1024 * 1024
NEG_BIG = -1e30
LOG2E = 1.4426950408889634


def _cparams(sem):
    return pltpu.CompilerParams(dimension_semantics=sem, vmem_limit_bytes=VMEM_LIMIT)


def _resident(shape):
    nd = len(shape)
    return pl.BlockSpec(shape, lambda *_: (0,) * nd, pipeline_mode=pl.Buffered(1))


def _split3(x):
    p1 = x.astype(BF16)
    r1 = x - p1.astype(F32)
    p2 = r1.astype(BF16)
    p3 = (r1 - p2.astype(F32)).astype(BF16)
    return p1, p2, p3


def _dot(a, b):
    return jnp.dot(a, b, preferred_element_type=F32)


def _dot_nt(a, b):
    return lax.dot_general(a, b, (((1,), (1,)), ((), ())), preferred_element_type=F32)


def _sigmoid(x):
    return 0.5 * jnp.tanh(0.5 * x) + 0.5


def _gelu_tanh(x):
    return 0.5 * x * (1.0 + jnp.tanh(0.7978845608028654 * (x + 0.044715 * x * x * x)))


def _rms(x, g):
    ms = jnp.mean(x * x, axis=-1, keepdims=True)
    return x * lax.rsqrt(ms + EPS) * g


def _pack_pairs(x):
    n = x.shape[1] // 2
    hi = pltpu.bitcast(x[:, :n].astype(BF16).astype(F32), jnp.uint32)
    lo = pltpu.bitcast(x[:, n:].astype(BF16).astype(F32), jnp.uint32)
    return hi | lax.shift_right_logical(lo, jnp.uint32(16))


def _unpack_pairs(w):
    hi = pltpu.bitcast(w & jnp.uint32(0xFFFF0000), F32)
    lo = pltpu.bitcast(lax.shift_left(w, jnp.uint32(16)), F32)
    return jnp.concatenate([hi, lo], axis=1)


def _div_p2(x, n):
    assert n & (n - 1) == 0
    return lax.shift_right_logical(x, jnp.int32(n.bit_length() - 1))


def _mod_p2(x, n):
    assert n & (n - 1) == 0
    return x & (n - 1)


def _upper_tri(n):
    r = lax.broadcasted_iota(jnp.int32, (n, n), 0)
    c = lax.broadcasted_iota(jnp.int32, (n, n), 1)
    return jnp.where(r <= c, 1.0, 0.0).astype(BF16)


def _cumsum_lanes(x, u):
    p1, p2, p3 = _split3(x)
    return _dot(p1, u) + _dot(p2, u) + _dot(p3, u)


def _mod_kernel(c_ref, w_ref, b_ref, o_ref):
    c = c_ref[...]
    s = c * _sigmoid(c)
    s1 = s.astype(BF16)
    s2 = (s - s1.astype(F32)).astype(BF16)
    w = w_ref[...]
    w1 = w.astype(BF16)
    w2 = (w - w1.astype(F32)).astype(BF16)
    o_ref[...] = _dot(s1, w1) + _dot(s1, w2) + _dot(s2, w1) + b_ref[...]


def _mod(c, w, b):
    n = c.shape[0]
    cols = w.shape[1]
    return pl.pallas_call(
        _mod_kernel,
        grid=(cols // D_MODEL,),
        in_specs=[pl.BlockSpec((n, D_MODEL), lambda j: (0, 0)),
                  pl.BlockSpec((D_MODEL, D_MODEL), lambda j: (0, j)),
                  pl.BlockSpec((1, D_MODEL), lambda j: (0, j))],
        out_specs=pl.BlockSpec((n, D_MODEL), lambda j: (0, j)),
        out_shape=jax.ShapeDtypeStruct((n, cols), F32),
        compiler_params=_cparams(("arbitrary",)),
        name="mod",
    )(c, w, b.reshape(1, cols))


def _in_kernel(x_ref, sh_ref, sc_ref, n1_ref, w_ref, wfl_ref, wflt_ref, bg_ref, bf_ref, bft_ref, *refs,
               tiles_per_seq, with_mixer):
    q_ref, kb_ref, vb_ref, k5_ref, v5_ref = refs[:5]
    lf_ref, lft_ref, fc_ref, carry_ref = refs[-4:]
    x = x_ref[...]
    h = (_rms(x, n1_ref[...]) * (1.0 + sc_ref[...]) + sh_ref[...]).astype(BF16)

    def mm(c):
        return _dot(h, w_ref[:, c * D_MODEL:(c + 1) * D_MODEL])

    def head_major(y):
        slabs = []
        for p in range(N_PAIRS):
            pair = y[:, p * HEAD_PAIR:(p + 1) * HEAD_PAIR]
            slabs += [pair, pltpu.roll(pair, HEAD_DIM, 1)]
        return jnp.swapaxes(jnp.stack(slabs, axis=0), 0, 1)[:, :, :HEAD_DIM]

    q_ref[...] = mm(0).astype(BF16)
    k = mm(1)
    kb_ref[...] = k.astype(BF16)
    k5_ref[...] = head_major(k)
    v = mm(2)
    vb_ref[...] = v.astype(BF16)
    v5_ref[...] = head_major(v)
    if with_mixer:
        xb_ref, gb_ref, ga_ref, gg_ref = refs[5:9]
        xb_ref[...] = mm(3).astype(BF16)
        gb_ref[...] = mm(4).astype(BF16)
        ga_ref[...] = _sigmoid(mm(5) + bg_ref[:, :D_MODEL]).astype(BF16)
        gg_ref[...] = _sigmoid(mm(6) + bg_ref[:, D_MODEL:]).astype(BF16)

    def log_sigmoid(z):
        return jnp.minimum(z, 0.0) - jnp.log(1.0 + jnp.exp(-jnp.abs(z)))

    lf = log_sigmoid(_dot(h, wfl_ref[...]) + bf_ref[...])
    lf_ref[...] = lf[:, :N_HEADS]
    lft_ref[...] = log_sigmoid(_dot_nt(wflt_ref[...], h)[:N_HEADS, :] + bft_ref[...])

    @pl.when(pl.program_id(0) % tiles_per_seq == 0)
    def _():
        carry_ref[...] = jnp.zeros_like(carry_ref)

    tm = x.shape[0]
    r = lax.broadcasted_iota(jnp.int32, (tm, tm), 0)
    c = lax.broadcasted_iota(jnp.int32, (tm, tm), 1)
    lower = jnp.where(c <= r, 1.0, 0.0).astype(BF16)
    p1, p2, p3 = _split3(lf)
    fc = _dot(lower, p1) + _dot(lower, p2) + _dot(lower, p3) + carry_ref[...]
    fc_ref[...] = fc[:, :N_HEADS]
    carry_ref[...] = fc[tm - 1:tm, :]


def _in_proj(x2, mod3, norm1, wcat, wfl, wflt, b_gate, b_f, *, tiles_per_mod, tiles_per_seq, with_mixer):
    rows = x2.shape[0]
    tm = IN_TILE
    nt = rows // tm
    row_blk = pl.BlockSpec((tm, D_MODEL), lambda i: (i, 0))
    head_blk = pl.BlockSpec((tm, N_HEADS, HEAD_DIM), lambda i: (i, 0, 0))
    if mod3.shape[1] == 1:
        modc = lambda c: pl.BlockSpec((None, 1, D_MODEL), lambda i: (i // tiles_per_mod, 0, c))
    else:
        modc = lambda c: pl.BlockSpec((None, tm, D_MODEL), lambda i: (0, i, c))
    outs = [jax.ShapeDtypeStruct((rows, D_MODEL), BF16),
            jax.ShapeDtypeStruct((rows, D_MODEL), BF16),
            jax.ShapeDtypeStruct((rows, D_MODEL), BF16),
            jax.ShapeDtypeStruct((rows, N_HEADS, HEAD_DIM), F32),
            jax.ShapeDtypeStruct((rows, N_HEADS, HEAD_DIM), F32),
            *([jax.ShapeDtypeStruct((rows, D_MODEL), BF16)] * (4 if with_mixer else 0)),
            jax.ShapeDtypeStruct((rows, N_HEADS), F32),
            jax.ShapeDtypeStruct((N_HEADS, rows), F32),
            jax.ShapeDtypeStruct((rows, N_HEADS), F32)]
    return pl.pallas_call(
        functools.partial(_in_kernel, tiles_per_seq=tiles_per_seq, with_mixer=with_mixer),
        grid=(nt,),
        in_specs=[row_blk, modc(0), modc(1),
                  _resident((1, D_MODEL)),
                  _resident(wcat.shape),
                  _resident(wfl.shape),
                  _resident(wflt.shape),
                  _resident((1, 2 * D_MODEL)),
                  _resident((1, LANES)),
                  _resident((N_HEADS, 1))],
        out_specs=[row_blk] * 3 + [head_blk] * 2 + [row_blk] * (4 if with_mixer else 0)
                  + [pl.BlockSpec((tm, N_HEADS), lambda i: (i, 0)),
                     pl.BlockSpec((N_HEADS, tm), lambda i: (0, i)),
                     pl.BlockSpec((tm, N_HEADS), lambda i: (i, 0))],
        out_shape=outs,
        scratch_shapes=[pltpu.VMEM((1, LANES), F32)],
        compiler_params=_cparams(("arbitrary",)),
        name="in_proj",
    )(x2, mod3, mod3, norm1.reshape(1, D_MODEL), wcat, wfl, wflt,
      b_gate.reshape(1, 2 * D_MODEL), jnp.pad(b_f, (0, LANES - N_HEADS)).reshape(1, LANES),
      b_f.reshape(N_HEADS, 1))


def _attn_kernel(q_ref, k_ref, v_ref, f_ref, o_ref, kb_ref, vt_ref, qt_ref, s0_ref, m_ref, acc_ref):
    qi = pl.program_id(2)
    bq = q_ref.shape[0]
    bk = ATTN_BLOCK
    nk = k_ref.shape[0] // bk
    own_lanes = (lambda i: i < HEAD_DIM, lambda i: i >= HEAD_DIM)
    bias_at = (HEAD_DIM, 0)

    @pl.when(qi == 0)
    def _():
        lane = lax.broadcasted_iota(jnp.int32, (bk, HEAD_PAIR), 1)
        ones_rows = jnp.where(lax.broadcasted_iota(jnp.int32, (ATTN_VROWS - HEAD_DIM, bk), 0) == 0, 1.0, 0.0)
        for c in range(nk):
            blk = slice(c * bk, (c + 1) * bk)
            k = k_ref[blk, :].astype(F32)
            f = f_ref[blk, :] * LOG2E
            head = lax.broadcasted_iota(jnp.int32, f.shape, 1)
            for hh in range(2):
                mine = head == 2 * pl.program_id(1) + hh
                f_head = jnp.sum(jnp.where(mine, f, 0.0), axis=1, keepdims=True)
                slab = jnp.zeros(k.shape, F32)
                for n, piece in enumerate(_split3(f_head)):
                    slab = jnp.where(lane == bias_at[hh] + n, piece.astype(F32), slab)
                kb_ref[hh, blk, :] = jnp.where(own_lanes[hh](lane), k, slab).astype(BF16)
            vt = v_ref[blk, :].astype(F32).T
            for hh in range(2):
                vt_ref[c, hh, 0:HEAD_DIM] = vt[hh * HEAD_DIM:(hh + 1) * HEAD_DIM].astype(BF16)
                vt_ref[c, hh, HEAD_DIM:ATTN_VROWS] = ones_rows.astype(BF16)

    qt = q_ref[...].astype(F32).T
    feat = lax.broadcasted_iota(jnp.int32, qt.shape, 0)
    for hh in range(2):
        is_bias = (feat >= bias_at[hh]) & (feat < bias_at[hh] + 3)
        qt_ref[hh] = jnp.where(own_lanes[hh](feat), qt, jnp.where(is_bias, -1.0, 0.0)).astype(BF16)

    m_ref[...] = jnp.full(m_ref.shape, NEG_BIG, F32)
    acc_ref[...] = jnp.zeros(acc_ref.shape, F32)

    nsub = bq // bk

    def scores(hh, u, j):
        keys = kb_ref[hh, pl.ds(pl.multiple_of(j * bk, bk), bk), :]
        return _dot(keys, qt_ref[hh, :, u * bk:(u + 1) * bk])

    def softmax_pv(hh, u, j, s, masked):
        cols = slice(u * bk, (u + 1) * bk)
        if masked:
            key = lax.broadcasted_iota(jnp.int32, s.shape, 0)
            qry = lax.broadcasted_iota(jnp.int32, s.shape, 1)
            s = jnp.where(key <= qry, s, NEG_BIG)
        m_prev = m_ref[hh, :, cols]
        m_new = jnp.maximum(m_prev, jnp.max(s, axis=0, keepdims=True))
        p = jnp.exp2(s - m_new).astype(BF16)
        acc_ref[hh, :, cols] = jnp.exp2(m_prev - m_new) * acc_ref[hh, :, cols] + _dot(vt_ref[j, hh], p)
        m_ref[hh, :, cols] = m_new

    def run(items, first_of_next):
        s = s0_ref[...]
        for n, (hh, u, j, masked) in enumerate(items):
            if n + 1 < len(items):
                nxt = scores(*items[n + 1][:3])
            elif first_of_next is not None:
                s0_ref[...] = scores(*first_of_next)
            softmax_pv(hh, u, j, s, masked)
            s = nxt if n + 1 < len(items) else None

    def body(j, carry):
        run([(hh, u, j, False) for u in range(nsub) for hh in range(2)], (0, 0, j + 1))
        return carry

    s0_ref[...] = scores(0, 0, 0)
    lax.fori_loop(0, nsub * qi, body, 0)
    for d in range(nsub):
        j = nsub * qi + d
        items = [(hh, u, j, u == d) for u in range(d, nsub) for hh in range(2)]
        run(items, (0, d + 1, j + 1) if d + 1 < nsub else None)

    heads = [acc_ref[hh, 0:HEAD_DIM] / acc_ref[hh, HEAD_DIM:HEAD_DIM + 1] for hh in range(2)]
    o_ref[...] = jnp.concatenate(heads, axis=0).T.astype(BF16)


def _attn_prompt(q3, k3, v3, fc3):
    B, T, _ = q3.shape
    bq, bk = ATTN_QBLOCK, ATTN_BLOCK
    nq, nk = T // bq, T // bk
    return pl.pallas_call(
        _attn_kernel,
        grid=(B, N_PAIRS, nq),
        in_specs=[pl.BlockSpec((None, bq, HEAD_PAIR), lambda b, p, i: (b, i, p)),
                  pl.BlockSpec((None, T, HEAD_PAIR), lambda b, p, i: (b, 0, p)),
                  pl.BlockSpec((None, T, HEAD_PAIR), lambda b, p, i: (b, 0, p)),
                  pl.BlockSpec((None, T, N_HEADS), lambda b, p, i: (b, 0, 0))],
        out_specs=pl.BlockSpec((None, bq, HEAD_PAIR), lambda b, p, i: (b, i, p)),
        out_shape=jax.ShapeDtypeStruct((B, T, D_MODEL), BF16),
        scratch_shapes=[pltpu.VMEM((2, T, HEAD_PAIR), BF16),
                        pltpu.VMEM((nk, 2, ATTN_VROWS, bk), BF16),
                        pltpu.VMEM((2, HEAD_PAIR, bq), BF16),
                        pltpu.VMEM((bk, bk), F32),
                        pltpu.VMEM((2, 1, bq), F32),
                        pltpu.VMEM((2, ATTN_VROWS, bq), F32)],
        compiler_params=_cparams(("arbitrary", "arbitrary", "arbitrary")),
        name="attn_prompt",
    )(q3, k3, v3, fc3)


def _attn_sample_kernel(q_ref, kn_ref, vn_ref, kc_ref, vc_ref, lfp_ref, lfn_ref, o_ref):
    tq = q_ref.shape[0]
    past = kc_ref.shape[0]
    nrow = N_HEADS * tq
    row = lax.broadcasted_iota(jnp.int32, (nrow, D_MODEL), 0)
    col = lax.broadcasted_iota(jnp.int32, (nrow, D_MODEL), 1)
    own = _div_p2(row, tq) == _div_p2(col, HEAD_DIM)

    rr = lax.broadcasted_iota(jnp.int32, (nrow, tq), 0)
    rc = lax.broadcasted_iota(jnp.int32, (nrow, tq), 1)
    rep = jnp.where(_mod_p2(rr, tq) == rc, 1.0, 0.0).astype(BF16)
    qx = jnp.where(own, _dot(rep, q_ref[...]), 0.0).astype(BF16)

    fp = _cumsum_lanes(lfp_ref[...] * LOG2E, _upper_tri(past))
    fn = _cumsum_lanes(lfn_ref[...] * LOG2E, _upper_tri(tq)) + fp[:, past - 1:past]
    hr = lax.broadcasted_iota(jnp.int32, (nrow, N_HEADS), 0)
    hc = lax.broadcasted_iota(jnp.int32, (nrow, N_HEADS), 1)
    hsel = jnp.where(_div_p2(hr, tq) == hc, 1.0, 0.0).astype(BF16)

    def expand(f):
        p1, p2, p3 = _split3(f)
        return _dot(hsel, p1) + _dot(hsel, p2) + _dot(hsel, p3)

    s_p = _dot_nt(qx, kc_ref[...].astype(BF16)) - expand(fp)
    s_n = _dot_nt(qx, kn_ref[...]) - expand(fn)
    nr = lax.broadcasted_iota(jnp.int32, (nrow, tq), 0)
    nc = lax.broadcasted_iota(jnp.int32, (nrow, tq), 1)
    s_n = jnp.where(nc <= _mod_p2(nr, tq), s_n, NEG_BIG)
    m = jnp.maximum(jnp.max(s_p, axis=-1, keepdims=True), jnp.max(s_n, axis=-1, keepdims=True))
    p_p = jnp.exp2(s_p - m)
    p_n = jnp.exp2(s_n - m)
    l = jnp.sum(p_p, axis=-1, keepdims=True) + jnp.sum(p_n, axis=-1, keepdims=True)
    o_full = _dot(p_p.astype(BF16), vc_ref[...].astype(BF16)) + _dot(p_n.astype(BF16), vn_ref[...])
    o_own = jnp.where(own, o_full / l, 0.0).astype(BF16)
    cr = lax.broadcasted_iota(jnp.int32, (tq, nrow), 0)
    cc = lax.broadcasted_iota(jnp.int32, (tq, nrow), 1)
    col_sel = jnp.where(_mod_p2(cc, tq) == cr, 1.0, 0.0).astype(BF16)
    o_ref[...] = _dot(col_sel, o_own).astype(BF16)


def _attn_sample(q3, kn3, vn3, kc3, vc3, lfp3, lfn3):
    B, tq, _ = q3.shape
    past = kc3.shape[1]
    blk = lambda n, w: pl.BlockSpec((None, n, w), lambda b: (b, 0, 0))
    return pl.pallas_call(
        _attn_sample_kernel,
        grid=(B,),
        in_specs=[blk(tq, D_MODEL), blk(tq, D_MODEL), blk(tq, D_MODEL),
                  blk(past, D_MODEL), blk(past, D_MODEL), blk(N_HEADS, past), blk(N_HEADS, tq)],
        out_specs=blk(tq, D_MODEL),
        out_shape=jax.ShapeDtypeStruct((B, tq, D_MODEL), BF16),
        compiler_params=_cparams(("arbitrary",)),
        name="attn_sample",
    )(q3, kn3, vn3, kc3, vc3, lfp3, lfn3)


def _rglru_gates(xb, xp_ref, cw_ref, cb_ref, wr_ref, wi_ref):
    tt = xb.shape[0]
    pad = SUBLANES
    xp_ref[pad:pad + tt, :] = xb
    xc = cb_ref[...] + xb * cw_ref[CONV_WIDTH - 1:CONV_WIDTH, :]
    for k in range(CONV_WIDTH - 1):
        back = CONV_WIDTH - 1 - k
        xc = xc + xp_ref[pad - back:pad - back + tt, :] * cw_ref[k:k + 1, :]
    tail = xp_ref[tt:tt + pad, :]
    xp_ref[0:pad, :] = tail

    xcb = xc.astype(BF16)
    ng = D_MODEL // GATE_TILE
    r = jnp.concatenate([_dot(xcb[:, g * GATE_TILE:(g + 1) * GATE_TILE], wr_ref[g]) for g in range(ng)], axis=-1)
    i = jnp.concatenate([_dot(xcb[:, g * GATE_TILE:(g + 1) * GATE_TILE], wi_ref[g]) for g in range(ng)], axis=-1)
    return xc, r, i, tail


def _rglru_scan(xc, r, i, gb, hc_ref, br_ref, bi_ref, lam_ref):
    tt = xc.shape[0]
    r = _sigmoid(r + br_ref[...])
    i = _sigmoid(i + bi_ref[...])
    lam = lam_ref[...]
    softplus_neg = jnp.maximum(-lam, 0.0) + jnp.log(1.0 + jnp.exp(-jnp.abs(lam)))
    log_a = (-RGLRU_C) * r * softplus_neg
    a = jnp.exp(log_a)
    th = jnp.tanh(log_a)
    mult = jnp.sqrt(-2.0 * th / (1.0 - th))
    b = mult * (i * xc)

    ng8 = tt // SUBLANES
    a3 = a.reshape(ng8, SUBLANES, D_MODEL)
    b3 = b.reshape(ng8, SUBLANES, D_MODEL)
    sub = lax.broadcasted_iota(jnp.int32, a3.shape, 1)
    for s in (1, 2, 4):
        keep = sub >= s
        a_sh = jnp.where(keep, pltpu.roll(a3, s, 1), 1.0)
        b_sh = jnp.where(keep, pltpu.roll(b3, s, 1), 0.0)
        b3 = a3 * b_sh + b3
        a3 = a3 * a_sh
    hc = hc_ref[...]
    hs = []
    for g in range(ng8):
        hg = b3[g] + a3[g] * hc
        hs.append(hg)
        hc = hg[SUBLANES - 1:SUBLANES, :]
    hc_ref[...] = hc
    return jnp.concatenate(hs, axis=0) * _gelu_tanh(gb), hc


def _rnn_kernel(xb_ref, gb_ref, c0_ref, h0_ref, cw_ref, cb_ref, wr_ref, wi_ref, br_ref, bi_ref, lam_ref,
                o_ref, cout_ref, hout_ref, xp_ref, hc_ref):
    ti = pl.program_id(1)

    @pl.when(ti == 0)
    def _():
        xp_ref[0:SUBLANES, :] = c0_ref[...]
        hc_ref[...] = h0_ref[...]

    xc, r, i, tail = _rglru_gates(xb_ref[...].astype(F32), xp_ref, cw_ref, cb_ref, wr_ref, wi_ref)
    ob, hc = _rglru_scan(xc, r, i, gb_ref[...].astype(F32), hc_ref, br_ref, bi_ref, lam_ref)
    o_ref[...] = ob.astype(BF16)

    @pl.when(ti == pl.num_programs(1) - 1)
    def _():
        cout_ref[...] = tail
        hout_ref[...] = hc


def _rnn(xb3, gb3, conv0, h0, conv_w, conv_b, wr4, wi4, b_r, b_i, lam):
    B, T, _ = xb3.shape
    tt = min(RNN_TILE, T)
    vec = lambda a: a.reshape(1, D_MODEL)
    seq = pl.BlockSpec((None, tt, D_MODEL), lambda b, t: (b, t, 0))
    st8 = pl.BlockSpec((None, SUBLANES, D_MODEL), lambda b, t: (b, 0, 0))
    st1 = pl.BlockSpec((None, 1, D_MODEL), lambda b, t: (b, 0, 0))
    return pl.pallas_call(
        _rnn_kernel,
        grid=(B, T // tt),
        in_specs=[seq, seq, st8, st1,
                  _resident((CONV_WIDTH, D_MODEL)), _resident((1, D_MODEL)),
                  _resident(wr4.shape), _resident(wi4.shape),
                  _resident((1, D_MODEL)), _resident((1, D_MODEL)), _resident((1, D_MODEL))],
        out_specs=[seq, st8, st1],
        out_shape=[jax.ShapeDtypeStruct((B, T, D_MODEL), BF16),
                   jax.ShapeDtypeStruct((B, SUBLANES, D_MODEL), F32),
                   jax.ShapeDtypeStruct((B, 1, D_MODEL), F32)],
        scratch_shapes=[pltpu.VMEM((tt + SUBLANES, D_MODEL), F32),
                        pltpu.VMEM((1, D_MODEL), F32)],
        compiler_params=_cparams(("arbitrary", "arbitrary")),
        name="rnn",
    )(xb3, gb3, conv0, h0, conv_w, vec(conv_b), wr4, wi4, vec(b_r), vec(b_i), vec(lam))


def _merge_route_tile(x, pa, ob, ga, gg, g1, sh2, sc2, n2_ref, wpb_ref, wo_ref, wr1_ref, wr2_ref, br_ref):
    mix = ga * pa + gg * _dot(ob, wpb_ref[...])
    x1 = x + g1 * _dot(mix.astype(BF16), wo_ref[...])
    h2 = _rms(x1, n2_ref[...]) * (1.0 + sc2) + sh2
    h2a = h2.astype(BF16)
    h2b = (h2 - h2a.astype(F32)).astype(BF16)
    lg = _dot(h2a, wr1_ref[...]) + _dot(h2a, wr2_ref[...]) + _dot(h2b, wr1_ref[...]) + br_ref[...]

    lane = lax.broadcasted_iota(jnp.int32, lg.shape, 1).astype(F32)
    big = float(LANES)
    is_g = lane < N_GROUPS
    gl = jnp.where(is_g, lg, NEG_BIG)
    mg = jnp.max(gl, axis=-1, keepdims=True)
    sg = jnp.sum(jnp.where(is_g, jnp.exp(gl - mg), 0.0), axis=-1, keepdims=True)
    pg = 1.0 / sg
    gidx = jnp.min(jnp.where(is_g & (gl == mg), lane, big), axis=-1, keepdims=True)
    lo = N_GROUPS + EXPERTS_PER_GROUP * gidx
    ing = (lane >= lo) & (lane < lo + EXPERTS_PER_GROUP)
    el = jnp.where(ing, lg, NEG_BIG)
    m1 = jnp.max(el, axis=-1, keepdims=True)
    se = jnp.sum(jnp.where(ing, jnp.exp(el - m1), 0.0), axis=-1, keepdims=True)
    i1 = jnp.min(jnp.where(ing & (el == m1), lane, big), axis=-1, keepdims=True)
    el2 = jnp.where(lane == i1, NEG_BIG, el)
    m2 = jnp.max(el2, axis=-1, keepdims=True)
    i2 = jnp.min(jnp.where(ing & (el2 == m2) & (lane != i1), lane, big), axis=-1, keepdims=True)
    p1 = 1.0 / se
    p2 = jnp.exp(m2 - m1) / se
    den = p1 + p2
    w1 = pg * p1 / den
    w2 = pg * p2 / den
    rt = jnp.where(lane == 0.0, i1 - N_GROUPS,
                   jnp.where(lane == 1.0, i2 - N_GROUPS,
                             jnp.where(lane == 2.0, w1, jnp.where(lane == 3.0, w2, 0.0))))
    return x1, h2, rt


def _post_kernel(x_ref, oa_ref, ob_ref, ga_ref, gg_ref, g1_ref, sh_ref, sc_ref, n2_ref,
                 wpa_ref, wpb_ref, wo_ref, wr1_ref, wr2_ref, br_ref,
                 x1_ref, h2_ref, rt_ref):
    x1, h2, rt = _merge_route_tile(x_ref[...], _dot(oa_ref[...], wpa_ref[...]), ob_ref[...],
                                   ga_ref[...].astype(F32), gg_ref[...].astype(F32), g1_ref[...], sh_ref[...],
                                   sc_ref[...], n2_ref, wpb_ref, wo_ref, wr1_ref, wr2_ref, br_ref)
    x1_ref[...] = x1
    h2_ref[...] = _pack_pairs(h2)
    rt_ref[...] = rt


def _post(x2, oa2, ob2, ga2, gg2, mod3, norm2, wpa, wpb, wo, wr1, wr2, br, *, tiles_per_mod):
    rows = x2.shape[0]
    tm = ROW_TILE
    mr = mod3.shape[1]
    row = pl.BlockSpec((tm, D_MODEL), lambda i: (i, 0))
    modc = lambda c: pl.BlockSpec((None, mr, D_MODEL), lambda i: (i // tiles_per_mod, 0, c))
    return pl.pallas_call(
        _post_kernel,
        grid=(rows // tm,),
        in_specs=[row, row, row, row, row, modc(2), modc(3), modc(4), _resident((1, D_MODEL)),
                  _resident(wpa.shape), _resident(wpb.shape), _resident(wo.shape),
                  _resident(wr1.shape), _resident(wr2.shape), _resident((1, LANES))],
        out_specs=[row, pl.BlockSpec((tm, D_MODEL // 2), lambda i: (i, 0)), pl.BlockSpec((tm, LANES), lambda i: (i, 0))],
        out_shape=[jax.ShapeDtypeStruct((rows, D_MODEL), F32),
                   jax.ShapeDtypeStruct((rows, D_MODEL // 2), jnp.uint32),
                   jax.ShapeDtypeStruct((rows, LANES), F32)],
        compiler_params=_cparams(("arbitrary",)),
        name="post",
    )(x2, oa2, ob2, ga2, gg2, mod3, mod3, mod3, norm2.reshape(1, D_MODEL), wpa, wpb, wo, wr1, wr2, br)


def _mix_kernel(x_ref, oa_ref, sh1_ref, sc1_ref, g1_ref, sh2_ref, sc2_ref, n1_ref, n2_ref, w_ref, bg_ref,
                c0_ref, h0_ref, cw_ref, cb_ref, wr_ref, wi_ref, br_ref, bi_ref, lam_ref,
                wpa_ref, wpb_ref, wo_ref, wr1_ref, wr2_ref, brt_ref,
                x1_ref, h2_ref, rt_ref, cout_ref, hout_ref, xp_ref, hc_ref):
    ti = pl.program_id(1)

    @pl.when(ti == 0)
    def _():
        xp_ref[0:SUBLANES, :] = c0_ref[...]
        hc_ref[...] = h0_ref[...]

    def project(rows):
        x = x_ref[rows, :]
        h = (_rms(x, n1_ref[...]) * (1.0 + sc1_ref[...]) + sh1_ref[...]).astype(BF16)
        mm = lambda c: _dot(h, w_ref[:, c * D_MODEL:(c + 1) * D_MODEL])
        xc, r, i, tail = _rglru_gates(mm(0), xp_ref, cw_ref, cb_ref, wr_ref, wi_ref)
        return dict(x=x, xc=xc, r=r, i=i, tail=tail, gb=mm(1), ga=mm(2), gg=mm(3),
                    pa=_dot(oa_ref[rows, :], wpa_ref[...]))

    def recur(st):
        ob, hc = _rglru_scan(st['xc'], st['r'], st['i'], st['gb'], hc_ref, br_ref, bi_ref, lam_ref)
        return ob.astype(BF16), hc

    def merge(rows, st, ob):
        ga = _sigmoid(st['ga'] + bg_ref[:, :D_MODEL])
        gg = _sigmoid(st['gg'] + bg_ref[:, D_MODEL:])
        x1, h2, rt = _merge_route_tile(st['x'], st['pa'], ob, ga, gg, g1_ref[...], sh2_ref[...], sc2_ref[...],
                                       n2_ref, wpb_ref, wo_ref, wr1_ref, wr2_ref, brt_ref)
        x1_ref[rows, :] = x1
        h2_ref[rows, :] = _pack_pairs(h2)
        rt_ref[rows, :] = rt

    subs = [slice(s, s + RNN_TILE) for s in range(0, x_ref.shape[0], RNN_TILE)]
    states = [project(subs[0])]
    for n, rows in enumerate(subs):
        if n + 1 < len(subs):
            states.append(project(subs[n + 1]))
        ob, hc = recur(states[n])
        merge(rows, states[n], ob)

    @pl.when(ti == pl.num_programs(1) - 1)
    def _():
        cout_ref[...] = states[-1]['tail']
        hout_ref[...] = hc


def _mix(x3, oa3, mod3, conv0, h0, norm1, norm2, w_mix, b_gate, conv_w, conv_b, wr4, wi4, b_r, b_i, lam,
         wpa, wpb, wo, wr1, wr2, br):
    B, T, _ = x3.shape
    tt = MIX_TILE
    vec = lambda a: a.reshape(1, D_MODEL)
    seq = lambda w: pl.BlockSpec((None, tt, w), lambda b, t: (b, t, 0))
    modc = lambda c: pl.BlockSpec((None, 1, D_MODEL), lambda b, t: (b, 0, c))
    st8 = pl.BlockSpec((None, SUBLANES, D_MODEL), lambda b, t: (b, 0, 0))
    st1 = pl.BlockSpec((None, 1, D_MODEL), lambda b, t: (b, 0, 0))
    v1 = _resident((1, D_MODEL))
    return pl.pallas_call(
        _mix_kernel,
        grid=(B, T // tt),
        in_specs=[seq(D_MODEL), seq(D_MODEL), modc(0), modc(1), modc(2), modc(3), modc(4), v1, v1,
                  _resident(w_mix.shape), _resident((1, 2 * D_MODEL)), st8, st1,
                  _resident((CONV_WIDTH, D_MODEL)), v1, _resident(wr4.shape), _resident(wi4.shape), v1, v1, v1,
                  _resident(wpa.shape), _resident(wpb.shape), _resident(wo.shape),
                  _resident(wr1.shape), _resident(wr2.shape), _resident((1, LANES))],
        out_specs=[seq(D_MODEL), seq(D_MODEL // 2), seq(LANES), st8, st1],
        out_shape=[jax.ShapeDtypeStruct((B, T, D_MODEL), F32),
                   jax.ShapeDtypeStruct((B, T, D_MODEL // 2), jnp.uint32),
                   jax.ShapeDtypeStruct((B, T, LANES), F32),
                   jax.ShapeDtypeStruct((B, SUBLANES, D_MODEL), F32),
                   jax.ShapeDtypeStruct((B, 1, D_MODEL), F32)],
        scratch_shapes=[pltpu.VMEM((RNN_TILE + SUBLANES, D_MODEL), F32),
                        pltpu.VMEM((1, D_MODEL), F32)],
        compiler_params=_cparams(("arbitrary", "arbitrary")),
        name="mix",
    )(x3, oa3, mod3, mod3, mod3, mod3, mod3, vec(norm1), vec(norm2), w_mix, b_gate.reshape(1, 2 * D_MODEL),
      conv0, h0, conv_w, vec(conv_b), wr4, wi4, vec(b_r), vec(b_i), vec(lam), wpa, wpb, wo, wr1, wr2, br)


def _dispatch_kernel(dest_ref, x_ref, slots_in_ref, slots_ref, sem):
    del slots_in_ref
    tm = x_ref.shape[0]

    def row_copy(r, slot):
        return pltpu.make_async_copy(x_ref.at[pl.ds(r, 1)], slots_ref.at[pl.ds(slot, 1)], sem)

    def issue(r, carry):
        for k in range(TOP_K):
            row_copy(r, dest_ref[TOP_K * r + k]).start()
        return carry

    lax.fori_loop(0, tm, issue, 0, unroll=8)

    for _ in range(TOP_K):
        pltpu.make_async_copy(x_ref, slots_ref.at[pl.ds(0, tm)], sem).wait()


def _dispatch(x, dest, slots, *, tile_offset):
    rows, width = x.shape
    tm = ROW_TILE
    return pl.pallas_call(
        _dispatch_kernel,
        grid=(rows // tm,),
        in_specs=[pl.BlockSpec((tm * TOP_K,), lambda i: (i + tile_offset,), memory_space=pltpu.SMEM),
                  pl.BlockSpec((tm, width), lambda i: (i, 0)),
                  pl.BlockSpec(memory_space=pl.ANY)],
        out_specs=pl.BlockSpec(memory_space=pl.ANY),
        out_shape=jax.ShapeDtypeStruct(slots.shape, slots.dtype),
        scratch_shapes=[pltpu.SemaphoreType.DMA(())],
        input_output_aliases={2: 0},
        compiler_params=_cparams(("arbitrary",)),
        name="dispatch",
    )(dest, x, slots)


def _moe_kernel(be_ref, nb_ref, x_ref, w1_ref, w3_ref, w2_ref, o_ref, w1b_ref, w3b_ref, w2b_ref):
    i = pl.program_id(0)

    @pl.when((i == 0) | (be_ref[i] != be_ref[jnp.maximum(i - 1, 0)]))
    def _():
        w1b_ref[...] = w1_ref[...].astype(BF16)
        w3b_ref[...] = w3_ref[...].astype(BF16)
        w2b_ref[...] = w2_ref[...].astype(BF16)

    @pl.when(i < nb_ref[0])
    def _():
        subs = [slice(s, s + MOE_SUB) for s in range(0, x_ref.shape[0], MOE_SUB)]
        ups = []
        for rows in subs:
            x = _unpack_pairs(x_ref[rows, :]).astype(BF16)
            ups.append((_dot(x, w1b_ref[...]), _dot(x, w3b_ref[...])))
        for rows, (a, b) in zip(subs, ups):
            g = a * _sigmoid(a) * b
            o_ref[rows, :] = _pack_pairs(_dot(g.astype(BF16), w2b_ref[...]))

    @pl.when(i >= nb_ref[0])
    def _():
        o_ref[...] = jnp.zeros_like(o_ref)


def _moe(xs, block_e, n_used, w1, w3, w2):
    n_blocks = block_e.shape[0]
    bm = MOE_BLOCK
    wspec = lambda s: pl.BlockSpec((None,) + s, lambda i, be, nb: (be[i], 0, 0))
    gs = pltpu.PrefetchScalarGridSpec(
        num_scalar_prefetch=2,
        grid=(n_blocks,),
        in_specs=[pl.BlockSpec((bm, D_MODEL // 2), lambda i, be, nb: (i, 0)),
                  wspec((D_MODEL, D_EXPERT)), wspec((D_MODEL, D_EXPERT)), wspec((D_EXPERT, D_MODEL))],
        out_specs=pl.BlockSpec((bm, D_MODEL // 2), lambda i, be, nb: (i, 0)),
        scratch_shapes=[pltpu.VMEM((D_MODEL, D_EXPERT), BF16),
                        pltpu.VMEM((D_MODEL, D_EXPERT), BF16),
                        pltpu.VMEM((D_EXPERT, D_MODEL), BF16)],
    )
    return pl.pallas_call(
        _moe_kernel,
        grid_spec=gs,
        out_shape=jax.ShapeDtypeStruct((n_blocks * bm, D_MODEL // 2), jnp.uint32),
        compiler_params=_cparams(("arbitrary",)),
        name="moe",
    )(block_e, n_used, xs, w1, w3, w2)


def _final_kernel(x1_ref, ya_ref, yb_ref, rt_ref, g2_ref, nf_ref, o_ref):
    rt = rt_ref[...]
    moe = (rt[:, TOP_K:TOP_K + 1] * _unpack_pairs(ya_ref[...])
           + rt[:, TOP_K + 1:TOP_K + 2] * _unpack_pairs(yb_ref[...]))
    o_ref[...] = _rms(x1_ref[...] + g2_ref[...] * moe, nf_ref[...])


def _final(x1, ya, yb, rt, mod3, norm_f, *, rows_per_mod, row_offset):
    rows = x1.shape[0]
    tm = min(FINAL_TILE, rows)
    assert row_offset % tm == 0 and rows_per_mod % tm == 0
    tile_offset, tiles_per_mod = row_offset // tm, rows_per_mod // tm
    mr = mod3.shape[1]
    row = pl.BlockSpec((tm, D_MODEL), lambda i: (i, 0))
    tok = lambda w: pl.BlockSpec((tm, w), lambda i: (i + tile_offset, 0))
    return pl.pallas_call(
        _final_kernel,
        grid=(rows // tm,),
        in_specs=[row, tok(D_MODEL // 2), tok(D_MODEL // 2), tok(LANES),
                  pl.BlockSpec((None, mr, D_MODEL), lambda i: (i // tiles_per_mod, 0, 5)),
                  _resident((1, D_MODEL))],
        out_specs=row,
        out_shape=jax.ShapeDtypeStruct((rows, D_MODEL), F32),
        compiler_params=_cparams(("arbitrary",)),
        name="final",
    )(x1, ya, yb, rt, mod3, norm_f.reshape(1, D_MODEL))


def _routing_plan(eid, n_tokens):
    bm = MOE_BLOCK
    n_assign = n_tokens * TOP_K
    n_blocks = -(-n_assign // bm) + N_EXPERTS
    flat_e = eid.reshape(-1)
    onehot = (flat_e[:, None] == jnp.arange(N_EXPERTS, dtype=jnp.int32)[None, :]).astype(jnp.int32)
    csum = jnp.cumsum(onehot, axis=0)
    counts = csum[-1]
    padded = (counts + bm - 1) // bm * bm
    pad_end = jnp.cumsum(padded)
    pad_start = pad_end - padded
    dest = jnp.sum(onehot * (csum - 1 + pad_start[None, :]), axis=1).astype(jnp.int32)
    block_start = jnp.arange(n_blocks, dtype=jnp.int32) * bm
    block_e = jnp.minimum(jnp.sum((pad_end[None, :] <= block_start[:, None]).astype(jnp.int32), axis=1),
                          N_EXPERTS - 1).astype(jnp.int32)
    n_used = (pad_end[-1] // bm).astype(jnp.int32).reshape(1)
    return block_e, n_used, dest.reshape(n_tokens, TOP_K)


def _layer(xp, xs, c_all, cache_k, cache_v, cache_logf, state_conv, state_h, p, norm_f):
    Bp, Tp, _ = xp.shape
    Bs, Ts, _ = xs.shape
    past = cache_k.shape[1]
    np_rows, ns_rows = Bp * Tp, Bs * Ts
    assert Tp % ROW_TILE == 0 and ns_rows == ROW_TILE and Tp % ATTN_BLOCK == 0

    mod = _mod(c_all, p['w_mod'], p['b_mod'])
    mod_p = mod[:Bp].reshape(Bp, 1, 6 * D_MODEL)
    mod_s = jnp.repeat(mod[Bp:], Ts, axis=0).reshape(1, ns_rows, 6 * D_MODEL)

    w_in = p['w_in']
    d3 = 3 * D_MODEL
    wcat = jnp.concatenate([w_in[:, :D_MODEL] * (LOG2E * HEAD_DIM ** -0.5), w_in[:, D_MODEL:d3],
                            w_in[:, d3 + N_HEADS:], p['w_gate']], axis=1).astype(BF16)
    wfl = jnp.pad(w_in[:, d3:d3 + N_HEADS], ((0, 0), (0, LANES - N_HEADS))).astype(BF16)
    wflt = wfl.T
    eye = jnp.eye(GATE_TILE // (D_MODEL // N_RNN_BLOCKS), dtype=F32)

    def gate_tiles(w):
        per = GATE_TILE // w.shape[1]
        w4 = w.reshape(N_RNN_BLOCKS // per, per, w.shape[1], w.shape[2])
        return jnp.einsum('gpcd,pq->gpcqd', w4, eye).reshape(N_RNN_BLOCKS // per, GATE_TILE, GATE_TILE).astype(BF16)

    wr4, wi4 = gate_tiles(p['w_r']), gate_tiles(p['w_i'])
    wpa, wpb, wo = p['w_pa'].astype(BF16), p['w_pb'].astype(BF16), p['w_out'].astype(BF16)
    wr = jnp.pad(jnp.concatenate([p['w_rg'], p['w_re']], axis=1), ((0, 0), (0, LANES - N_GROUPS - N_EXPERTS)))
    wr1 = wr.astype(BF16)
    wr2 = (wr - wr1.astype(F32)).astype(BF16)
    br = jnp.pad(jnp.concatenate([p['b_rg'], p['b_re']]), (0, LANES - N_GROUPS - N_EXPERTS)).reshape(1, LANES)

    in_p = _in_proj(xp.reshape(np_rows, D_MODEL), mod_p, p['norm1'], wcat[:, :d3], wfl, wflt, p['b_gate'], p['b_f'],
                    tiles_per_mod=Tp // IN_TILE, tiles_per_seq=Tp // IN_TILE, with_mixer=False)
    in_s = _in_proj(xs.reshape(ns_rows, D_MODEL), mod_s, p['norm1'], wcat, wfl, wflt, p['b_gate'], p['b_f'],
                    tiles_per_mod=1, tiles_per_seq=1, with_mixer=True)
    q_p, kb_p, vb_p, k_p, v_p, lf_p, _, fc_p = in_p
    q_s, kb_s, vb_s, k_s, v_s, xb_s, gb_s, ga_s, gg_s, lf_s, lft_s, _ = in_s
    r3 = lambda a, B, T: a.reshape(B, T, D_MODEL)

    oa_p = _attn_prompt(r3(q_p, Bp, Tp), r3(kb_p, Bp, Tp), r3(vb_p, Bp, Tp), fc_p.reshape(Bp, Tp, N_HEADS))

    lfp3 = jnp.transpose(cache_logf.astype(F32), (0, 2, 1))
    lfn3 = lft_s.reshape(N_HEADS, Bs, Ts).transpose(1, 0, 2)
    oa_s = _attn_sample(r3(q_s, Bs, Ts), r3(kb_s, Bs, Ts), r3(vb_s, Bs, Ts),
                        cache_k.reshape(Bs, past, D_MODEL), cache_v.reshape(Bs, past, D_MODEL), lfp3, lfn3)

    zc = jnp.zeros((Bp, SUBLANES, D_MODEL), F32)
    zh = jnp.zeros((Bp, 1, D_MODEL), F32)
    rnn_w = (p['conv_w'], p['conv_b'], wr4, wi4, p['b_r'], p['b_i'], p['rglru_lambda'])
    post_w = (wpa, wpb, wo, wr1, wr2, br)
    x1_p, h2_p, rt_p, conv_p, hl_p = _mix(xp, oa_p, mod_p, zc, zh, p['norm1'], p['norm2'], wcat[:, d3:], p['b_gate'],
                                          *rnn_w, *post_w)
    x1_p, h2_p, rt_p = (a.reshape(np_rows, a.shape[-1]) for a in (x1_p, h2_p, rt_p))

    sc = jnp.pad(state_conv.astype(F32), ((0, 0), (SUBLANES - (CONV_WIDTH - 1), 0), (0, 0)))
    ob_s, conv_s, hl_s = _rnn(r3(xb_s, Bs, Ts), r3(gb_s, Bs, Ts), sc, state_h.astype(F32).reshape(Bs, 1, D_MODEL),
                              *rnn_w)
    x1_s, h2_s, rt_s = _post(xs.reshape(ns_rows, D_MODEL), oa_s.reshape(ns_rows, D_MODEL),
                             ob_s.reshape(ns_rows, D_MODEL), ga_s, gg_s, mod_s, p['norm2'], *post_w, tiles_per_mod=1)

    n_tok = np_rows + ns_rows
    rt = jnp.concatenate([rt_p, rt_s], axis=0)
    block_e, n_used, dest = _routing_plan(rt[:, :TOP_K].astype(jnp.int32), n_tok)
    slots = jnp.zeros((block_e.shape[0] * MOE_BLOCK, D_MODEL // 2), jnp.uint32)
    slots = _dispatch(h2_p, dest.reshape(-1), slots, tile_offset=0)
    slots = _dispatch(h2_s, dest.reshape(-1), slots, tile_offset=np_rows // ROW_TILE)
    take = lambda a, idx: a.at[idx].get(mode='promise_in_bounds')
    yb = _moe(slots, block_e, n_used, p['w1'], p['w3'], p['w2'])
    ya0 = take(yb, dest[:, 0])
    ya1 = take(yb, dest[:, 1])

    y_p = _final(x1_p, ya0, ya1, rt, mod_p, norm_f, rows_per_mod=Tp, row_offset=0)
    y_s = _final(x1_s, ya0, ya1, rt, mod_s, norm_f, rows_per_mod=ns_rows, row_offset=np_rows)

    st = CONV_WIDTH - 1
    state_p = (k_p.reshape(Bp, Tp, N_HEADS, HEAD_DIM), v_p.reshape(Bp, Tp, N_HEADS, HEAD_DIM),
               lf_p.reshape(Bp, Tp, N_HEADS), conv_p[:, SUBLANES - st:], hl_p.reshape(Bp, D_MODEL))
    state_s = (k_s.reshape(Bs, Ts, N_HEADS, HEAD_DIM), v_s.reshape(Bs, Ts, N_HEADS, HEAD_DIM),
               lf_s.reshape(Bs, Ts, N_HEADS), conv_s[:, SUBLANES - st:], hl_s.reshape(Bs, D_MODEL))
    return y_p.reshape(Bp, Tp, D_MODEL), y_s.reshape(Bs, Ts, D_MODEL), state_p, state_s


def kernel(x_prompt, x_sample, cache_k, cache_v, cache_logf, state_conv, state_h, c_prompt, c_sample, w_mod, b_mod, norm1, w_in, b_f, conv_w, conv_b, w_r, b_r, w_i, b_i, rglru_lambda, w_gate, b_gate, w_pa, w_pb, w_out, norm2, w_rg, b_rg, w_re, b_re, w1, w3, w2, norm_f):
    assert w_mod.shape[0] == 1, "single-layer trunk: the final norm is fused into the layer's last stage"
    names = ('w_mod', 'b_mod', 'norm1', 'w_in', 'b_f', 'conv_w', 'conv_b', 'w_r', 'b_r', 'w_i', 'b_i',
             'rglru_lambda', 'w_gate', 'b_gate', 'w_pa', 'w_pb', 'w_out', 'norm2', 'w_rg', 'b_rg', 'w_re', 'b_re',
             'w1', 'w3', 'w2')
    vals = (w_mod, b_mod, norm1, w_in, b_f, conv_w, conv_b, w_r, b_r, w_i, b_i, rglru_lambda, w_gate, b_gate,
            w_pa, w_pb, w_out, norm2, w_rg, b_rg, w_re, b_re, w1, w3, w2)
    p = {n: v[0] for n, v in zip(names, vals)}
    c_all = jnp.concatenate([c_prompt, c_sample], axis=0)
    y_p, y_s, sp, ss = _layer(x_prompt, x_sample, c_all, cache_k[0], cache_v[0], cache_logf[0],
                              state_conv[0], state_h[0], p, norm_f)
    lead = lambda a: a[None]
    return (y_p, y_s, lead(sp[0]), lead(sp[1]), lead(sp[2]), lead(sp[3]), lead(sp[4]),
            lead(ss[0]), lead(ss[1]), lead(ss[2]), lead(ss[3]), lead(ss[4]))
```

```python
import functools

import jax
import jax.numpy as jnp
from jax import lax
from jax.experimental import pallas as pl
from jax.experimental.pallas import tpu as pltpu

F32 = jnp.float32
BF16 = jnp.bfloat16

D_MODEL = 1024
N_HEADS = 16
HEAD_DIM = 64
HEAD_PAIR = 2 * HEAD_DIM
N_PAIRS = N_HEADS // 2
N_RNN_BLOCKS = 16
CONV_WIDTH = 4
RGLRU_C = 8.0
N_GROUPS = 4
EXPERTS_PER_GROUP = 8
N_EXPERTS = N_GROUPS * EXPERTS_PER_GROUP
TOP_K = 2
D_EXPERT = 512
EPS = 1e-6

LANES = 128
SUBLANES = 8
ROW_TILE = 512
FINAL_TILE = 1024
IN_TILE = 256
ATTN_BLOCK = 512
ATTN_QBLOCK = 1024
ATTN_VROWS = HEAD_DIM + 16
RNN_TILE = 256
MIX_TILE = 512
MOE_BLOCK = 512
MOE_SUB = 256
GATE_TILE = 256
VMEM_LIMIT = 56 * 1024 * 1024
NEG_BIG = -1e30
LOG2E = 1.4426950408889634


def _cparams(sem):
    return pltpu.CompilerParams(dimension_semantics=sem, vmem_limit_bytes=VMEM_LIMIT)


def _resident(shape):
    nd = len(shape)
    return pl.BlockSpec(shape, lambda *_: (0,) * nd, pipeline_mode=pl.Buffered(1))


def _split3(x):
    p1 = x.astype(BF16)
    r1 = x - p1.astype(F32)
    p2 = r1.astype(BF16)
    p3 = (r1 - p2.astype(F32)).astype(BF16)
    return p1, p2, p3


def _dot(a, b):
    return jnp.dot(a, b, preferred_element_type=F32)


def _dot_nt(a, b):
    return lax.dot_general(a, b, (((1,), (1,)), ((), ())), preferred_element_type=F32)


def _sigmoid(x):
    return 0.5 * jnp.tanh(0.5 * x) + 0.5


def _gelu_tanh(x):
    return 0.5 * x * (1.0 + jnp.tanh(0.7978845608028654 * (x + 0.044715 * x * x * x)))


def _rms(x, g):
    ms = jnp.mean(x * x, axis=-1, keepdims=True)
    return x * lax.rsqrt(ms + EPS) * g


def _pack_pairs(x):
    n = x.shape[1] // 2
    hi = pltpu.bitcast(x[:, :n].astype(BF16).astype(F32), jnp.uint32)
    lo = pltpu.bitcast(x[:, n:].astype(BF16).astype(F32), jnp.uint32)
    return hi | lax.shift_right_logical(lo, jnp.uint32(16))


def _unpack_pairs(w):
    hi = pltpu.bitcast(w & jnp.uint32(0xFFFF0000), F32)
    lo = pltpu.bitcast(lax.shift_left(w, jnp.uint32(16)), F32)
    return jnp.concatenate([hi, lo], axis=1)


def _div_p2(x, n):
    assert n & (n - 1) == 0
    return lax.shift_right_logical(x, jnp.int32(n.bit_length() - 1))


def _mod_p2(x, n):
    assert n & (n - 1) == 0
    return x & (n - 1)


def _upper_tri(n):
    r = lax.broadcasted_iota(jnp.int32, (n, n), 0)
    c = lax.broadcasted_iota(jnp.int32, (n, n), 1)
    return jnp.where(r <= c, 1.0, 0.0).astype(BF16)


def _cumsum_lanes(x, u):
    p1, p2, p3 = _split3(x)
    return _dot(p1, u) + _dot(p2, u) + _dot(p3, u)


def _mod_kernel(c_ref, w_ref, b_ref, o_ref):
    c = c_ref[...]
    s = c * _sigmoid(c)
    s1 = s.astype(BF16)
    s2 = (s - s1.astype(F32)).astype(BF16)
    w = w_ref[...]
    w1 = w.astype(BF16)
    w2 = (w - w1.astype(F32)).astype(BF16)
    o_ref[...] = _dot(s1, w1) + _dot(s1, w2) + _dot(s2, w1) + b_ref[...]


def _mod(c, w, b):
    n = c.shape[0]
    cols = w.shape[1]
    return pl.pallas_call(
        _mod_kernel,
        grid=(cols // D_MODEL,),
        in_specs=[pl.BlockSpec((n, D_MODEL), lambda j: (0, 0)),
                  pl.BlockSpec((D_MODEL, D_MODEL), lambda j: (0, j)),
                  pl.BlockSpec((1, D_MODEL), lambda j: (0, j))],
        out_specs=pl.BlockSpec((n, D_MODEL), lambda j: (0, j)),
        out_shape=jax.ShapeDtypeStruct((n, cols), F32),
        compiler_params=_cparams(("arbitrary",)),
        name="mod",
    )(c, w, b.reshape(1, cols))


def _in_kernel(x_ref, sh_ref, sc_ref, n1_ref, w_ref, wfl_ref, wflt_ref, bg_ref, bf_ref, bft_ref, *refs,
               tiles_per_seq, with_mixer):
    lf_ref, lft_ref, fc_ref, carry_ref = refs[-4:]
    x = x_ref[...]
    h = (_rms(x, n1_ref[...]) * (1.0 + sc_ref[...]) + sh_ref[...]).astype(BF16)

    def mm(c):
        return _dot(h, w_ref[:, c * D_MODEL:(c + 1) * D_MODEL])

    def head_major(y):
        slabs = []
        for p in range(N_PAIRS):
            pair = y[:, p * HEAD_PAIR:(p + 1) * HEAD_PAIR]
            slabs += [pair, pltpu.roll(pair, HEAD_DIM, 1)]
        return jnp.swapaxes(jnp.stack(slabs, axis=0), 0, 1)[:, :, :HEAD_DIM]

    def time_minor(y):
        return y.T.reshape(N_HEADS, HEAD_DIM, y.shape[0])

    refs[0][...] = mm(0).astype(BF16)
    k = mm(1)
    refs[1][...] = k.astype(BF16)
    v = mm(2)
    if with_mixer:
        vb_ref, k5_ref, v5_ref, xb_ref, gb_ref, ga_ref, gg_ref = refs[2:9]
        vb_ref[...] = v.astype(BF16)
        k5_ref[...] = head_major(k)
        v5_ref[...] = head_major(v)
        xb_ref[...] = mm(3).astype(BF16)
        gb_ref[...] = mm(4).astype(BF16)
        ga_ref[...] = _sigmoid(mm(5) + bg_ref[:, :D_MODEL]).astype(BF16)
        gg_ref[...] = _sigmoid(mm(6) + bg_ref[:, D_MODEL:]).astype(BF16)
    else:
        refs[2][...] = time_minor(k)
        refs[3][...] = time_minor(v)

    def log_sigmoid(z):
        return jnp.minimum(z, 0.0) - jnp.log(1.0 + jnp.exp(-jnp.abs(z)))

    lf = log_sigmoid(_dot(h, wfl_ref[...]) + bf_ref[...])
    lf_ref[...] = lf[:, :N_HEADS]
    lft_ref[...] = log_sigmoid(_dot_nt(wflt_ref[...], h)[:N_HEADS, :] + bft_ref[...])

    @pl.when(pl.program_id(0) % tiles_per_seq == 0)
    def _():
        carry_ref[...] = jnp.zeros_like(carry_ref)

    tm = x.shape[0]
    r = lax.broadcasted_iota(jnp.int32, (tm, tm), 0)
    c = lax.broadcasted_iota(jnp.int32, (tm, tm), 1)
    lower = jnp.where(c <= r, 1.0, 0.0).astype(BF16)
    p1, p2, p3 = _split3(lf)
    fc = _dot(lower, p1) + _dot(lower, p2) + _dot(lower, p3) + carry_ref[...]
    fc_ref[...] = fc[:, :N_HEADS]
    carry_ref[...] = fc[tm - 1:tm, :]


def _in_proj(x2, mod3, norm1, wcat, wfl, wflt, b_gate, b_f, *, tiles_per_mod, tiles_per_seq, with_mixer):
    rows = x2.shape[0]
    tm = IN_TILE
    nt = rows // tm
    row_blk = pl.BlockSpec((tm, D_MODEL), lambda i: (i, 0))
    if mod3.shape[1] == 1:
        modc = lambda c: pl.BlockSpec((None, 1, D_MODEL), lambda i: (i // tiles_per_mod, 0, c))
    else:
        modc = lambda c: pl.BlockSpec((None, tm, D_MODEL), lambda i: (0, i, c))
    row_out = jax.ShapeDtypeStruct((rows, D_MODEL), BF16)
    if with_mixer:
        kv_shape = jax.ShapeDtypeStruct((rows, N_HEADS, HEAD_DIM), F32)
        kv_blk = pl.BlockSpec((tm, N_HEADS, HEAD_DIM), lambda i: (i, 0, 0))
        lead = ([row_out] * 3 + [kv_shape] * 2 + [row_out] * 4, [row_blk] * 3 + [kv_blk] * 2 + [row_blk] * 4)
    else:
        n_seq = nt // tiles_per_seq
        kv_shape = jax.ShapeDtypeStruct((n_seq, N_HEADS, HEAD_DIM, tiles_per_seq * tm), F32)
        kv_blk = pl.BlockSpec((None, N_HEADS, HEAD_DIM, tm), lambda i: (i // tiles_per_seq, 0, 0, i % tiles_per_seq))
        lead = ([row_out] * 2 + [kv_shape] * 2, [row_blk] * 2 + [kv_blk] * 2)
    outs = [*lead[0],
            jax.ShapeDtypeStruct((rows, N_HEADS), F32),
            jax.ShapeDtypeStruct((N_HEADS, rows), F32),
            jax.ShapeDtypeStruct((rows, N_HEADS), F32)]
    return pl.pallas_call(
        functools.partial(_in_kernel, tiles_per_seq=tiles_per_seq, with_mixer=with_mixer),
        grid=(nt,),
        in_specs=[row_blk, modc(0), modc(1),
                  _resident((1, D_MODEL)),
                  _resident(wcat.shape),
                  _resident(wfl.shape),
                  _resident(wflt.shape),
                  _resident((1, 2 * D_MODEL)),
                  _resident((1, LANES)),
                  _resident((N_HEADS, 1))],
        out_specs=lead[1]
                  + [pl.BlockSpec((tm, N_HEADS), lambda i: (i, 0)),
                     pl.BlockSpec((N_HEADS, tm), lambda i: (0, i)),
                     pl.BlockSpec((tm, N_HEADS), lambda i: (i, 0))],
        out_shape=outs,
        scratch_shapes=[pltpu.VMEM((1, LANES), F32)],
        compiler_params=_cparams(("arbitrary",)),
        name="in_proj",
    )(x2, mod3, mod3, norm1.reshape(1, D_MODEL), wcat, wfl, wflt,
      b_gate.reshape(1, 2 * D_MODEL), jnp.pad(b_f, (0, LANES - N_HEADS)).reshape(1, LANES),
      b_f.reshape(N_HEADS, 1))


def _attn_kernel(q_ref, k_ref, v_ref, f_ref, o_ref, kb_ref, vt_ref, qt_ref, s0_ref, m_ref, acc_ref):
    qi = pl.program_id(2)
    bq = q_ref.shape[0]
    bk = ATTN_BLOCK
    nk = k_ref.shape[0] // bk
    own_lanes = (lambda i: i < HEAD_DIM, lambda i: i >= HEAD_DIM)
    bias_at = (HEAD_DIM, 0)

    @pl.when(qi == 0)
    def _():
        lane = lax.broadcasted_iota(jnp.int32, (bk, HEAD_PAIR), 1)
        ones_rows = jnp.where(lax.broadcasted_iota(jnp.int32, (ATTN_VROWS - HEAD_DIM, bk), 0) == 0, 1.0, 0.0)
        for c in range(nk):
            blk = slice(c * bk, (c + 1) * bk)
            k = k_ref[blk, :].astype(F32)
            f = f_ref[blk, :] * LOG2E
            head = lax.broadcasted_iota(jnp.int32, f.shape, 1)
            for hh in range(2):
                mine = head == 2 * pl.program_id(1) + hh
                f_head = jnp.sum(jnp.where(mine, f, 0.0), axis=1, keepdims=True)
                slab = jnp.zeros(k.shape, F32)
                for n, piece in enumerate(_split3(f_head)):
                    slab = jnp.where(lane == bias_at[hh] + n, piece.astype(F32), slab)
                kb_ref[hh, blk, :] = jnp.where(own_lanes[hh](lane), k, slab).astype(BF16)
            for hh in range(2):
                vt_ref[c, hh, 0:HEAD_DIM] = v_ref[hh, :, blk].astype(BF16)
                vt_ref[c, hh, HEAD_DIM:ATTN_VROWS] = ones_rows.astype(BF16)

    qt = q_ref[...].astype(F32).T
    feat = lax.broadcasted_iota(jnp.int32, qt.shape, 0)
    for hh in range(2):
        is_bias = (feat >= bias_at[hh]) & (feat < bias_at[hh] + 3)
        qt_ref[hh] = jnp.where(own_lanes[hh](feat), qt, jnp.where(is_bias, -1.0, 0.0)).astype(BF16)

    m_ref[...] = jnp.full(m_ref.shape, NEG_BIG, F32)
    acc_ref[...] = jnp.zeros(acc_ref.shape, F32)

    nsub = bq // bk

    def scores(hh, u, j):
        keys = kb_ref[hh, pl.ds(pl.multiple_of(j * bk, bk), bk), :]
        return _dot(keys, qt_ref[hh, :, u * bk:(u + 1) * bk])

    def softmax_pv(hh, u, j, s, masked):
        cols = slice(u * bk, (u + 1) * bk)
        if masked:
            key = lax.broadcasted_iota(jnp.int32, s.shape, 0)
            qry = lax.broadcasted_iota(jnp.int32, s.shape, 1)
            s = jnp.where(key <= qry, s, NEG_BIG)
        m_prev = m_ref[hh, :, cols]
        m_new = jnp.maximum(m_prev, jnp.max(s, axis=0, keepdims=True))
        p = jnp.exp2(s - m_new).astype(BF16)
        acc_ref[hh, :, cols] = jnp.exp2(m_prev - m_new) * acc_ref[hh, :, cols] + _dot(vt_ref[j, hh], p)
        m_ref[hh, :, cols] = m_new

    def run(items, first_of_next):
        s = s0_ref[...]
        for n, (hh, u, j, masked) in enumerate(items):
            if n + 1 < len(items):
                nxt = scores(*items[n + 1][:3])
            elif first_of_next is not None:
                s0_ref[...] = scores(*first_of_next)
            softmax_pv(hh, u, j, s, masked)
            s = nxt if n + 1 < len(items) else None

    def body(j, carry):
        run([(hh, u, j, False) for u in range(nsub) for hh in range(2)], (0, 0, j + 1))
        return carry

    s0_ref[...] = scores(0, 0, 0)
    lax.fori_loop(0, nsub * qi, body, 0)
    for d in range(nsub):
        j = nsub * qi + d
        items = [(hh, u, j, u == d) for u in range(d, nsub) for hh in range(2)]
        run(items, (0, d + 1, j + 1) if d + 1 < nsub else None)

    heads = [acc_ref[hh, 0:HEAD_DIM] / acc_ref[hh, HEAD_DIM:HEAD_DIM + 1] for hh in range(2)]
    o_ref[...] = jnp.concatenate(heads, axis=0).T.astype(BF16)


def _attn_prompt(q3, k3, vt4, fc3):
    B, T, _ = q3.shape
    bq, bk = ATTN_QBLOCK, ATTN_BLOCK
    nq, nk = T // bq, T // bk
    return pl.pallas_call(
        _attn_kernel,
        grid=(B, N_PAIRS, nq),
        in_specs=[pl.BlockSpec((None, bq, HEAD_PAIR), lambda b, p, i: (b, i, p)),
                  pl.BlockSpec((None, T, HEAD_PAIR), lambda b, p, i: (b, 0, p)),
                  pl.BlockSpec((None, 2, HEAD_DIM, T), lambda b, p, i: (b, p, 0, 0)),
                  pl.BlockSpec((None, T, N_HEADS), lambda b, p, i: (b, 0, 0))],
        out_specs=pl.BlockSpec((None, bq, HEAD_PAIR), lambda b, p, i: (b, i, p)),
        out_shape=jax.ShapeDtypeStruct((B, T, D_MODEL), BF16),
        scratch_shapes=[pltpu.VMEM((2, T, HEAD_PAIR), BF16),
                        pltpu.VMEM((nk, 2, ATTN_VROWS, bk), BF16),
                        pltpu.VMEM((2, HEAD_PAIR, bq), BF16),
                        pltpu.VMEM((bk, bk), F32),
                        pltpu.VMEM((2, 1, bq), F32),
                        pltpu.VMEM((2, ATTN_VROWS, bq), F32)],
        compiler_params=_cparams(("arbitrary", "arbitrary", "arbitrary")),
        name="attn_prompt",
    )(q3, k3, vt4, fc3)


def _attn_sample_kernel(q_ref, kn_ref, vn_ref, kc_ref, vc_ref, lfp_ref, lfn_ref, o_ref):
    tq = q_ref.shape[0]
    past = kc_ref.shape[0]
    nrow = N_HEADS * tq
    row = lax.broadcasted_iota(jnp.int32, (nrow, D_MODEL), 0)
    col = lax.broadcasted_iota(jnp.int32, (nrow, D_MODEL), 1)
    own = _div_p2(row, tq) == _div_p2(col, HEAD_DIM)

    rr = lax.broadcasted_iota(jnp.int32, (nrow, tq), 0)
    rc = lax.broadcasted_iota(jnp.int32, (nrow, tq), 1)
    rep = jnp.where(_mod_p2(rr, tq) == rc, 1.0, 0.0).astype(BF16)
    qx = jnp.where(own, _dot(rep, q_ref[...]), 0.0).astype(BF16)

    fp = _cumsum_lanes(lfp_ref[...] * LOG2E, _upper_tri(past))
    fn = _cumsum_lanes(lfn_ref[...] * LOG2E, _upper_tri(tq)) + fp[:, past - 1:past]
    hr = lax.broadcasted_iota(jnp.int32, (nrow, N_HEADS), 0)
    hc = lax.broadcasted_iota(jnp.int32, (nrow, N_HEADS), 1)
    hsel = jnp.where(_div_p2(hr, tq) == hc, 1.0, 0.0).astype(BF16)

    def expand(f):
        p1, p2, p3 = _split3(f)
        return _dot(hsel, p1) + _dot(hsel, p2) + _dot(hsel, p3)

    s_p = _dot_nt(qx, kc_ref[...].astype(BF16)) - expand(fp)
    s_n = _dot_nt(qx, kn_ref[...]) - expand(fn)
    nr = lax.broadcasted_iota(jnp.int32, (nrow, tq), 0)
    nc = lax.broadcasted_iota(jnp.int32, (nrow, tq), 1)
    s_n = jnp.where(nc <= _mod_p2(nr, tq), s_n, NEG_BIG)
    m = jnp.maximum(jnp.max(s_p, axis=-1, keepdims=True), jnp.max(s_n, axis=-1, keepdims=True))
    p_p = jnp.exp2(s_p - m)
    p_n = jnp.exp2(s_n - m)
    l = jnp.sum(p_p, axis=-1, keepdims=True) + jnp.sum(p_n, axis=-1, keepdims=True)
    o_full = _dot(p_p.astype(BF16), vc_ref[...].astype(BF16)) + _dot(p_n.astype(BF16), vn_ref[...])
    o_own = jnp.where(own, o_full / l, 0.0).astype(BF16)
    cr = lax.broadcasted_iota(jnp.int32, (tq, nrow), 0)
    cc = lax.broadcasted_iota(jnp.int32, (tq, nrow), 1)
    col_sel = jnp.where(_mod_p2(cc, tq) == cr, 1.0, 0.0).astype(BF16)
    o_ref[...] = _dot(col_sel, o_own).astype(BF16)


def _attn_sample(q3, kn3, vn3, kc3, vc3, lfp3, lfn3):
    B, tq, _ = q3.shape
    past = kc3.shape[1]
    blk = lambda n, w: pl.BlockSpec((None, n, w), lambda b: (b, 0, 0))
    return pl.pallas_call(
        _attn_sample_kernel,
        grid=(B,),
        in_specs=[blk(tq, D_MODEL), blk(tq, D_MODEL), blk(tq, D_MODEL),
                  blk(past, D_MODEL), blk(past, D_MODEL), blk(N_HEADS, past), blk(N_HEADS, tq)],
        out_specs=blk(tq, D_MODEL),
        out_shape=jax.ShapeDtypeStruct((B, tq, D_MODEL), BF16),
        compiler_params=_cparams(("arbitrary",)),
        name="attn_sample",
    )(q3, kn3, vn3, kc3, vc3, lfp3, lfn3)


def _rglru_gates(xb, xp_ref, cw_ref, cb_ref, wr_ref, wi_ref):
    tt = xb.shape[0]
    pad = SUBLANES
    xp_ref[pad:pad + tt, :] = xb
    xc = cb_ref[...] + xb * cw_ref[CONV_WIDTH - 1:CONV_WIDTH, :]
    for k in range(CONV_WIDTH - 1):
        back = CONV_WIDTH - 1 - k
        xc = xc + xp_ref[pad - back:pad - back + tt, :] * cw_ref[k:k + 1, :]
    tail = xp_ref[tt:tt + pad, :]
    xp_ref[0:pad, :] = tail

    xcb = xc.astype(BF16)
    ng = D_MODEL // GATE_TILE
    r = jnp.concatenate([_dot(xcb[:, g * GATE_TILE:(g + 1) * GATE_TILE], wr_ref[g]) for g in range(ng)], axis=-1)
    i = jnp.concatenate([_dot(xcb[:, g * GATE_TILE:(g + 1) * GATE_TILE], wi_ref[g]) for g in range(ng)], axis=-1)
    return xc, r, i, tail


def _rglru_scan(xc, r, i, gb, hc_ref, br_ref, bi_ref, lam_ref):
    tt = xc.shape[0]
    r = _sigmoid(r + br_ref[...])
    i = _sigmoid(i + bi_ref[...])
    lam = lam_ref[...]
    softplus_neg = jnp.maximum(-lam, 0.0) + jnp.log(1.0 + jnp.exp(-jnp.abs(lam)))
    log_a = (-RGLRU_C) * r * softplus_neg
    a = jnp.exp(log_a)
    th = jnp.tanh(log_a)
    mult = jnp.sqrt(-2.0 * th / (1.0 - th))
    b = mult * (i * xc)

    ng8 = tt // SUBLANES
    a3 = a.reshape(ng8, SUBLANES, D_MODEL)
    b3 = b.reshape(ng8, SUBLANES, D_MODEL)
    sub = lax.broadcasted_iota(jnp.int32, a3.shape, 1)
    for s in (1, 2, 4):
        keep = sub >= s
        a_sh = jnp.where(keep, pltpu.roll(a3, s, 1), 1.0)
        b_sh = jnp.where(keep, pltpu.roll(b3, s, 1), 0.0)
        b3 = a3 * b_sh + b3
        a3 = a3 * a_sh
    hc = hc_ref[...]
    hs = []
    for g in range(ng8):
        hg = b3[g] + a3[g] * hc
        hs.append(hg)
        hc = hg[SUBLANES - 1:SUBLANES, :]
    hc_ref[...] = hc
    return jnp.concatenate(hs, axis=0) * _gelu_tanh(gb), hc


def _rnn_kernel(xb_ref, gb_ref, c0_ref, h0_ref, cw_ref, cb_ref, wr_ref, wi_ref, br_ref, bi_ref, lam_ref,
                o_ref, cout_ref, hout_ref, xp_ref, hc_ref):
    ti = pl.program_id(1)

    @pl.when(ti == 0)
    def _():
        xp_ref[0:SUBLANES, :] = c0_ref[...]
        hc_ref[...] = h0_ref[...]

    xc, r, i, tail = _rglru_gates(xb_ref[...].astype(F32), xp_ref, cw_ref, cb_ref, wr_ref, wi_ref)
    ob, hc = _rglru_scan(xc, r, i, gb_ref[...].astype(F32), hc_ref, br_ref, bi_ref, lam_ref)
    o_ref[...] = ob.astype(BF16)

    @pl.when(ti == pl.num_programs(1) - 1)
    def _():
        cout_ref[...] = tail
        hout_ref[...] = hc


def _rnn(xb3, gb3, conv0, h0, conv_w, conv_b, wr4, wi4, b_r, b_i, lam):
    B, T, _ = xb3.shape
    tt = min(RNN_TILE, T)
    vec = lambda a: a.reshape(1, D_MODEL)
    seq = pl.BlockSpec((None, tt, D_MODEL), lambda b, t: (b, t, 0))
    st8 = pl.BlockSpec((None, SUBLANES, D_MODEL), lambda b, t: (b, 0, 0))
    st1 = pl.BlockSpec((None, 1, D_MODEL), lambda b, t: (b, 0, 0))
    return pl.pallas_call(
        _rnn_kernel,
        grid=(B, T // tt),
        in_specs=[seq, seq, st8, st1,
                  _resident((CONV_WIDTH, D_MODEL)), _resident((1, D_MODEL)),
                  _resident(wr4.shape), _resident(wi4.shape),
                  _resident((1, D_MODEL)), _resident((1, D_MODEL)), _resident((1, D_MODEL))],
        out_specs=[seq, st8, st1],
        out_shape=[jax.ShapeDtypeStruct((B, T, D_MODEL), BF16),
                   jax.ShapeDtypeStruct((B, SUBLANES, D_MODEL), F32),
                   jax.ShapeDtypeStruct((B, 1, D_MODEL), F32)],
        scratch_shapes=[pltpu.VMEM((tt + SUBLANES, D_MODEL), F32),
                        pltpu.VMEM((1, D_MODEL), F32)],
        compiler_params=_cparams(("arbitrary", "arbitrary")),
        name="rnn",
    )(xb3, gb3, conv0, h0, conv_w, vec(conv_b), wr4, wi4, vec(b_r), vec(b_i), vec(lam))


def _merge_route_tile(x, pa, ob, ga, gg, g1, sh2, sc2, n2_ref, wpb_ref, wo_ref, wr1_ref, wr2_ref, br_ref):
    mix = ga * pa + gg * _dot(ob, wpb_ref[...])
    x1 = x + g1 * _dot(mix.astype(BF16), wo_ref[...])
    h2 = _rms(x1, n2_ref[...]) * (1.0 + sc2) + sh2
    h2a = h2.astype(BF16)
    h2b = (h2 - h2a.astype(F32)).astype(BF16)
    lg = _dot(h2a, wr1_ref[...]) + _dot(h2a, wr2_ref[...]) + _dot(h2b, wr1_ref[...]) + br_ref[...]

    lane = lax.broadcasted_iota(jnp.int32, lg.shape, 1).astype(F32)
    big = float(LANES)
    is_g = lane < N_GROUPS
    gl = jnp.where(is_g, lg, NEG_BIG)
    mg = jnp.max(gl, axis=-1, keepdims=True)
    sg = jnp.sum(jnp.where(is_g, jnp.exp(gl - mg), 0.0), axis=-1, keepdims=True)
    pg = 1.0 / sg
    gidx = jnp.min(jnp.where(is_g & (gl == mg), lane, big), axis=-1, keepdims=True)
    lo = N_GROUPS + EXPERTS_PER_GROUP * gidx
    ing = (lane >= lo) & (lane < lo + EXPERTS_PER_GROUP)
    el = jnp.where(ing, lg, NEG_BIG)
    m1 = jnp.max(el, axis=-1, keepdims=True)
    se = jnp.sum(jnp.where(ing, jnp.exp(el - m1), 0.0), axis=-1, keepdims=True)
    i1 = jnp.min(jnp.where(ing & (el == m1), lane, big), axis=-1, keepdims=True)
    el2 = jnp.where(lane == i1, NEG_BIG, el)
    m2 = jnp.max(el2, axis=-1, keepdims=True)
    i2 = jnp.min(jnp.where(ing & (el2 == m2) & (lane != i1), lane, big), axis=-1, keepdims=True)
    p1 = 1.0 / se
    p2 = jnp.exp(m2 - m1) / se
    den = p1 + p2
    w1 = pg * p1 / den
    w2 = pg * p2 / den
    rt = jnp.where(lane == 0.0, i1 - N_GROUPS,
                   jnp.where(lane == 1.0, i2 - N_GROUPS,
                             jnp.where(lane == 2.0, w1, jnp.where(lane == 3.0, w2, 0.0))))
    return x1, h2, rt


def _post_kernel(x_ref, oa_ref, ob_ref, ga_ref, gg_ref, g1_ref, sh_ref, sc_ref, n2_ref,
                 wpa_ref, wpb_ref, wo_ref, wr1_ref, wr2_ref, br_ref,
                 x1_ref, h2_ref, rt_ref):
    x1, h2, rt = _merge_route_tile(x_ref[...], _dot(oa_ref[...], wpa_ref[...]), ob_ref[...],
                                   ga_ref[...].astype(F32), gg_ref[...].astype(F32), g1_ref[...], sh_ref[...],
                                   sc_ref[...], n2_ref, wpb_ref, wo_ref, wr1_ref, wr2_ref, br_ref)
    x1_ref[...] = x1
    h2_ref[...] = _pack_pairs(h2)
    rt_ref[...] = rt


def _post(x2, oa2, ob2, ga2, gg2, mod3, norm2, wpa, wpb, wo, wr1, wr2, br, *, tiles_per_mod):
    rows = x2.shape[0]
    tm = ROW_TILE
    mr = mod3.shape[1]
    row = pl.BlockSpec((tm, D_MODEL), lambda i: (i, 0))
    modc = lambda c: pl.BlockSpec((None, mr, D_MODEL), lambda i: (i // tiles_per_mod, 0, c))
    return pl.pallas_call(
        _post_kernel,
        grid=(rows // tm,),
        in_specs=[row, row, row, row, row, modc(2), modc(3), modc(4), _resident((1, D_MODEL)),
                  _resident(wpa.shape), _resident(wpb.shape), _resident(wo.shape),
                  _resident(wr1.shape), _resident(wr2.shape), _resident((1, LANES))],
        out_specs=[row, pl.BlockSpec((tm, D_MODEL // 2), lambda i: (i, 0)), pl.BlockSpec((tm, LANES), lambda i: (i, 0))],
        out_shape=[jax.ShapeDtypeStruct((rows, D_MODEL), F32),
                   jax.ShapeDtypeStruct((rows, D_MODEL // 2), jnp.uint32),
                   jax.ShapeDtypeStruct((rows, LANES), F32)],
        compiler_params=_cparams(("arbitrary",)),
        name="post",
    )(x2, oa2, ob2, ga2, gg2, mod3, mod3, mod3, norm2.reshape(1, D_MODEL), wpa, wpb, wo, wr1, wr2, br)


def _mix_kernel(x_ref, oa_ref, sh1_ref, sc1_ref, g1_ref, sh2_ref, sc2_ref, n1_ref, n2_ref, w_ref, bg_ref,
                c0_ref, h0_ref, cw_ref, cb_ref, wr_ref, wi_ref, br_ref, bi_ref, lam_ref,
                wpa_ref, wpb_ref, wo_ref, wr1_ref, wr2_ref, brt_ref,
                x1_ref, h2_ref, rt_ref, cout_ref, hout_ref, xp_ref, hc_ref):
    ti = pl.program_id(1)

    @pl.when(ti == 0)
    def _():
        xp_ref[0:SUBLANES, :] = c0_ref[...]
        hc_ref[...] = h0_ref[...]

    def project(rows):
        x = x_ref[rows, :]
        h = (_rms(x, n1_ref[...]) * (1.0 + sc1_ref[...]) + sh1_ref[...]).astype(BF16)
        mm = lambda c: _dot(h, w_ref[:, c * D_MODEL:(c + 1) * D_MODEL])
        xc, r, i, tail = _rglru_gates(mm(0), xp_ref, cw_ref, cb_ref, wr_ref, wi_ref)
        return dict(x=x, xc=xc, r=r, i=i, tail=tail, gb=mm(1), ga=mm(2), gg=mm(3),
                    pa=_dot(oa_ref[rows, :], wpa_ref[...]))

    def recur(st):
        ob, hc = _rglru_scan(st['xc'], st['r'], st['i'], st['gb'], hc_ref, br_ref, bi_ref, lam_ref)
        return ob.astype(BF16), hc

    def merge(rows, st, ob):
        ga = _sigmoid(st['ga'] + bg_ref[:, :D_MODEL])
        gg = _sigmoid(st['gg'] + bg_ref[:, D_MODEL:])
        x1, h2, rt = _merge_route_tile(st['x'], st['pa'], ob, ga, gg, g1_ref[...], sh2_ref[...], sc2_ref[...],
                                       n2_ref, wpb_ref, wo_ref, wr1_ref, wr2_ref, brt_ref)
        x1_ref[rows, :] = x1
        h2_ref[rows, :] = _pack_pairs(h2)
        rt_ref[rows, :] = rt

    subs = [slice(s, s + RNN_TILE) for s in range(0, x_ref.shape[0], RNN_TILE)]
    states = [project(subs[0])]
    for n, rows in enumerate(subs):
        if n + 1 < len(subs):
            states.append(project(subs[n + 1]))
        ob, hc = recur(states[n])
        merge(rows, states[n], ob)

    @pl.when(ti == pl.num_programs(1) - 1)
    def _():
        cout_ref[...] = states[-1]['tail']
        hout_ref[...] = hc


def _mix(x3, oa3, mod3, conv0, h0, norm1, norm2, w_mix, b_gate, conv_w, conv_b, wr4, wi4, b_r, b_i, lam,
         wpa, wpb, wo, wr1, wr2, br):
    B, T, _ = x3.shape
    tt = MIX_TILE
    vec = lambda a: a.reshape(1, D_MODEL)
    seq = lambda w: pl.BlockSpec((None, tt, w), lambda b, t: (b, t, 0))
    modc = lambda c: pl.BlockSpec((None, 1, D_MODEL), lambda b, t: (b, 0, c))
    st8 = pl.BlockSpec((None, SUBLANES, D_MODEL), lambda b, t: (b, 0, 0))
    st1 = pl.BlockSpec((None, 1, D_MODEL), lambda b, t: (b, 0, 0))
    v1 = _resident((1, D_MODEL))
    return pl.pallas_call(
        _mix_kernel,
        grid=(B, T // tt),
        in_specs=[seq(D_MODEL), seq(D_MODEL), modc(0), modc(1), modc(2), modc(3), modc(4), v1, v1,
                  _resident(w_mix.shape), _resident((1, 2 * D_MODEL)), st8, st1,
                  _resident((CONV_WIDTH, D_MODEL)), v1, _resident(wr4.shape), _resident(wi4.shape), v1, v1, v1,
                  _resident(wpa.shape), _resident(wpb.shape), _resident(wo.shape),
                  _resident(wr1.shape), _resident(wr2.shape), _resident((1, LANES))],
        out_specs=[seq(D_MODEL), seq(D_MODEL // 2), seq(LANES), st8, st1],
        out_shape=[jax.ShapeDtypeStruct((B, T, D_MODEL), F32),
                   jax.ShapeDtypeStruct((B, T, D_MODEL // 2), jnp.uint32),
                   jax.ShapeDtypeStruct((B, T, LANES), F32),
                   jax.ShapeDtypeStruct((B, SUBLANES, D_MODEL), F32),
                   jax.ShapeDtypeStruct((B, 1, D_MODEL), F32)],
        scratch_shapes=[pltpu.VMEM((RNN_TILE + SUBLANES, D_MODEL), F32),
                        pltpu.VMEM((1, D_MODEL), F32)],
        compiler_params=_cparams(("arbitrary", "arbitrary")),
        name="mix",
    )(x3, oa3, mod3, mod3, mod3, mod3, mod3, vec(norm1), vec(norm2), w_mix, b_gate.reshape(1, 2 * D_MODEL),
      conv0, h0, conv_w, vec(conv_b), wr4, wi4, vec(b_r), vec(b_i), vec(lam), wpa, wpb, wo, wr1, wr2, br)


def _dispatch_kernel(dest_ref, x_ref, slots_in_ref, slots_ref, sem):
    del slots_in_ref
    tm = x_ref.shape[0]

    def row_copy(r, slot):
        return pltpu.make_async_copy(x_ref.at[pl.ds(r, 1)], slots_ref.at[pl.ds(slot, 1)], sem)

    def issue(r, carry):
        for k in range(TOP_K):
            row_copy(r, dest_ref[TOP_K * r + k]).start()
        return carry

    lax.fori_loop(0, tm, issue, 0, unroll=8)

    for _ in range(TOP_K):
        pltpu.make_async_copy(x_ref, slots_ref.at[pl.ds(0, tm)], sem).wait()


def _dispatch(x, dest, slots, *, tile_offset):
    rows, width = x.shape
    tm = ROW_TILE
    return pl.pallas_call(
        _dispatch_kernel,
        grid=(rows // tm,),
        in_specs=[pl.BlockSpec((tm * TOP_K,), lambda i: (i + tile_offset,), memory_space=pltpu.SMEM),
                  pl.BlockSpec((tm, width), lambda i: (i, 0)),
                  pl.BlockSpec(memory_space=pl.ANY)],
        out_specs=pl.BlockSpec(memory_space=pl.ANY),
        out_shape=jax.ShapeDtypeStruct(slots.shape, slots.dtype),
        scratch_shapes=[pltpu.SemaphoreType.DMA(())],
        input_output_aliases={2: 0},
        compiler_params=_cparams(("arbitrary",)),
        name="dispatch",
    )(dest, x, slots)


def _moe_kernel(be_ref, nb_ref, x_ref, w1_ref, w3_ref, w2_ref, o_ref, w1b_ref, w3b_ref, w2b_ref):
    i = pl.program_id(0)

    @pl.when((i == 0) | (be_ref[i] != be_ref[jnp.maximum(i - 1, 0)]))
    def _():
        w1b_ref[...] = w1_ref[...].astype(BF16)
        w3b_ref[...] = w3_ref[...].astype(BF16)
        w2b_ref[...] = w2_ref[...].astype(BF16)

    @pl.when(i < nb_ref[0])
    def _():
        subs = [slice(s, s + MOE_SUB) for s in range(0, x_ref.shape[0], MOE_SUB)]
        ups = []
        for rows in subs:
            x = _unpack_pairs(x_ref[rows, :]).astype(BF16)
            ups.append((_dot(x, w1b_ref[...]), _dot(x, w3b_ref[...])))
        for rows, (a, b) in zip(subs, ups):
            g = a * _sigmoid(a) * b
            o_ref[rows, :] = _pack_pairs(_dot(g.astype(BF16), w2b_ref[...]))

    @pl.when(i >= nb_ref[0])
    def _():
        o_ref[...] = jnp.zeros_like(o_ref)


def _moe(xs, block_e, n_used, w1, w3, w2):
    n_blocks = block_e.shape[0]
    bm = MOE_BLOCK
    wspec = lambda s: pl.BlockSpec((None,) + s, lambda i, be, nb: (be[i], 0, 0))
    gs = pltpu.PrefetchScalarGridSpec(
        num_scalar_prefetch=2,
        grid=(n_blocks,),
        in_specs=[pl.BlockSpec((bm, D_MODEL // 2), lambda i, be, nb: (i, 0)),
                  wspec((D_MODEL, D_EXPERT)), wspec((D_MODEL, D_EXPERT)), wspec((D_EXPERT, D_MODEL))],
        out_specs=pl.BlockSpec((bm, D_MODEL // 2), lambda i, be, nb: (i, 0)),
        scratch_shapes=[pltpu.VMEM((D_MODEL, D_EXPERT), BF16),
                        pltpu.VMEM((D_MODEL, D_EXPERT), BF16),
                        pltpu.VMEM((D_EXPERT, D_MODEL), BF16)],
    )
    return pl.pallas_call(
        _moe_kernel,
        grid_spec=gs,
        out_shape=jax.ShapeDtypeStruct((n_blocks * bm, D_MODEL // 2), jnp.uint32),
        compiler_params=_cparams(("arbitrary",)),
        name="moe",
    )(block_e, n_used, xs, w1, w3, w2)


def _final_kernel(x1_ref, ya_ref, yb_ref, rt_ref, g2_ref, nf_ref, o_ref):
    rt = rt_ref[...]
    moe = (rt[:, TOP_K:TOP_K + 1] * _unpack_pairs(ya_ref[...])
           + rt[:, TOP_K + 1:TOP_K + 2] * _unpack_pairs(yb_ref[...]))
    o_ref[...] = _rms(x1_ref[...] + g2_ref[...] * moe, nf_ref[...])


def _final(x1, ya, yb, rt, mod3, norm_f, *, rows_per_mod, row_offset):
    rows = x1.shape[0]
    tm = min(FINAL_TILE, rows)
    assert row_offset % tm == 0 and rows_per_mod % tm == 0
    tile_offset, tiles_per_mod = row_offset // tm, rows_per_mod // tm
    mr = mod3.shape[1]
    row = pl.BlockSpec((tm, D_MODEL), lambda i: (i, 0))
    tok = lambda w: pl.BlockSpec((tm, w), lambda i: (i + tile_offset, 0))
    return pl.pallas_call(
        _final_kernel,
        grid=(rows // tm,),
        in_specs=[row, tok(D_MODEL // 2), tok(D_MODEL // 2), tok(LANES),
                  pl.BlockSpec((None, mr, D_MODEL), lambda i: (i // tiles_per_mod, 0, 5)),
                  _resident((1, D_MODEL))],
        out_specs=row,
        out_shape=jax.ShapeDtypeStruct((rows, D_MODEL), F32),
        compiler_params=_cparams(("arbitrary",)),
        name="final",
    )(x1, ya, yb, rt, mod3, norm_f.reshape(1, D_MODEL))


def _routing_plan(eid, n_tokens):
    bm = MOE_BLOCK
    n_assign = n_tokens * TOP_K
    n_blocks = -(-n_assign // bm) + N_EXPERTS
    flat_e = eid.reshape(-1)
    onehot = (flat_e[:, None] == jnp.arange(N_EXPERTS, dtype=jnp.int32)[None, :]).astype(jnp.int32)
    csum = jnp.cumsum(onehot, axis=0)
    counts = csum[-1]
    padded = (counts + bm - 1) // bm * bm
    pad_end = jnp.cumsum(padded)
    pad_start = pad_end - padded
    dest = jnp.sum(onehot * (csum - 1 + pad_start[None, :]), axis=1).astype(jnp.int32)
    block_start = jnp.arange(n_blocks, dtype=jnp.int32) * bm
    block_e = jnp.minimum(jnp.sum((pad_end[None, :] <= block_start[:, None]).astype(jnp.int32), axis=1),
                          N_EXPERTS - 1).astype(jnp.int32)
    n_used = (pad_end[-1] // bm).astype(jnp.int32).reshape(1)
    return block_e, n_used, dest.reshape(n_tokens, TOP_K)


def _layer(xp, xs, c_all, cache_k, cache_v, cache_logf, state_conv, state_h, p, norm_f):
    Bp, Tp, _ = xp.shape
    Bs, Ts, _ = xs.shape
    past = cache_k.shape[1]
    np_rows, ns_rows = Bp * Tp, Bs * Ts
    assert Tp % ROW_TILE == 0 and ns_rows == ROW_TILE and Tp % ATTN_BLOCK == 0

    mod = _mod(c_all, p['w_mod'], p['b_mod'])
    mod_p = mod[:Bp].reshape(Bp, 1, 6 * D_MODEL)
    mod_s = jnp.repeat(mod[Bp:], Ts, axis=0).reshape(1, ns_rows, 6 * D_MODEL)

    w_in = p['w_in']
    d3 = 3 * D_MODEL
    wcat = jnp.concatenate([w_in[:, :D_MODEL] * (LOG2E * HEAD_DIM ** -0.5), w_in[:, D_MODEL:d3],
                            w_in[:, d3 + N_HEADS:], p['w_gate']], axis=1).astype(BF16)
    wfl = jnp.pad(w_in[:, d3:d3 + N_HEADS], ((0, 0), (0, LANES - N_HEADS))).astype(BF16)
    wflt = wfl.T
    eye = jnp.eye(GATE_TILE // (D_MODEL // N_RNN_BLOCKS), dtype=F32)

    def gate_tiles(w):
        per = GATE_TILE // w.shape[1]
        w4 = w.reshape(N_RNN_BLOCKS // per, per, w.shape[1], w.shape[2])
        return jnp.einsum('gpcd,pq->gpcqd', w4, eye).reshape(N_RNN_BLOCKS // per, GATE_TILE, GATE_TILE).astype(BF16)

    wr4, wi4 = gate_tiles(p['w_r']), gate_tiles(p['w_i'])
    wpa, wpb, wo = p['w_pa'].astype(BF16), p['w_pb'].astype(BF16), p['w_out'].astype(BF16)
    wr = jnp.pad(jnp.concatenate([p['w_rg'], p['w_re']], axis=1), ((0, 0), (0, LANES - N_GROUPS - N_EXPERTS)))
    wr1 = wr.astype(BF16)
    wr2 = (wr - wr1.astype(F32)).astype(BF16)
    br = jnp.pad(jnp.concatenate([p['b_rg'], p['b_re']]), (0, LANES - N_GROUPS - N_EXPERTS)).reshape(1, LANES)

    in_p = _in_proj(xp.reshape(np_rows, D_MODEL), mod_p, p['norm1'], wcat[:, :d3], wfl, wflt, p['b_gate'], p['b_f'],
                    tiles_per_mod=Tp // IN_TILE, tiles_per_seq=Tp // IN_TILE, with_mixer=False)
    in_s = _in_proj(xs.reshape(ns_rows, D_MODEL), mod_s, p['norm1'], wcat, wfl, wflt, p['b_gate'], p['b_f'],
                    tiles_per_mod=1, tiles_per_seq=1, with_mixer=True)
    q_p, kb_p, kt_p, vt_p, lf_p, _, fc_p = in_p
    q_s, kb_s, vb_s, k_s, v_s, xb_s, gb_s, ga_s, gg_s, lf_s, lft_s, _ = in_s
    r3 = lambda a, B, T: a.reshape(B, T, D_MODEL)

    oa_p = _attn_prompt(r3(q_p, Bp, Tp), r3(kb_p, Bp, Tp), vt_p, fc_p.reshape(Bp, Tp, N_HEADS))

    lfp3 = jnp.transpose(cache_logf.astype(F32), (0, 2, 1))
    lfn3 = lft_s.reshape(N_HEADS, Bs, Ts).transpose(1, 0, 2)
    oa_s = _attn_sample(r3(q_s, Bs, Ts), r3(kb_s, Bs, Ts), r3(vb_s, Bs, Ts),
                        cache_k.reshape(Bs, past, D_MODEL), cache_v.reshape(Bs, past, D_MODEL), lfp3, lfn3)

    zc = jnp.zeros((Bp, SUBLANES, D_MODEL), F32)
    zh = jnp.zeros((Bp, 1, D_MODEL), F32)
    rnn_w = (p['conv_w'], p['conv_b'], wr4, wi4, p['b_r'], p['b_i'], p['rglru_lambda'])
    post_w = (wpa, wpb, wo, wr1, wr2, br)
    x1_p, h2_p, rt_p, conv_p, hl_p = _mix(xp, oa_p, mod_p, zc, zh, p['norm1'], p['norm2'], wcat[:, d3:], p['b_gate'],
                                          *rnn_w, *post_w)
    x1_p, h2_p, rt_p = (a.reshape(np_rows, a.shape[-1]) for a in (x1_p, h2_p, rt_p))

    sc = jnp.pad(state_conv.astype(F32), ((0, 0), (SUBLANES - (CONV_WIDTH - 1), 0), (0, 0)))
    ob_s, conv_s, hl_s = _rnn(r3(xb_s, Bs, Ts), r3(gb_s, Bs, Ts), sc, state_h.astype(F32).reshape(Bs, 1, D_MODEL),
                              *rnn_w)
    x1_s, h2_s, rt_s = _post(xs.reshape(ns_rows, D_MODEL), oa_s.reshape(ns_rows, D_MODEL),
                             ob_s.reshape(ns_rows, D_MODEL), ga_s, gg_s, mod_s, p['norm2'], *post_w, tiles_per_mod=1)

    n_tok = np_rows + ns_rows
    rt = jnp.concatenate([rt_p, rt_s], axis=0)
    block_e, n_used, dest = _routing_plan(rt[:, :TOP_K].astype(jnp.int32), n_tok)
    slots = jnp.zeros((block_e.shape[0] * MOE_BLOCK, D_MODEL // 2), jnp.uint32)
    slots = _dispatch(h2_p, dest.reshape(-1), slots, tile_offset=0)
    slots = _dispatch(h2_s, dest.reshape(-1), slots, tile_offset=np_rows // ROW_TILE)
    take = lambda a, idx: a.at[idx].get(mode='promise_in_bounds')
    yb = _moe(slots, block_e, n_used, p['w1'], p['w3'], p['w2'])
    ya0 = take(yb, dest[:, 0])
    ya1 = take(yb, dest[:, 1])

    y_p = _final(x1_p, ya0, ya1, rt, mod_p, norm_f, rows_per_mod=Tp, row_offset=0)
    y_s = _final(x1_s, ya0, ya1, rt, mod_s, norm_f, rows_per_mod=ns_rows, row_offset=np_rows)

    st = CONV_WIDTH - 1
    state_p = (jnp.transpose(kt_p, (0, 3, 1, 2)), jnp.transpose(vt_p, (0, 3, 1, 2)),
               lf_p.reshape(Bp, Tp, N_HEADS), conv_p[:, SUBLANES - st:], hl_p.reshape(Bp, D_MODEL))
    state_s = (k_s.reshape(Bs, Ts, N_HEADS, HEAD_DIM), v_s.reshape(Bs, Ts, N_HEADS, HEAD_DIM),
               lf_s.reshape(Bs, Ts, N_HEADS), conv_s[:, SUBLANES - st:], hl_s.reshape(Bs, D_MODEL))
    return y_p.reshape(Bp, Tp, D_MODEL), y_s.reshape(Bs, Ts, D_MODEL), state_p, state_s


def kernel(x_prompt, x_sample, cache_k, cache_v, cache_logf, state_conv, state_h, c_prompt, c_sample, w_mod, b_mod, norm1, w_in, b_f, conv_w, conv_b, w_r, b_r, w_i, b_i, rglru_lambda, w_gate, b_gate, w_pa, w_pb, w_out, norm2, w_rg, b_rg, w_re, b_re, w1, w3, w2, norm_f):
    assert w_mod.shape[0] == 1, "single-layer trunk: the final norm is fused into the layer's last stage"
    names = ('w_mod', 'b_mod', 'norm1', 'w_in', 'b_f', 'conv_w', 'conv_b', 'w_r', 'b_r', 'w_i', 'b_i',
             'rglru_lambda', 'w_gate', 'b_gate', 'w_pa', 'w_pb', 'w_out', 'norm2', 'w_rg', 'b_rg', 'w_re', 'b_re',
             'w1', 'w3', 'w2')
    vals = (w_mod, b_mod, norm1, w_in, b_f, conv_w, conv_b, w_r, b_r, w_i, b_i, rglru_lambda, w_gate, b_gate,
            w_pa, w_pb, w_out, norm2, w_rg, b_rg, w_re, b_re, w1, w3, w2)
    p = {n: v[0] for n, v in zip(names, vals)}
    c_all = jnp.concatenate([c_prompt, c_sample], axis=0)
    y_p, y_s, sp, ss = _layer(x_prompt, x_sample, c_all, cache_k[0], cache_v[0], cache_logf[0],
                              state_conv[0], state_h[0], p, norm_f)
    lead = lambda a: a[None]
    return (y_p, y_s, lead(sp[0]), lead(sp[1]), lead(sp[2]), lead(sp[3]), lead(sp[4]),
            lead(ss[0]), lead(ss[1]), lead(ss[2]), lead(ss[3]), lead(ss[4]))
```

```python
import functools

import jax
import jax.numpy as jnp
from jax import lax
from jax.experimental import pallas as pl
from jax.experimental.pallas import tpu as pltpu

F32 = jnp.float32
BF16 = jnp.bfloat16

D_MODEL = 1024
N_HEADS = 16
HEAD_DIM = 64
HEAD_PAIR = 2 * HEAD_DIM
N_PAIRS = N_HEADS // 2
N_RNN_BLOCKS = 16
CONV_WIDTH = 4
RGLRU_C = 8.0
N_GROUPS = 4
EXPERTS_PER_GROUP = 8
N_EXPERTS = N_GROUPS * EXPERTS_PER_GROUP
TOP_K = 2
D_EXPERT = 512
EPS = 1e-6

LANES = 128
SUBLANES = 8
ROW_TILE = 512
FINAL_TILE = 1024
IN_TILE = 256
ATTN_BLOCK = 512
ATTN_QBLOCK = 1024
ATTN_VROWS = HEAD_DIM + 16
RNN_TILE = 256
MIX_TILE = 512
MOE_BLOCK = 512
MOE_SUB = 256
GATE_TILE = 256
VMEM_LIMIT = 56 * 1024 * 1024
NEG_BIG = -1e30
LOG2E = 1.4426950408889634


def _cparams(sem):
    return pltpu.CompilerParams(dimension_semantics=sem, vmem_limit_bytes=VMEM_LIMIT)


def _resident(shape):
    nd = len(shape)
    return pl.BlockSpec(shape, lambda *_: (0,) * nd, pipeline_mode=pl.Buffered(1))


def _split3(x):
    p1 = x.astype(BF16)
    r1 = x - p1.astype(F32)
    p2 = r1.astype(BF16)
    p3 = (r1 - p2.astype(F32)).astype(BF16)
    return p1, p2, p3


def _dot(a, b):
    return jnp.dot(a, b, preferred_element_type=F32)


def _dot_nt(a, b):
    return lax.dot_general(a, b, (((1,), (1,)), ((), ())), preferred_element_type=F32)


def _sigmoid(x):
    return 0.5 * jnp.tanh(0.5 * x) + 0.5


def _gelu_tanh(x):
    return 0.5 * x * (1.0 + jnp.tanh(0.7978845608028654 * (x + 0.044715 * x * x * x)))


def _rms(x, g):
    ms = jnp.mean(x * x, axis=-1, keepdims=True)
    return x * lax.rsqrt(ms + EPS) * g


def _pack_pairs(x):
    n = x.shape[1] // 2
    hi = pltpu.bitcast(x[:, :n].astype(BF16).astype(F32), jnp.uint32)
    lo = pltpu.bitcast(x[:, n:].astype(BF16).astype(F32), jnp.uint32)
    return hi | lax.shift_right_logical(lo, jnp.uint32(16))


def _unpack_pairs(w):
    hi = pltpu.bitcast(w & jnp.uint32(0xFFFF0000), F32)
    lo = pltpu.bitcast(lax.shift_left(w, jnp.uint32(16)), F32)
    return jnp.concatenate([hi, lo], axis=1)


def _div_p2(x, n):
    assert n & (n - 1) == 0
    return lax.shift_right_logical(x, jnp.int32(n.bit_length() - 1))


def _mod_p2(x, n):
    assert n & (n - 1) == 0
    return x & (n - 1)


def _upper_tri(n):
    r = lax.broadcasted_iota(jnp.int32, (n, n), 0)
    c = lax.broadcasted_iota(jnp.int32, (n, n), 1)
    return jnp.where(r <= c, 1.0, 0.0).astype(BF16)


def _cumsum_lanes(x, u):
    p1, p2, p3 = _split3(x)
    return _dot(p1, u) + _dot(p2, u) + _dot(p3, u)


def _mod_kernel(c_ref, w_ref, b_ref, o_ref):
    c = c_ref[...]
    s = c * _sigmoid(c)
    s1 = s.astype(BF16)
    s2 = (s - s1.astype(F32)).astype(BF16)
    w = w_ref[...]
    w1 = w.astype(BF16)
    w2 = (w - w1.astype(F32)).astype(BF16)
    o_ref[...] = _dot(s1, w1) + _dot(s1, w2) + _dot(s2, w1) + b_ref[...]


def _mod(c, w, b):
    n = c.shape[0]
    cols = w.shape[1]
    return pl.pallas_call(
        _mod_kernel,
        grid=(cols // D_MODEL,),
        in_specs=[pl.BlockSpec((n, D_MODEL), lambda j: (0, 0)),
                  pl.BlockSpec((D_MODEL, D_MODEL), lambda j: (0, j)),
                  pl.BlockSpec((1, D_MODEL), lambda j: (0, j))],
        out_specs=pl.BlockSpec((n, D_MODEL), lambda j: (0, j)),
        out_shape=jax.ShapeDtypeStruct((n, cols), F32),
        compiler_params=_cparams(("arbitrary",)),
        name="mod",
    )(c, w, b.reshape(1, cols))


def _in_kernel(x_ref, sh_ref, sc_ref, n1_ref, w_ref, wfl_ref, wflt_ref, bg_ref, bf_ref, bft_ref, *refs,
               tiles_per_seq, with_mixer):
    lf_ref, lft_ref, fc_ref, carry_ref = refs[-4:]
    x = x_ref[...]
    h = (_rms(x, n1_ref[...]) * (1.0 + sc_ref[...]) + sh_ref[...]).astype(BF16)

    def mm(c):
        return _dot(h, w_ref[:, c * D_MODEL:(c + 1) * D_MODEL])

    def head_major(y):
        slabs = []
        for p in range(N_PAIRS):
            pair = y[:, p * HEAD_PAIR:(p + 1) * HEAD_PAIR]
            slabs += [pair, pltpu.roll(pair, HEAD_DIM, 1)]
        return jnp.swapaxes(jnp.stack(slabs, axis=0), 0, 1)[:, :, :HEAD_DIM]

    def time_minor(y):
        return y.T.reshape(N_HEADS, HEAD_DIM, y.shape[0])

    refs[0][...] = mm(0).astype(BF16)
    k = mm(1)
    refs[1][...] = k.astype(BF16)
    v = mm(2)
    if with_mixer:
        vb_ref, k5_ref, v5_ref, xb_ref, gb_ref, ga_ref, gg_ref = refs[2:9]
        vb_ref[...] = v.astype(BF16)
        k5_ref[...] = head_major(k)
        v5_ref[...] = head_major(v)
        xb_ref[...] = mm(3).astype(BF16)
        gb_ref[...] = mm(4).astype(BF16)
        ga_ref[...] = _sigmoid(mm(5) + bg_ref[:, :D_MODEL]).astype(BF16)
        gg_ref[...] = _sigmoid(mm(6) + bg_ref[:, D_MODEL:]).astype(BF16)
    else:
        refs[2][...] = time_minor(k)
        refs[3][...] = time_minor(v)

    def log_sigmoid(z):
        return jnp.minimum(z, 0.0) - jnp.log(1.0 + jnp.exp(-jnp.abs(z)))

    lf = log_sigmoid(_dot(h, wfl_ref[...]) + bf_ref[...])
    lf_ref[...] = lf[:, :N_HEADS]
    lft_ref[...] = log_sigmoid(_dot_nt(wflt_ref[...], h)[:N_HEADS, :] + bft_ref[...])

    @pl.when(pl.program_id(0) % tiles_per_seq == 0)
    def _():
        carry_ref[...] = jnp.zeros_like(carry_ref)

    tm = x.shape[0]
    r = lax.broadcasted_iota(jnp.int32, (tm, tm), 0)
    c = lax.broadcasted_iota(jnp.int32, (tm, tm), 1)
    lower = jnp.where(c <= r, 1.0, 0.0).astype(BF16)
    p1, p2, p3 = _split3(lf)
    fc = _dot(lower, p1) + _dot(lower, p2) + _dot(lower, p3) + carry_ref[...]
    fc_ref[...] = fc[:, :N_HEADS]
    carry_ref[...] = fc[tm - 1:tm, :]


def _in_proj(x2, mod3, norm1, wcat, wfl, wflt, b_gate, b_f, *, tiles_per_mod, tiles_per_seq, with_mixer):
    rows = x2.shape[0]
    tm = IN_TILE
    nt = rows // tm
    row_blk = pl.BlockSpec((tm, D_MODEL), lambda i: (i, 0))
    if mod3.shape[1] == 1:
        modc = lambda c: pl.BlockSpec((None, 1, D_MODEL), lambda i: (i // tiles_per_mod, 0, c))
    else:
        modc = lambda c: pl.BlockSpec((None, tm, D_MODEL), lambda i: (0, i, c))
    row_out = jax.ShapeDtypeStruct((rows, D_MODEL), BF16)
    if with_mixer:
        kv_shape = jax.ShapeDtypeStruct((rows, N_HEADS, HEAD_DIM), F32)
        kv_blk = pl.BlockSpec((tm, N_HEADS, HEAD_DIM), lambda i: (i, 0, 0))
        lead = ([row_out] * 3 + [kv_shape] * 2 + [row_out] * 4, [row_blk] * 3 + [kv_blk] * 2 + [row_blk] * 4)
    else:
        n_seq = nt // tiles_per_seq
        kv_shape = jax.ShapeDtypeStruct((n_seq, N_HEADS, HEAD_DIM, tiles_per_seq * tm), F32)
        kv_blk = pl.BlockSpec((None, N_HEADS, HEAD_DIM, tm), lambda i: (i // tiles_per_seq, 0, 0, i % tiles_per_seq))
        lead = ([row_out] * 2 + [kv_shape] * 2, [row_blk] * 2 + [kv_blk] * 2)
    outs = [*lead[0],
            jax.ShapeDtypeStruct((rows, N_HEADS), F32),
            jax.ShapeDtypeStruct((N_HEADS, rows), F32),
            jax.ShapeDtypeStruct((rows, N_HEADS), F32)]
    return pl.pallas_call(
        functools.partial(_in_kernel, tiles_per_seq=tiles_per_seq, with_mixer=with_mixer),
        grid=(nt,),
        in_specs=[row_blk, modc(0), modc(1),
                  _resident((1, D_MODEL)),
                  _resident(wcat.shape),
                  _resident(wfl.shape),
                  _resident(wflt.shape),
                  _resident((1, 2 * D_MODEL)),
                  _resident((1, LANES)),
                  _resident((N_HEADS, 1))],
        out_specs=lead[1]
                  + [pl.BlockSpec((tm, N_HEADS), lambda i: (i, 0)),
                     pl.BlockSpec((N_HEADS, tm), lambda i: (0, i)),
                     pl.BlockSpec((tm, N_HEADS), lambda i: (i, 0))],
        out_shape=outs,
        scratch_shapes=[pltpu.VMEM((1, LANES), F32)],
        compiler_params=_cparams(("arbitrary",)),
        name="in_proj",
    )(x2, mod3, mod3, norm1.reshape(1, D_MODEL), wcat, wfl, wflt,
      b_gate.reshape(1, 2 * D_MODEL), jnp.pad(b_f, (0, LANES - N_HEADS)).reshape(1, LANES),
      b_f.reshape(N_HEADS, 1))


def _attn_kernel(q_ref, k_ref, v_ref, f_ref, o_ref, kb_ref, vt_ref, qt_ref, s0_ref, m_ref, acc_ref):
    qi = pl.program_id(2)
    bq = q_ref.shape[0]
    bk = ATTN_BLOCK
    nk = k_ref.shape[0] // bk
    own_lanes = (lambda i: i < HEAD_DIM, lambda i: i >= HEAD_DIM)
    bias_at = (HEAD_DIM, 0)

    @pl.when(qi == 0)
    def _():
        lane = lax.broadcasted_iota(jnp.int32, (bk, HEAD_PAIR), 1)
        ones_rows = jnp.where(lax.broadcasted_iota(jnp.int32, (ATTN_VROWS - HEAD_DIM, bk), 0) == 0, 1.0, 0.0)
        for c in range(nk):
            blk = slice(c * bk, (c + 1) * bk)
            k = k_ref[blk, :].astype(F32)
            f = f_ref[blk, :] * LOG2E
            head = lax.broadcasted_iota(jnp.int32, f.shape, 1)
            for hh in range(2):
                mine = head == 2 * pl.program_id(1) + hh
                f_head = jnp.sum(jnp.where(mine, f, 0.0), axis=1, keepdims=True)
                slab = jnp.zeros(k.shape, F32)
                for n, piece in enumerate(_split3(f_head)):
                    slab = jnp.where(lane == bias_at[hh] + n, piece.astype(F32), slab)
                kb_ref[hh, blk, :] = jnp.where(own_lanes[hh](lane), k, slab).astype(BF16)
            for hh in range(2):
                vt_ref[c, hh, 0:HEAD_DIM] = v_ref[hh, :, blk].astype(BF16)
                vt_ref[c, hh, HEAD_DIM:ATTN_VROWS] = ones_rows.astype(BF16)

    qt = q_ref[...].astype(F32).T
    feat = lax.broadcasted_iota(jnp.int32, qt.shape, 0)
    for hh in range(2):
        is_bias = (feat >= bias_at[hh]) & (feat < bias_at[hh] + 3)
        qt_ref[hh] = jnp.where(own_lanes[hh](feat), qt, jnp.where(is_bias, -1.0, 0.0)).astype(BF16)

    m_ref[...] = jnp.full(m_ref.shape, NEG_BIG, F32)
    acc_ref[...] = jnp.zeros(acc_ref.shape, F32)

    nsub = bq // bk

    def scores(hh, u, j):
        keys = kb_ref[hh, pl.ds(pl.multiple_of(j * bk, bk), bk), :]
        return _dot(keys, qt_ref[hh, :, u * bk:(u + 1) * bk])

    def softmax_pv(hh, u, j, s, masked):
        cols = slice(u * bk, (u + 1) * bk)
        if masked:
            key = lax.broadcasted_iota(jnp.int32, s.shape, 0)
            qry = lax.broadcasted_iota(jnp.int32, s.shape, 1)
            s = jnp.where(key <= qry, s, NEG_BIG)
        m_prev = m_ref[hh, :, cols]
        m_new = jnp.maximum(m_prev, jnp.max(s, axis=0, keepdims=True))
        p = jnp.exp2(s - m_new).astype(BF16)
        acc_ref[hh, :, cols] = jnp.exp2(m_prev - m_new) * acc_ref[hh, :, cols] + _dot(vt_ref[j, hh], p)
        m_ref[hh, :, cols] = m_new

    def run(items, first_of_next):
        s = s0_ref[...]
        for n, (hh, u, j, masked) in enumerate(items):
            if n + 1 < len(items):
                nxt = scores(*items[n + 1][:3])
            elif first_of_next is not None:
                s0_ref[...] = scores(*first_of_next)
            softmax_pv(hh, u, j, s, masked)
            s = nxt if n + 1 < len(items) else None

    def body(j, carry):
        run([(hh, u, j, False) for u in range(nsub) for hh in range(2)], (0, 0, j + 1))
        return carry

    s0_ref[...] = scores(0, 0, 0)
    lax.fori_loop(0, nsub * qi, body, 0)
    for d in range(nsub):
        j = nsub * qi + d
        items = [(hh, u, j, u == d) for u in range(d, nsub) for hh in range(2)]
        run(items, (0, d + 1, j + 1) if d + 1 < nsub else None)

    heads = [acc_ref[hh, 0:HEAD_DIM] / acc_ref[hh, HEAD_DIM:HEAD_DIM + 1] for hh in range(2)]
    o_ref[...] = jnp.concatenate(heads, axis=0).T.astype(BF16)


def _attn_prompt(q3, k3, vt4, fc3):
    B, T, _ = q3.shape
    bq, bk = ATTN_QBLOCK, ATTN_BLOCK
    nq, nk = T // bq, T // bk
    return pl.pallas_call(
        _attn_kernel,
        grid=(B, N_PAIRS, nq),
        in_specs=[pl.BlockSpec((None, bq, HEAD_PAIR), lambda b, p, i: (b, i, p)),
                  pl.BlockSpec((None, T, HEAD_PAIR), lambda b, p, i: (b, 0, p)),
                  pl.BlockSpec((None, 2, HEAD_DIM, T), lambda b, p, i: (b, p, 0, 0)),
                  pl.BlockSpec((None, T, N_HEADS), lambda b, p, i: (b, 0, 0))],
        out_specs=pl.BlockSpec((None, bq, HEAD_PAIR), lambda b, p, i: (b, i, p)),
        out_shape=jax.ShapeDtypeStruct((B, T, D_MODEL), BF16),
        scratch_shapes=[pltpu.VMEM((2, T, HEAD_PAIR), BF16),
                        pltpu.VMEM((nk, 2, ATTN_VROWS, bk), BF16),
                        pltpu.VMEM((2, HEAD_PAIR, bq), BF16),
                        pltpu.VMEM((bk, bk), F32),
                        pltpu.VMEM((2, 1, bq), F32),
                        pltpu.VMEM((2, ATTN_VROWS, bq), F32)],
        compiler_params=_cparams(("arbitrary", "arbitrary", "arbitrary")),
        name="attn_prompt",
    )(q3, k3, vt4, fc3)


def _attn_sample_kernel(q_ref, kn_ref, vn_ref, kc_ref, vc_ref, lfp_ref, lfn_ref, o_ref):
    tq = q_ref.shape[0]
    past = kc_ref.shape[-1]
    nrow = N_HEADS * tq
    row = lax.broadcasted_iota(jnp.int32, (nrow, D_MODEL), 0)
    col = lax.broadcasted_iota(jnp.int32, (nrow, D_MODEL), 1)
    own = _div_p2(row, tq) == _div_p2(col, HEAD_DIM)

    rr = lax.broadcasted_iota(jnp.int32, (nrow, tq), 0)
    rc = lax.broadcasted_iota(jnp.int32, (nrow, tq), 1)
    rep = jnp.where(_mod_p2(rr, tq) == rc, 1.0, 0.0).astype(BF16)
    qx = jnp.where(own, _dot(rep, q_ref[...]), 0.0).astype(BF16)

    fp = _cumsum_lanes(lfp_ref[...] * LOG2E, _upper_tri(past))
    fn = _cumsum_lanes(lfn_ref[...] * LOG2E, _upper_tri(tq)) + fp[:, past - 1:past]
    hr = lax.broadcasted_iota(jnp.int32, (nrow, N_HEADS), 0)
    hc = lax.broadcasted_iota(jnp.int32, (nrow, N_HEADS), 1)
    hsel = jnp.where(_div_p2(hr, tq) == hc, 1.0, 0.0).astype(BF16)

    def expand(f):
        p1, p2, p3 = _split3(f)
        return _dot(hsel, p1) + _dot(hsel, p2) + _dot(hsel, p3)

    s_p = _dot(qx, kc_ref[...].reshape(D_MODEL, past).astype(BF16)) - expand(fp)
    s_n = _dot_nt(qx, kn_ref[...]) - expand(fn)
    nr = lax.broadcasted_iota(jnp.int32, (nrow, tq), 0)
    nc = lax.broadcasted_iota(jnp.int32, (nrow, tq), 1)
    s_n = jnp.where(nc <= _mod_p2(nr, tq), s_n, NEG_BIG)
    m = jnp.maximum(jnp.max(s_p, axis=-1, keepdims=True), jnp.max(s_n, axis=-1, keepdims=True))
    p_p = jnp.exp2(s_p - m)
    p_n = jnp.exp2(s_n - m)
    l = jnp.sum(p_p, axis=-1, keepdims=True) + jnp.sum(p_n, axis=-1, keepdims=True)
    o_full = (_dot_nt(p_p.astype(BF16), vc_ref[...].reshape(D_MODEL, past).astype(BF16))
              + _dot(p_n.astype(BF16), vn_ref[...]))
    o_own = jnp.where(own, o_full / l, 0.0).astype(BF16)
    cr = lax.broadcasted_iota(jnp.int32, (tq, nrow), 0)
    cc = lax.broadcasted_iota(jnp.int32, (tq, nrow), 1)
    col_sel = jnp.where(_mod_p2(cc, tq) == cr, 1.0, 0.0).astype(BF16)
    o_ref[...] = _dot(col_sel, o_own).astype(BF16)


def _attn_sample(q3, kn3, vn3, kct, vct, lfp3, lfn3):
    B, tq, _ = q3.shape
    past = kct.shape[-1]
    blk = lambda n, w: pl.BlockSpec((None, n, w), lambda b: (b, 0, 0))
    cache = pl.BlockSpec((None, N_HEADS, HEAD_DIM, past), lambda b: (b, 0, 0, 0))
    return pl.pallas_call(
        _attn_sample_kernel,
        grid=(B,),
        in_specs=[blk(tq, D_MODEL), blk(tq, D_MODEL), blk(tq, D_MODEL),
                  cache, cache, blk(N_HEADS, past), blk(N_HEADS, tq)],
        out_specs=blk(tq, D_MODEL),
        out_shape=jax.ShapeDtypeStruct((B, tq, D_MODEL), BF16),
        compiler_params=_cparams(("arbitrary",)),
        name="attn_sample",
    )(q3, kn3, vn3, kct, vct, lfp3, lfn3)


def _rglru_gates(xb, xp_ref, cw_ref, cb_ref, wr_ref, wi_ref):
    tt = xb.shape[0]
    pad = SUBLANES
    xp_ref[pad:pad + tt, :] = xb
    xc = cb_ref[...] + xb * cw_ref[CONV_WIDTH - 1:CONV_WIDTH, :]
    for k in range(CONV_WIDTH - 1):
        back = CONV_WIDTH - 1 - k
        xc = xc + xp_ref[pad - back:pad - back + tt, :] * cw_ref[k:k + 1, :]
    tail = xp_ref[tt:tt + pad, :]
    xp_ref[0:pad, :] = tail

    xcb = xc.astype(BF16)
    ng = D_MODEL // GATE_TILE
    r = jnp.concatenate([_dot(xcb[:, g * GATE_TILE:(g + 1) * GATE_TILE], wr_ref[g]) for g in range(ng)], axis=-1)
    i = jnp.concatenate([_dot(xcb[:, g * GATE_TILE:(g + 1) * GATE_TILE], wi_ref[g]) for g in range(ng)], axis=-1)
    return xc, r, i, tail


def _rglru_scan(xc, r, i, gb, hc_ref, br_ref, bi_ref, lam_ref):
    tt = xc.shape[0]
    r = _sigmoid(r + br_ref[...])
    i = _sigmoid(i + bi_ref[...])
    lam = lam_ref[...]
    softplus_neg = jnp.maximum(-lam, 0.0) + jnp.log(1.0 + jnp.exp(-jnp.abs(lam)))
    log_a = (-RGLRU_C) * r * softplus_neg
    a = jnp.exp(log_a)
    th = jnp.tanh(log_a)
    mult = jnp.sqrt(-2.0 * th / (1.0 - th))
    b = mult * (i * xc)

    ng8 = tt // SUBLANES
    a3 = a.reshape(ng8, SUBLANES, D_MODEL)
    b3 = b.reshape(ng8, SUBLANES, D_MODEL)
    sub = lax.broadcasted_iota(jnp.int32, a3.shape, 1)
    for s in (1, 2, 4):
        keep = sub >= s
        a_sh = jnp.where(keep, pltpu.roll(a3, s, 1), 1.0)
        b_sh = jnp.where(keep, pltpu.roll(b3, s, 1), 0.0)
        b3 = a3 * b_sh + b3
        a3 = a3 * a_sh
    hc = hc_ref[...]
    hs = []
    for g in range(ng8):
        hg = b3[g] + a3[g] * hc
        hs.append(hg)
        hc = hg[SUBLANES - 1:SUBLANES, :]
    hc_ref[...] = hc
    return jnp.concatenate(hs, axis=0) * _gelu_tanh(gb), hc


def _rnn_kernel(xb_ref, gb_ref, c0_ref, h0_ref, cw_ref, cb_ref, wr_ref, wi_ref, br_ref, bi_ref, lam_ref,
                o_ref, cout_ref, hout_ref, xp_ref, hc_ref):
    ti = pl.program_id(1)

    @pl.when(ti == 0)
    def _():
        xp_ref[0:SUBLANES, :] = c0_ref[...]
        hc_ref[...] = h0_ref[...]

    xc, r, i, tail = _rglru_gates(xb_ref[...].astype(F32), xp_ref, cw_ref, cb_ref, wr_ref, wi_ref)
    ob, hc = _rglru_scan(xc, r, i, gb_ref[...].astype(F32), hc_ref, br_ref, bi_ref, lam_ref)
    o_ref[...] = ob.astype(BF16)

    @pl.when(ti == pl.num_programs(1) - 1)
    def _():
        cout_ref[...] = tail
        hout_ref[...] = hc


def _rnn(xb3, gb3, conv0, h0, conv_w, conv_b, wr4, wi4, b_r, b_i, lam):
    B, T, _ = xb3.shape
    tt = min(RNN_TILE, T)
    vec = lambda a: a.reshape(1, D_MODEL)
    seq = pl.BlockSpec((None, tt, D_MODEL), lambda b, t: (b, t, 0))
    st8 = pl.BlockSpec((None, SUBLANES, D_MODEL), lambda b, t: (b, 0, 0))
    st1 = pl.BlockSpec((None, 1, D_MODEL), lambda b, t: (b, 0, 0))
    return pl.pallas_call(
        _rnn_kernel,
        grid=(B, T // tt),
        in_specs=[seq, seq, st8, st1,
                  _resident((CONV_WIDTH, D_MODEL)), _resident((1, D_MODEL)),
                  _resident(wr4.shape), _resident(wi4.shape),
                  _resident((1, D_MODEL)), _resident((1, D_MODEL)), _resident((1, D_MODEL))],
        out_specs=[seq, st8, st1],
        out_shape=[jax.ShapeDtypeStruct((B, T, D_MODEL), BF16),
                   jax.ShapeDtypeStruct((B, SUBLANES, D_MODEL), F32),
                   jax.ShapeDtypeStruct((B, 1, D_MODEL), F32)],
        scratch_shapes=[pltpu.VMEM((tt + SUBLANES, D_MODEL), F32),
                        pltpu.VMEM((1, D_MODEL), F32)],
        compiler_params=_cparams(("arbitrary", "arbitrary")),
        name="rnn",
    )(xb3, gb3, conv0, h0, conv_w, vec(conv_b), wr4, wi4, vec(b_r), vec(b_i), vec(lam))


def _merge_route_tile(x, pa, ob, ga, gg, g1, sh2, sc2, n2_ref, wpb_ref, wo_ref, wr1_ref, wr2_ref, br_ref):
    mix = ga * pa + gg * _dot(ob, wpb_ref[...])
    x1 = x + g1 * _dot(mix.astype(BF16), wo_ref[...])
    h2 = _rms(x1, n2_ref[...]) * (1.0 + sc2) + sh2
    h2a = h2.astype(BF16)
    h2b = (h2 - h2a.astype(F32)).astype(BF16)
    lg = _dot(h2a, wr1_ref[...]) + _dot(h2a, wr2_ref[...]) + _dot(h2b, wr1_ref[...]) + br_ref[...]

    lane = lax.broadcasted_iota(jnp.int32, lg.shape, 1).astype(F32)
    big = float(LANES)
    is_g = lane < N_GROUPS
    gl = jnp.where(is_g, lg, NEG_BIG)
    mg = jnp.max(gl, axis=-1, keepdims=True)
    sg = jnp.sum(jnp.where(is_g, jnp.exp(gl - mg), 0.0), axis=-1, keepdims=True)
    pg = 1.0 / sg
    gidx = jnp.min(jnp.where(is_g & (gl == mg), lane, big), axis=-1, keepdims=True)
    lo = N_GROUPS + EXPERTS_PER_GROUP * gidx
    ing = (lane >= lo) & (lane < lo + EXPERTS_PER_GROUP)
    el = jnp.where(ing, lg, NEG_BIG)
    m1 = jnp.max(el, axis=-1, keepdims=True)
    se = jnp.sum(jnp.where(ing, jnp.exp(el - m1), 0.0), axis=-1, keepdims=True)
    i1 = jnp.min(jnp.where(ing & (el == m1), lane, big), axis=-1, keepdims=True)
    el2 = jnp.where(lane == i1, NEG_BIG, el)
    m2 = jnp.max(el2, axis=-1, keepdims=True)
    i2 = jnp.min(jnp.where(ing & (el2 == m2) & (lane != i1), lane, big), axis=-1, keepdims=True)
    p1 = 1.0 / se
    p2 = jnp.exp(m2 - m1) / se
    den = p1 + p2
    w1 = pg * p1 / den
    w2 = pg * p2 / den
    rt = jnp.where(lane == 0.0, i1 - N_GROUPS,
                   jnp.where(lane == 1.0, i2 - N_GROUPS,
                             jnp.where(lane == 2.0, w1, jnp.where(lane == 3.0, w2, 0.0))))
    return x1, h2, rt


def _post_kernel(x_ref, oa_ref, ob_ref, ga_ref, gg_ref, g1_ref, sh_ref, sc_ref, n2_ref,
                 wpa_ref, wpb_ref, wo_ref, wr1_ref, wr2_ref, br_ref,
                 x1_ref, h2_ref, rt_ref):
    x1, h2, rt = _merge_route_tile(x_ref[...], _dot(oa_ref[...], wpa_ref[...]), ob_ref[...],
                                   ga_ref[...].astype(F32), gg_ref[...].astype(F32), g1_ref[...], sh_ref[...],
                                   sc_ref[...], n2_ref, wpb_ref, wo_ref, wr1_ref, wr2_ref, br_ref)
    x1_ref[...] = x1
    h2_ref[...] = _pack_pairs(h2)
    rt_ref[...] = rt


def _post(x2, oa2, ob2, ga2, gg2, mod3, norm2, wpa, wpb, wo, wr1, wr2, br, *, tiles_per_mod):
    rows = x2.shape[0]
    tm = ROW_TILE
    mr = mod3.shape[1]
    row = pl.BlockSpec((tm, D_MODEL), lambda i: (i, 0))
    modc = lambda c: pl.BlockSpec((None, mr, D_MODEL), lambda i: (i // tiles_per_mod, 0, c))
    return pl.pallas_call(
        _post_kernel,
        grid=(rows // tm,),
        in_specs=[row, row, row, row, row, modc(2), modc(3), modc(4), _resident((1, D_MODEL)),
                  _resident(wpa.shape), _resident(wpb.shape), _resident(wo.shape),
                  _resident(wr1.shape), _resident(wr2.shape), _resident((1, LANES))],
        out_specs=[row, pl.BlockSpec((tm, D_MODEL // 2), lambda i: (i, 0)), pl.BlockSpec((tm, LANES), lambda i: (i, 0))],
        out_shape=[jax.ShapeDtypeStruct((rows, D_MODEL), F32),
                   jax.ShapeDtypeStruct((rows, D_MODEL // 2), jnp.uint32),
                   jax.ShapeDtypeStruct((rows, LANES), F32)],
        compiler_params=_cparams(("arbitrary",)),
        name="post",
    )(x2, oa2, ob2, ga2, gg2, mod3, mod3, mod3, norm2.reshape(1, D_MODEL), wpa, wpb, wo, wr1, wr2, br)


def _mix_kernel(x_ref, oa_ref, sh1_ref, sc1_ref, g1_ref, sh2_ref, sc2_ref, n1_ref, n2_ref, w_ref, bg_ref,
                c0_ref, h0_ref, cw_ref, cb_ref, wr_ref, wi_ref, br_ref, bi_ref, lam_ref,
                wpa_ref, wpb_ref, wo_ref, wr1_ref, wr2_ref, brt_ref,
                x1_ref, h2_ref, rt_ref, cout_ref, hout_ref, xp_ref, hc_ref):
    ti = pl.program_id(1)

    @pl.when(ti == 0)
    def _():
        xp_ref[0:SUBLANES, :] = c0_ref[...]
        hc_ref[...] = h0_ref[...]

    def project(rows):
        x = x_ref[rows, :]
        h = (_rms(x, n1_ref[...]) * (1.0 + sc1_ref[...]) + sh1_ref[...]).astype(BF16)
        mm = lambda c: _dot(h, w_ref[:, c * D_MODEL:(c + 1) * D_MODEL])
        xc, r, i, tail = _rglru_gates(mm(0), xp_ref, cw_ref, cb_ref, wr_ref, wi_ref)
        return dict(x=x, xc=xc, r=r, i=i, tail=tail, gb=mm(1), ga=mm(2), gg=mm(3),
                    pa=_dot(oa_ref[rows, :], wpa_ref[...]))

    def recur(st):
        ob, hc = _rglru_scan(st['xc'], st['r'], st['i'], st['gb'], hc_ref, br_ref, bi_ref, lam_ref)
        return ob.astype(BF16), hc

    def merge(rows, st, ob):
        ga = _sigmoid(st['ga'] + bg_ref[:, :D_MODEL])
        gg = _sigmoid(st['gg'] + bg_ref[:, D_MODEL:])
        x1, h2, rt = _merge_route_tile(st['x'], st['pa'], ob, ga, gg, g1_ref[...], sh2_ref[...], sc2_ref[...],
                                       n2_ref, wpb_ref, wo_ref, wr1_ref, wr2_ref, brt_ref)
        x1_ref[rows, :] = x1
        h2_ref[rows, :] = _pack_pairs(h2)
        rt_ref[rows, :] = rt

    subs = [slice(s, s + RNN_TILE) for s in range(0, x_ref.shape[0], RNN_TILE)]
    states = [project(subs[0])]
    for n, rows in enumerate(subs):
        if n + 1 < len(subs):
            states.append(project(subs[n + 1]))
        ob, hc = recur(states[n])
        merge(rows, states[n], ob)

    @pl.when(ti == pl.num_programs(1) - 1)
    def _():
        cout_ref[...] = states[-1]['tail']
        hout_ref[...] = hc


def _mix(x3, oa3, mod3, conv0, h0, norm1, norm2, w_mix, b_gate, conv_w, conv_b, wr4, wi4, b_r, b_i, lam,
         wpa, wpb, wo, wr1, wr2, br):
    B, T, _ = x3.shape
    tt = MIX_TILE
    vec = lambda a: a.reshape(1, D_MODEL)
    seq = lambda w: pl.BlockSpec((None, tt, w), lambda b, t: (b, t, 0))
    modc = lambda c: pl.BlockSpec((None, 1, D_MODEL), lambda b, t: (b, 0, c))
    st8 = pl.BlockSpec((None, SUBLANES, D_MODEL), lambda b, t: (b, 0, 0))
    st1 = pl.BlockSpec((None, 1, D_MODEL), lambda b, t: (b, 0, 0))
    v1 = _resident((1, D_MODEL))
    return pl.pallas_call(
        _mix_kernel,
        grid=(B, T // tt),
        in_specs=[seq(D_MODEL), seq(D_MODEL), modc(0), modc(1), modc(2), modc(3), modc(4), v1, v1,
                  _resident(w_mix.shape), _resident((1, 2 * D_MODEL)), st8, st1,
                  _resident((CONV_WIDTH, D_MODEL)), v1, _resident(wr4.shape), _resident(wi4.shape), v1, v1, v1,
                  _resident(wpa.shape), _resident(wpb.shape), _resident(wo.shape),
                  _resident(wr1.shape), _resident(wr2.shape), _resident((1, LANES))],
        out_specs=[seq(D_MODEL), seq(D_MODEL // 2), seq(LANES), st8, st1],
        out_shape=[jax.ShapeDtypeStruct((B, T, D_MODEL), F32),
                   jax.ShapeDtypeStruct((B, T, D_MODEL // 2), jnp.uint32),
                   jax.ShapeDtypeStruct((B, T, LANES), F32),
                   jax.ShapeDtypeStruct((B, SUBLANES, D_MODEL), F32),
                   jax.ShapeDtypeStruct((B, 1, D_MODEL), F32)],
        scratch_shapes=[pltpu.VMEM((RNN_TILE + SUBLANES, D_MODEL), F32),
                        pltpu.VMEM((1, D_MODEL), F32)],
        compiler_params=_cparams(("arbitrary", "arbitrary")),
        name="mix",
    )(x3, oa3, mod3, mod3, mod3, mod3, mod3, vec(norm1), vec(norm2), w_mix, b_gate.reshape(1, 2 * D_MODEL),
      conv0, h0, conv_w, vec(conv_b), wr4, wi4, vec(b_r), vec(b_i), vec(lam), wpa, wpb, wo, wr1, wr2, br)


def _dispatch_kernel(dest_ref, x_ref, slots_in_ref, slots_ref, sem):
    del slots_in_ref
    tm = x_ref.shape[0]

    def row_copy(r, slot):
        return pltpu.make_async_copy(x_ref.at[pl.ds(r, 1)], slots_ref.at[pl.ds(slot, 1)], sem)

    def issue(r, carry):
        for k in range(TOP_K):
            row_copy(r, dest_ref[TOP_K * r + k]).start()
        return carry

    lax.fori_loop(0, tm, issue, 0, unroll=8)

    for _ in range(TOP_K):
        pltpu.make_async_copy(x_ref, slots_ref.at[pl.ds(0, tm)], sem).wait()


def _dispatch(x, dest, slots, *, tile_offset):
    rows, width = x.shape
    tm = ROW_TILE
    return pl.pallas_call(
        _dispatch_kernel,
        grid=(rows // tm,),
        in_specs=[pl.BlockSpec((tm * TOP_K,), lambda i: (i + tile_offset,), memory_space=pltpu.SMEM),
                  pl.BlockSpec((tm, width), lambda i: (i, 0)),
                  pl.BlockSpec(memory_space=pl.ANY)],
        out_specs=pl.BlockSpec(memory_space=pl.ANY),
        out_shape=jax.ShapeDtypeStruct(slots.shape, slots.dtype),
        scratch_shapes=[pltpu.SemaphoreType.DMA(())],
        input_output_aliases={2: 0},
        compiler_params=_cparams(("arbitrary",)),
        name="dispatch",
    )(dest, x, slots)


def _moe_kernel(be_ref, nb_ref, x_ref, w1_ref, w3_ref, w2_ref, o_ref, w1b_ref, w3b_ref, w2b_ref):
    i = pl.program_id(0)

    @pl.when((i == 0) | (be_ref[i] != be_ref[jnp.maximum(i - 1, 0)]))
    def _():
        w1b_ref[...] = w1_ref[...].astype(BF16)
        w3b_ref[...] = w3_ref[...].astype(BF16)
        w2b_ref[...] = w2_ref[...].astype(BF16)

    @pl.when(i < nb_ref[0])
    def _():
        subs = [slice(s, s + MOE_SUB) for s in range(0, x_ref.shape[0], MOE_SUB)]
        ups = []
        for rows in subs:
            x = _unpack_pairs(x_ref[rows, :]).astype(BF16)
            ups.append((_dot(x, w1b_ref[...]), _dot(x, w3b_ref[...])))
        for rows, (a, b) in zip(subs, ups):
            g = a * _sigmoid(a) * b
            o_ref[rows, :] = _pack_pairs(_dot(g.astype(BF16), w2b_ref[...]))

    @pl.when(i >= nb_ref[0])
    def _():
        o_ref[...] = jnp.zeros_like(o_ref)


def _moe(xs, block_e, n_used, w1, w3, w2):
    n_blocks = block_e.shape[0]
    bm = MOE_BLOCK
    wspec = lambda s: pl.BlockSpec((None,) + s, lambda i, be, nb: (be[i], 0, 0))
    gs = pltpu.PrefetchScalarGridSpec(
        num_scalar_prefetch=2,
        grid=(n_blocks,),
        in_specs=[pl.BlockSpec((bm, D_MODEL // 2), lambda i, be, nb: (i, 0)),
                  wspec((D_MODEL, D_EXPERT)), wspec((D_MODEL, D_EXPERT)), wspec((D_EXPERT, D_MODEL))],
        out_specs=pl.BlockSpec((bm, D_MODEL // 2), lambda i, be, nb: (i, 0)),
        scratch_shapes=[pltpu.VMEM((D_MODEL, D_EXPERT), BF16),
                        pltpu.VMEM((D_MODEL, D_EXPERT), BF16),
                        pltpu.VMEM((D_EXPERT, D_MODEL), BF16)],
    )
    return pl.pallas_call(
        _moe_kernel,
        grid_spec=gs,
        out_shape=jax.ShapeDtypeStruct((n_blocks * bm, D_MODEL // 2), jnp.uint32),
        compiler_params=_cparams(("arbitrary",)),
        name="moe",
    )(block_e, n_used, xs, w1, w3, w2)


def _final_kernel(x1_ref, ya_ref, yb_ref, rt_ref, g2_ref, nf_ref, o_ref):
    rt = rt_ref[...]
    moe = (rt[:, TOP_K:TOP_K + 1] * _unpack_pairs(ya_ref[...])
           + rt[:, TOP_K + 1:TOP_K + 2] * _unpack_pairs(yb_ref[...]))
    o_ref[...] = _rms(x1_ref[...] + g2_ref[...] * moe, nf_ref[...])


def _final(x1, ya, yb, rt, mod3, norm_f, *, rows_per_mod, row_offset):
    rows = x1.shape[0]
    tm = min(FINAL_TILE, rows)
    assert row_offset % tm == 0 and rows_per_mod % tm == 0
    tile_offset, tiles_per_mod = row_offset // tm, rows_per_mod // tm
    mr = mod3.shape[1]
    row = pl.BlockSpec((tm, D_MODEL), lambda i: (i, 0))
    tok = lambda w: pl.BlockSpec((tm, w), lambda i: (i + tile_offset, 0))
    return pl.pallas_call(
        _final_kernel,
        grid=(rows // tm,),
        in_specs=[row, tok(D_MODEL // 2), tok(D_MODEL // 2), tok(LANES),
                  pl.BlockSpec((None, mr, D_MODEL), lambda i: (i // tiles_per_mod, 0, 5)),
                  _resident((1, D_MODEL))],
        out_specs=row,
        out_shape=jax.ShapeDtypeStruct((rows, D_MODEL), F32),
        compiler_params=_cparams(("arbitrary",)),
        name="final",
    )(x1, ya, yb, rt, mod3, norm_f.reshape(1, D_MODEL))


def _routing_plan(eid, n_tokens):
    bm = MOE_BLOCK
    n_assign = n_tokens * TOP_K
    n_blocks = -(-n_assign // bm) + N_EXPERTS
    flat_e = eid.reshape(-1)
    onehot = (flat_e[:, None] == jnp.arange(N_EXPERTS, dtype=jnp.int32)[None, :]).astype(jnp.int32)
    csum = jnp.cumsum(onehot, axis=0)
    counts = csum[-1]
    padded = (counts + bm - 1) // bm * bm
    pad_end = jnp.cumsum(padded)
    pad_start = pad_end - padded
    dest = jnp.sum(onehot * (csum - 1 + pad_start[None, :]), axis=1).astype(jnp.int32)
    block_start = jnp.arange(n_blocks, dtype=jnp.int32) * bm
    block_e = jnp.minimum(jnp.sum((pad_end[None, :] <= block_start[:, None]).astype(jnp.int32), axis=1),
                          N_EXPERTS - 1).astype(jnp.int32)
    n_used = (pad_end[-1] // bm).astype(jnp.int32).reshape(1)
    return block_e, n_used, dest.reshape(n_tokens, TOP_K)


def _layer(xp, xs, c_all, cache_k, cache_v, cache_logf, state_conv, state_h, p, norm_f):
    Bp, Tp, _ = xp.shape
    Bs, Ts, _ = xs.shape
    past = cache_k.shape[1]
    np_rows, ns_rows = Bp * Tp, Bs * Ts
    assert Tp % ROW_TILE == 0 and ns_rows == ROW_TILE and Tp % ATTN_BLOCK == 0

    mod = _mod(c_all, p['w_mod'], p['b_mod'])
    mod_p = mod[:Bp].reshape(Bp, 1, 6 * D_MODEL)
    mod_s = jnp.repeat(mod[Bp:], Ts, axis=0).reshape(1, ns_rows, 6 * D_MODEL)

    w_in = p['w_in']
    d3 = 3 * D_MODEL
    wcat = jnp.concatenate([w_in[:, :D_MODEL] * (LOG2E * HEAD_DIM ** -0.5), w_in[:, D_MODEL:d3],
                            w_in[:, d3 + N_HEADS:], p['w_gate']], axis=1).astype(BF16)
    wfl = jnp.pad(w_in[:, d3:d3 + N_HEADS], ((0, 0), (0, LANES - N_HEADS))).astype(BF16)
    wflt = wfl.T
    eye = jnp.eye(GATE_TILE // (D_MODEL // N_RNN_BLOCKS), dtype=F32)

    def gate_tiles(w):
        per = GATE_TILE // w.shape[1]
        w4 = w.reshape(N_RNN_BLOCKS // per, per, w.shape[1], w.shape[2])
        return jnp.einsum('gpcd,pq->gpcqd', w4, eye).reshape(N_RNN_BLOCKS // per, GATE_TILE, GATE_TILE).astype(BF16)

    wr4, wi4 = gate_tiles(p['w_r']), gate_tiles(p['w_i'])
    wpa, wpb, wo = p['w_pa'].astype(BF16), p['w_pb'].astype(BF16), p['w_out'].astype(BF16)
    wr = jnp.pad(jnp.concatenate([p['w_rg'], p['w_re']], axis=1), ((0, 0), (0, LANES - N_GROUPS - N_EXPERTS)))
    wr1 = wr.astype(BF16)
    wr2 = (wr - wr1.astype(F32)).astype(BF16)
    br = jnp.pad(jnp.concatenate([p['b_rg'], p['b_re']]), (0, LANES - N_GROUPS - N_EXPERTS)).reshape(1, LANES)

    in_p = _in_proj(xp.reshape(np_rows, D_MODEL), mod_p, p['norm1'], wcat[:, :d3], wfl, wflt, p['b_gate'], p['b_f'],
                    tiles_per_mod=Tp // IN_TILE, tiles_per_seq=Tp // IN_TILE, with_mixer=False)
    in_s = _in_proj(xs.reshape(ns_rows, D_MODEL), mod_s, p['norm1'], wcat, wfl, wflt, p['b_gate'], p['b_f'],
                    tiles_per_mod=1, tiles_per_seq=1, with_mixer=True)
    q_p, kb_p, kt_p, vt_p, lf_p, _, fc_p = in_p
    q_s, kb_s, vb_s, k_s, v_s, xb_s, gb_s, ga_s, gg_s, lf_s, lft_s, _ = in_s
    r3 = lambda a, B, T: a.reshape(B, T, D_MODEL)

    oa_p = _attn_prompt(r3(q_p, Bp, Tp), r3(kb_p, Bp, Tp), vt_p, fc_p.reshape(Bp, Tp, N_HEADS))

    lfp3 = jnp.transpose(cache_logf.astype(F32), (0, 2, 1))
    lfn3 = lft_s.reshape(N_HEADS, Bs, Ts).transpose(1, 0, 2)
    oa_s = _attn_sample(r3(q_s, Bs, Ts), r3(kb_s, Bs, Ts), r3(vb_s, Bs, Ts),
                        jnp.transpose(cache_k, (0, 2, 3, 1)), jnp.transpose(cache_v, (0, 2, 3, 1)), lfp3, lfn3)

    zc = jnp.zeros((Bp, SUBLANES, D_MODEL), F32)
    zh = jnp.zeros((Bp, 1, D_MODEL), F32)
    rnn_w = (p['conv_w'], p['conv_b'], wr4, wi4, p['b_r'], p['b_i'], p['rglru_lambda'])
    post_w = (wpa, wpb, wo, wr1, wr2, br)
    x1_p, h2_p, rt_p, conv_p, hl_p = _mix(xp, oa_p, mod_p, zc, zh, p['norm1'], p['norm2'], wcat[:, d3:], p['b_gate'],
                                          *rnn_w, *post_w)
    x1_p, h2_p, rt_p = (a.reshape(np_rows, a.shape[-1]) for a in (x1_p, h2_p, rt_p))

    sc = jnp.pad(state_conv.astype(F32), ((0, 0), (SUBLANES - (CONV_WIDTH - 1), 0), (0, 0)))
    ob_s, conv_s, hl_s = _rnn(r3(xb_s, Bs, Ts), r3(gb_s, Bs, Ts), sc, state_h.astype(F32).reshape(Bs, 1, D_MODEL),
                              *rnn_w)
    x1_s, h2_s, rt_s = _post(xs.reshape(ns_rows, D_MODEL), oa_s.reshape(ns_rows, D_MODEL),
                             ob_s.reshape(ns_rows, D_MODEL), ga_s, gg_s, mod_s, p['norm2'], *post_w, tiles_per_mod=1)

    n_tok = np_rows + ns_rows
    rt = jnp.concatenate([rt_p, rt_s], axis=0)
    block_e, n_used, dest = _routing_plan(rt[:, :TOP_K].astype(jnp.int32), n_tok)
    slots = jnp.zeros((block_e.shape[0] * MOE_BLOCK, D_MODEL // 2), jnp.uint32)
    slots = _dispatch(h2_p, dest.reshape(-1), slots, tile_offset=0)
    slots = _dispatch(h2_s, dest.reshape(-1), slots, tile_offset=np_rows // ROW_TILE)
    take = lambda a, idx: a.at[idx].get(mode='promise_in_bounds')
    yb = _moe(slots, block_e, n_used, p['w1'], p['w3'], p['w2'])
    ya0 = take(yb, dest[:, 0])
    ya1 = take(yb, dest[:, 1])

    y_p = _final(x1_p, ya0, ya1, rt, mod_p, norm_f, rows_per_mod=Tp, row_offset=0)
    y_s = _final(x1_s, ya0, ya1, rt, mod_s, norm_f, rows_per_mod=ns_rows, row_offset=np_rows)

    st = CONV_WIDTH - 1
    state_p = (jnp.transpose(kt_p, (0, 3, 1, 2)), jnp.transpose(vt_p, (0, 3, 1, 2)),
               lf_p.reshape(Bp, Tp, N_HEADS), conv_p[:, SUBLANES - st:], hl_p.reshape(Bp, D_MODEL))
    state_s = (k_s.reshape(Bs, Ts, N_HEADS, HEAD_DIM), v_s.reshape(Bs, Ts, N_HEADS, HEAD_DIM),
               lf_s.reshape(Bs, Ts, N_HEADS), conv_s[:, SUBLANES - st:], hl_s.reshape(Bs, D_MODEL))
    return y_p.reshape(Bp, Tp, D_MODEL), y_s.reshape(Bs, Ts, D_MODEL), state_p, state_s


def kernel(x_prompt, x_sample, cache_k, cache_v, cache_logf, state_conv, state_h, c_prompt, c_sample, w_mod, b_mod, norm1, w_in, b_f, conv_w, conv_b, w_r, b_r, w_i, b_i, rglru_lambda, w_gate, b_gate, w_pa, w_pb, w_out, norm2, w_rg, b_rg, w_re, b_re, w1, w3, w2, norm_f):
    assert w_mod.shape[0] == 1, "single-layer trunk: the final norm is fused into the layer's last stage"
    names = ('w_mod', 'b_mod', 'norm1', 'w_in', 'b_f', 'conv_w', 'conv_b', 'w_r', 'b_r', 'w_i', 'b_i',
             'rglru_lambda', 'w_gate', 'b_gate', 'w_pa', 'w_pb', 'w_out', 'norm2', 'w_rg', 'b_rg', 'w_re', 'b_re',
             'w1', 'w3', 'w2')
    vals = (w_mod, b_mod, norm1, w_in, b_f, conv_w, conv_b, w_r, b_r, w_i, b_i, rglru_lambda, w_gate, b_gate,
            w_pa, w_pb, w_out, norm2, w_rg, b_rg, w_re, b_re, w1, w3, w2)
    p = {n: v[0] for n, v in zip(names, vals)}
    c_all = jnp.concatenate([c_prompt, c_sample], axis=0)
    y_p, y_s, sp, ss = _layer(x_prompt, x_sample, c_all, cache_k[0], cache_v[0], cache_logf[0],
                              state_conv[0], state_h[0], p, norm_f)
    lead = lambda a: a[None]
    return (y_p, y_s, lead(sp[0]), lead(sp[1]), lead(sp[2]), lead(sp[3]), lead(sp[4]),
            lead(ss[0]), lead(ss[1]), lead(ss[2]), lead(ss[3]), lead(ss[4]))
```

```python
import functools

import jax
import jax.numpy as jnp
from jax import lax
from jax.experimental import pallas as pl
from jax.experimental.pallas import tpu as pltpu

F32 = jnp.float32
BF16 = jnp.bfloat16

D_MODEL = 1024
N_HEADS = 16
HEAD_DIM = 64
HEAD_PAIR = 2 * HEAD_DIM
N_PAIRS = N_HEADS // 2
N_RNN_BLOCKS = 16
CONV_WIDTH = 4
RGLRU_C = 8.0
N_GROUPS = 4
EXPERTS_PER_GROUP = 8
N_EXPERTS = N_GROUPS * EXPERTS_PER_GROUP
TOP_K = 2
D_EXPERT = 512
EPS = 1e-6

LANES = 128
SUBLANES = 8
ROW_TILE = 512
FINAL_TILE = 1024
IN_TILE = 256
IN_TILE_QKV = 512
ATTN_BLOCK = 512
ATTN_QBLOCK = 2048
ATTN_VROWS = HEAD_DIM + 16
RNN_TILE = 256
MIX_TILE = 512
MOE_BLOCK = 512
MOE_SUB = 256
GATE_TILE = 256
VMEM_LIMIT = 56 * 1024 * 1024
NEG_BIG = -1e30
LOG2E = 1.4426950408889634


def _cparams(sem):
    return pltpu.CompilerParams(dimension_semantics=sem, vmem_limit_bytes=VMEM_LIMIT)


def _resident(shape):
    nd = len(shape)
    return pl.BlockSpec(shape, lambda *_: (0,) * nd, pipeline_mode=pl.Buffered(1))


def _split3(x):
    p1 = x.astype(BF16)
    r1 = x - p1.astype(F32)
    p2 = r1.astype(BF16)
    p3 = (r1 - p2.astype(F32)).astype(BF16)
    return p1, p2, p3


def _dot(a, b):
    return jnp.dot(a, b, preferred_element_type=F32)


def _dot_nt(a, b):
    return lax.dot_general(a, b, (((1,), (1,)), ((), ())), preferred_element_type=F32)


def _sigmoid(x):
    return 0.5 * jnp.tanh(0.5 * x) + 0.5


def _gelu_tanh(x):
    return 0.5 * x * (1.0 + jnp.tanh(0.7978845608028654 * (x + 0.044715 * x * x * x)))


def _rms(x, g):
    ms = jnp.mean(x * x, axis=-1, keepdims=True)
    return x * lax.rsqrt(ms + EPS) * g


def _pack_pairs(x):
    n = x.shape[1] // 2
    hi = pltpu.bitcast(x[:, :n].astype(BF16).astype(F32), jnp.uint32)
    lo = pltpu.bitcast(x[:, n:].astype(BF16).astype(F32), jnp.uint32)
    return hi | lax.shift_right_logical(lo, jnp.uint32(16))


def _unpack_pairs(w):
    hi = pltpu.bitcast(w & jnp.uint32(0xFFFF0000), F32)
    lo = pltpu.bitcast(lax.shift_left(w, jnp.uint32(16)), F32)
    return jnp.concatenate([hi, lo], axis=1)


def _div_p2(x, n):
    assert n & (n - 1) == 0
    return lax.shift_right_logical(x, jnp.int32(n.bit_length() - 1))


def _mod_p2(x, n):
    assert n & (n - 1) == 0
    return x & (n - 1)


def _upper_tri(n):
    r = lax.broadcasted_iota(jnp.int32, (n, n), 0)
    c = lax.broadcasted_iota(jnp.int32, (n, n), 1)
    return jnp.where(r <= c, 1.0, 0.0).astype(BF16)


def _cumsum_lanes(x, u):
    p1, p2, p3 = _split3(x)
    return _dot(p1, u) + _dot(p2, u) + _dot(p3, u)


def _mod_kernel(c_ref, w_ref, b_ref, o_ref):
    c = c_ref[...]
    s = c * _sigmoid(c)
    s1 = s.astype(BF16)
    s2 = (s - s1.astype(F32)).astype(BF16)
    w = w_ref[...]
    w1 = w.astype(BF16)
    w2 = (w - w1.astype(F32)).astype(BF16)
    o_ref[...] = _dot(s1, w1) + _dot(s1, w2) + _dot(s2, w1) + b_ref[...]


def _mod(c, w, b):
    n = c.shape[0]
    cols = w.shape[1]
    return pl.pallas_call(
        _mod_kernel,
        grid=(cols // D_MODEL,),
        in_specs=[pl.BlockSpec((n, D_MODEL), lambda j: (0, 0)),
                  pl.BlockSpec((D_MODEL, D_MODEL), lambda j: (0, j)),
                  pl.BlockSpec((1, D_MODEL), lambda j: (0, j))],
        out_specs=pl.BlockSpec((n, D_MODEL), lambda j: (0, j)),
        out_shape=jax.ShapeDtypeStruct((n, cols), F32),
        compiler_params=_cparams(("arbitrary",)),
        name="mod",
    )(c, w, b.reshape(1, cols))


def _in_kernel(x_ref, sh_ref, sc_ref, n1_ref, w_ref, wfl_ref, wflt_ref, bg_ref, bf_ref, bft_ref, *refs,
               tiles_per_seq, with_mixer):
    lf_ref, lft_ref, fc_ref, carry_ref = refs[-4:]
    x = x_ref[...]
    h = (_rms(x, n1_ref[...]) * (1.0 + sc_ref[...]) + sh_ref[...]).astype(BF16)

    def mm(c):
        return _dot(h, w_ref[:, c * D_MODEL:(c + 1) * D_MODEL])

    def head_major(y):
        slabs = []
        for p in range(N_PAIRS):
            pair = y[:, p * HEAD_PAIR:(p + 1) * HEAD_PAIR]
            slabs += [pair, pltpu.roll(pair, HEAD_DIM, 1)]
        return jnp.swapaxes(jnp.stack(slabs, axis=0), 0, 1)[:, :, :HEAD_DIM]

    def time_minor(y):
        return y.T.reshape(N_HEADS, HEAD_DIM, y.shape[0])

    refs[0][...] = mm(0).astype(BF16)
    k = mm(1)
    refs[1][...] = k.astype(BF16)
    v = mm(2)
    if with_mixer:
        vb_ref, k5_ref, v5_ref, xb_ref, gb_ref, ga_ref, gg_ref = refs[2:9]
        vb_ref[...] = v.astype(BF16)
        k5_ref[...] = head_major(k)
        v5_ref[...] = head_major(v)
        xb_ref[...] = mm(3).astype(BF16)
        gb_ref[...] = mm(4).astype(BF16)
        ga_ref[...] = _sigmoid(mm(5) + bg_ref[:, :D_MODEL]).astype(BF16)
        gg_ref[...] = _sigmoid(mm(6) + bg_ref[:, D_MODEL:]).astype(BF16)
    else:
        refs[2][...] = time_minor(k)
        refs[3][...] = time_minor(v)

    def log_sigmoid(z):
        return jnp.minimum(z, 0.0) - jnp.log(1.0 + jnp.exp(-jnp.abs(z)))

    lf = log_sigmoid(_dot(h, wfl_ref[...]) + bf_ref[...])
    lf_ref[...] = lf[:, :N_HEADS]
    lft_ref[...] = log_sigmoid(_dot_nt(wflt_ref[...], h)[:N_HEADS, :] + bft_ref[...])

    @pl.when(pl.program_id(0) % tiles_per_seq == 0)
    def _():
        carry_ref[...] = jnp.zeros_like(carry_ref)

    tm = x.shape[0]
    r = lax.broadcasted_iota(jnp.int32, (tm, tm), 0)
    c = lax.broadcasted_iota(jnp.int32, (tm, tm), 1)
    lower = jnp.where(c <= r, 1.0, 0.0).astype(BF16)
    p1, p2, p3 = _split3(lf)
    fc = _dot(lower, p1) + _dot(lower, p2) + _dot(lower, p3) + carry_ref[...]
    fc_ref[...] = fc[:, :N_HEADS]
    carry_ref[...] = fc[tm - 1:tm, :]


def _in_proj(x2, mod3, norm1, wcat, wfl, wflt, b_gate, b_f, *, tiles_per_mod, tiles_per_seq, with_mixer):
    rows = x2.shape[0]
    tm = IN_TILE if with_mixer else IN_TILE_QKV
    nt = rows // tm
    row_blk = pl.BlockSpec((tm, D_MODEL), lambda i: (i, 0))
    if mod3.shape[1] == 1:
        modc = lambda c: pl.BlockSpec((None, 1, D_MODEL), lambda i: (i // tiles_per_mod, 0, c))
    else:
        modc = lambda c: pl.BlockSpec((None, tm, D_MODEL), lambda i: (0, i, c))
    row_out = jax.ShapeDtypeStruct((rows, D_MODEL), BF16)
    if with_mixer:
        kv_shape = jax.ShapeDtypeStruct((rows, N_HEADS, HEAD_DIM), F32)
        kv_blk = pl.BlockSpec((tm, N_HEADS, HEAD_DIM), lambda i: (i, 0, 0))
        lead = ([row_out] * 3 + [kv_shape] * 2 + [row_out] * 4, [row_blk] * 3 + [kv_blk] * 2 + [row_blk] * 4)
    else:
        n_seq = nt // tiles_per_seq
        kv_shape = jax.ShapeDtypeStruct((n_seq, N_HEADS, HEAD_DIM, tiles_per_seq * tm), F32)
        kv_blk = pl.BlockSpec((None, N_HEADS, HEAD_DIM, tm), lambda i: (i // tiles_per_seq, 0, 0, i % tiles_per_seq))
        lead = ([row_out] * 2 + [kv_shape] * 2, [row_blk] * 2 + [kv_blk] * 2)
    outs = [*lead[0],
            jax.ShapeDtypeStruct((rows, N_HEADS), F32),
            jax.ShapeDtypeStruct((N_HEADS, rows), F32),
            jax.ShapeDtypeStruct((rows, N_HEADS), F32)]
    return pl.pallas_call(
        functools.partial(_in_kernel, tiles_per_seq=tiles_per_seq, with_mixer=with_mixer),
        grid=(nt,),
        in_specs=[row_blk, modc(0), modc(1),
                  _resident((1, D_MODEL)),
                  _resident(wcat.shape),
                  _resident(wfl.shape),
                  _resident(wflt.shape),
                  _resident((1, 2 * D_MODEL)),
                  _resident((1, LANES)),
                  _resident((N_HEADS, 1))],
        out_specs=lead[1]
                  + [pl.BlockSpec((tm, N_HEADS), lambda i: (i, 0)),
                     pl.BlockSpec((N_HEADS, tm), lambda i: (0, i)),
                     pl.BlockSpec((tm, N_HEADS), lambda i: (i, 0))],
        out_shape=outs,
        scratch_shapes=[pltpu.VMEM((1, LANES), F32)],
        compiler_params=_cparams(("arbitrary",)),
        name="in_proj",
    )(x2, mod3, mod3, norm1.reshape(1, D_MODEL), wcat, wfl, wflt,
      b_gate.reshape(1, 2 * D_MODEL), jnp.pad(b_f, (0, LANES - N_HEADS)).reshape(1, LANES),
      b_f.reshape(N_HEADS, 1))


def _attn_kernel(q_ref, k_ref, v_ref, f_ref, o_ref, kb_ref, vt_ref, qt_ref, s0_ref, m_ref, acc_ref):
    qi = pl.program_id(2)
    bq = q_ref.shape[0]
    bk = ATTN_BLOCK
    nk = k_ref.shape[0] // bk
    own_lanes = (lambda i: i < HEAD_DIM, lambda i: i >= HEAD_DIM)
    bias_at = (HEAD_DIM, 0)

    @pl.when(qi == 0)
    def _():
        lane = lax.broadcasted_iota(jnp.int32, (bk, HEAD_PAIR), 1)
        ones_rows = jnp.where(lax.broadcasted_iota(jnp.int32, (ATTN_VROWS - HEAD_DIM, bk), 0) == 0, 1.0, 0.0)
        for c in range(nk):
            blk = slice(c * bk, (c + 1) * bk)
            k = k_ref[blk, :].astype(F32)
            f = f_ref[blk, :] * LOG2E
            head = lax.broadcasted_iota(jnp.int32, f.shape, 1)
            for hh in range(2):
                mine = head == 2 * pl.program_id(1) + hh
                f_head = jnp.sum(jnp.where(mine, f, 0.0), axis=1, keepdims=True)
                slab = jnp.zeros(k.shape, F32)
                for n, piece in enumerate(_split3(f_head)):
                    slab = jnp.where(lane == bias_at[hh] + n, piece.astype(F32), slab)
                kb_ref[hh, blk, :] = jnp.where(own_lanes[hh](lane), k, slab).astype(BF16)
            for hh in range(2):
                vt_ref[c, hh, 0:HEAD_DIM] = v_ref[hh, :, blk].astype(BF16)
                vt_ref[c, hh, HEAD_DIM:ATTN_VROWS] = ones_rows.astype(BF16)

    qt = q_ref[...].astype(F32).T
    feat = lax.broadcasted_iota(jnp.int32, qt.shape, 0)
    for hh in range(2):
        is_bias = (feat >= bias_at[hh]) & (feat < bias_at[hh] + 3)
        qt_ref[hh] = jnp.where(own_lanes[hh](feat), qt, jnp.where(is_bias, -1.0, 0.0)).astype(BF16)

    m_ref[...] = jnp.full(m_ref.shape, NEG_BIG, F32)
    acc_ref[...] = jnp.zeros(acc_ref.shape, F32)

    nsub = bq // bk

    def scores(hh, u, j):
        keys = kb_ref[hh, pl.ds(pl.multiple_of(j * bk, bk), bk), :]
        return _dot(keys, qt_ref[hh, :, u * bk:(u + 1) * bk])

    def softmax_pv(hh, u, j, s, masked):
        cols = slice(u * bk, (u + 1) * bk)
        if masked:
            key = lax.broadcasted_iota(jnp.int32, s.shape, 0)
            qry = lax.broadcasted_iota(jnp.int32, s.shape, 1)
            s = jnp.where(key <= qry, s, NEG_BIG)
        m_prev = m_ref[hh, :, cols]
        m_new = jnp.maximum(m_prev, jnp.max(s, axis=0, keepdims=True))
        p = jnp.exp2(s - m_new).astype(BF16)
        acc_ref[hh, :, cols] = jnp.exp2(m_prev - m_new) * acc_ref[hh, :, cols] + _dot(vt_ref[j, hh], p)
        m_ref[hh, :, cols] = m_new

    def run(items, first_of_next):
        s = s0_ref[...]
        for n, (hh, u, j, masked) in enumerate(items):
            if n + 1 < len(items):
                nxt = scores(*items[n + 1][:3])
            elif first_of_next is not None:
                s0_ref[...] = scores(*first_of_next)
            softmax_pv(hh, u, j, s, masked)
            s = nxt if n + 1 < len(items) else None

    def body(j, carry):
        run([(hh, u, j, False) for u in range(nsub) for hh in range(2)], (0, 0, j + 1))
        return carry

    s0_ref[...] = scores(0, 0, 0)
    lax.fori_loop(0, nsub * qi, body, 0)
    for d in range(nsub):
        j = nsub * qi + d
        items = [(hh, u, j, u == d) for u in range(d, nsub) for hh in range(2)]
        run(items, (0, d + 1, j + 1) if d + 1 < nsub else None)

    heads = [acc_ref[hh, 0:HEAD_DIM] / acc_ref[hh, HEAD_DIM:HEAD_DIM + 1] for hh in range(2)]
    o_ref[...] = jnp.concatenate(heads, axis=0).T.astype(BF16)


def _attn_prompt(q3, k3, vt4, fc3):
    B, T, _ = q3.shape
    bq, bk = ATTN_QBLOCK, ATTN_BLOCK
    nq, nk = T // bq, T // bk
    return pl.pallas_call(
        _attn_kernel,
        grid=(B, N_PAIRS, nq),
        in_specs=[pl.BlockSpec((None, bq, HEAD_PAIR), lambda b, p, i: (b, i, p)),
                  pl.BlockSpec((None, T, HEAD_PAIR), lambda b, p, i: (b, 0, p)),
                  pl.BlockSpec((None, 2, HEAD_DIM, T), lambda b, p, i: (b, p, 0, 0)),
                  pl.BlockSpec((None, T, N_HEADS), lambda b, p, i: (b, 0, 0))],
        out_specs=pl.BlockSpec((None, bq, HEAD_PAIR), lambda b, p, i: (b, i, p)),
        out_shape=jax.ShapeDtypeStruct((B, T, D_MODEL), BF16),
        scratch_shapes=[pltpu.VMEM((2, T, HEAD_PAIR), BF16),
                        pltpu.VMEM((nk, 2, ATTN_VROWS, bk), BF16),
                        pltpu.VMEM((2, HEAD_PAIR, bq), BF16),
                        pltpu.VMEM((bk, bk), F32),
                        pltpu.VMEM((2, 1, bq), F32),
                        pltpu.VMEM((2, ATTN_VROWS, bq), F32)],
        compiler_params=_cparams(("arbitrary", "arbitrary", "arbitrary")),
        name="attn_prompt",
    )(q3, k3, vt4, fc3)


def _attn_sample_kernel(q_ref, kn_ref, vn_ref, kc_ref, vc_ref, lfp_ref, lfn_ref, o_ref):
    tq = q_ref.shape[0]
    past = kc_ref.shape[-1]
    nrow = N_HEADS * tq
    row = lax.broadcasted_iota(jnp.int32, (nrow, D_MODEL), 0)
    col = lax.broadcasted_iota(jnp.int32, (nrow, D_MODEL), 1)
    own = _div_p2(row, tq) == _div_p2(col, HEAD_DIM)

    rr = lax.broadcasted_iota(jnp.int32, (nrow, tq), 0)
    rc = lax.broadcasted_iota(jnp.int32, (nrow, tq), 1)
    rep = jnp.where(_mod_p2(rr, tq) == rc, 1.0, 0.0).astype(BF16)
    qx = jnp.where(own, _dot(rep, q_ref[...]), 0.0).astype(BF16)

    fp = _cumsum_lanes(lfp_ref[...] * LOG2E, _upper_tri(past))
    fn = _cumsum_lanes(lfn_ref[...] * LOG2E, _upper_tri(tq)) + fp[:, past - 1:past]
    hr = lax.broadcasted_iota(jnp.int32, (nrow, N_HEADS), 0)
    hc = lax.broadcasted_iota(jnp.int32, (nrow, N_HEADS), 1)
    hsel = jnp.where(_div_p2(hr, tq) == hc, 1.0, 0.0).astype(BF16)

    def expand(f):
        p1, p2, p3 = _split3(f)
        return _dot(hsel, p1) + _dot(hsel, p2) + _dot(hsel, p3)

    s_p = _dot(qx, kc_ref[...].reshape(D_MODEL, past).astype(BF16)) - expand(fp)
    s_n = _dot_nt(qx, kn_ref[...]) - expand(fn)
    nr = lax.broadcasted_iota(jnp.int32, (nrow, tq), 0)
    nc = lax.broadcasted_iota(jnp.int32, (nrow, tq), 1)
    s_n = jnp.where(nc <= _mod_p2(nr, tq), s_n, NEG_BIG)
    m = jnp.maximum(jnp.max(s_p, axis=-1, keepdims=True), jnp.max(s_n, axis=-1, keepdims=True))
    p_p = jnp.exp2(s_p - m)
    p_n = jnp.exp2(s_n - m)
    l = jnp.sum(p_p, axis=-1, keepdims=True) + jnp.sum(p_n, axis=-1, keepdims=True)
    o_full = (_dot_nt(p_p.astype(BF16), vc_ref[...].reshape(D_MODEL, past).astype(BF16))
              + _dot(p_n.astype(BF16), vn_ref[...]))
    o_own = jnp.where(own, o_full / l, 0.0).astype(BF16)
    cr = lax.broadcasted_iota(jnp.int32, (tq, nrow), 0)
    cc = lax.broadcasted_iota(jnp.int32, (tq, nrow), 1)
    col_sel = jnp.where(_mod_p2(cc, tq) == cr, 1.0, 0.0).astype(BF16)
    o_ref[...] = _dot(col_sel, o_own).astype(BF16)


def _attn_sample(q3, kn3, vn3, kct, vct, lfp3, lfn3):
    B, tq, _ = q3.shape
    past = kct.shape[-1]
    blk = lambda n, w: pl.BlockSpec((None, n, w), lambda b: (b, 0, 0))
    cache = pl.BlockSpec((None, N_HEADS, HEAD_DIM, past), lambda b: (b, 0, 0, 0))
    return pl.pallas_call(
        _attn_sample_kernel,
        grid=(B,),
        in_specs=[blk(tq, D_MODEL), blk(tq, D_MODEL), blk(tq, D_MODEL),
                  cache, cache, blk(N_HEADS, past), blk(N_HEADS, tq)],
        out_specs=blk(tq, D_MODEL),
        out_shape=jax.ShapeDtypeStruct((B, tq, D_MODEL), BF16),
        compiler_params=_cparams(("arbitrary",)),
        name="attn_sample",
    )(q3, kn3, vn3, kct, vct, lfp3, lfn3)


def _rglru_gates(xb, xp_ref, cw_ref, cb_ref, wr_ref, wi_ref):
    tt = xb.shape[0]
    pad = SUBLANES
    xp_ref[pad:pad + tt, :] = xb
    xc = cb_ref[...] + xb * cw_ref[CONV_WIDTH - 1:CONV_WIDTH, :]
    for k in range(CONV_WIDTH - 1):
        back = CONV_WIDTH - 1 - k
        xc = xc + xp_ref[pad - back:pad - back + tt, :] * cw_ref[k:k + 1, :]
    tail = xp_ref[tt:tt + pad, :]
    xp_ref[0:pad, :] = tail

    xcb = xc.astype(BF16)
    ng = D_MODEL // GATE_TILE
    r = jnp.concatenate([_dot(xcb[:, g * GATE_TILE:(g + 1) * GATE_TILE], wr_ref[g]) for g in range(ng)], axis=-1)
    i = jnp.concatenate([_dot(xcb[:, g * GATE_TILE:(g + 1) * GATE_TILE], wi_ref[g]) for g in range(ng)], axis=-1)
    return xc, r, i, tail


def _rglru_scan(xc, r, i, gb, hc_ref, br_ref, bi_ref, lam_ref):
    tt = xc.shape[0]
    r = _sigmoid(r + br_ref[...])
    i = _sigmoid(i + bi_ref[...])
    lam = lam_ref[...]
    softplus_neg = jnp.maximum(-lam, 0.0) + jnp.log(1.0 + jnp.exp(-jnp.abs(lam)))
    log_a = (-RGLRU_C) * r * softplus_neg
    a = jnp.exp(log_a)
    th = jnp.tanh(log_a)
    mult = jnp.sqrt(-2.0 * th / (1.0 - th))
    b = mult * (i * xc)

    ng8 = tt // SUBLANES
    a3 = a.reshape(ng8, SUBLANES, D_MODEL)
    b3 = b.reshape(ng8, SUBLANES, D_MODEL)
    sub = lax.broadcasted_iota(jnp.int32, a3.shape, 1)
    for s in (1, 2, 4):
        keep = sub >= s
        a_sh = jnp.where(keep, pltpu.roll(a3, s, 1), 1.0)
        b_sh = jnp.where(keep, pltpu.roll(b3, s, 1), 0.0)
        b3 = a3 * b_sh + b3
        a3 = a3 * a_sh
    hc = hc_ref[...]
    hs = []
    for g in range(ng8):
        hg = b3[g] + a3[g] * hc
        hs.append(hg)
        hc = hg[SUBLANES - 1:SUBLANES, :]
    hc_ref[...] = hc
    return jnp.concatenate(hs, axis=0) * _gelu_tanh(gb), hc


def _rnn_kernel(xb_ref, gb_ref, c0_ref, h0_ref, cw_ref, cb_ref, wr_ref, wi_ref, br_ref, bi_ref, lam_ref,
                o_ref, cout_ref, hout_ref, xp_ref, hc_ref):
    ti = pl.program_id(1)

    @pl.when(ti == 0)
    def _():
        xp_ref[0:SUBLANES, :] = c0_ref[...]
        hc_ref[...] = h0_ref[...]

    xc, r, i, tail = _rglru_gates(xb_ref[...].astype(F32), xp_ref, cw_ref, cb_ref, wr_ref, wi_ref)
    ob, hc = _rglru_scan(xc, r, i, gb_ref[...].astype(F32), hc_ref, br_ref, bi_ref, lam_ref)
    o_ref[...] = ob.astype(BF16)

    @pl.when(ti == pl.num_programs(1) - 1)
    def _():
        cout_ref[...] = tail
        hout_ref[...] = hc


def _rnn(xb3, gb3, conv0, h0, conv_w, conv_b, wr4, wi4, b_r, b_i, lam):
    B, T, _ = xb3.shape
    tt = min(RNN_TILE, T)
    vec = lambda a: a.reshape(1, D_MODEL)
    seq = pl.BlockSpec((None, tt, D_MODEL), lambda b, t: (b, t, 0))
    st8 = pl.BlockSpec((None, SUBLANES, D_MODEL), lambda b, t: (b, 0, 0))
    st1 = pl.BlockSpec((None, 1, D_MODEL), lambda b, t: (b, 0, 0))
    return pl.pallas_call(
        _rnn_kernel,
        grid=(B, T // tt),
        in_specs=[seq, seq, st8, st1,
                  _resident((CONV_WIDTH, D_MODEL)), _resident((1, D_MODEL)),
                  _resident(wr4.shape), _resident(wi4.shape),
                  _resident((1, D_MODEL)), _resident((1, D_MODEL)), _resident((1, D_MODEL))],
        out_specs=[seq, st8, st1],
        out_shape=[jax.ShapeDtypeStruct((B, T, D_MODEL), BF16),
                   jax.ShapeDtypeStruct((B, SUBLANES, D_MODEL), F32),
                   jax.ShapeDtypeStruct((B, 1, D_MODEL), F32)],
        scratch_shapes=[pltpu.VMEM((tt + SUBLANES, D_MODEL), F32),
                        pltpu.VMEM((1, D_MODEL), F32)],
        compiler_params=_cparams(("arbitrary", "arbitrary")),
        name="rnn",
    )(xb3, gb3, conv0, h0, conv_w, vec(conv_b), wr4, wi4, vec(b_r), vec(b_i), vec(lam))


def _merge_route_tile(x, pa, ob, ga, gg, g1, sh2, sc2, n2_ref, wpb_ref, wo_ref, wr1_ref, wr2_ref, br_ref):
    mix = ga * pa + gg * _dot(ob, wpb_ref[...])
    x1 = x + g1 * _dot(mix.astype(BF16), wo_ref[...])
    h2 = _rms(x1, n2_ref[...]) * (1.0 + sc2) + sh2
    h2a = h2.astype(BF16)
    h2b = (h2 - h2a.astype(F32)).astype(BF16)
    lg = _dot(h2a, wr1_ref[...]) + _dot(h2a, wr2_ref[...]) + _dot(h2b, wr1_ref[...]) + br_ref[...]

    lane = lax.broadcasted_iota(jnp.int32, lg.shape, 1).astype(F32)
    big = float(LANES)
    is_g = lane < N_GROUPS
    gl = jnp.where(is_g, lg, NEG_BIG)
    mg = jnp.max(gl, axis=-1, keepdims=True)
    sg = jnp.sum(jnp.where(is_g, jnp.exp(gl - mg), 0.0), axis=-1, keepdims=True)
    pg = 1.0 / sg
    gidx = jnp.min(jnp.where(is_g & (gl == mg), lane, big), axis=-1, keepdims=True)
    lo = N_GROUPS + EXPERTS_PER_GROUP * gidx
    ing = (lane >= lo) & (lane < lo + EXPERTS_PER_GROUP)
    el = jnp.where(ing, lg, NEG_BIG)
    m1 = jnp.max(el, axis=-1, keepdims=True)
    se = jnp.sum(jnp.where(ing, jnp.exp(el - m1), 0.0), axis=-1, keepdims=True)
    i1 = jnp.min(jnp.where(ing & (el == m1), lane, big), axis=-1, keepdims=True)
    el2 = jnp.where(lane == i1, NEG_BIG, el)
    m2 = jnp.max(el2, axis=-1, keepdims=True)
    i2 = jnp.min(jnp.where(ing & (el2 == m2) & (lane != i1), lane, big), axis=-1, keepdims=True)
    p1 = 1.0 / se
    p2 = jnp.exp(m2 - m1) / se
    den = p1 + p2
    w1 = pg * p1 / den
    w2 = pg * p2 / den
    rt = jnp.where(lane == 0.0, i1 - N_GROUPS,
                   jnp.where(lane == 1.0, i2 - N_GROUPS,
                             jnp.where(lane == 2.0, w1, jnp.where(lane == 3.0, w2, 0.0))))
    return x1, h2, rt


def _post_kernel(x_ref, oa_ref, ob_ref, ga_ref, gg_ref, g1_ref, sh_ref, sc_ref, n2_ref,
                 wpa_ref, wpb_ref, wo_ref, wr1_ref, wr2_ref, br_ref,
                 x1_ref, h2_ref, rt_ref):
    x1, h2, rt = _merge_route_tile(x_ref[...], _dot(oa_ref[...], wpa_ref[...]), ob_ref[...],
                                   ga_ref[...].astype(F32), gg_ref[...].astype(F32), g1_ref[...], sh_ref[...],
                                   sc_ref[...], n2_ref, wpb_ref, wo_ref, wr1_ref, wr2_ref, br_ref)
    x1_ref[...] = x1
    h2_ref[...] = _pack_pairs(h2)
    rt_ref[...] = rt


def _post(x2, oa2, ob2, ga2, gg2, mod3, norm2, wpa, wpb, wo, wr1, wr2, br, *, tiles_per_mod):
    rows = x2.shape[0]
    tm = ROW_TILE
    mr = mod3.shape[1]
    row = pl.BlockSpec((tm, D_MODEL), lambda i: (i, 0))
    modc = lambda c: pl.BlockSpec((None, mr, D_MODEL), lambda i: (i // tiles_per_mod, 0, c))
    return pl.pallas_call(
        _post_kernel,
        grid=(rows // tm,),
        in_specs=[row, row, row, row, row, modc(2), modc(3), modc(4), _resident((1, D_MODEL)),
                  _resident(wpa.shape), _resident(wpb.shape), _resident(wo.shape),
                  _resident(wr1.shape), _resident(wr2.shape), _resident((1, LANES))],
        out_specs=[row, pl.BlockSpec((tm, D_MODEL // 2), lambda i: (i, 0)), pl.BlockSpec((tm, LANES), lambda i: (i, 0))],
        out_shape=[jax.ShapeDtypeStruct((rows, D_MODEL), F32),
                   jax.ShapeDtypeStruct((rows, D_MODEL // 2), jnp.uint32),
                   jax.ShapeDtypeStruct((rows, LANES), F32)],
        compiler_params=_cparams(("arbitrary",)),
        name="post",
    )(x2, oa2, ob2, ga2, gg2, mod3, mod3, mod3, norm2.reshape(1, D_MODEL), wpa, wpb, wo, wr1, wr2, br)


def _mix_kernel(x_ref, oa_ref, sh1_ref, sc1_ref, g1_ref, sh2_ref, sc2_ref, n1_ref, n2_ref, w_ref, bg_ref,
                c0_ref, h0_ref, cw_ref, cb_ref, wr_ref, wi_ref, br_ref, bi_ref, lam_ref,
                wpa_ref, wpb_ref, wo_ref, wr1_ref, wr2_ref, brt_ref,
                x1_ref, h2_ref, rt_ref, cout_ref, hout_ref, xp_ref, hc_ref):
    ti = pl.program_id(1)

    @pl.when(ti == 0)
    def _():
        xp_ref[0:SUBLANES, :] = c0_ref[...]
        hc_ref[...] = h0_ref[...]

    def project(rows):
        x = x_ref[rows, :]
        h = (_rms(x, n1_ref[...]) * (1.0 + sc1_ref[...]) + sh1_ref[...]).astype(BF16)
        mm = lambda c: _dot(h, w_ref[:, c * D_MODEL:(c + 1) * D_MODEL])
        xc, r, i, tail = _rglru_gates(mm(0), xp_ref, cw_ref, cb_ref, wr_ref, wi_ref)
        return dict(x=x, xc=xc, r=r, i=i, tail=tail, gb=mm(1), ga=mm(2), gg=mm(3),
                    pa=_dot(oa_ref[rows, :], wpa_ref[...]))

    def recur(st):
        ob, hc = _rglru_scan(st['xc'], st['r'], st['i'], st['gb'], hc_ref, br_ref, bi_ref, lam_ref)
        return ob.astype(BF16), hc

    def merge(rows, st, ob):
        ga = _sigmoid(st['ga'] + bg_ref[:, :D_MODEL])
        gg = _sigmoid(st['gg'] + bg_ref[:, D_MODEL:])
        x1, h2, rt = _merge_route_tile(st['x'], st['pa'], ob, ga, gg, g1_ref[...], sh2_ref[...], sc2_ref[...],
                                       n2_ref, wpb_ref, wo_ref, wr1_ref, wr2_ref, brt_ref)
        x1_ref[rows, :] = x1
        h2_ref[rows, :] = _pack_pairs(h2)
        rt_ref[rows, :] = rt

    subs = [slice(s, s + RNN_TILE) for s in range(0, x_ref.shape[0], RNN_TILE)]
    states = [project(subs[0])]
    for n, rows in enumerate(subs):
        if n + 1 < len(subs):
            states.append(project(subs[n + 1]))
        ob, hc = recur(states[n])
        merge(rows, states[n], ob)

    @pl.when(ti == pl.num_programs(1) - 1)
    def _():
        cout_ref[...] = states[-1]['tail']
        hout_ref[...] = hc


def _mix(x3, oa3, mod3, conv0, h0, norm1, norm2, w_mix, b_gate, conv_w, conv_b, wr4, wi4, b_r, b_i, lam,
         wpa, wpb, wo, wr1, wr2, br):
    B, T, _ = x3.shape
    tt = MIX_TILE
    vec = lambda a: a.reshape(1, D_MODEL)
    seq = lambda w: pl.BlockSpec((None, tt, w), lambda b, t: (b, t, 0))
    modc = lambda c: pl.BlockSpec((None, 1, D_MODEL), lambda b, t: (b, 0, c))
    st8 = pl.BlockSpec((None, SUBLANES, D_MODEL), lambda b, t: (b, 0, 0))
    st1 = pl.BlockSpec((None, 1, D_MODEL), lambda b, t: (b, 0, 0))
    v1 = _resident((1, D_MODEL))
    return pl.pallas_call(
        _mix_kernel,
        grid=(B, T // tt),
        in_specs=[seq(D_MODEL), seq(D_MODEL), modc(0), modc(1), modc(2), modc(3), modc(4), v1, v1,
                  _resident(w_mix.shape), _resident((1, 2 * D_MODEL)), st8, st1,
                  _resident((CONV_WIDTH, D_MODEL)), v1, _resident(wr4.shape), _resident(wi4.shape), v1, v1, v1,
                  _resident(wpa.shape), _resident(wpb.shape), _resident(wo.shape),
                  _resident(wr1.shape), _resident(wr2.shape), _resident((1, LANES))],
        out_specs=[seq(D_MODEL), seq(D_MODEL // 2), seq(LANES), st8, st1],
        out_shape=[jax.ShapeDtypeStruct((B, T, D_MODEL), F32),
                   jax.ShapeDtypeStruct((B, T, D_MODEL // 2), jnp.uint32),
                   jax.ShapeDtypeStruct((B, T, LANES), F32),
                   jax.ShapeDtypeStruct((B, SUBLANES, D_MODEL), F32),
                   jax.ShapeDtypeStruct((B, 1, D_MODEL), F32)],
        scratch_shapes=[pltpu.VMEM((RNN_TILE + SUBLANES, D_MODEL), F32),
                        pltpu.VMEM((1, D_MODEL), F32)],
        compiler_params=_cparams(("arbitrary", "arbitrary")),
        name="mix",
    )(x3, oa3, mod3, mod3, mod3, mod3, mod3, vec(norm1), vec(norm2), w_mix, b_gate.reshape(1, 2 * D_MODEL),
      conv0, h0, conv_w, vec(conv_b), wr4, wi4, vec(b_r), vec(b_i), vec(lam), wpa, wpb, wo, wr1, wr2, br)


def _dispatch_kernel(dest_ref, x_ref, slots_in_ref, slots_ref, sem):
    del slots_in_ref
    tm = x_ref.shape[0]

    def row_copy(r, slot):
        return pltpu.make_async_copy(x_ref.at[pl.ds(r, 1)], slots_ref.at[pl.ds(slot, 1)], sem)

    def issue(r, carry):
        for k in range(TOP_K):
            row_copy(r, dest_ref[TOP_K * r + k]).start()
        return carry

    lax.fori_loop(0, tm, issue, 0, unroll=8)

    for _ in range(TOP_K):
        pltpu.make_async_copy(x_ref, slots_ref.at[pl.ds(0, tm)], sem).wait()


def _dispatch(x, dest, slots, *, tile_offset):
    rows, width = x.shape
    tm = ROW_TILE
    return pl.pallas_call(
        _dispatch_kernel,
        grid=(rows // tm,),
        in_specs=[pl.BlockSpec((tm * TOP_K,), lambda i: (i + tile_offset,), memory_space=pltpu.SMEM),
                  pl.BlockSpec((tm, width), lambda i: (i, 0)),
                  pl.BlockSpec(memory_space=pl.ANY)],
        out_specs=pl.BlockSpec(memory_space=pl.ANY),
        out_shape=jax.ShapeDtypeStruct(slots.shape, slots.dtype),
        scratch_shapes=[pltpu.SemaphoreType.DMA(())],
        input_output_aliases={2: 0},
        compiler_params=_cparams(("arbitrary",)),
        name="dispatch",
    )(dest, x, slots)


def _moe_kernel(be_ref, nb_ref, x_ref, w1_ref, w3_ref, w2_ref, o_ref, w1b_ref, w3b_ref, w2b_ref):
    i = pl.program_id(0)

    @pl.when((i == 0) | (be_ref[i] != be_ref[jnp.maximum(i - 1, 0)]))
    def _():
        w1b_ref[...] = w1_ref[...].astype(BF16)
        w3b_ref[...] = w3_ref[...].astype(BF16)
        w2b_ref[...] = w2_ref[...].astype(BF16)

    @pl.when(i < nb_ref[0])
    def _():
        subs = [slice(s, s + MOE_SUB) for s in range(0, x_ref.shape[0], MOE_SUB)]
        ups = []
        for rows in subs:
            x = _unpack_pairs(x_ref[rows, :]).astype(BF16)
            ups.append((_dot(x, w1b_ref[...]), _dot(x, w3b_ref[...])))
        for rows, (a, b) in zip(subs, ups):
            g = a * _sigmoid(a) * b
            o_ref[rows, :] = _pack_pairs(_dot(g.astype(BF16), w2b_ref[...]))

    @pl.when(i >= nb_ref[0])
    def _():
        o_ref[...] = jnp.zeros_like(o_ref)


def _moe(xs, block_e, n_used, w1, w3, w2):
    n_blocks = block_e.shape[0]
    bm = MOE_BLOCK
    wspec = lambda s: pl.BlockSpec((None,) + s, lambda i, be, nb: (be[i], 0, 0))
    gs = pltpu.PrefetchScalarGridSpec(
        num_scalar_prefetch=2,
        grid=(n_blocks,),
        in_specs=[pl.BlockSpec((bm, D_MODEL // 2), lambda i, be, nb: (i, 0)),
                  wspec((D_MODEL, D_EXPERT)), wspec((D_MODEL, D_EXPERT)), wspec((D_EXPERT, D_MODEL))],
        out_specs=pl.BlockSpec((bm, D_MODEL // 2), lambda i, be, nb: (i, 0)),
        scratch_shapes=[pltpu.VMEM((D_MODEL, D_EXPERT), BF16),
                        pltpu.VMEM((D_MODEL, D_EXPERT), BF16),
                        pltpu.VMEM((D_EXPERT, D_MODEL), BF16)],
    )
    return pl.pallas_call(
        _moe_kernel,
        grid_spec=gs,
        out_shape=jax.ShapeDtypeStruct((n_blocks * bm, D_MODEL // 2), jnp.uint32),
        compiler_params=_cparams(("arbitrary",)),
        name="moe",
    )(block_e, n_used, xs, w1, w3, w2)


def _final_kernel(x1_ref, ya_ref, yb_ref, rt_ref, g2_ref, nf_ref, o_ref):
    rt = rt_ref[...]
    moe = (rt[:, TOP_K:TOP_K + 1] * _unpack_pairs(ya_ref[...])
           + rt[:, TOP_K + 1:TOP_K + 2] * _unpack_pairs(yb_ref[...]))
    o_ref[...] = _rms(x1_ref[...] + g2_ref[...] * moe, nf_ref[...])


def _final(x1, ya, yb, rt, mod3, norm_f, *, rows_per_mod, row_offset):
    rows = x1.shape[0]
    tm = min(FINAL_TILE, rows)
    assert row_offset % tm == 0 and rows_per_mod % tm == 0
    tile_offset, tiles_per_mod = row_offset // tm, rows_per_mod // tm
    mr = mod3.shape[1]
    row = pl.BlockSpec((tm, D_MODEL), lambda i: (i, 0))
    tok = lambda w: pl.BlockSpec((tm, w), lambda i: (i + tile_offset, 0))
    return pl.pallas_call(
        _final_kernel,
        grid=(rows // tm,),
        in_specs=[row, tok(D_MODEL // 2), tok(D_MODEL // 2), tok(LANES),
                  pl.BlockSpec((None, mr, D_MODEL), lambda i: (i // tiles_per_mod, 0, 5)),
                  _resident((1, D_MODEL))],
        out_specs=row,
        out_shape=jax.ShapeDtypeStruct((rows, D_MODEL), F32),
        compiler_params=_cparams(("arbitrary",)),
        name="final",
    )(x1, ya, yb, rt, mod3, norm_f.reshape(1, D_MODEL))


def _routing_plan(eid, n_tokens):
    bm = MOE_BLOCK
    n_assign = n_tokens * TOP_K
    n_blocks = -(-n_assign // bm) + N_EXPERTS
    flat_e = eid.reshape(-1)
    onehot = (flat_e[:, None] == jnp.arange(N_EXPERTS, dtype=jnp.int32)[None, :]).astype(jnp.int32)
    csum = jnp.cumsum(onehot, axis=0)
    counts = csum[-1]
    padded = (counts + bm - 1) // bm * bm
    pad_end = jnp.cumsum(padded)
    pad_start = pad_end - padded
    dest = jnp.sum(onehot * (csum - 1 + pad_start[None, :]), axis=1).astype(jnp.int32)
    block_start = jnp.arange(n_blocks, dtype=jnp.int32) * bm
    block_e = jnp.minimum(jnp.sum((pad_end[None, :] <= block_start[:, None]).astype(jnp.int32), axis=1),
                          N_EXPERTS - 1).astype(jnp.int32)
    n_used = (pad_end[-1] // bm).astype(jnp.int32).reshape(1)
    return block_e, n_used, dest.reshape(n_tokens, TOP_K)


def _layer(xp, xs, c_all, cache_k, cache_v, cache_logf, state_conv, state_h, p, norm_f):
    Bp, Tp, _ = xp.shape
    Bs, Ts, _ = xs.shape
    past = cache_k.shape[1]
    np_rows, ns_rows = Bp * Tp, Bs * Ts
    assert Tp % ROW_TILE == 0 and ns_rows == ROW_TILE and Tp % ATTN_BLOCK == 0

    mod = _mod(c_all, p['w_mod'], p['b_mod'])
    mod_p = mod[:Bp].reshape(Bp, 1, 6 * D_MODEL)
    mod_s = jnp.repeat(mod[Bp:], Ts, axis=0).reshape(1, ns_rows, 6 * D_MODEL)

    w_in = p['w_in']
    d3 = 3 * D_MODEL
    wcat = jnp.concatenate([w_in[:, :D_MODEL] * (LOG2E * HEAD_DIM ** -0.5), w_in[:, D_MODEL:d3],
                            w_in[:, d3 + N_HEADS:], p['w_gate']], axis=1).astype(BF16)
    wfl = jnp.pad(w_in[:, d3:d3 + N_HEADS], ((0, 0), (0, LANES - N_HEADS))).astype(BF16)
    wflt = wfl.T
    eye = jnp.eye(GATE_TILE // (D_MODEL // N_RNN_BLOCKS), dtype=F32)

    def gate_tiles(w):
        per = GATE_TILE // w.shape[1]
        w4 = w.reshape(N_RNN_BLOCKS // per, per, w.shape[1], w.shape[2])
        return jnp.einsum('gpcd,pq->gpcqd', w4, eye).reshape(N_RNN_BLOCKS // per, GATE_TILE, GATE_TILE).astype(BF16)

    wr4, wi4 = gate_tiles(p['w_r']), gate_tiles(p['w_i'])
    wpa, wpb, wo = p['w_pa'].astype(BF16), p['w_pb'].astype(BF16), p['w_out'].astype(BF16)
    wr = jnp.pad(jnp.concatenate([p['w_rg'], p['w_re']], axis=1), ((0, 0), (0, LANES - N_GROUPS - N_EXPERTS)))
    wr1 = wr.astype(BF16)
    wr2 = (wr - wr1.astype(F32)).astype(BF16)
    br = jnp.pad(jnp.concatenate([p['b_rg'], p['b_re']]), (0, LANES - N_GROUPS - N_EXPERTS)).reshape(1, LANES)

    in_p = _in_proj(xp.reshape(np_rows, D_MODEL), mod_p, p['norm1'], wcat[:, :d3], wfl, wflt, p['b_gate'], p['b_f'],
                    tiles_per_mod=Tp // IN_TILE_QKV, tiles_per_seq=Tp // IN_TILE_QKV, with_mixer=False)
    in_s = _in_proj(xs.reshape(ns_rows, D_MODEL), mod_s, p['norm1'], wcat, wfl, wflt, p['b_gate'], p['b_f'],
                    tiles_per_mod=1, tiles_per_seq=1, with_mixer=True)
    q_p, kb_p, kt_p, vt_p, lf_p, _, fc_p = in_p
    q_s, kb_s, vb_s, k_s, v_s, xb_s, gb_s, ga_s, gg_s, lf_s, lft_s, _ = in_s
    r3 = lambda a, B, T: a.reshape(B, T, D_MODEL)

    oa_p = _attn_prompt(r3(q_p, Bp, Tp), r3(kb_p, Bp, Tp), vt_p, fc_p.reshape(Bp, Tp, N_HEADS))

    lfp3 = jnp.transpose(cache_logf.astype(F32), (0, 2, 1))
    lfn3 = lft_s.reshape(N_HEADS, Bs, Ts).transpose(1, 0, 2)
    oa_s = _attn_sample(r3(q_s, Bs, Ts), r3(kb_s, Bs, Ts), r3(vb_s, Bs, Ts),
                        jnp.transpose(cache_k, (0, 2, 3, 1)), jnp.transpose(cache_v, (0, 2, 3, 1)), lfp3, lfn3)

    zc = jnp.zeros((Bp, SUBLANES, D_MODEL), F32)
    zh = jnp.zeros((Bp, 1, D_MODEL), F32)
    rnn_w = (p['conv_w'], p['conv_b'], wr4, wi4, p['b_r'], p['b_i'], p['rglru_lambda'])
    post_w = (wpa, wpb, wo, wr1, wr2, br)
    x1_p, h2_p, rt_p, conv_p, hl_p = _mix(xp, oa_p, mod_p, zc, zh, p['norm1'], p['norm2'], wcat[:, d3:], p['b_gate'],
                                          *rnn_w, *post_w)
    x1_p, h2_p, rt_p = (a.reshape(np_rows, a.shape[-1]) for a in (x1_p, h2_p, rt_p))

    sc = jnp.pad(state_conv.astype(F32), ((0, 0), (SUBLANES - (CONV_WIDTH - 1), 0), (0, 0)))
    ob_s, conv_s, hl_s = _rnn(r3(xb_s, Bs, Ts), r3(gb_s, Bs, Ts), sc, state_h.astype(F32).reshape(Bs, 1, D_MODEL),
                              *rnn_w)
    x1_s, h2_s, rt_s = _post(xs.reshape(ns_rows, D_MODEL), oa_s.reshape(ns_rows, D_MODEL),
                             ob_s.reshape(ns_rows, D_MODEL), ga_s, gg_s, mod_s, p['norm2'], *post_w, tiles_per_mod=1)

    n_tok = np_rows + ns_rows
    rt = jnp.concatenate([rt_p, rt_s], axis=0)
    block_e, n_used, dest = _routing_plan(rt[:, :TOP_K].astype(jnp.int32), n_tok)
    slots = jnp.zeros((block_e.shape[0] * MOE_BLOCK, D_MODEL // 2), jnp.uint32)
    slots = _dispatch(h2_p, dest.reshape(-1), slots, tile_offset=0)
    slots = _dispatch(h2_s, dest.reshape(-1), slots, tile_offset=np_rows // ROW_TILE)
    take = lambda a, idx: a.at[idx].get(mode='promise_in_bounds')
    yb = _moe(slots, block_e, n_used, p['w1'], p['w3'], p['w2'])
    ya0 = take(yb, dest[:, 0])
    ya1 = take(yb, dest[:, 1])

    y_p = _final(x1_p, ya0, ya1, rt, mod_p, norm_f, rows_per_mod=Tp, row_offset=0)
    y_s = _final(x1_s, ya0, ya1, rt, mod_s, norm_f, rows_per_mod=ns_rows, row_offset=np_rows)

    st = CONV_WIDTH - 1
    state_p = (jnp.transpose(kt_p, (0, 3, 1, 2)), jnp.transpose(vt_p, (0, 3, 1, 2)),
               lf_p.reshape(Bp, Tp, N_HEADS), conv_p[:, SUBLANES - st:], hl_p.reshape(Bp, D_MODEL))
    state_s = (k_s.reshape(Bs, Ts, N_HEADS, HEAD_DIM), v_s.reshape(Bs, Ts, N_HEADS, HEAD_DIM),
               lf_s.reshape(Bs, Ts, N_HEADS), conv_s[:, SUBLANES - st:], hl_s.reshape(Bs, D_MODEL))
    return y_p.reshape(Bp, Tp, D_MODEL), y_s.reshape(Bs, Ts, D_MODEL), state_p, state_s


def kernel(x_prompt, x_sample, cache_k, cache_v, cache_logf, state_conv, state_h, c_prompt, c_sample, w_mod, b_mod, norm1, w_in, b_f, conv_w, conv_b, w_r, b_r, w_i, b_i, rglru_lambda, w_gate, b_gate, w_pa, w_pb, w_out, norm2, w_rg, b_rg, w_re, b_re, w1, w3, w2, norm_f):
    assert w_mod.shape[0] == 1, "single-layer trunk: the final norm is fused into the layer's last stage"
    names = ('w_mod', 'b_mod', 'norm1', 'w_in', 'b_f', 'conv_w', 'conv_b', 'w_r', 'b_r', 'w_i', 'b_i',
             'rglru_lambda', 'w_gate', 'b_gate', 'w_pa', 'w_pb', 'w_out', 'norm2', 'w_rg', 'b_rg', 'w_re', 'b_re',
             'w1', 'w3', 'w2')
    vals = (w_mod, b_mod, norm1, w_in, b_f, conv_w, conv_b, w_r, b_r, w_i, b_i, rglru_lambda, w_gate, b_gate,
            w_pa, w_pb, w_out, norm2, w_rg, b_rg, w_re, b_re, w1, w3, w2)
    p = {n: v[0] for n, v in zip(names, vals)}
    c_all = jnp.concatenate([c_prompt, c_sample], axis=0)
    y_p, y_s, sp, ss = _layer(x_prompt, x_sample, c_all, cache_k[0], cache_v[0], cache_logf[0],
                              state_conv[0], state_h[0], p, norm_f)
    lead = lambda a: a[None]
    return (y_p, y_s, lead(sp[0]), lead(sp[1]), lead(sp[2]), lead(sp[3]), lead(sp[4]),
            lead(ss[0]), lead(ss[1]), lead(ss[2]), lead(ss[3]), lead(ss[4]))
```

```python
import functools

import jax
import jax.numpy as jnp
from jax import lax
from jax.experimental import pallas as pl
from jax.experimental.pallas import tpu as pltpu

F32 = jnp.float32
BF16 = jnp.bfloat16

D_MODEL = 1024
N_HEADS = 16
HEAD_DIM = 64
HEAD_PAIR = 2 * HEAD_DIM
N_PAIRS = N_HEADS // 2
N_RNN_BLOCKS = 16
CONV_WIDTH = 4
RGLRU_C = 8.0
N_GROUPS = 4
EXPERTS_PER_GROUP = 8
N_EXPERTS = N_GROUPS * EXPERTS_PER_GROUP
TOP_K = 2
D_EXPERT = 512
EPS = 1e-6

LANES = 128
SUBLANES = 8
ROW_TILE = 512
FINAL_TILE = 1024
IN_TILE = 256
IN_TILE_QKV = 512
ATTN_BLOCK = 512
ATTN_QBLOCK = 2048
ATTN_VROWS = HEAD_DIM + 16
RNN_TILE = 256
MIX_TILE = 512
MOE_BLOCK = 512
MOE_SUB = 256
GATE_TILE = 256
VMEM_LIMIT = 56 * 1024 * 1024
NEG_BIG = -1e30
LOG2E = 1.4426950408889634


def _cparams(sem):
    return pltpu.CompilerParams(dimension_semantics=sem, vmem_limit_bytes=VMEM_LIMIT)


def _resident(shape):
    nd = len(shape)
    return pl.BlockSpec(shape, lambda *_: (0,) * nd, pipeline_mode=pl.Buffered(1))


def _split3(x):
    p1 = x.astype(BF16)
    r1 = x - p1.astype(F32)
    p2 = r1.astype(BF16)
    p3 = (r1 - p2.astype(F32)).astype(BF16)
    return p1, p2, p3


def _dot(a, b):
    return jnp.dot(a, b, preferred_element_type=F32)


def _dot_nt(a, b):
    return lax.dot_general(a, b, (((1,), (1,)), ((), ())), preferred_element_type=F32)


def _sigmoid(x):
    return 0.5 * jnp.tanh(0.5 * x) + 0.5


def _gelu_tanh(x):
    return 0.5 * x * (1.0 + jnp.tanh(0.7978845608028654 * (x + 0.044715 * x * x * x)))


def _rms(x, g):
    ms = jnp.mean(x * x, axis=-1, keepdims=True)
    return x * lax.rsqrt(ms + EPS) * g


def _pack_pairs(x):
    n = x.shape[1] // 2
    hi = pltpu.bitcast(x[:, :n].astype(BF16).astype(F32), jnp.uint32)
    lo = pltpu.bitcast(x[:, n:].astype(BF16).astype(F32), jnp.uint32)
    return hi | lax.shift_right_logical(lo, jnp.uint32(16))


def _unpack_pairs(w):
    hi = pltpu.bitcast(w & jnp.uint32(0xFFFF0000), F32)
    lo = pltpu.bitcast(lax.shift_left(w, jnp.uint32(16)), F32)
    return jnp.concatenate([hi, lo], axis=1)


def _div_p2(x, n):
    assert n & (n - 1) == 0
    return lax.shift_right_logical(x, jnp.int32(n.bit_length() - 1))


def _mod_p2(x, n):
    assert n & (n - 1) == 0
    return x & (n - 1)


def _upper_tri(n):
    r = lax.broadcasted_iota(jnp.int32, (n, n), 0)
    c = lax.broadcasted_iota(jnp.int32, (n, n), 1)
    return jnp.where(r <= c, 1.0, 0.0).astype(BF16)


def _cumsum_lanes(x, u):
    p1, p2, p3 = _split3(x)
    return _dot(p1, u) + _dot(p2, u) + _dot(p3, u)


def _mod_kernel(c_ref, w_ref, b_ref, o_ref):
    c = c_ref[...]
    s = c * _sigmoid(c)
    s1 = s.astype(BF16)
    s2 = (s - s1.astype(F32)).astype(BF16)
    w = w_ref[...]
    w1 = w.astype(BF16)
    w2 = (w - w1.astype(F32)).astype(BF16)
    o_ref[...] = _dot(s1, w1) + _dot(s1, w2) + _dot(s2, w1) + b_ref[...]


def _mod(c, w, b):
    n = c.shape[0]
    cols = w.shape[1]
    return pl.pallas_call(
        _mod_kernel,
        grid=(cols // D_MODEL,),
        in_specs=[pl.BlockSpec((n, D_MODEL), lambda j: (0, 0)),
                  pl.BlockSpec((D_MODEL, D_MODEL), lambda j: (0, j)),
                  pl.BlockSpec((1, D_MODEL), lambda j: (0, j))],
        out_specs=pl.BlockSpec((n, D_MODEL), lambda j: (0, j)),
        out_shape=jax.ShapeDtypeStruct((n, cols), F32),
        compiler_params=_cparams(("arbitrary",)),
        name="mod",
    )(c, w, b.reshape(1, cols))


def _in_kernel(x_ref, sh_ref, sc_ref, n1_ref, w_ref, wfl_ref, wflt_ref, bg_ref, bf_ref, bft_ref, *refs,
               tiles_per_seq, with_mixer):
    lf_ref, lft_ref, fc_ref, carry_ref = refs[-4:]
    x = x_ref[...]
    h = (_rms(x, n1_ref[...]) * (1.0 + sc_ref[...]) + sh_ref[...]).astype(BF16)

    def mm(c):
        return _dot(h, w_ref[:, c * D_MODEL:(c + 1) * D_MODEL])

    def head_major(y):
        slabs = []
        for p in range(N_PAIRS):
            pair = y[:, p * HEAD_PAIR:(p + 1) * HEAD_PAIR]
            slabs += [pair, pltpu.roll(pair, HEAD_DIM, 1)]
        return jnp.swapaxes(jnp.stack(slabs, axis=0), 0, 1)[:, :, :HEAD_DIM]

    def time_minor(y):
        return y.T.reshape(N_HEADS, HEAD_DIM, y.shape[0])

    refs[0][...] = mm(0).astype(BF16)
    k = mm(1)
    refs[1][...] = k.astype(BF16)
    v = mm(2)
    if with_mixer:
        vb_ref, k5_ref, v5_ref, xb_ref, gb_ref, ga_ref, gg_ref = refs[2:9]
        vb_ref[...] = v.astype(BF16)
        k5_ref[...] = head_major(k)
        v5_ref[...] = head_major(v)
        xb_ref[...] = mm(3).astype(BF16)
        gb_ref[...] = mm(4).astype(BF16)
        ga_ref[...] = _sigmoid(mm(5) + bg_ref[:, :D_MODEL]).astype(BF16)
        gg_ref[...] = _sigmoid(mm(6) + bg_ref[:, D_MODEL:]).astype(BF16)
    else:
        refs[2][...] = time_minor(k)
        refs[3][...] = time_minor(v)

    def log_sigmoid(z):
        return jnp.minimum(z, 0.0) - jnp.log(1.0 + jnp.exp(-jnp.abs(z)))

    lf = log_sigmoid(_dot(h, wfl_ref[...]) + bf_ref[...])
    lf_ref[...] = lf[:, :N_HEADS]
    lft_ref[...] = log_sigmoid(_dot_nt(wflt_ref[...], h)[:N_HEADS, :] + bft_ref[...])

    @pl.when(pl.program_id(0) % tiles_per_seq == 0)
    def _():
        carry_ref[...] = jnp.zeros_like(carry_ref)

    tm = x.shape[0]
    r = lax.broadcasted_iota(jnp.int32, (tm, tm), 0)
    c = lax.broadcasted_iota(jnp.int32, (tm, tm), 1)
    lower = jnp.where(c <= r, 1.0, 0.0).astype(BF16)
    p1, p2, p3 = _split3(lf)
    fc = _dot(lower, p1) + _dot(lower, p2) + _dot(lower, p3) + carry_ref[...]
    fc_ref[...] = fc[:, :N_HEADS]
    carry_ref[...] = fc[tm - 1:tm, :]


def _in_proj(x2, mod3, norm1, wcat, wfl, wflt, b_gate, b_f, *, tiles_per_mod, tiles_per_seq, with_mixer):
    rows = x2.shape[0]
    tm = IN_TILE if with_mixer else IN_TILE_QKV
    nt = rows // tm
    row_blk = pl.BlockSpec((tm, D_MODEL), lambda i: (i, 0))
    if mod3.shape[1] == 1:
        modc = lambda c: pl.BlockSpec((None, 1, D_MODEL), lambda i: (i // tiles_per_mod, 0, c))
    else:
        modc = lambda c: pl.BlockSpec((None, tm, D_MODEL), lambda i: (0, i, c))
    row_out = jax.ShapeDtypeStruct((rows, D_MODEL), BF16)
    if with_mixer:
        kv_shape = jax.ShapeDtypeStruct((rows, N_HEADS, HEAD_DIM), F32)
        kv_blk = pl.BlockSpec((tm, N_HEADS, HEAD_DIM), lambda i: (i, 0, 0))
        lead = ([row_out] * 3 + [kv_shape] * 2 + [row_out] * 4, [row_blk] * 3 + [kv_blk] * 2 + [row_blk] * 4)
    else:
        n_seq = nt // tiles_per_seq
        kv_shape = jax.ShapeDtypeStruct((n_seq, N_HEADS, HEAD_DIM, tiles_per_seq * tm), F32)
        kv_blk = pl.BlockSpec((None, N_HEADS, HEAD_DIM, tm), lambda i: (i // tiles_per_seq, 0, 0, i % tiles_per_seq))
        lead = ([row_out] * 2 + [kv_shape] * 2, [row_blk] * 2 + [kv_blk] * 2)
    outs = [*lead[0],
            jax.ShapeDtypeStruct((rows, N_HEADS), F32),
            jax.ShapeDtypeStruct((N_HEADS, rows), F32),
            jax.ShapeDtypeStruct((rows, N_HEADS), F32)]
    return pl.pallas_call(
        functools.partial(_in_kernel, tiles_per_seq=tiles_per_seq, with_mixer=with_mixer),
        grid=(nt,),
        in_specs=[row_blk, modc(0), modc(1),
                  _resident((1, D_MODEL)),
                  _resident(wcat.shape),
                  _resident(wfl.shape),
                  _resident(wflt.shape),
                  _resident((1, 2 * D_MODEL)),
                  _resident((1, LANES)),
                  _resident((N_HEADS, 1))],
        out_specs=lead[1]
                  + [pl.BlockSpec((tm, N_HEADS), lambda i: (i, 0)),
                     pl.BlockSpec((N_HEADS, tm), lambda i: (0, i)),
                     pl.BlockSpec((tm, N_HEADS), lambda i: (i, 0))],
        out_shape=outs,
        scratch_shapes=[pltpu.VMEM((1, LANES), F32)],
        compiler_params=_cparams(("arbitrary",)),
        name="in_proj",
    )(x2, mod3, mod3, norm1.reshape(1, D_MODEL), wcat, wfl, wflt,
      b_gate.reshape(1, 2 * D_MODEL), jnp.pad(b_f, (0, LANES - N_HEADS)).reshape(1, LANES),
      b_f.reshape(N_HEADS, 1))


def _attn_kernel(q_ref, k_ref, v_ref, f_ref, o_ref, kb_ref, vt_ref, qt_ref, s0_ref, m_ref, acc_ref):
    qi = pl.program_id(2)
    bq = q_ref.shape[0]
    bk = ATTN_BLOCK
    nk = k_ref.shape[0] // bk
    own_lanes = (lambda i: i < HEAD_DIM, lambda i: i >= HEAD_DIM)
    bias_at = (HEAD_DIM, 0)

    @pl.when(qi == 0)
    def _():
        lane = lax.broadcasted_iota(jnp.int32, (bk, HEAD_PAIR), 1)
        ones_rows = jnp.where(lax.broadcasted_iota(jnp.int32, (ATTN_VROWS - HEAD_DIM, bk), 0) == 0, 1.0, 0.0)
        for c in range(nk):
            blk = slice(c * bk, (c + 1) * bk)
            k = k_ref[blk, :].astype(F32)
            f = f_ref[blk, :] * LOG2E
            head = lax.broadcasted_iota(jnp.int32, f.shape, 1)
            for hh in range(2):
                mine = head == 2 * pl.program_id(1) + hh
                f_head = jnp.sum(jnp.where(mine, f, 0.0), axis=1, keepdims=True)
                slab = jnp.zeros(k.shape, F32)
                for n, piece in enumerate(_split3(f_head)):
                    slab = jnp.where(lane == bias_at[hh] + n, piece.astype(F32), slab)
                kb_ref[hh, blk, :] = jnp.where(own_lanes[hh](lane), k, slab).astype(BF16)
            for hh in range(2):
                vt_ref[c, hh, 0:HEAD_DIM] = v_ref[hh, :, blk].astype(BF16)
                vt_ref[c, hh, HEAD_DIM:ATTN_VROWS] = ones_rows.astype(BF16)

    qt = q_ref[...].astype(F32).T
    feat = lax.broadcasted_iota(jnp.int32, qt.shape, 0)
    for hh in range(2):
        is_bias = (feat >= bias_at[hh]) & (feat < bias_at[hh] + 3)
        qt_ref[hh] = jnp.where(own_lanes[hh](feat), qt, jnp.where(is_bias, -1.0, 0.0)).astype(BF16)

    m_ref[...] = jnp.full(m_ref.shape, NEG_BIG, F32)
    acc_ref[...] = jnp.zeros(acc_ref.shape, F32)

    nsub = bq // bk

    def scores(hh, u, j):
        keys = kb_ref[hh, pl.ds(pl.multiple_of(j * bk, bk), bk), :]
        return _dot(keys, qt_ref[hh, :, u * bk:(u + 1) * bk])

    def softmax_pv(hh, u, j, s, masked):
        cols = slice(u * bk, (u + 1) * bk)
        if masked:
            key = lax.broadcasted_iota(jnp.int32, s.shape, 0)
            qry = lax.broadcasted_iota(jnp.int32, s.shape, 1)
            s = jnp.where(key <= qry, s, NEG_BIG)
        m_prev = m_ref[hh, :, cols]
        m_new = jnp.maximum(m_prev, jnp.max(s, axis=0, keepdims=True))
        p = jnp.exp2(s - m_new).astype(BF16)
        acc_ref[hh, :, cols] = jnp.exp2(m_prev - m_new) * acc_ref[hh, :, cols] + _dot(vt_ref[j, hh], p)
        m_ref[hh, :, cols] = m_new

    def run(items, first_of_next):
        s = s0_ref[...]
        for n, (hh, u, j, masked) in enumerate(items):
            if n + 1 < len(items):
                nxt = scores(*items[n + 1][:3])
            elif first_of_next is not None:
                s0_ref[...] = scores(*first_of_next)
            softmax_pv(hh, u, j, s, masked)
            s = nxt if n + 1 < len(items) else None

    def body(j, carry):
        run([(hh, u, j, False) for u in range(nsub) for hh in range(2)], (0, 0, j + 1))
        return carry

    s0_ref[...] = scores(0, 0, 0)
    lax.fori_loop(0, nsub * qi, body, 0)
    for d in range(nsub):
        j = nsub * qi + d
        items = [(hh, u, j, u == d) for u in range(d, nsub) for hh in range(2)]
        run(items, (0, d + 1, j + 1) if d + 1 < nsub else None)

    heads = [acc_ref[hh, 0:HEAD_DIM] / acc_ref[hh, HEAD_DIM:HEAD_DIM + 1] for hh in range(2)]
    o_ref[...] = jnp.concatenate(heads, axis=0).T.astype(BF16)


def _attn_prompt(q3, k3, vt4, fc3):
    B, T, _ = q3.shape
    bq, bk = ATTN_QBLOCK, ATTN_BLOCK
    nq, nk = T // bq, T // bk
    return pl.pallas_call(
        _attn_kernel,
        grid=(B, N_PAIRS, nq),
        in_specs=[pl.BlockSpec((None, bq, HEAD_PAIR), lambda b, p, i: (b, i, p)),
                  pl.BlockSpec((None, T, HEAD_PAIR), lambda b, p, i: (b, 0, p)),
                  pl.BlockSpec((None, 2, HEAD_DIM, T), lambda b, p, i: (b, p, 0, 0)),
                  pl.BlockSpec((None, T, N_HEADS), lambda b, p, i: (b, 0, 0))],
        out_specs=pl.BlockSpec((None, bq, HEAD_PAIR), lambda b, p, i: (b, i, p)),
        out_shape=jax.ShapeDtypeStruct((B, T, D_MODEL), BF16),
        scratch_shapes=[pltpu.VMEM((2, T, HEAD_PAIR), BF16),
                        pltpu.VMEM((nk, 2, ATTN_VROWS, bk), BF16),
                        pltpu.VMEM((2, HEAD_PAIR, bq), BF16),
                        pltpu.VMEM((bk, bk), F32),
                        pltpu.VMEM((2, 1, bq), F32),
                        pltpu.VMEM((2, ATTN_VROWS, bq), F32)],
        compiler_params=_cparams(("arbitrary", "arbitrary", "arbitrary")),
        name="attn_prompt",
    )(q3, k3, vt4, fc3)


def _attn_sample_kernel(q_ref, kn_ref, vn_ref, kc_ref, vc_ref, lfp_ref, lfn_ref, o_ref):
    tq = q_ref.shape[0]
    past = kc_ref.shape[-1]
    nrow = N_HEADS * tq
    row = lax.broadcasted_iota(jnp.int32, (nrow, D_MODEL), 0)
    col = lax.broadcasted_iota(jnp.int32, (nrow, D_MODEL), 1)
    own = _div_p2(row, tq) == _div_p2(col, HEAD_DIM)

    rr = lax.broadcasted_iota(jnp.int32, (nrow, tq), 0)
    rc = lax.broadcasted_iota(jnp.int32, (nrow, tq), 1)
    rep = jnp.where(_mod_p2(rr, tq) == rc, 1.0, 0.0).astype(BF16)
    qx = jnp.where(own, _dot(rep, q_ref[...]), 0.0).astype(BF16)

    fp = _cumsum_lanes(lfp_ref[...] * LOG2E, _upper_tri(past))
    fn = _cumsum_lanes(lfn_ref[...] * LOG2E, _upper_tri(tq)) + fp[:, past - 1:past]
    hr = lax.broadcasted_iota(jnp.int32, (nrow, N_HEADS), 0)
    hc = lax.broadcasted_iota(jnp.int32, (nrow, N_HEADS), 1)
    hsel = jnp.where(_div_p2(hr, tq) == hc, 1.0, 0.0).astype(BF16)

    def expand(f):
        p1, p2, p3 = _split3(f)
        return _dot(hsel, p1) + _dot(hsel, p2) + _dot(hsel, p3)

    s_p = _dot(qx, kc_ref[...].reshape(D_MODEL, past).astype(BF16)) - expand(fp)
    s_n = _dot_nt(qx, kn_ref[...]) - expand(fn)
    nr = lax.broadcasted_iota(jnp.int32, (nrow, tq), 0)
    nc = lax.broadcasted_iota(jnp.int32, (nrow, tq), 1)
    s_n = jnp.where(nc <= _mod_p2(nr, tq), s_n, NEG_BIG)
    m = jnp.maximum(jnp.max(s_p, axis=-1, keepdims=True), jnp.max(s_n, axis=-1, keepdims=True))
    p_p = jnp.exp2(s_p - m)
    p_n = jnp.exp2(s_n - m)
    l = jnp.sum(p_p, axis=-1, keepdims=True) + jnp.sum(p_n, axis=-1, keepdims=True)
    o_full = (_dot_nt(p_p.astype(BF16), vc_ref[...].reshape(D_MODEL, past).astype(BF16))
              + _dot(p_n.astype(BF16), vn_ref[...]))
    o_own = jnp.where(own, o_full / l, 0.0).astype(BF16)
    cr = lax.broadcasted_iota(jnp.int32, (tq, nrow), 0)
    cc = lax.broadcasted_iota(jnp.int32, (tq, nrow), 1)
    col_sel = jnp.where(_mod_p2(cc, tq) == cr, 1.0, 0.0).astype(BF16)
    o_ref[...] = _dot(col_sel, o_own).astype(BF16)


def _attn_sample(q3, kn3, vn3, kct, vct, lfp3, lfn3):
    B, tq, _ = q3.shape
    past = kct.shape[-1]
    blk = lambda n, w: pl.BlockSpec((None, n, w), lambda b: (b, 0, 0))
    cache = pl.BlockSpec((None, N_HEADS, HEAD_DIM, past), lambda b: (b, 0, 0, 0))
    return pl.pallas_call(
        _attn_sample_kernel,
        grid=(B,),
        in_specs=[blk(tq, D_MODEL), blk(tq, D_MODEL), blk(tq, D_MODEL),
                  cache, cache, blk(N_HEADS, past), blk(N_HEADS, tq)],
        out_specs=blk(tq, D_MODEL),
        out_shape=jax.ShapeDtypeStruct((B, tq, D_MODEL), BF16),
        compiler_params=_cparams(("arbitrary",)),
        name="attn_sample",
    )(q3, kn3, vn3, kct, vct, lfp3, lfn3)


def _rglru_gates(xb, xp_ref, cw_ref, cb_ref, wr_ref, wi_ref):
    tt = xb.shape[0]
    pad = SUBLANES
    xp_ref[pad:pad + tt, :] = xb
    xc = cb_ref[...] + xb * cw_ref[CONV_WIDTH - 1:CONV_WIDTH, :]
    for k in range(CONV_WIDTH - 1):
        back = CONV_WIDTH - 1 - k
        xc = xc + xp_ref[pad - back:pad - back + tt, :] * cw_ref[k:k + 1, :]
    tail = xp_ref[tt:tt + pad, :]
    xp_ref[0:pad, :] = tail

    xcb = xc.astype(BF16)
    ng = D_MODEL // GATE_TILE
    r = jnp.concatenate([_dot(xcb[:, g * GATE_TILE:(g + 1) * GATE_TILE], wr_ref[g]) for g in range(ng)], axis=-1)
    i = jnp.concatenate([_dot(xcb[:, g * GATE_TILE:(g + 1) * GATE_TILE], wi_ref[g]) for g in range(ng)], axis=-1)
    return xc, r, i, tail


def _rglru_scan(xc, r, i, gb, hc_ref, br_ref, bi_ref, lam_ref):
    tt = xc.shape[0]
    r = _sigmoid(r + br_ref[...])
    i = _sigmoid(i + bi_ref[...])
    lam = lam_ref[...]
    softplus_neg = jnp.maximum(-lam, 0.0) + jnp.log(1.0 + jnp.exp(-jnp.abs(lam)))
    log_a = (-RGLRU_C) * r * softplus_neg
    a = jnp.exp(log_a)
    th = jnp.tanh(log_a)
    mult = jnp.sqrt(-2.0 * th / (1.0 - th))
    b = mult * (i * xc)

    ng8 = tt // SUBLANES
    a3 = a.reshape(ng8, SUBLANES, D_MODEL)
    b3 = b.reshape(ng8, SUBLANES, D_MODEL)
    sub = lax.broadcasted_iota(jnp.int32, a3.shape, 1)
    for s in (1, 2, 4):
        keep = sub >= s
        a_sh = jnp.where(keep, pltpu.roll(a3, s, 1), 1.0)
        b_sh = jnp.where(keep, pltpu.roll(b3, s, 1), 0.0)
        b3 = a3 * b_sh + b3
        a3 = a3 * a_sh
    hc = hc_ref[...]
    hs = []
    for g in range(ng8):
        hg = b3[g] + a3[g] * hc
        hs.append(hg)
        hc = hg[SUBLANES - 1:SUBLANES, :]
    hc_ref[...] = hc
    return jnp.concatenate(hs, axis=0) * _gelu_tanh(gb), hc


def _rnn_kernel(xb_ref, gb_ref, c0_ref, h0_ref, cw_ref, cb_ref, wr_ref, wi_ref, br_ref, bi_ref, lam_ref,
                o_ref, cout_ref, hout_ref, xp_ref, hc_ref):
    ti = pl.program_id(1)

    @pl.when(ti == 0)
    def _():
        xp_ref[0:SUBLANES, :] = c0_ref[...]
        hc_ref[...] = h0_ref[...]

    xc, r, i, tail = _rglru_gates(xb_ref[...].astype(F32), xp_ref, cw_ref, cb_ref, wr_ref, wi_ref)
    ob, hc = _rglru_scan(xc, r, i, gb_ref[...].astype(F32), hc_ref, br_ref, bi_ref, lam_ref)
    o_ref[...] = ob.astype(BF16)

    @pl.when(ti == pl.num_programs(1) - 1)
    def _():
        cout_ref[...] = tail
        hout_ref[...] = hc


def _rnn(xb3, gb3, conv0, h0, conv_w, conv_b, wr4, wi4, b_r, b_i, lam):
    B, T, _ = xb3.shape
    tt = min(RNN_TILE, T)
    vec = lambda a: a.reshape(1, D_MODEL)
    seq = pl.BlockSpec((None, tt, D_MODEL), lambda b, t: (b, t, 0))
    st8 = pl.BlockSpec((None, SUBLANES, D_MODEL), lambda b, t: (b, 0, 0))
    st1 = pl.BlockSpec((None, 1, D_MODEL), lambda b, t: (b, 0, 0))
    return pl.pallas_call(
        _rnn_kernel,
        grid=(B, T // tt),
        in_specs=[seq, seq, st8, st1,
                  _resident((CONV_WIDTH, D_MODEL)), _resident((1, D_MODEL)),
                  _resident(wr4.shape), _resident(wi4.shape),
                  _resident((1, D_MODEL)), _resident((1, D_MODEL)), _resident((1, D_MODEL))],
        out_specs=[seq, st8, st1],
        out_shape=[jax.ShapeDtypeStruct((B, T, D_MODEL), BF16),
                   jax.ShapeDtypeStruct((B, SUBLANES, D_MODEL), F32),
                   jax.ShapeDtypeStruct((B, 1, D_MODEL), F32)],
        scratch_shapes=[pltpu.VMEM((tt + SUBLANES, D_MODEL), F32),
                        pltpu.VMEM((1, D_MODEL), F32)],
        compiler_params=_cparams(("arbitrary", "arbitrary")),
        name="rnn",
    )(xb3, gb3, conv0, h0, conv_w, vec(conv_b), wr4, wi4, vec(b_r), vec(b_i), vec(lam))


def _merge_route_tile(x, pa, ob, ga, gg, g1, sh2, sc2, n2_ref, wpb_ref, wo_ref, wr1_ref, wr2_ref, br_ref):
    mix = ga * pa + gg * _dot(ob, wpb_ref[...])
    x1 = x + g1 * _dot(mix.astype(BF16), wo_ref[...])
    h2 = _rms(x1, n2_ref[...]) * (1.0 + sc2) + sh2
    h2a = h2.astype(BF16)
    h2b = (h2 - h2a.astype(F32)).astype(BF16)
    lg = _dot(h2a, wr1_ref[...]) + _dot(h2a, wr2_ref[...]) + _dot(h2b, wr1_ref[...]) + br_ref[...]

    lane = lax.broadcasted_iota(jnp.int32, lg.shape, 1).astype(F32)
    big = float(LANES)
    is_g = lane < N_GROUPS
    gl = jnp.where(is_g, lg, NEG_BIG)
    mg = jnp.max(gl, axis=-1, keepdims=True)
    sg = jnp.sum(jnp.where(is_g, jnp.exp(gl - mg), 0.0), axis=-1, keepdims=True)
    pg = 1.0 / sg
    gidx = jnp.min(jnp.where(is_g & (gl == mg), lane, big), axis=-1, keepdims=True)
    lo = N_GROUPS + EXPERTS_PER_GROUP * gidx
    ing = (lane >= lo) & (lane < lo + EXPERTS_PER_GROUP)
    el = jnp.where(ing, lg, NEG_BIG)
    m1 = jnp.max(el, axis=-1, keepdims=True)
    se = jnp.sum(jnp.where(ing, jnp.exp(el - m1), 0.0), axis=-1, keepdims=True)
    i1 = jnp.min(jnp.where(ing & (el == m1), lane, big), axis=-1, keepdims=True)
    el2 = jnp.where(lane == i1, NEG_BIG, el)
    m2 = jnp.max(el2, axis=-1, keepdims=True)
    i2 = jnp.min(jnp.where(ing & (el2 == m2) & (lane != i1), lane, big), axis=-1, keepdims=True)
    p1 = 1.0 / se
    p2 = jnp.exp(m2 - m1) / se
    den = p1 + p2
    w1 = pg * p1 / den
    w2 = pg * p2 / den
    rt = jnp.where(lane == 0.0, i1 - N_GROUPS,
                   jnp.where(lane == 1.0, i2 - N_GROUPS,
                             jnp.where(lane == 2.0, w1, jnp.where(lane == 3.0, w2, 0.0))))
    return x1, h2, rt


def _post_kernel(x_ref, oa_ref, ob_ref, ga_ref, gg_ref, g1_ref, sh_ref, sc_ref, n2_ref,
                 wpa_ref, wpb_ref, wo_ref, wr1_ref, wr2_ref, br_ref,
                 x1_ref, h2_ref, rt_ref):
    x1, h2, rt = _merge_route_tile(x_ref[...], _dot(oa_ref[...], wpa_ref[...]), ob_ref[...],
                                   ga_ref[...].astype(F32), gg_ref[...].astype(F32), g1_ref[...], sh_ref[...],
                                   sc_ref[...], n2_ref, wpb_ref, wo_ref, wr1_ref, wr2_ref, br_ref)
    x1_ref[...] = x1
    h2_ref[...] = _pack_pairs(h2)
    rt_ref[...] = rt


def _post(x2, oa2, ob2, ga2, gg2, mod3, norm2, wpa, wpb, wo, wr1, wr2, br, *, tiles_per_mod):
    rows = x2.shape[0]
    tm = ROW_TILE
    mr = mod3.shape[1]
    row = pl.BlockSpec((tm, D_MODEL), lambda i: (i, 0))
    modc = lambda c: pl.BlockSpec((None, mr, D_MODEL), lambda i: (i // tiles_per_mod, 0, c))
    return pl.pallas_call(
        _post_kernel,
        grid=(rows // tm,),
        in_specs=[row, row, row, row, row, modc(2), modc(3), modc(4), _resident((1, D_MODEL)),
                  _resident(wpa.shape), _resident(wpb.shape), _resident(wo.shape),
                  _resident(wr1.shape), _resident(wr2.shape), _resident((1, LANES))],
        out_specs=[row, pl.BlockSpec((tm, D_MODEL // 2), lambda i: (i, 0)), pl.BlockSpec((tm, LANES), lambda i: (i, 0))],
        out_shape=[jax.ShapeDtypeStruct((rows, D_MODEL), F32),
                   jax.ShapeDtypeStruct((rows, D_MODEL // 2), jnp.uint32),
                   jax.ShapeDtypeStruct((rows, LANES), F32)],
        compiler_params=_cparams(("arbitrary",)),
        name="post",
    )(x2, oa2, ob2, ga2, gg2, mod3, mod3, mod3, norm2.reshape(1, D_MODEL), wpa, wpb, wo, wr1, wr2, br)


def _mix_kernel(x_ref, oa_ref, sh1_ref, sc1_ref, g1_ref, sh2_ref, sc2_ref, n1_ref, n2_ref, w_ref, bg_ref,
                c0_ref, h0_ref, cw_ref, cb_ref, wr_ref, wi_ref, br_ref, bi_ref, lam_ref,
                wpa_ref, wpb_ref, wo_ref, wr1_ref, wr2_ref, brt_ref,
                x1_ref, h2_ref, rt_ref, cout_ref, hout_ref, xp_ref, hc_ref):
    ti = pl.program_id(1)

    @pl.when(ti == 0)
    def _():
        xp_ref[0:SUBLANES, :] = c0_ref[...]
        hc_ref[...] = h0_ref[...]

    def project(rows):
        x = x_ref[rows, :]
        h = (_rms(x, n1_ref[...]) * (1.0 + sc1_ref[...]) + sh1_ref[...]).astype(BF16)
        mm = lambda c: _dot(h, w_ref[:, c * D_MODEL:(c + 1) * D_MODEL])
        xc, r, i, tail = _rglru_gates(mm(0), xp_ref, cw_ref, cb_ref, wr_ref, wi_ref)
        return dict(x=x, xc=xc, r=r, i=i, tail=tail, gb=mm(1), ga=mm(2), gg=mm(3),
                    pa=_dot(oa_ref[rows, :], wpa_ref[...]))

    def recur(st):
        ob, hc = _rglru_scan(st['xc'], st['r'], st['i'], st['gb'], hc_ref, br_ref, bi_ref, lam_ref)
        return ob.astype(BF16), hc

    def merge(rows, st, ob):
        ga = _sigmoid(st['ga'] + bg_ref[:, :D_MODEL])
        gg = _sigmoid(st['gg'] + bg_ref[:, D_MODEL:])
        x1, h2, rt = _merge_route_tile(st['x'], st['pa'], ob, ga, gg, g1_ref[...], sh2_ref[...], sc2_ref[...],
                                       n2_ref, wpb_ref, wo_ref, wr1_ref, wr2_ref, brt_ref)
        x1_ref[rows, :] = x1
        h2_ref[rows, :] = _pack_pairs(h2)
        rt_ref[rows, :] = rt

    subs = [slice(s, s + RNN_TILE) for s in range(0, x_ref.shape[0], RNN_TILE)]
    states = [project(subs[0])]
    for n, rows in enumerate(subs):
        if n + 1 < len(subs):
            states.append(project(subs[n + 1]))
        ob, hc = recur(states[n])
        merge(rows, states[n], ob)

    @pl.when(ti == pl.num_programs(1) - 1)
    def _():
        cout_ref[...] = states[-1]['tail']
        hout_ref[...] = hc


def _mix(x3, oa3, mod3, conv0, h0, norm1, norm2, w_mix, b_gate, conv_w, conv_b, wr4, wi4, b_r, b_i, lam,
         wpa, wpb, wo, wr1, wr2, br):
    B, T, _ = x3.shape
    tt = MIX_TILE
    vec = lambda a: a.reshape(1, D_MODEL)
    seq = lambda w: pl.BlockSpec((None, tt, w), lambda b, t: (b, t, 0))
    modc = lambda c: pl.BlockSpec((None, 1, D_MODEL), lambda b, t: (b, 0, c))
    st8 = pl.BlockSpec((None, SUBLANES, D_MODEL), lambda b, t: (b, 0, 0))
    st1 = pl.BlockSpec((None, 1, D_MODEL), lambda b, t: (b, 0, 0))
    v1 = _resident((1, D_MODEL))
    return pl.pallas_call(
        _mix_kernel,
        grid=(B, T // tt),
        in_specs=[seq(D_MODEL), seq(D_MODEL), modc(0), modc(1), modc(2), modc(3), modc(4), v1, v1,
                  _resident(w_mix.shape), _resident((1, 2 * D_MODEL)), st8, st1,
                  _resident((CONV_WIDTH, D_MODEL)), v1, _resident(wr4.shape), _resident(wi4.shape), v1, v1, v1,
                  _resident(wpa.shape), _resident(wpb.shape), _resident(wo.shape),
                  _resident(wr1.shape), _resident(wr2.shape), _resident((1, LANES))],
        out_specs=[seq(D_MODEL), seq(D_MODEL // 2), seq(LANES), st8, st1],
        out_shape=[jax.ShapeDtypeStruct((B, T, D_MODEL), F32),
                   jax.ShapeDtypeStruct((B, T, D_MODEL // 2), jnp.uint32),
                   jax.ShapeDtypeStruct((B, T, LANES), F32),
                   jax.ShapeDtypeStruct((B, SUBLANES, D_MODEL), F32),
                   jax.ShapeDtypeStruct((B, 1, D_MODEL), F32)],
        scratch_shapes=[pltpu.VMEM((RNN_TILE + SUBLANES, D_MODEL), F32),
                        pltpu.VMEM((1, D_MODEL), F32)],
        compiler_params=_cparams(("arbitrary", "arbitrary")),
        name="mix",
    )(x3, oa3, mod3, mod3, mod3, mod3, mod3, vec(norm1), vec(norm2), w_mix, b_gate.reshape(1, 2 * D_MODEL),
      conv0, h0, conv_w, vec(conv_b), wr4, wi4, vec(b_r), vec(b_i), vec(lam), wpa, wpb, wo, wr1, wr2, br)


def _dispatch_kernel(dest_ref, x_ref, slots_in_ref, slots_ref, sem):
    del slots_in_ref
    tm = x_ref.shape[0]

    def row_copy(r, slot):
        return pltpu.make_async_copy(x_ref.at[pl.ds(r, 1)], slots_ref.at[pl.ds(slot, 1)], sem)

    def issue(r, carry):
        for k in range(TOP_K):
            row_copy(r, dest_ref[TOP_K * r + k]).start()
        return carry

    lax.fori_loop(0, tm, issue, 0, unroll=8)

    for _ in range(TOP_K):
        pltpu.make_async_copy(x_ref, slots_ref.at[pl.ds(0, tm)], sem).wait()


def _dispatch(x, dest, slots, *, tile_offset):
    rows, width = x.shape
    tm = ROW_TILE
    return pl.pallas_call(
        _dispatch_kernel,
        grid=(rows // tm,),
        in_specs=[pl.BlockSpec((tm * TOP_K,), lambda i: (i + tile_offset,), memory_space=pltpu.SMEM),
                  pl.BlockSpec((tm, width), lambda i: (i, 0)),
                  pl.BlockSpec(memory_space=pl.ANY)],
        out_specs=pl.BlockSpec(memory_space=pl.ANY),
        out_shape=jax.ShapeDtypeStruct(slots.shape, slots.dtype),
        scratch_shapes=[pltpu.SemaphoreType.DMA(())],
        input_output_aliases={2: 0},
        compiler_params=_cparams(("arbitrary",)),
        name="dispatch",
    )(dest, x, slots)


def _moe_kernel(be_ref, nb_ref, x_ref, w1_ref, w3_ref, w2_ref, o_ref, w1b_ref, w3b_ref, w2b_ref):
    i = pl.program_id(0)

    @pl.when((i == 0) | (be_ref[i] != be_ref[jnp.maximum(i - 1, 0)]))
    def _():
        w1b_ref[...] = w1_ref[...].astype(BF16)
        w3b_ref[...] = w3_ref[...].astype(BF16)
        w2b_ref[...] = w2_ref[...].astype(BF16)

    @pl.when(i < nb_ref[0])
    def _():
        subs = [slice(s, s + MOE_SUB) for s in range(0, x_ref.shape[0], MOE_SUB)]
        ups = []
        for rows in subs:
            x = _unpack_pairs(x_ref[rows, :]).astype(BF16)
            ups.append((_dot(x, w1b_ref[...]), _dot(x, w3b_ref[...])))
        for rows, (a, b) in zip(subs, ups):
            g = a * _sigmoid(a) * b
            o_ref[rows, :] = _pack_pairs(_dot(g.astype(BF16), w2b_ref[...]))

    @pl.when(i >= nb_ref[0])
    def _():
        o_ref[...] = jnp.zeros_like(o_ref)


def _moe(xs, block_e, n_used, w1, w3, w2):
    n_blocks = block_e.shape[0]
    bm = MOE_BLOCK
    wspec = lambda s: pl.BlockSpec((None,) + s, lambda i, be, nb: (be[i], 0, 0))
    gs = pltpu.PrefetchScalarGridSpec(
        num_scalar_prefetch=2,
        grid=(n_blocks,),
        in_specs=[pl.BlockSpec((bm, D_MODEL // 2), lambda i, be, nb: (i, 0)),
                  wspec((D_MODEL, D_EXPERT)), wspec((D_MODEL, D_EXPERT)), wspec((D_EXPERT, D_MODEL))],
        out_specs=pl.BlockSpec((bm, D_MODEL // 2), lambda i, be, nb: (i, 0)),
        scratch_shapes=[pltpu.VMEM((D_MODEL, D_EXPERT), BF16),
                        pltpu.VMEM((D_MODEL, D_EXPERT), BF16),
                        pltpu.VMEM((D_EXPERT, D_MODEL), BF16)],
    )
    return pl.pallas_call(
        _moe_kernel,
        grid_spec=gs,
        out_shape=jax.ShapeDtypeStruct((n_blocks * bm, D_MODEL // 2), jnp.uint32),
        compiler_params=_cparams(("arbitrary",)),
        name="moe",
    )(block_e, n_used, xs, w1, w3, w2)


def _final_kernel(x1_ref, ya_ref, yb_ref, rt_ref, g2_ref, nf_ref, o_ref):
    rt = rt_ref[...]
    moe = (rt[:, TOP_K:TOP_K + 1] * _unpack_pairs(ya_ref[...])
           + rt[:, TOP_K + 1:TOP_K + 2] * _unpack_pairs(yb_ref[...]))
    o_ref[...] = _rms(x1_ref[...] + g2_ref[...] * moe, nf_ref[...])


def _final(x1, ya, yb, rt, mod3, norm_f, *, rows_per_mod, row_offset):
    rows = x1.shape[0]
    tm = min(FINAL_TILE, rows)
    assert row_offset % tm == 0 and rows_per_mod % tm == 0
    tile_offset, tiles_per_mod = row_offset // tm, rows_per_mod // tm
    mr = mod3.shape[1]
    row = pl.BlockSpec((tm, D_MODEL), lambda i: (i, 0))
    tok = lambda w: pl.BlockSpec((tm, w), lambda i: (i + tile_offset, 0))
    return pl.pallas_call(
        _final_kernel,
        grid=(rows // tm,),
        in_specs=[row, tok(D_MODEL // 2), tok(D_MODEL // 2), tok(LANES),
                  pl.BlockSpec((None, mr, D_MODEL), lambda i: (i // tiles_per_mod, 0, 5)),
                  _resident((1, D_MODEL))],
        out_specs=row,
        out_shape=jax.ShapeDtypeStruct((rows, D_MODEL), F32),
        compiler_params=_cparams(("arbitrary",)),
        name="final",
    )(x1, ya, yb, rt, mod3, norm_f.reshape(1, D_MODEL))


def _routing_plan(eid, n_tokens):
    bm = MOE_BLOCK
    n_assign = n_tokens * TOP_K
    n_blocks = -(-n_assign // bm) + N_EXPERTS
    flat_e = eid.reshape(-1)
    hit = flat_e[:, None] == jnp.arange(N_EXPERTS, dtype=jnp.int32)[None, :]
    assert n_assign % ROW_TILE == 0
    runs = hit.astype(BF16).reshape(n_assign // ROW_TILE, ROW_TILE, N_EXPERTS)
    tri = (jnp.arange(ROW_TILE)[:, None] >= jnp.arange(ROW_TILE)[None, :]).astype(BF16)
    within = jnp.einsum('ts,rse->rte', tri, runs, preferred_element_type=F32)
    run_total = within[:, -1, :]
    run_start = jnp.cumsum(run_total, axis=0) - run_total
    csum = (within + run_start[:, None, :]).reshape(n_assign, N_EXPERTS)
    counts = (run_start[-1] + run_total[-1]).astype(jnp.int32)
    padded = (counts + bm - 1) // bm * bm
    pad_end = jnp.cumsum(padded)
    pad_start = pad_end - padded
    dest = jnp.sum(jnp.where(hit, csum - 1.0 + pad_start[None, :].astype(F32), 0.0),
                   axis=1).astype(jnp.int32)
    block_start = jnp.arange(n_blocks, dtype=jnp.int32) * bm
    block_e = jnp.minimum(jnp.sum((pad_end[None, :] <= block_start[:, None]).astype(jnp.int32), axis=1),
                          N_EXPERTS - 1).astype(jnp.int32)
    n_used = (pad_end[-1] // bm).astype(jnp.int32).reshape(1)
    return block_e, n_used, dest.reshape(n_tokens, TOP_K)


def _layer(xp, xs, c_all, cache_k, cache_v, cache_logf, state_conv, state_h, p, norm_f):
    Bp, Tp, _ = xp.shape
    Bs, Ts, _ = xs.shape
    past = cache_k.shape[1]
    np_rows, ns_rows = Bp * Tp, Bs * Ts
    assert Tp % ROW_TILE == 0 and ns_rows == ROW_TILE and Tp % ATTN_BLOCK == 0

    mod = _mod(c_all, p['w_mod'], p['b_mod'])
    mod_p = mod[:Bp].reshape(Bp, 1, 6 * D_MODEL)
    mod_s = jnp.repeat(mod[Bp:], Ts, axis=0).reshape(1, ns_rows, 6 * D_MODEL)

    w_in = p['w_in']
    d3 = 3 * D_MODEL
    wcat = jnp.concatenate([w_in[:, :D_MODEL] * (LOG2E * HEAD_DIM ** -0.5), w_in[:, D_MODEL:d3],
                            w_in[:, d3 + N_HEADS:], p['w_gate']], axis=1).astype(BF16)
    wfl = jnp.pad(w_in[:, d3:d3 + N_HEADS], ((0, 0), (0, LANES - N_HEADS))).astype(BF16)
    wflt = wfl.T
    eye = jnp.eye(GATE_TILE // (D_MODEL // N_RNN_BLOCKS), dtype=F32)

    def gate_tiles(w):
        per = GATE_TILE // w.shape[1]
        w4 = w.reshape(N_RNN_BLOCKS // per, per, w.shape[1], w.shape[2])
        return jnp.einsum('gpcd,pq->gpcqd', w4, eye).reshape(N_RNN_BLOCKS // per, GATE_TILE, GATE_TILE).astype(BF16)

    wr4, wi4 = gate_tiles(p['w_r']), gate_tiles(p['w_i'])
    wpa, wpb, wo = p['w_pa'].astype(BF16), p['w_pb'].astype(BF16), p['w_out'].astype(BF16)
    wr = jnp.pad(jnp.concatenate([p['w_rg'], p['w_re']], axis=1), ((0, 0), (0, LANES - N_GROUPS - N_EXPERTS)))
    wr1 = wr.astype(BF16)
    wr2 = (wr - wr1.astype(F32)).astype(BF16)
    br = jnp.pad(jnp.concatenate([p['b_rg'], p['b_re']]), (0, LANES - N_GROUPS - N_EXPERTS)).reshape(1, LANES)

    in_p = _in_proj(xp.reshape(np_rows, D_MODEL), mod_p, p['norm1'], wcat[:, :d3], wfl, wflt, p['b_gate'], p['b_f'],
                    tiles_per_mod=Tp // IN_TILE_QKV, tiles_per_seq=Tp // IN_TILE_QKV, with_mixer=False)
    in_s = _in_proj(xs.reshape(ns_rows, D_MODEL), mod_s, p['norm1'], wcat, wfl, wflt, p['b_gate'], p['b_f'],
                    tiles_per_mod=1, tiles_per_seq=1, with_mixer=True)
    q_p, kb_p, kt_p, vt_p, lf_p, _, fc_p = in_p
    q_s, kb_s, vb_s, k_s, v_s, xb_s, gb_s, ga_s, gg_s, lf_s, lft_s, _ = in_s
    r3 = lambda a, B, T: a.reshape(B, T, D_MODEL)

    oa_p = _attn_prompt(r3(q_p, Bp, Tp), r3(kb_p, Bp, Tp), vt_p, fc_p.reshape(Bp, Tp, N_HEADS))

    lfp3 = jnp.transpose(cache_logf.astype(F32), (0, 2, 1))
    lfn3 = lft_s.reshape(N_HEADS, Bs, Ts).transpose(1, 0, 2)
    oa_s = _attn_sample(r3(q_s, Bs, Ts), r3(kb_s, Bs, Ts), r3(vb_s, Bs, Ts),
                        jnp.transpose(cache_k, (0, 2, 3, 1)), jnp.transpose(cache_v, (0, 2, 3, 1)), lfp3, lfn3)

    zc = jnp.zeros((Bp, SUBLANES, D_MODEL), F32)
    zh = jnp.zeros((Bp, 1, D_MODEL), F32)
    rnn_w = (p['conv_w'], p['conv_b'], wr4, wi4, p['b_r'], p['b_i'], p['rglru_lambda'])
    post_w = (wpa, wpb, wo, wr1, wr2, br)
    x1_p, h2_p, rt_p, conv_p, hl_p = _mix(xp, oa_p, mod_p, zc, zh, p['norm1'], p['norm2'], wcat[:, d3:], p['b_gate'],
                                          *rnn_w, *post_w)
    x1_p, h2_p, rt_p = (a.reshape(np_rows, a.shape[-1]) for a in (x1_p, h2_p, rt_p))

    sc = jnp.pad(state_conv.astype(F32), ((0, 0), (SUBLANES - (CONV_WIDTH - 1), 0), (0, 0)))
    ob_s, conv_s, hl_s = _rnn(r3(xb_s, Bs, Ts), r3(gb_s, Bs, Ts), sc, state_h.astype(F32).reshape(Bs, 1, D_MODEL),
                              *rnn_w)
    x1_s, h2_s, rt_s = _post(xs.reshape(ns_rows, D_MODEL), oa_s.reshape(ns_rows, D_MODEL),
                             ob_s.reshape(ns_rows, D_MODEL), ga_s, gg_s, mod_s, p['norm2'], *post_w, tiles_per_mod=1)

    n_tok = np_rows + ns_rows
    rt = jnp.concatenate([rt_p, rt_s], axis=0)
    block_e, n_used, dest = _routing_plan(rt[:, :TOP_K].astype(jnp.int32), n_tok)
    slots = jnp.zeros((block_e.shape[0] * MOE_BLOCK, D_MODEL // 2), jnp.uint32)
    slots = _dispatch(h2_p, dest.reshape(-1), slots, tile_offset=0)
    slots = _dispatch(h2_s, dest.reshape(-1), slots, tile_offset=np_rows // ROW_TILE)
    take = lambda a, idx: a.at[idx].get(mode='promise_in_bounds')
    yb = _moe(slots, block_e, n_used, p['w1'], p['w3'], p['w2'])
    ya0 = take(yb, dest[:, 0])
    ya1 = take(yb, dest[:, 1])

    y_p = _final(x1_p, ya0, ya1, rt, mod_p, norm_f, rows_per_mod=Tp, row_offset=0)
    y_s = _final(x1_s, ya0, ya1, rt, mod_s, norm_f, rows_per_mod=ns_rows, row_offset=np_rows)

    st = CONV_WIDTH - 1
    state_p = (jnp.transpose(kt_p, (0, 3, 1, 2)), jnp.transpose(vt_p, (0, 3, 1, 2)),
               lf_p.reshape(Bp, Tp, N_HEADS), conv_p[:, SUBLANES - st:], hl_p.reshape(Bp, D_MODEL))
    state_s = (k_s.reshape(Bs, Ts, N_HEADS, HEAD_DIM), v_s.reshape(Bs, Ts, N_HEADS, HEAD_DIM),
               lf_s.reshape(Bs, Ts, N_HEADS), conv_s[:, SUBLANES - st:], hl_s.reshape(Bs, D_MODEL))
    return y_p.reshape(Bp, Tp, D_MODEL), y_s.reshape(Bs, Ts, D_MODEL), state_p, state_s


def kernel(x_prompt, x_sample, cache_k, cache_v, cache_logf, state_conv, state_h, c_prompt, c_sample, w_mod, b_mod, norm1, w_in, b_f, conv_w, conv_b, w_r, b_r, w_i, b_i, rglru_lambda, w_gate, b_gate, w_pa, w_pb, w_out, norm2, w_rg, b_rg, w_re, b_re, w1, w3, w2, norm_f):
    assert w_mod.shape[0] == 1, "single-layer trunk: the final norm is fused into the layer's last stage"
    names = ('w_mod', 'b_mod', 'norm1', 'w_in', 'b_f', 'conv_w', 'conv_b', 'w_r', 'b_r', 'w_i', 'b_i',
             'rglru_lambda', 'w_gate', 'b_gate', 'w_pa', 'w_pb', 'w_out', 'norm2', 'w_rg', 'b_rg', 'w_re', 'b_re',
             'w1', 'w3', 'w2')
    vals = (w_mod, b_mod, norm1, w_in, b_f, conv_w, conv_b, w_r, b_r, w_i, b_i, rglru_lambda, w_gate, b_gate,
            w_pa, w_pb, w_out, norm2, w_rg, b_rg, w_re, b_re, w1, w3, w2)
    p = {n: v[0] for n, v in zip(names, vals)}
    c_all = jnp.concatenate([c_prompt, c_sample], axis=0)
    y_p, y_s, sp, ss = _layer(x_prompt, x_sample, c_all, cache_k[0], cache_v[0], cache_logf[0],
                              state_conv[0], state_h[0], p, norm_f)
    lead = lambda a: a[None]
    return (y_p, y_s, lead(sp[0]), lead(sp[1]), lead(sp[2]), lead(sp[3]), lead(sp[4]),
            lead(ss[0]), lead(ss[1]), lead(ss[2]), lead(ss[3]), lead(ss[4]))
```

```python
import functools

import jax
import jax.numpy as jnp
from jax import lax
from jax.experimental import pallas as pl
from jax.experimental.pallas import tpu as pltpu

F32 = jnp.float32
BF16 = jnp.bfloat16

D_MODEL = 1024
N_HEADS = 16
HEAD_DIM = 64
HEAD_PAIR = 2 * HEAD_DIM
N_PAIRS = N_HEADS // 2
N_RNN_BLOCKS = 16
CONV_WIDTH = 4
RGLRU_C = 8.0
N_GROUPS = 4
EXPERTS_PER_GROUP = 8
N_EXPERTS = N_GROUPS * EXPERTS_PER_GROUP
TOP_K = 2
D_EXPERT = 512
EPS = 1e-6

LANES = 128
SUBLANES = 8
ROW_TILE = 512
FINAL_TILE = 1024
IN_TILE = 256
IN_TILE_QKV = 512
ATTN_BLOCK = 512
ATTN_QBLOCK = 2048
ATTN_VROWS = HEAD_DIM + 16
RNN_TILE = 256
MIX_TILE = 512
DISPATCH_TILE = 2048
MOE_BLOCK = 512
MOE_SUB = 256
GATE_TILE = 256
VMEM_LIMIT = 56 * 1024 * 1024
NEG_BIG = -1e30
LOG2E = 1.4426950408889634


def _cparams(sem):
    return pltpu.CompilerParams(dimension_semantics=sem, vmem_limit_bytes=VMEM_LIMIT)


def _resident(shape):
    nd = len(shape)
    return pl.BlockSpec(shape, lambda *_: (0,) * nd, pipeline_mode=pl.Buffered(1))


def _split3(x):
    p1 = x.astype(BF16)
    r1 = x - p1.astype(F32)
    p2 = r1.astype(BF16)
    p3 = (r1 - p2.astype(F32)).astype(BF16)
    return p1, p2, p3


def _dot(a, b):
    return jnp.dot(a, b, preferred_element_type=F32)


def _dot_nt(a, b):
    return lax.dot_general(a, b, (((1,), (1,)), ((), ())), preferred_element_type=F32)


def _sigmoid(x):
    return 0.5 * jnp.tanh(0.5 * x) + 0.5


def _gelu_tanh(x):
    return 0.5 * x * (1.0 + jnp.tanh(0.7978845608028654 * (x + 0.044715 * x * x * x)))


def _rms(x, g):
    ms = jnp.mean(x * x, axis=-1, keepdims=True)
    return x * lax.rsqrt(ms + EPS) * g


def _pack_pairs(x):
    n = x.shape[1] // 2
    hi = pltpu.bitcast(x[:, :n].astype(BF16).astype(F32), jnp.uint32)
    lo = pltpu.bitcast(x[:, n:].astype(BF16).astype(F32), jnp.uint32)
    return hi | lax.shift_right_logical(lo, jnp.uint32(16))


def _unpack_pairs(w):
    hi = pltpu.bitcast(w & jnp.uint32(0xFFFF0000), F32)
    lo = pltpu.bitcast(lax.shift_left(w, jnp.uint32(16)), F32)
    return jnp.concatenate([hi, lo], axis=1)


def _div_p2(x, n):
    assert n & (n - 1) == 0
    return lax.shift_right_logical(x, jnp.int32(n.bit_length() - 1))


def _mod_p2(x, n):
    assert n & (n - 1) == 0
    return x & (n - 1)


def _upper_tri(n):
    r = lax.broadcasted_iota(jnp.int32, (n, n), 0)
    c = lax.broadcasted_iota(jnp.int32, (n, n), 1)
    return jnp.where(r <= c, 1.0, 0.0).astype(BF16)


def _cumsum_lanes(x, u):
    p1, p2, p3 = _split3(x)
    return _dot(p1, u) + _dot(p2, u) + _dot(p3, u)


def _mod_kernel(c_ref, w_ref, b_ref, o_ref):
    c = c_ref[...]
    s = c * _sigmoid(c)
    s1 = s.astype(BF16)
    s2 = (s - s1.astype(F32)).astype(BF16)
    w = w_ref[...]
    w1 = w.astype(BF16)
    w2 = (w - w1.astype(F32)).astype(BF16)
    o_ref[...] = _dot(s1, w1) + _dot(s1, w2) + _dot(s2, w1) + b_ref[...]


def _mod(c, w, b):
    n = c.shape[0]
    cols = w.shape[1]
    return pl.pallas_call(
        _mod_kernel,
        grid=(cols // D_MODEL,),
        in_specs=[pl.BlockSpec((n, D_MODEL), lambda j: (0, 0)),
                  pl.BlockSpec((D_MODEL, D_MODEL), lambda j: (0, j)),
                  pl.BlockSpec((1, D_MODEL), lambda j: (0, j))],
        out_specs=pl.BlockSpec((n, D_MODEL), lambda j: (0, j)),
        out_shape=jax.ShapeDtypeStruct((n, cols), F32),
        compiler_params=_cparams(("arbitrary",)),
        name="mod",
    )(c, w, b.reshape(1, cols))


def _in_kernel(x_ref, sh_ref, sc_ref, n1_ref, w_ref, wfl_ref, wflt_ref, bg_ref, bf_ref, bft_ref, *refs,
               tiles_per_seq, with_mixer):
    lf_ref, lft_ref, fc_ref, carry_ref = refs[-4:]
    x = x_ref[...]
    h = (_rms(x, n1_ref[...]) * (1.0 + sc_ref[...]) + sh_ref[...]).astype(BF16)

    def mm(c):
        return _dot(h, w_ref[:, c * D_MODEL:(c + 1) * D_MODEL])

    def head_major(y):
        slabs = []
        for p in range(N_PAIRS):
            pair = y[:, p * HEAD_PAIR:(p + 1) * HEAD_PAIR]
            slabs += [pair, pltpu.roll(pair, HEAD_DIM, 1)]
        return jnp.swapaxes(jnp.stack(slabs, axis=0), 0, 1)[:, :, :HEAD_DIM]

    def time_minor(y):
        return y.T.reshape(N_HEADS, HEAD_DIM, y.shape[0])

    refs[0][...] = mm(0).astype(BF16)
    k = mm(1)
    refs[1][...] = k.astype(BF16)
    v = mm(2)
    if with_mixer:
        vb_ref, k5_ref, v5_ref, xb_ref, gb_ref, ga_ref, gg_ref = refs[2:9]
        vb_ref[...] = v.astype(BF16)
        k5_ref[...] = head_major(k)
        v5_ref[...] = head_major(v)
        xb_ref[...] = mm(3).astype(BF16)
        gb_ref[...] = mm(4).astype(BF16)
        ga_ref[...] = _sigmoid(mm(5) + bg_ref[:, :D_MODEL]).astype(BF16)
        gg_ref[...] = _sigmoid(mm(6) + bg_ref[:, D_MODEL:]).astype(BF16)
    else:
        refs[2][...] = time_minor(k)
        refs[3][...] = time_minor(v)

    def log_sigmoid(z):
        return jnp.minimum(z, 0.0) - jnp.log(1.0 + jnp.exp(-jnp.abs(z)))

    lf = log_sigmoid(_dot(h, wfl_ref[...]) + bf_ref[...])
    lf_ref[...] = lf[:, :N_HEADS]
    lft_ref[...] = log_sigmoid(_dot_nt(wflt_ref[...], h)[:N_HEADS, :] + bft_ref[...])

    @pl.when(pl.program_id(0) % tiles_per_seq == 0)
    def _():
        carry_ref[...] = jnp.zeros_like(carry_ref)

    tm = x.shape[0]
    r = lax.broadcasted_iota(jnp.int32, (tm, tm), 0)
    c = lax.broadcasted_iota(jnp.int32, (tm, tm), 1)
    lower = jnp.where(c <= r, 1.0, 0.0).astype(BF16)
    p1, p2, p3 = _split3(lf)
    fc = _dot(lower, p1) + _dot(lower, p2) + _dot(lower, p3) + carry_ref[...]
    fc_ref[...] = fc[:, :N_HEADS]
    carry_ref[...] = fc[tm - 1:tm, :]


def _in_proj(x2, mod3, norm1, wcat, wfl, wflt, b_gate, b_f, *, tiles_per_mod, tiles_per_seq, with_mixer):
    rows = x2.shape[0]
    tm = IN_TILE if with_mixer else IN_TILE_QKV
    nt = rows // tm
    row_blk = pl.BlockSpec((tm, D_MODEL), lambda i: (i, 0))
    if mod3.shape[1] == 1:
        modc = lambda c: pl.BlockSpec((None, 1, D_MODEL), lambda i: (i // tiles_per_mod, 0, c))
    else:
        modc = lambda c: pl.BlockSpec((None, tm, D_MODEL), lambda i: (0, i, c))
    row_out = jax.ShapeDtypeStruct((rows, D_MODEL), BF16)
    if with_mixer:
        kv_shape = jax.ShapeDtypeStruct((rows, N_HEADS, HEAD_DIM), F32)
        kv_blk = pl.BlockSpec((tm, N_HEADS, HEAD_DIM), lambda i: (i, 0, 0))
        lead = ([row_out] * 3 + [kv_shape] * 2 + [row_out] * 4, [row_blk] * 3 + [kv_blk] * 2 + [row_blk] * 4)
    else:
        n_seq = nt // tiles_per_seq
        kv_shape = jax.ShapeDtypeStruct((n_seq, N_HEADS, HEAD_DIM, tiles_per_seq * tm), F32)
        kv_blk = pl.BlockSpec((None, N_HEADS, HEAD_DIM, tm), lambda i: (i // tiles_per_seq, 0, 0, i % tiles_per_seq))
        lead = ([row_out] * 2 + [kv_shape] * 2, [row_blk] * 2 + [kv_blk] * 2)
    outs = [*lead[0],
            jax.ShapeDtypeStruct((rows, N_HEADS), F32),
            jax.ShapeDtypeStruct((N_HEADS, rows), F32),
            jax.ShapeDtypeStruct((rows, N_HEADS), F32)]
    return pl.pallas_call(
        functools.partial(_in_kernel, tiles_per_seq=tiles_per_seq, with_mixer=with_mixer),
        grid=(nt,),
        in_specs=[row_blk, modc(0), modc(1),
                  _resident((1, D_MODEL)),
                  _resident(wcat.shape),
                  _resident(wfl.shape),
                  _resident(wflt.shape),
                  _resident((1, 2 * D_MODEL)),
                  _resident((1, LANES)),
                  _resident((N_HEADS, 1))],
        out_specs=lead[1]
                  + [pl.BlockSpec((tm, N_HEADS), lambda i: (i, 0)),
                     pl.BlockSpec((N_HEADS, tm), lambda i: (0, i)),
                     pl.BlockSpec((tm, N_HEADS), lambda i: (i, 0))],
        out_shape=outs,
        scratch_shapes=[pltpu.VMEM((1, LANES), F32)],
        compiler_params=_cparams(("arbitrary",)),
        name="in_proj",
    )(x2, mod3, mod3, norm1.reshape(1, D_MODEL), wcat, wfl, wflt,
      b_gate.reshape(1, 2 * D_MODEL), jnp.pad(b_f, (0, LANES - N_HEADS)).reshape(1, LANES),
      b_f.reshape(N_HEADS, 1))


def _attn_kernel(q_ref, k_ref, v_ref, f_ref, o_ref, kb_ref, vt_ref, qt_ref, s0_ref, m_ref, acc_ref):
    qi = pl.program_id(2)
    bq = q_ref.shape[0]
    bk = ATTN_BLOCK
    nk = k_ref.shape[0] // bk
    own_lanes = (lambda i: i < HEAD_DIM, lambda i: i >= HEAD_DIM)
    bias_at = (HEAD_DIM, 0)

    @pl.when(qi == 0)
    def _():
        lane = lax.broadcasted_iota(jnp.int32, (bk, HEAD_PAIR), 1)
        ones_rows = jnp.where(lax.broadcasted_iota(jnp.int32, (ATTN_VROWS - HEAD_DIM, bk), 0) == 0, 1.0, 0.0)
        for c in range(nk):
            blk = slice(c * bk, (c + 1) * bk)
            k = k_ref[blk, :].astype(F32)
            f = f_ref[blk, :] * LOG2E
            head = lax.broadcasted_iota(jnp.int32, f.shape, 1)
            for hh in range(2):
                mine = head == 2 * pl.program_id(1) + hh
                f_head = jnp.sum(jnp.where(mine, f, 0.0), axis=1, keepdims=True)
                slab = jnp.zeros(k.shape, F32)
                for n, piece in enumerate(_split3(f_head)):
                    slab = jnp.where(lane == bias_at[hh] + n, piece.astype(F32), slab)
                kb_ref[hh, blk, :] = jnp.where(own_lanes[hh](lane), k, slab).astype(BF16)
            for hh in range(2):
                vt_ref[c, hh, 0:HEAD_DIM] = v_ref[hh, :, blk].astype(BF16)
                vt_ref[c, hh, HEAD_DIM:ATTN_VROWS] = ones_rows.astype(BF16)

    qt = q_ref[...].astype(F32).T
    feat = lax.broadcasted_iota(jnp.int32, qt.shape, 0)
    for hh in range(2):
        is_bias = (feat >= bias_at[hh]) & (feat < bias_at[hh] + 3)
        qt_ref[hh] = jnp.where(own_lanes[hh](feat), qt, jnp.where(is_bias, -1.0, 0.0)).astype(BF16)

    m_ref[...] = jnp.full(m_ref.shape, NEG_BIG, F32)
    acc_ref[...] = jnp.zeros(acc_ref.shape, F32)

    nsub = bq // bk

    def scores(hh, u, j):
        keys = kb_ref[hh, pl.ds(pl.multiple_of(j * bk, bk), bk), :]
        return _dot(keys, qt_ref[hh, :, u * bk:(u + 1) * bk])

    def softmax_pv(hh, u, j, s, masked):
        cols = slice(u * bk, (u + 1) * bk)
        if masked:
            key = lax.broadcasted_iota(jnp.int32, s.shape, 0)
            qry = lax.broadcasted_iota(jnp.int32, s.shape, 1)
            s = jnp.where(key <= qry, s, NEG_BIG)
        m_prev = m_ref[hh, :, cols]
        m_new = jnp.maximum(m_prev, jnp.max(s, axis=0, keepdims=True))
        p = jnp.exp2(s - m_new).astype(BF16)
        acc_ref[hh, :, cols] = jnp.exp2(m_prev - m_new) * acc_ref[hh, :, cols] + _dot(vt_ref[j, hh], p)
        m_ref[hh, :, cols] = m_new

    def run(items, first_of_next):
        s = s0_ref[...]
        for n, (hh, u, j, masked) in enumerate(items):
            if n + 1 < len(items):
                nxt = scores(*items[n + 1][:3])
            elif first_of_next is not None:
                s0_ref[...] = scores(*first_of_next)
            softmax_pv(hh, u, j, s, masked)
            s = nxt if n + 1 < len(items) else None

    def body(j, carry):
        run([(hh, u, j, False) for u in range(nsub) for hh in range(2)], (0, 0, j + 1))
        return carry

    s0_ref[...] = scores(0, 0, 0)
    lax.fori_loop(0, nsub * qi, body, 0)
    for d in range(nsub):
        j = nsub * qi + d
        items = [(hh, u, j, u == d) for u in range(d, nsub) for hh in range(2)]
        run(items, (0, d + 1, j + 1) if d + 1 < nsub else None)

    heads = [acc_ref[hh, 0:HEAD_DIM] / acc_ref[hh, HEAD_DIM:HEAD_DIM + 1] for hh in range(2)]
    o_ref[...] = jnp.concatenate(heads, axis=0).T.astype(BF16)


def _attn_prompt(q3, k3, vt4, fc3):
    B, T, _ = q3.shape
    bq, bk = ATTN_QBLOCK, ATTN_BLOCK
    nq, nk = T // bq, T // bk
    return pl.pallas_call(
        _attn_kernel,
        grid=(B, N_PAIRS, nq),
        in_specs=[pl.BlockSpec((None, bq, HEAD_PAIR), lambda b, p, i: (b, i, p)),
                  pl.BlockSpec((None, T, HEAD_PAIR), lambda b, p, i: (b, 0, p)),
                  pl.BlockSpec((None, 2, HEAD_DIM, T), lambda b, p, i: (b, p, 0, 0)),
                  pl.BlockSpec((None, T, N_HEADS), lambda b, p, i: (b, 0, 0))],
        out_specs=pl.BlockSpec((None, bq, HEAD_PAIR), lambda b, p, i: (b, i, p)),
        out_shape=jax.ShapeDtypeStruct((B, T, D_MODEL), BF16),
        scratch_shapes=[pltpu.VMEM((2, T, HEAD_PAIR), BF16),
                        pltpu.VMEM((nk, 2, ATTN_VROWS, bk), BF16),
                        pltpu.VMEM((2, HEAD_PAIR, bq), BF16),
                        pltpu.VMEM((bk, bk), F32),
                        pltpu.VMEM((2, 1, bq), F32),
                        pltpu.VMEM((2, ATTN_VROWS, bq), F32)],
        compiler_params=_cparams(("arbitrary", "arbitrary", "arbitrary")),
        name="attn_prompt",
    )(q3, k3, vt4, fc3)


def _attn_sample_kernel(q_ref, kn_ref, vn_ref, kc_ref, vc_ref, lfp_ref, lfn_ref, o_ref):
    tq = q_ref.shape[0]
    past = kc_ref.shape[-1]
    nrow = N_HEADS * tq
    row = lax.broadcasted_iota(jnp.int32, (nrow, D_MODEL), 0)
    col = lax.broadcasted_iota(jnp.int32, (nrow, D_MODEL), 1)
    own = _div_p2(row, tq) == _div_p2(col, HEAD_DIM)

    rr = lax.broadcasted_iota(jnp.int32, (nrow, tq), 0)
    rc = lax.broadcasted_iota(jnp.int32, (nrow, tq), 1)
    rep = jnp.where(_mod_p2(rr, tq) == rc, 1.0, 0.0).astype(BF16)
    qx = jnp.where(own, _dot(rep, q_ref[...]), 0.0).astype(BF16)

    fp = _cumsum_lanes(lfp_ref[...] * LOG2E, _upper_tri(past))
    fn = _cumsum_lanes(lfn_ref[...] * LOG2E, _upper_tri(tq)) + fp[:, past - 1:past]
    hr = lax.broadcasted_iota(jnp.int32, (nrow, N_HEADS), 0)
    hc = lax.broadcasted_iota(jnp.int32, (nrow, N_HEADS), 1)
    hsel = jnp.where(_div_p2(hr, tq) == hc, 1.0, 0.0).astype(BF16)

    def expand(f):
        p1, p2, p3 = _split3(f)
        return _dot(hsel, p1) + _dot(hsel, p2) + _dot(hsel, p3)

    s_p = _dot(qx, kc_ref[...].reshape(D_MODEL, past).astype(BF16)) - expand(fp)
    s_n = _dot_nt(qx, kn_ref[...]) - expand(fn)
    nr = lax.broadcasted_iota(jnp.int32, (nrow, tq), 0)
    nc = lax.broadcasted_iota(jnp.int32, (nrow, tq), 1)
    s_n = jnp.where(nc <= _mod_p2(nr, tq), s_n, NEG_BIG)
    m = jnp.maximum(jnp.max(s_p, axis=-1, keepdims=True), jnp.max(s_n, axis=-1, keepdims=True))
    p_p = jnp.exp2(s_p - m)
    p_n = jnp.exp2(s_n - m)
    l = jnp.sum(p_p, axis=-1, keepdims=True) + jnp.sum(p_n, axis=-1, keepdims=True)
    o_full = (_dot_nt(p_p.astype(BF16), vc_ref[...].reshape(D_MODEL, past).astype(BF16))
              + _dot(p_n.astype(BF16), vn_ref[...]))
    o_own = jnp.where(own, o_full / l, 0.0).astype(BF16)
    cr = lax.broadcasted_iota(jnp.int32, (tq, nrow), 0)
    cc = lax.broadcasted_iota(jnp.int32, (tq, nrow), 1)
    col_sel = jnp.where(_mod_p2(cc, tq) == cr, 1.0, 0.0).astype(BF16)
    o_ref[...] = _dot(col_sel, o_own).astype(BF16)


def _attn_sample(q3, kn3, vn3, kct, vct, lfp3, lfn3):
    B, tq, _ = q3.shape
    past = kct.shape[-1]
    blk = lambda n, w: pl.BlockSpec((None, n, w), lambda b: (b, 0, 0))
    cache = pl.BlockSpec((None, N_HEADS, HEAD_DIM, past), lambda b: (b, 0, 0, 0))
    return pl.pallas_call(
        _attn_sample_kernel,
        grid=(B,),
        in_specs=[blk(tq, D_MODEL), blk(tq, D_MODEL), blk(tq, D_MODEL),
                  cache, cache, blk(N_HEADS, past), blk(N_HEADS, tq)],
        out_specs=blk(tq, D_MODEL),
        out_shape=jax.ShapeDtypeStruct((B, tq, D_MODEL), BF16),
        compiler_params=_cparams(("arbitrary",)),
        name="attn_sample",
    )(q3, kn3, vn3, kct, vct, lfp3, lfn3)


def _rglru_gates(xb, xp_ref, cw_ref, cb_ref, wr_ref, wi_ref):
    tt = xb.shape[0]
    pad = SUBLANES
    xp_ref[pad:pad + tt, :] = xb
    xc = cb_ref[...] + xb * cw_ref[CONV_WIDTH - 1:CONV_WIDTH, :]
    for k in range(CONV_WIDTH - 1):
        back = CONV_WIDTH - 1 - k
        xc = xc + xp_ref[pad - back:pad - back + tt, :] * cw_ref[k:k + 1, :]
    tail = xp_ref[tt:tt + pad, :]
    xp_ref[0:pad, :] = tail

    xcb = xc.astype(BF16)
    ng = D_MODEL // GATE_TILE
    r = jnp.concatenate([_dot(xcb[:, g * GATE_TILE:(g + 1) * GATE_TILE], wr_ref[g]) for g in range(ng)], axis=-1)
    i = jnp.concatenate([_dot(xcb[:, g * GATE_TILE:(g + 1) * GATE_TILE], wi_ref[g]) for g in range(ng)], axis=-1)
    return xc, r, i, tail


def _rglru_scan(xc, r, i, gb, hc_ref, br_ref, bi_ref, lam_ref):
    tt = xc.shape[0]
    r = _sigmoid(r + br_ref[...])
    i = _sigmoid(i + bi_ref[...])
    lam = lam_ref[...]
    softplus_neg = jnp.maximum(-lam, 0.0) + jnp.log(1.0 + jnp.exp(-jnp.abs(lam)))
    log_a = (-RGLRU_C) * r * softplus_neg
    a = jnp.exp(log_a)
    th = jnp.tanh(log_a)
    mult = jnp.sqrt(-2.0 * th / (1.0 - th))
    b = mult * (i * xc)

    ng8 = tt // SUBLANES
    a3 = a.reshape(ng8, SUBLANES, D_MODEL)
    b3 = b.reshape(ng8, SUBLANES, D_MODEL)
    sub = lax.broadcasted_iota(jnp.int32, a3.shape, 1)
    for s in (1, 2, 4):
        keep = sub >= s
        a_sh = jnp.where(keep, pltpu.roll(a3, s, 1), 1.0)
        b_sh = jnp.where(keep, pltpu.roll(b3, s, 1), 0.0)
        b3 = a3 * b_sh + b3
        a3 = a3 * a_sh
    hc = hc_ref[...]
    hs = []
    for g in range(ng8):
        hg = b3[g] + a3[g] * hc
        hs.append(hg)
        hc = hg[SUBLANES - 1:SUBLANES, :]
    hc_ref[...] = hc
    return jnp.concatenate(hs, axis=0) * _gelu_tanh(gb), hc


def _rnn_kernel(xb_ref, gb_ref, c0_ref, h0_ref, cw_ref, cb_ref, wr_ref, wi_ref, br_ref, bi_ref, lam_ref,
                o_ref, cout_ref, hout_ref, xp_ref, hc_ref):
    ti = pl.program_id(1)

    @pl.when(ti == 0)
    def _():
        xp_ref[0:SUBLANES, :] = c0_ref[...]
        hc_ref[...] = h0_ref[...]

    xc, r, i, tail = _rglru_gates(xb_ref[...].astype(F32), xp_ref, cw_ref, cb_ref, wr_ref, wi_ref)
    ob, hc = _rglru_scan(xc, r, i, gb_ref[...].astype(F32), hc_ref, br_ref, bi_ref, lam_ref)
    o_ref[...] = ob.astype(BF16)

    @pl.when(ti == pl.num_programs(1) - 1)
    def _():
        cout_ref[...] = tail
        hout_ref[...] = hc


def _rnn(xb3, gb3, conv0, h0, conv_w, conv_b, wr4, wi4, b_r, b_i, lam):
    B, T, _ = xb3.shape
    tt = min(RNN_TILE, T)
    vec = lambda a: a.reshape(1, D_MODEL)
    seq = pl.BlockSpec((None, tt, D_MODEL), lambda b, t: (b, t, 0))
    st8 = pl.BlockSpec((None, SUBLANES, D_MODEL), lambda b, t: (b, 0, 0))
    st1 = pl.BlockSpec((None, 1, D_MODEL), lambda b, t: (b, 0, 0))
    return pl.pallas_call(
        _rnn_kernel,
        grid=(B, T // tt),
        in_specs=[seq, seq, st8, st1,
                  _resident((CONV_WIDTH, D_MODEL)), _resident((1, D_MODEL)),
                  _resident(wr4.shape), _resident(wi4.shape),
                  _resident((1, D_MODEL)), _resident((1, D_MODEL)), _resident((1, D_MODEL))],
        out_specs=[seq, st8, st1],
        out_shape=[jax.ShapeDtypeStruct((B, T, D_MODEL), BF16),
                   jax.ShapeDtypeStruct((B, SUBLANES, D_MODEL), F32),
                   jax.ShapeDtypeStruct((B, 1, D_MODEL), F32)],
        scratch_shapes=[pltpu.VMEM((tt + SUBLANES, D_MODEL), F32),
                        pltpu.VMEM((1, D_MODEL), F32)],
        compiler_params=_cparams(("arbitrary", "arbitrary")),
        name="rnn",
    )(xb3, gb3, conv0, h0, conv_w, vec(conv_b), wr4, wi4, vec(b_r), vec(b_i), vec(lam))


def _merge_route_tile(x, pa, ob, ga, gg, g1, sh2, sc2, n2_ref, wpb_ref, wo_ref, wr1_ref, wr2_ref, br_ref):
    mix = ga * pa + gg * _dot(ob, wpb_ref[...])
    x1 = x + g1 * _dot(mix.astype(BF16), wo_ref[...])
    h2 = _rms(x1, n2_ref[...]) * (1.0 + sc2) + sh2
    h2a = h2.astype(BF16)
    h2b = (h2 - h2a.astype(F32)).astype(BF16)
    lg = _dot(h2a, wr1_ref[...]) + _dot(h2a, wr2_ref[...]) + _dot(h2b, wr1_ref[...]) + br_ref[...]

    lane = lax.broadcasted_iota(jnp.int32, lg.shape, 1).astype(F32)
    big = float(LANES)
    is_g = lane < N_GROUPS
    gl = jnp.where(is_g, lg, NEG_BIG)
    mg = jnp.max(gl, axis=-1, keepdims=True)
    sg = jnp.sum(jnp.where(is_g, jnp.exp(gl - mg), 0.0), axis=-1, keepdims=True)
    pg = 1.0 / sg
    gidx = jnp.min(jnp.where(is_g & (gl == mg), lane, big), axis=-1, keepdims=True)
    lo = N_GROUPS + EXPERTS_PER_GROUP * gidx
    ing = (lane >= lo) & (lane < lo + EXPERTS_PER_GROUP)
    el = jnp.where(ing, lg, NEG_BIG)
    m1 = jnp.max(el, axis=-1, keepdims=True)
    se = jnp.sum(jnp.where(ing, jnp.exp(el - m1), 0.0), axis=-1, keepdims=True)
    i1 = jnp.min(jnp.where(ing & (el == m1), lane, big), axis=-1, keepdims=True)
    el2 = jnp.where(lane == i1, NEG_BIG, el)
    m2 = jnp.max(el2, axis=-1, keepdims=True)
    i2 = jnp.min(jnp.where(ing & (el2 == m2) & (lane != i1), lane, big), axis=-1, keepdims=True)
    p1 = 1.0 / se
    p2 = jnp.exp(m2 - m1) / se
    den = p1 + p2
    w1 = pg * p1 / den
    w2 = pg * p2 / den
    rt = jnp.where(lane == 0.0, i1 - N_GROUPS,
                   jnp.where(lane == 1.0, i2 - N_GROUPS,
                             jnp.where(lane == 2.0, w1, jnp.where(lane == 3.0, w2, 0.0))))
    return x1, h2, rt


def _post_kernel(x_ref, oa_ref, ob_ref, ga_ref, gg_ref, g1_ref, sh_ref, sc_ref, n2_ref,
                 wpa_ref, wpb_ref, wo_ref, wr1_ref, wr2_ref, br_ref,
                 x1_ref, h2_ref, rt_ref):
    x1, h2, rt = _merge_route_tile(x_ref[...], _dot(oa_ref[...], wpa_ref[...]), ob_ref[...],
                                   ga_ref[...].astype(F32), gg_ref[...].astype(F32), g1_ref[...], sh_ref[...],
                                   sc_ref[...], n2_ref, wpb_ref, wo_ref, wr1_ref, wr2_ref, br_ref)
    x1_ref[...] = x1
    h2_ref[...] = _pack_pairs(h2)
    rt_ref[...] = rt


def _post(x2, oa2, ob2, ga2, gg2, mod3, norm2, wpa, wpb, wo, wr1, wr2, br, *, tiles_per_mod):
    rows = x2.shape[0]
    tm = ROW_TILE
    mr = mod3.shape[1]
    row = pl.BlockSpec((tm, D_MODEL), lambda i: (i, 0))
    modc = lambda c: pl.BlockSpec((None, mr, D_MODEL), lambda i: (i // tiles_per_mod, 0, c))
    return pl.pallas_call(
        _post_kernel,
        grid=(rows // tm,),
        in_specs=[row, row, row, row, row, modc(2), modc(3), modc(4), _resident((1, D_MODEL)),
                  _resident(wpa.shape), _resident(wpb.shape), _resident(wo.shape),
                  _resident(wr1.shape), _resident(wr2.shape), _resident((1, LANES))],
        out_specs=[row, pl.BlockSpec((tm, D_MODEL // 2), lambda i: (i, 0)), pl.BlockSpec((tm, LANES), lambda i: (i, 0))],
        out_shape=[jax.ShapeDtypeStruct((rows, D_MODEL), F32),
                   jax.ShapeDtypeStruct((rows, D_MODEL // 2), jnp.uint32),
                   jax.ShapeDtypeStruct((rows, LANES), F32)],
        compiler_params=_cparams(("arbitrary",)),
        name="post",
    )(x2, oa2, ob2, ga2, gg2, mod3, mod3, mod3, norm2.reshape(1, D_MODEL), wpa, wpb, wo, wr1, wr2, br)


def _mix_kernel(x_ref, oa_ref, sh1_ref, sc1_ref, g1_ref, sh2_ref, sc2_ref, n1_ref, n2_ref, w_ref, bg_ref,
                c0_ref, h0_ref, cw_ref, cb_ref, wr_ref, wi_ref, br_ref, bi_ref, lam_ref,
                wpa_ref, wpb_ref, wo_ref, wr1_ref, wr2_ref, brt_ref,
                x1_ref, h2_ref, rt_ref, cout_ref, hout_ref, xp_ref, hc_ref):
    ti = pl.program_id(1)

    @pl.when(ti == 0)
    def _():
        xp_ref[0:SUBLANES, :] = c0_ref[...]
        hc_ref[...] = h0_ref[...]

    def project(rows):
        x = x_ref[rows, :]
        h = (_rms(x, n1_ref[...]) * (1.0 + sc1_ref[...]) + sh1_ref[...]).astype(BF16)
        mm = lambda c: _dot(h, w_ref[:, c * D_MODEL:(c + 1) * D_MODEL])
        xc, r, i, tail = _rglru_gates(mm(0), xp_ref, cw_ref, cb_ref, wr_ref, wi_ref)
        return dict(x=x, xc=xc, r=r, i=i, tail=tail, gb=mm(1), ga=mm(2), gg=mm(3),
                    pa=_dot(oa_ref[rows, :], wpa_ref[...]))

    def recur(st):
        ob, hc = _rglru_scan(st['xc'], st['r'], st['i'], st['gb'], hc_ref, br_ref, bi_ref, lam_ref)
        return ob.astype(BF16), hc

    def merge(rows, st, ob):
        ga = _sigmoid(st['ga'] + bg_ref[:, :D_MODEL])
        gg = _sigmoid(st['gg'] + bg_ref[:, D_MODEL:])
        x1, h2, rt = _merge_route_tile(st['x'], st['pa'], ob, ga, gg, g1_ref[...], sh2_ref[...], sc2_ref[...],
                                       n2_ref, wpb_ref, wo_ref, wr1_ref, wr2_ref, brt_ref)
        x1_ref[rows, :] = x1
        h2_ref[rows, :] = _pack_pairs(h2)
        rt_ref[rows, :] = rt

    subs = [slice(s, s + RNN_TILE) for s in range(0, x_ref.shape[0], RNN_TILE)]
    states = [project(subs[0])]
    for n, rows in enumerate(subs):
        if n + 1 < len(subs):
            states.append(project(subs[n + 1]))
        ob, hc = recur(states[n])
        merge(rows, states[n], ob)

    @pl.when(ti == pl.num_programs(1) - 1)
    def _():
        cout_ref[...] = states[-1]['tail']
        hout_ref[...] = hc


def _mix(x3, oa3, mod3, conv0, h0, norm1, norm2, w_mix, b_gate, conv_w, conv_b, wr4, wi4, b_r, b_i, lam,
         wpa, wpb, wo, wr1, wr2, br):
    B, T, _ = x3.shape
    tt = MIX_TILE
    vec = lambda a: a.reshape(1, D_MODEL)
    seq = lambda w: pl.BlockSpec((None, tt, w), lambda b, t: (b, t, 0))
    modc = lambda c: pl.BlockSpec((None, 1, D_MODEL), lambda b, t: (b, 0, c))
    st8 = pl.BlockSpec((None, SUBLANES, D_MODEL), lambda b, t: (b, 0, 0))
    st1 = pl.BlockSpec((None, 1, D_MODEL), lambda b, t: (b, 0, 0))
    v1 = _resident((1, D_MODEL))
    return pl.pallas_call(
        _mix_kernel,
        grid=(B, T // tt),
        in_specs=[seq(D_MODEL), seq(D_MODEL), modc(0), modc(1), modc(2), modc(3), modc(4), v1, v1,
                  _resident(w_mix.shape), _resident((1, 2 * D_MODEL)), st8, st1,
                  _resident((CONV_WIDTH, D_MODEL)), v1, _resident(wr4.shape), _resident(wi4.shape), v1, v1, v1,
                  _resident(wpa.shape), _resident(wpb.shape), _resident(wo.shape),
                  _resident(wr1.shape), _resident(wr2.shape), _resident((1, LANES))],
        out_specs=[seq(D_MODEL), seq(D_MODEL // 2), seq(LANES), st8, st1],
        out_shape=[jax.ShapeDtypeStruct((B, T, D_MODEL), F32),
                   jax.ShapeDtypeStruct((B, T, D_MODEL // 2), jnp.uint32),
                   jax.ShapeDtypeStruct((B, T, LANES), F32),
                   jax.ShapeDtypeStruct((B, SUBLANES, D_MODEL), F32),
                   jax.ShapeDtypeStruct((B, 1, D_MODEL), F32)],
        scratch_shapes=[pltpu.VMEM((RNN_TILE + SUBLANES, D_MODEL), F32),
                        pltpu.VMEM((1, D_MODEL), F32)],
        compiler_params=_cparams(("arbitrary", "arbitrary")),
        name="mix",
    )(x3, oa3, mod3, mod3, mod3, mod3, mod3, vec(norm1), vec(norm2), w_mix, b_gate.reshape(1, 2 * D_MODEL),
      conv0, h0, conv_w, vec(conv_b), wr4, wi4, vec(b_r), vec(b_i), vec(lam), wpa, wpb, wo, wr1, wr2, br)


def _dispatch_kernel(dest_ref, x_ref, slots_in_ref, slots_ref, sem):
    del slots_in_ref
    tm = x_ref.shape[0]

    def row_copy(r, slot):
        return pltpu.make_async_copy(x_ref.at[pl.ds(r, 1)], slots_ref.at[pl.ds(slot, 1)], sem)

    def issue(r, carry):
        for k in range(TOP_K):
            row_copy(r, dest_ref[TOP_K * r + k]).start()
        return carry

    lax.fori_loop(0, tm, issue, 0, unroll=8)

    for _ in range(TOP_K):
        pltpu.make_async_copy(x_ref, slots_ref.at[pl.ds(0, tm)], sem).wait()


def _dispatch(x, dest, slots, *, row_offset):
    rows, width = x.shape
    tm = min(DISPATCH_TILE, rows)
    assert rows % tm == 0 and row_offset % tm == 0
    tile_offset = row_offset // tm
    return pl.pallas_call(
        _dispatch_kernel,
        grid=(rows // tm,),
        in_specs=[pl.BlockSpec((tm * TOP_K,), lambda i: (i + tile_offset,), memory_space=pltpu.SMEM),
                  pl.BlockSpec((tm, width), lambda i: (i, 0)),
                  pl.BlockSpec(memory_space=pl.ANY)],
        out_specs=pl.BlockSpec(memory_space=pl.ANY),
        out_shape=jax.ShapeDtypeStruct(slots.shape, slots.dtype),
        scratch_shapes=[pltpu.SemaphoreType.DMA(())],
        input_output_aliases={2: 0},
        compiler_params=_cparams(("arbitrary",)),
        name="dispatch",
    )(dest, x, slots)


def _moe_kernel(be_ref, nb_ref, x_ref, w1_ref, w3_ref, w2_ref, o_ref, w1b_ref, w3b_ref, w2b_ref):
    i = pl.program_id(0)

    @pl.when((i == 0) | (be_ref[i] != be_ref[jnp.maximum(i - 1, 0)]))
    def _():
        w1b_ref[...] = w1_ref[...].astype(BF16)
        w3b_ref[...] = w3_ref[...].astype(BF16)
        w2b_ref[...] = w2_ref[...].astype(BF16)

    @pl.when(i < nb_ref[0])
    def _():
        subs = [slice(s, s + MOE_SUB) for s in range(0, x_ref.shape[0], MOE_SUB)]
        ups = []
        for rows in subs:
            x = _unpack_pairs(x_ref[rows, :]).astype(BF16)
            ups.append((_dot(x, w1b_ref[...]), _dot(x, w3b_ref[...])))
        for rows, (a, b) in zip(subs, ups):
            g = a * _sigmoid(a) * b
            o_ref[rows, :] = _pack_pairs(_dot(g.astype(BF16), w2b_ref[...]))

    @pl.when(i >= nb_ref[0])
    def _():
        o_ref[...] = jnp.zeros_like(o_ref)


def _moe(xs, block_e, n_used, w1, w3, w2):
    n_blocks = block_e.shape[0]
    bm = MOE_BLOCK
    wspec = lambda s: pl.BlockSpec((None,) + s, lambda i, be, nb: (be[i], 0, 0))
    gs = pltpu.PrefetchScalarGridSpec(
        num_scalar_prefetch=2,
        grid=(n_blocks,),
        in_specs=[pl.BlockSpec((bm, D_MODEL // 2), lambda i, be, nb: (i, 0)),
                  wspec((D_MODEL, D_EXPERT)), wspec((D_MODEL, D_EXPERT)), wspec((D_EXPERT, D_MODEL))],
        out_specs=pl.BlockSpec((bm, D_MODEL // 2), lambda i, be, nb: (i, 0)),
        scratch_shapes=[pltpu.VMEM((D_MODEL, D_EXPERT), BF16),
                        pltpu.VMEM((D_MODEL, D_EXPERT), BF16),
                        pltpu.VMEM((D_EXPERT, D_MODEL), BF16)],
    )
    return pl.pallas_call(
        _moe_kernel,
        grid_spec=gs,
        out_shape=jax.ShapeDtypeStruct((n_blocks * bm, D_MODEL // 2), jnp.uint32),
        compiler_params=_cparams(("arbitrary",)),
        name="moe",
    )(block_e, n_used, xs, w1, w3, w2)


def _final_kernel(x1_ref, ya_ref, yb_ref, rt_ref, g2_ref, nf_ref, o_ref):
    rt = rt_ref[...]
    moe = (rt[:, TOP_K:TOP_K + 1] * _unpack_pairs(ya_ref[...])
           + rt[:, TOP_K + 1:TOP_K + 2] * _unpack_pairs(yb_ref[...]))
    o_ref[...] = _rms(x1_ref[...] + g2_ref[...] * moe, nf_ref[...])


def _final(x1, ya, yb, rt, mod3, norm_f, *, rows_per_mod, row_offset):
    rows = x1.shape[0]
    tm = min(FINAL_TILE, rows)
    assert row_offset % tm == 0 and rows_per_mod % tm == 0
    tile_offset, tiles_per_mod = row_offset // tm, rows_per_mod // tm
    mr = mod3.shape[1]
    row = pl.BlockSpec((tm, D_MODEL), lambda i: (i, 0))
    tok = lambda w: pl.BlockSpec((tm, w), lambda i: (i + tile_offset, 0))
    return pl.pallas_call(
        _final_kernel,
        grid=(rows // tm,),
        in_specs=[row, tok(D_MODEL // 2), tok(D_MODEL // 2), tok(LANES),
                  pl.BlockSpec((None, mr, D_MODEL), lambda i: (i // tiles_per_mod, 0, 5)),
                  _resident((1, D_MODEL))],
        out_specs=row,
        out_shape=jax.ShapeDtypeStruct((rows, D_MODEL), F32),
        compiler_params=_cparams(("arbitrary",)),
        name="final",
    )(x1, ya, yb, rt, mod3, norm_f.reshape(1, D_MODEL))


def _routing_plan(eid, n_tokens):
    bm = MOE_BLOCK
    n_assign = n_tokens * TOP_K
    n_blocks = -(-n_assign // bm) + N_EXPERTS
    flat_e = eid.reshape(-1)
    hit = flat_e[:, None] == jnp.arange(N_EXPERTS, dtype=jnp.int32)[None, :]
    assert n_assign % ROW_TILE == 0
    runs = hit.astype(BF16).reshape(n_assign // ROW_TILE, ROW_TILE, N_EXPERTS)
    tri = (jnp.arange(ROW_TILE)[:, None] >= jnp.arange(ROW_TILE)[None, :]).astype(BF16)
    within = jnp.einsum('ts,rse->rte', tri, runs, preferred_element_type=F32)
    run_total = within[:, -1, :]
    run_start = jnp.cumsum(run_total, axis=0) - run_total
    csum = (within + run_start[:, None, :]).reshape(n_assign, N_EXPERTS)
    counts = (run_start[-1] + run_total[-1]).astype(jnp.int32)
    padded = (counts + bm - 1) // bm * bm
    pad_end = jnp.cumsum(padded)
    pad_start = pad_end - padded
    dest = jnp.sum(jnp.where(hit, csum - 1.0 + pad_start[None, :].astype(F32), 0.0),
                   axis=1).astype(jnp.int32)
    block_start = jnp.arange(n_blocks, dtype=jnp.int32) * bm
    block_e = jnp.minimum(jnp.sum((pad_end[None, :] <= block_start[:, None]).astype(jnp.int32), axis=1),
                          N_EXPERTS - 1).astype(jnp.int32)
    n_used = (pad_end[-1] // bm).astype(jnp.int32).reshape(1)
    return block_e, n_used, dest.reshape(n_tokens, TOP_K)


def _layer(xp, xs, c_all, cache_k, cache_v, cache_logf, state_conv, state_h, p, norm_f):
    Bp, Tp, _ = xp.shape
    Bs, Ts, _ = xs.shape
    past = cache_k.shape[1]
    np_rows, ns_rows = Bp * Tp, Bs * Ts
    assert Tp % ROW_TILE == 0 and ns_rows == ROW_TILE and Tp % ATTN_BLOCK == 0

    mod = _mod(c_all, p['w_mod'], p['b_mod'])
    mod_p = mod[:Bp].reshape(Bp, 1, 6 * D_MODEL)
    mod_s = jnp.repeat(mod[Bp:], Ts, axis=0).reshape(1, ns_rows, 6 * D_MODEL)

    w_in = p['w_in']
    d3 = 3 * D_MODEL
    wcat = jnp.concatenate([w_in[:, :D_MODEL] * (LOG2E * HEAD_DIM ** -0.5), w_in[:, D_MODEL:d3],
                            w_in[:, d3 + N_HEADS:], p['w_gate']], axis=1).astype(BF16)
    wfl = jnp.pad(w_in[:, d3:d3 + N_HEADS], ((0, 0), (0, LANES - N_HEADS))).astype(BF16)
    wflt = wfl.T
    eye = jnp.eye(GATE_TILE // (D_MODEL // N_RNN_BLOCKS), dtype=F32)

    def gate_tiles(w):
        per = GATE_TILE // w.shape[1]
        w4 = w.reshape(N_RNN_BLOCKS // per, per, w.shape[1], w.shape[2])
        return jnp.einsum('gpcd,pq->gpcqd', w4, eye).reshape(N_RNN_BLOCKS // per, GATE_TILE, GATE_TILE).astype(BF16)

    wr4, wi4 = gate_tiles(p['w_r']), gate_tiles(p['w_i'])
    wpa, wpb, wo = p['w_pa'].astype(BF16), p['w_pb'].astype(BF16), p['w_out'].astype(BF16)
    wr = jnp.pad(jnp.concatenate([p['w_rg'], p['w_re']], axis=1), ((0, 0), (0, LANES - N_GROUPS - N_EXPERTS)))
    wr1 = wr.astype(BF16)
    wr2 = (wr - wr1.astype(F32)).astype(BF16)
    br = jnp.pad(jnp.concatenate([p['b_rg'], p['b_re']]), (0, LANES - N_GROUPS - N_EXPERTS)).reshape(1, LANES)

    in_p = _in_proj(xp.reshape(np_rows, D_MODEL), mod_p, p['norm1'], wcat[:, :d3], wfl, wflt, p['b_gate'], p['b_f'],
                    tiles_per_mod=Tp // IN_TILE_QKV, tiles_per_seq=Tp // IN_TILE_QKV, with_mixer=False)
    in_s = _in_proj(xs.reshape(ns_rows, D_MODEL), mod_s, p['norm1'], wcat, wfl, wflt, p['b_gate'], p['b_f'],
                    tiles_per_mod=1, tiles_per_seq=1, with_mixer=True)
    q_p, kb_p, kt_p, vt_p, lf_p, _, fc_p = in_p
    q_s, kb_s, vb_s, k_s, v_s, xb_s, gb_s, ga_s, gg_s, lf_s, lft_s, _ = in_s
    r3 = lambda a, B, T: a.reshape(B, T, D_MODEL)

    oa_p = _attn_prompt(r3(q_p, Bp, Tp), r3(kb_p, Bp, Tp), vt_p, fc_p.reshape(Bp, Tp, N_HEADS))

    lfp3 = jnp.transpose(cache_logf.astype(F32), (0, 2, 1))
    lfn3 = lft_s.reshape(N_HEADS, Bs, Ts).transpose(1, 0, 2)
    oa_s = _attn_sample(r3(q_s, Bs, Ts), r3(kb_s, Bs, Ts), r3(vb_s, Bs, Ts),
                        jnp.transpose(cache_k, (0, 2, 3, 1)), jnp.transpose(cache_v, (0, 2, 3, 1)), lfp3, lfn3)

    zc = jnp.zeros((Bp, SUBLANES, D_MODEL), F32)
    zh = jnp.zeros((Bp, 1, D_MODEL), F32)
    rnn_w = (p['conv_w'], p['conv_b'], wr4, wi4, p['b_r'], p['b_i'], p['rglru_lambda'])
    post_w = (wpa, wpb, wo, wr1, wr2, br)
    x1_p, h2_p, rt_p, conv_p, hl_p = _mix(xp, oa_p, mod_p, zc, zh, p['norm1'], p['norm2'], wcat[:, d3:], p['b_gate'],
                                          *rnn_w, *post_w)
    x1_p, h2_p, rt_p = (a.reshape(np_rows, a.shape[-1]) for a in (x1_p, h2_p, rt_p))

    sc = jnp.pad(state_conv.astype(F32), ((0, 0), (SUBLANES - (CONV_WIDTH - 1), 0), (0, 0)))
    ob_s, conv_s, hl_s = _rnn(r3(xb_s, Bs, Ts), r3(gb_s, Bs, Ts), sc, state_h.astype(F32).reshape(Bs, 1, D_MODEL),
                              *rnn_w)
    x1_s, h2_s, rt_s = _post(xs.reshape(ns_rows, D_MODEL), oa_s.reshape(ns_rows, D_MODEL),
                             ob_s.reshape(ns_rows, D_MODEL), ga_s, gg_s, mod_s, p['norm2'], *post_w, tiles_per_mod=1)

    n_tok = np_rows + ns_rows
    rt = jnp.concatenate([rt_p, rt_s], axis=0)
    block_e, n_used, dest = _routing_plan(rt[:, :TOP_K].astype(jnp.int32), n_tok)
    slots = jnp.zeros((block_e.shape[0] * MOE_BLOCK, D_MODEL // 2), jnp.uint32)
    slots = _dispatch(h2_p, dest.reshape(-1), slots, row_offset=0)
    slots = _dispatch(h2_s, dest.reshape(-1), slots, row_offset=np_rows)
    take = lambda a, idx: a.at[idx].get(mode='promise_in_bounds')
    yb = _moe(slots, block_e, n_used, p['w1'], p['w3'], p['w2'])
    ya0 = take(yb, dest[:, 0])
    ya1 = take(yb, dest[:, 1])

    y_p = _final(x1_p, ya0, ya1, rt, mod_p, norm_f, rows_per_mod=Tp, row_offset=0)
    y_s = _final(x1_s, ya0, ya1, rt, mod_s, norm_f, rows_per_mod=ns_rows, row_offset=np_rows)

    st = CONV_WIDTH - 1
    state_p = (jnp.transpose(kt_p, (0, 3, 1, 2)), jnp.transpose(vt_p, (0, 3, 1, 2)),
               lf_p.reshape(Bp, Tp, N_HEADS), conv_p[:, SUBLANES - st:], hl_p.reshape(Bp, D_MODEL))
    state_s = (k_s.reshape(Bs, Ts, N_HEADS, HEAD_DIM), v_s.reshape(Bs, Ts, N_HEADS, HEAD_DIM),
               lf_s.reshape(Bs, Ts, N_HEADS), conv_s[:, SUBLANES - st:], hl_s.reshape(Bs, D_MODEL))
    return y_p.reshape(Bp, Tp, D_MODEL), y_s.reshape(Bs, Ts, D_MODEL), state_p, state_s


def kernel(x_prompt, x_sample, cache_k, cache_v, cache_logf, state_conv, state_h, c_prompt, c_sample, w_mod, b_mod, norm1, w_in, b_f, conv_w, conv_b, w_r, b_r, w_i, b_i, rglru_lambda, w_gate, b_gate, w_pa, w_pb, w_out, norm2, w_rg, b_rg, w_re, b_re, w1, w3, w2, norm_f):
    assert w_mod.shape[0] == 1, "single-layer trunk: the final norm is fused into the layer's last stage"
    names = ('w_mod', 'b_mod', 'norm1', 'w_in', 'b_f', 'conv_w', 'conv_b', 'w_r', 'b_r', 'w_i', 'b_i',
             'rglru_lambda', 'w_gate', 'b_gate', 'w_pa', 'w_pb', 'w_out', 'norm2', 'w_rg', 'b_rg', 'w_re', 'b_re',
             'w1', 'w3', 'w2')
    vals = (w_mod, b_mod, norm1, w_in, b_f, conv_w, conv_b, w_r, b_r, w_i, b_i, rglru_lambda, w_gate, b_gate,
            w_pa, w_pb, w_out, norm2, w_rg, b_rg, w_re, b_re, w1, w3, w2)
    p = {n: v[0] for n, v in zip(names, vals)}
    c_all = jnp.concatenate([c_prompt, c_sample], axis=0)
    y_p, y_s, sp, ss = _layer(x_prompt, x_sample, c_all, cache_k[0], cache_v[0], cache_logf[0],
                              state_conv[0], state_h[0], p, norm_f)
    lead = lambda a: a[None]
    return (y_p, y_s, lead(sp[0]), lead(sp[1]), lead(sp[2]), lead(sp[3]), lead(sp[4]),
            lead(ss[0]), lead(ss[1]), lead(ss[2]), lead(ss[3]), lead(ss[4]))
```

```python
import functools

import jax
import jax.numpy as jnp
from jax import lax
from jax.experimental import pallas as pl
from jax.experimental.pallas import tpu as pltpu

F32 = jnp.float32
BF16 = jnp.bfloat16

D_MODEL = 1024
N_HEADS = 16
HEAD_DIM = 64
HEAD_PAIR = 2 * HEAD_DIM
N_PAIRS = N_HEADS // 2
N_RNN_BLOCKS = 16
CONV_WIDTH = 4
RGLRU_C = 8.0
N_GROUPS = 4
EXPERTS_PER_GROUP = 8
N_EXPERTS = N_GROUPS * EXPERTS_PER_GROUP
TOP_K = 2
D_EXPERT = 512
EPS = 1e-6

LANES = 128
SUBLANES = 8
ROW_TILE = 512
FINAL_TILE = 1024
IN_TILE = 256
IN_TILE_QKV = 512
ATTN_BLOCK = 512
ATTN_QBLOCK = 2048
ATTN_VROWS = HEAD_DIM + 16
RNN_TILE = 256
MIX_TILE = 512
DISPATCH_TILE = 2048
MOE_BLOCK = 512
MOE_SUB = 256
GATE_TILE = 256
VMEM_LIMIT = 56 * 1024 * 1024
NEG_BIG = -1e30
LOG2E = 1.4426950408889634


def _cparams(sem):
    return pltpu.CompilerParams(dimension_semantics=sem, vmem_limit_bytes=VMEM_LIMIT)


def _resident(shape):
    nd = len(shape)
    return pl.BlockSpec(shape, lambda *_: (0,) * nd, pipeline_mode=pl.Buffered(1))


def _split3(x):
    p1 = x.astype(BF16)
    r1 = x - p1.astype(F32)
    p2 = r1.astype(BF16)
    p3 = (r1 - p2.astype(F32)).astype(BF16)
    return p1, p2, p3


def _dot(a, b):
    return jnp.dot(a, b, preferred_element_type=F32)


def _dot_nt(a, b):
    return lax.dot_general(a, b, (((1,), (1,)), ((), ())), preferred_element_type=F32)


def _sigmoid(x):
    return 0.5 * jnp.tanh(0.5 * x) + 0.5


def _gelu_tanh(x):
    return 0.5 * x * (1.0 + jnp.tanh(0.7978845608028654 * (x + 0.044715 * x * x * x)))


def _rms(x, g):
    ms = jnp.mean(x * x, axis=-1, keepdims=True)
    return x * lax.rsqrt(ms + EPS) * g


def _pack_pairs(x):
    n = x.shape[1] // 2
    hi = pltpu.bitcast(x[:, :n].astype(BF16).astype(F32), jnp.uint32)
    lo = pltpu.bitcast(x[:, n:].astype(BF16).astype(F32), jnp.uint32)
    return hi | lax.shift_right_logical(lo, jnp.uint32(16))


def _unpack_pairs(w):
    hi = pltpu.bitcast(w & jnp.uint32(0xFFFF0000), F32)
    lo = pltpu.bitcast(lax.shift_left(w, jnp.uint32(16)), F32)
    return jnp.concatenate([hi, lo], axis=1)


def _div_p2(x, n):
    assert n & (n - 1) == 0
    return lax.shift_right_logical(x, jnp.int32(n.bit_length() - 1))


def _mod_p2(x, n):
    assert n & (n - 1) == 0
    return x & (n - 1)


def _upper_tri(n):
    r = lax.broadcasted_iota(jnp.int32, (n, n), 0)
    c = lax.broadcasted_iota(jnp.int32, (n, n), 1)
    return jnp.where(r <= c, 1.0, 0.0).astype(BF16)


def _cumsum_lanes(x, u):
    p1, p2, p3 = _split3(x)
    return _dot(p1, u) + _dot(p2, u) + _dot(p3, u)


def _mod_kernel(c_ref, w_ref, b_ref, o_ref):
    c = c_ref[...]
    s = c * _sigmoid(c)
    s1 = s.astype(BF16)
    s2 = (s - s1.astype(F32)).astype(BF16)
    w = w_ref[...]
    w1 = w.astype(BF16)
    w2 = (w - w1.astype(F32)).astype(BF16)
    o_ref[...] = _dot(s1, w1) + _dot(s1, w2) + _dot(s2, w1) + b_ref[...]


def _mod(c, w, b):
    n = c.shape[0]
    cols = w.shape[1]
    return pl.pallas_call(
        _mod_kernel,
        grid=(cols // D_MODEL,),
        in_specs=[pl.BlockSpec((n, D_MODEL), lambda j: (0, 0)),
                  pl.BlockSpec((D_MODEL, D_MODEL), lambda j: (0, j)),
                  pl.BlockSpec((1, D_MODEL), lambda j: (0, j))],
        out_specs=pl.BlockSpec((n, D_MODEL), lambda j: (0, j)),
        out_shape=jax.ShapeDtypeStruct((n, cols), F32),
        compiler_params=_cparams(("arbitrary",)),
        name="mod",
    )(c, w, b.reshape(1, cols))


def _in_kernel(x_ref, sh_ref, sc_ref, n1_ref, w_ref, wfl_ref, wflt_ref, bg_ref, bf_ref, bft_ref, *refs,
               tiles_per_seq, with_mixer):
    lf_ref, lft_ref, fc_ref, carry_ref = refs[-4:]
    x = x_ref[...]
    h = (_rms(x, n1_ref[...]) * (1.0 + sc_ref[...]) + sh_ref[...]).astype(BF16)

    def mm(c):
        return _dot(h, w_ref[:, c * D_MODEL:(c + 1) * D_MODEL])

    def head_major(y):
        slabs = []
        for p in range(N_PAIRS):
            pair = y[:, p * HEAD_PAIR:(p + 1) * HEAD_PAIR]
            slabs += [pair, pltpu.roll(pair, HEAD_DIM, 1)]
        return jnp.swapaxes(jnp.stack(slabs, axis=0), 0, 1)[:, :, :HEAD_DIM]

    def time_minor(y):
        return y.T.reshape(N_HEADS, HEAD_DIM, y.shape[0])

    refs[0][...] = mm(0).astype(BF16)
    k = mm(1)
    refs[1][...] = k.astype(BF16)
    v = mm(2)
    if with_mixer:
        vb_ref, k5_ref, v5_ref, xb_ref, gb_ref, ga_ref, gg_ref = refs[2:9]
        vb_ref[...] = v.astype(BF16)
        k5_ref[...] = head_major(k)
        v5_ref[...] = head_major(v)
        xb_ref[...] = mm(3).astype(BF16)
        gb_ref[...] = mm(4).astype(BF16)
        ga_ref[...] = _sigmoid(mm(5) + bg_ref[:, :D_MODEL]).astype(BF16)
        gg_ref[...] = _sigmoid(mm(6) + bg_ref[:, D_MODEL:]).astype(BF16)
    else:
        refs[2][...] = time_minor(k)
        refs[3][...] = time_minor(v)

    def log_sigmoid(z):
        return jnp.minimum(z, 0.0) - jnp.log(1.0 + jnp.exp(-jnp.abs(z)))

    lf = log_sigmoid(_dot(h, wfl_ref[...]) + bf_ref[...])
    lf_ref[...] = lf[:, :N_HEADS]
    lft_ref[...] = log_sigmoid(_dot_nt(wflt_ref[...], h)[:N_HEADS, :] + bft_ref[...])

    @pl.when(pl.program_id(0) % tiles_per_seq == 0)
    def _():
        carry_ref[...] = jnp.zeros_like(carry_ref)

    tm = x.shape[0]
    r = lax.broadcasted_iota(jnp.int32, (tm, tm), 0)
    c = lax.broadcasted_iota(jnp.int32, (tm, tm), 1)
    lower = jnp.where(c <= r, 1.0, 0.0).astype(BF16)
    p1, p2, p3 = _split3(lf)
    fc = _dot(lower, p1) + _dot(lower, p2) + _dot(lower, p3) + carry_ref[...]
    fc_ref[...] = fc[:, :N_HEADS]
    carry_ref[...] = fc[tm - 1:tm, :]


def _in_proj(x2, mod3, norm1, wcat, wfl, wflt, b_gate, b_f, *, tiles_per_mod, tiles_per_seq, with_mixer):
    rows = x2.shape[0]
    tm = IN_TILE if with_mixer else IN_TILE_QKV
    nt = rows // tm
    row_blk = pl.BlockSpec((tm, D_MODEL), lambda i: (i, 0))
    if mod3.shape[1] == 1:
        modc = lambda c: pl.BlockSpec((None, 1, D_MODEL), lambda i: (i // tiles_per_mod, 0, c))
    else:
        modc = lambda c: pl.BlockSpec((None, tm, D_MODEL), lambda i: (0, i, c))
    row_out = jax.ShapeDtypeStruct((rows, D_MODEL), BF16)
    if with_mixer:
        kv_shape = jax.ShapeDtypeStruct((rows, N_HEADS, HEAD_DIM), F32)
        kv_blk = pl.BlockSpec((tm, N_HEADS, HEAD_DIM), lambda i: (i, 0, 0))
        lead = ([row_out] * 3 + [kv_shape] * 2 + [row_out] * 4, [row_blk] * 3 + [kv_blk] * 2 + [row_blk] * 4)
    else:
        n_seq = nt // tiles_per_seq
        kv_shape = jax.ShapeDtypeStruct((n_seq, N_HEADS, HEAD_DIM, tiles_per_seq * tm), F32)
        kv_blk = pl.BlockSpec((None, N_HEADS, HEAD_DIM, tm), lambda i: (i // tiles_per_seq, 0, 0, i % tiles_per_seq))
        lead = ([row_out] * 2 + [kv_shape] * 2, [row_blk] * 2 + [kv_blk] * 2)
    outs = [*lead[0],
            jax.ShapeDtypeStruct((rows, N_HEADS), F32),
            jax.ShapeDtypeStruct((N_HEADS, rows), F32),
            jax.ShapeDtypeStruct((rows, N_HEADS), F32)]
    return pl.pallas_call(
        functools.partial(_in_kernel, tiles_per_seq=tiles_per_seq, with_mixer=with_mixer),
        grid=(nt,),
        in_specs=[row_blk, modc(0), modc(1),
                  _resident((1, D_MODEL)),
                  _resident(wcat.shape),
                  _resident(wfl.shape),
                  _resident(wflt.shape),
                  _resident((1, 2 * D_MODEL)),
                  _resident((1, LANES)),
                  _resident((N_HEADS, 1))],
        out_specs=lead[1]
                  + [pl.BlockSpec((tm, N_HEADS), lambda i: (i, 0)),
                     pl.BlockSpec((N_HEADS, tm), lambda i: (0, i)),
                     pl.BlockSpec((tm, N_HEADS), lambda i: (i, 0))],
        out_shape=outs,
        scratch_shapes=[pltpu.VMEM((1, LANES), F32)],
        compiler_params=_cparams(("arbitrary",)),
        name="in_proj",
    )(x2, mod3, mod3, norm1.reshape(1, D_MODEL), wcat, wfl, wflt,
      b_gate.reshape(1, 2 * D_MODEL), jnp.pad(b_f, (0, LANES - N_HEADS)).reshape(1, LANES),
      b_f.reshape(N_HEADS, 1))


def _attn_kernel(q_ref, k_ref, v_ref, f_ref, o_ref, kb_ref, vt_ref, qt_ref, s0_ref, m_ref, acc_ref):
    qi = pl.program_id(2)
    bq = q_ref.shape[0]
    bk = ATTN_BLOCK
    nk = k_ref.shape[0] // bk
    own_lanes = (lambda i: i < HEAD_DIM, lambda i: i >= HEAD_DIM)
    bias_at = (HEAD_DIM, 0)

    @pl.when(qi == 0)
    def _():
        lane = lax.broadcasted_iota(jnp.int32, (bk, HEAD_PAIR), 1)
        ones_rows = jnp.where(lax.broadcasted_iota(jnp.int32, (ATTN_VROWS - HEAD_DIM, bk), 0) == 0, 1.0, 0.0)
        for c in range(nk):
            blk = slice(c * bk, (c + 1) * bk)
            k = k_ref[blk, :].astype(F32)
            f = f_ref[blk, :] * LOG2E
            head = lax.broadcasted_iota(jnp.int32, f.shape, 1)
            for hh in range(2):
                mine = head == 2 * pl.program_id(1) + hh
                f_head = jnp.sum(jnp.where(mine, f, 0.0), axis=1, keepdims=True)
                slab = jnp.zeros(k.shape, F32)
                for n, piece in enumerate(_split3(f_head)):
                    slab = jnp.where(lane == bias_at[hh] + n, piece.astype(F32), slab)
                kb_ref[hh, blk, :] = jnp.where(own_lanes[hh](lane), k, slab).astype(BF16)
            for hh in range(2):
                vt_ref[c, hh, 0:HEAD_DIM] = v_ref[hh, :, blk].astype(BF16)
                vt_ref[c, hh, HEAD_DIM:ATTN_VROWS] = ones_rows.astype(BF16)

    qt = q_ref[...].astype(F32).T
    feat = lax.broadcasted_iota(jnp.int32, qt.shape, 0)
    for hh in range(2):
        is_bias = (feat >= bias_at[hh]) & (feat < bias_at[hh] + 3)
        qt_ref[hh] = jnp.where(own_lanes[hh](feat), qt, jnp.where(is_bias, -1.0, 0.0)).astype(BF16)

    m_ref[...] = jnp.full(m_ref.shape, NEG_BIG, F32)
    acc_ref[...] = jnp.zeros(acc_ref.shape, F32)

    nsub = bq // bk

    def scores(hh, u, j):
        keys = kb_ref[hh, pl.ds(pl.multiple_of(j * bk, bk), bk), :]
        return _dot(keys, qt_ref[hh, :, u * bk:(u + 1) * bk])

    def softmax_pv(hh, u, j, s, masked):
        cols = slice(u * bk, (u + 1) * bk)
        if masked:
            key = lax.broadcasted_iota(jnp.int32, s.shape, 0)
            qry = lax.broadcasted_iota(jnp.int32, s.shape, 1)
            s = jnp.where(key <= qry, s, NEG_BIG)
        m_prev = m_ref[hh, :, cols]
        m_new = jnp.maximum(m_prev, jnp.max(s, axis=0, keepdims=True))
        p = jnp.exp2(s - m_new).astype(BF16)
        acc_ref[hh, :, cols] = jnp.exp2(m_prev - m_new) * acc_ref[hh, :, cols] + _dot(vt_ref[j, hh], p)
        m_ref[hh, :, cols] = m_new

    def run(items, first_of_next):
        s = s0_ref[...]
        for n, (hh, u, j, masked) in enumerate(items):
            if n + 1 < len(items):
                nxt = scores(*items[n + 1][:3])
            elif first_of_next is not None:
                s0_ref[...] = scores(*first_of_next)
            softmax_pv(hh, u, j, s, masked)
            s = nxt if n + 1 < len(items) else None

    def body(j, carry):
        run([(hh, u, j, False) for u in range(nsub) for hh in range(2)], (0, 0, j + 1))
        return carry

    s0_ref[...] = scores(0, 0, 0)
    lax.fori_loop(0, nsub * qi, body, 0)
    for d in range(nsub):
        j = nsub * qi + d
        items = [(hh, u, j, u == d) for u in range(d, nsub) for hh in range(2)]
        run(items, (0, d + 1, j + 1) if d + 1 < nsub else None)

    heads = [acc_ref[hh, 0:HEAD_DIM] / acc_ref[hh, HEAD_DIM:HEAD_DIM + 1] for hh in range(2)]
    o_ref[...] = jnp.concatenate(heads, axis=0).T.astype(BF16)


def _attn_prompt(q3, k3, vt4, fc3):
    B, T, _ = q3.shape
    bq, bk = ATTN_QBLOCK, ATTN_BLOCK
    nq, nk = T // bq, T // bk
    return pl.pallas_call(
        _attn_kernel,
        grid=(B, N_PAIRS, nq),
        in_specs=[pl.BlockSpec((None, bq, HEAD_PAIR), lambda b, p, i: (b, i, p)),
                  pl.BlockSpec((None, T, HEAD_PAIR), lambda b, p, i: (b, 0, p)),
                  pl.BlockSpec((None, 2, HEAD_DIM, T), lambda b, p, i: (b, p, 0, 0)),
                  pl.BlockSpec((None, T, N_HEADS), lambda b, p, i: (b, 0, 0))],
        out_specs=pl.BlockSpec((None, bq, HEAD_PAIR), lambda b, p, i: (b, i, p)),
        out_shape=jax.ShapeDtypeStruct((B, T, D_MODEL), BF16),
        scratch_shapes=[pltpu.VMEM((2, T, HEAD_PAIR), BF16),
                        pltpu.VMEM((nk, 2, ATTN_VROWS, bk), BF16),
                        pltpu.VMEM((2, HEAD_PAIR, bq), BF16),
                        pltpu.VMEM((bk, bk), F32),
                        pltpu.VMEM((2, 1, bq), F32),
                        pltpu.VMEM((2, ATTN_VROWS, bq), F32)],
        compiler_params=_cparams(("arbitrary", "arbitrary", "arbitrary")),
        name="attn_prompt",
    )(q3, k3, vt4, fc3)


def _attn_sample_kernel(q_ref, kn_ref, vn_ref, kc_ref, vc_ref, lfp_ref, lfn_ref, o_ref):
    tq = q_ref.shape[0]
    past = kc_ref.shape[-1]
    nrow = N_HEADS * tq
    row = lax.broadcasted_iota(jnp.int32, (nrow, D_MODEL), 0)
    col = lax.broadcasted_iota(jnp.int32, (nrow, D_MODEL), 1)
    own = _div_p2(row, tq) == _div_p2(col, HEAD_DIM)

    rr = lax.broadcasted_iota(jnp.int32, (nrow, tq), 0)
    rc = lax.broadcasted_iota(jnp.int32, (nrow, tq), 1)
    rep = jnp.where(_mod_p2(rr, tq) == rc, 1.0, 0.0).astype(BF16)
    qx = jnp.where(own, _dot(rep, q_ref[...]), 0.0).astype(BF16)

    fp = _cumsum_lanes(lfp_ref[...] * LOG2E, _upper_tri(past))
    fn = _cumsum_lanes(lfn_ref[...] * LOG2E, _upper_tri(tq)) + fp[:, past - 1:past]
    hr = lax.broadcasted_iota(jnp.int32, (nrow, N_HEADS), 0)
    hc = lax.broadcasted_iota(jnp.int32, (nrow, N_HEADS), 1)
    hsel = jnp.where(_div_p2(hr, tq) == hc, 1.0, 0.0).astype(BF16)

    def expand(f):
        p1, p2, p3 = _split3(f)
        return _dot(hsel, p1) + _dot(hsel, p2) + _dot(hsel, p3)

    s_p = _dot(qx, kc_ref[...].reshape(D_MODEL, past).astype(BF16)) - expand(fp)
    s_n = _dot_nt(qx, kn_ref[...]) - expand(fn)
    nr = lax.broadcasted_iota(jnp.int32, (nrow, tq), 0)
    nc = lax.broadcasted_iota(jnp.int32, (nrow, tq), 1)
    s_n = jnp.where(nc <= _mod_p2(nr, tq), s_n, NEG_BIG)
    m = jnp.maximum(jnp.max(s_p, axis=-1, keepdims=True), jnp.max(s_n, axis=-1, keepdims=True))
    p_p = jnp.exp2(s_p - m)
    p_n = jnp.exp2(s_n - m)
    l = jnp.sum(p_p, axis=-1, keepdims=True) + jnp.sum(p_n, axis=-1, keepdims=True)
    o_full = (_dot_nt(p_p.astype(BF16), vc_ref[...].reshape(D_MODEL, past).astype(BF16))
              + _dot(p_n.astype(BF16), vn_ref[...]))
    o_own = jnp.where(own, o_full / l, 0.0).astype(BF16)
    cr = lax.broadcasted_iota(jnp.int32, (tq, nrow), 0)
    cc = lax.broadcasted_iota(jnp.int32, (tq, nrow), 1)
    col_sel = jnp.where(_mod_p2(cc, tq) == cr, 1.0, 0.0).astype(BF16)
    o_ref[...] = _dot(col_sel, o_own).astype(BF16)


def _attn_sample(q3, kn3, vn3, kct, vct, lfp3, lfn3):
    B, tq, _ = q3.shape
    past = kct.shape[-1]
    blk = lambda n, w: pl.BlockSpec((None, n, w), lambda b: (b, 0, 0))
    cache = pl.BlockSpec((None, N_HEADS, HEAD_DIM, past), lambda b: (b, 0, 0, 0))
    return pl.pallas_call(
        _attn_sample_kernel,
        grid=(B,),
        in_specs=[blk(tq, D_MODEL), blk(tq, D_MODEL), blk(tq, D_MODEL),
                  cache, cache, blk(N_HEADS, past), blk(N_HEADS, tq)],
        out_specs=blk(tq, D_MODEL),
        out_shape=jax.ShapeDtypeStruct((B, tq, D_MODEL), BF16),
        compiler_params=_cparams(("arbitrary",)),
        name="attn_sample",
    )(q3, kn3, vn3, kct, vct, lfp3, lfn3)


def _rglru_gates(xb, xp_ref, cw_ref, cb_ref, wr_ref, wi_ref):
    tt = xb.shape[0]
    pad = SUBLANES
    xp_ref[pad:pad + tt, :] = xb
    xc = cb_ref[...] + xb * cw_ref[CONV_WIDTH - 1:CONV_WIDTH, :]
    for k in range(CONV_WIDTH - 1):
        back = CONV_WIDTH - 1 - k
        xc = xc + xp_ref[pad - back:pad - back + tt, :] * cw_ref[k:k + 1, :]
    tail = xp_ref[tt:tt + pad, :]
    xp_ref[0:pad, :] = tail

    xcb = xc.astype(BF16)
    ng = D_MODEL // GATE_TILE
    r = jnp.concatenate([_dot(xcb[:, g * GATE_TILE:(g + 1) * GATE_TILE], wr_ref[g]) for g in range(ng)], axis=-1)
    i = jnp.concatenate([_dot(xcb[:, g * GATE_TILE:(g + 1) * GATE_TILE], wi_ref[g]) for g in range(ng)], axis=-1)
    return xc, r, i, tail


def _rglru_scan(xc, r, i, gb, hc_ref, br_ref, bi_ref, lam_ref):
    tt = xc.shape[0]
    r = _sigmoid(r + br_ref[...])
    i = _sigmoid(i + bi_ref[...])
    lam = lam_ref[...]
    softplus_neg = jnp.maximum(-lam, 0.0) + jnp.log(1.0 + jnp.exp(-jnp.abs(lam)))
    log_a = (-RGLRU_C) * r * softplus_neg
    a = jnp.exp(log_a)
    th = jnp.tanh(log_a)
    mult = jnp.sqrt(-2.0 * th / (1.0 - th))
    b = mult * (i * xc)

    ng8 = tt // SUBLANES
    a3 = a.reshape(ng8, SUBLANES, D_MODEL)
    b3 = b.reshape(ng8, SUBLANES, D_MODEL)
    sub = lax.broadcasted_iota(jnp.int32, a3.shape, 1)
    for s in (1, 2, 4):
        keep = sub >= s
        a_sh = jnp.where(keep, pltpu.roll(a3, s, 1), 1.0)
        b_sh = jnp.where(keep, pltpu.roll(b3, s, 1), 0.0)
        b3 = a3 * b_sh + b3
        a3 = a3 * a_sh
    hc = hc_ref[...]
    hs = []
    for g in range(ng8):
        hg = b3[g] + a3[g] * hc
        hs.append(hg)
        hc = hg[SUBLANES - 1:SUBLANES, :]
    hc_ref[...] = hc
    return jnp.concatenate(hs, axis=0) * _gelu_tanh(gb), hc


def _rnn_kernel(xb_ref, gb_ref, c0_ref, h0_ref, cw_ref, cb_ref, wr_ref, wi_ref, br_ref, bi_ref, lam_ref,
                o_ref, cout_ref, hout_ref, xp_ref, hc_ref):
    ti = pl.program_id(1)

    @pl.when(ti == 0)
    def _():
        xp_ref[0:SUBLANES, :] = c0_ref[...]
        hc_ref[...] = h0_ref[...]

    xc, r, i, tail = _rglru_gates(xb_ref[...].astype(F32), xp_ref, cw_ref, cb_ref, wr_ref, wi_ref)
    ob, hc = _rglru_scan(xc, r, i, gb_ref[...].astype(F32), hc_ref, br_ref, bi_ref, lam_ref)
    o_ref[...] = ob.astype(BF16)

    @pl.when(ti == pl.num_programs(1) - 1)
    def _():
        cout_ref[...] = tail
        hout_ref[...] = hc


def _rnn(xb3, gb3, conv0, h0, conv_w, conv_b, wr4, wi4, b_r, b_i, lam):
    B, T, _ = xb3.shape
    tt = min(RNN_TILE, T)
    vec = lambda a: a.reshape(1, D_MODEL)
    seq = pl.BlockSpec((None, tt, D_MODEL), lambda b, t: (b, t, 0))
    st8 = pl.BlockSpec((None, SUBLANES, D_MODEL), lambda b, t: (b, 0, 0))
    st1 = pl.BlockSpec((None, 1, D_MODEL), lambda b, t: (b, 0, 0))
    return pl.pallas_call(
        _rnn_kernel,
        grid=(B, T // tt),
        in_specs=[seq, seq, st8, st1,
                  _resident((CONV_WIDTH, D_MODEL)), _resident((1, D_MODEL)),
                  _resident(wr4.shape), _resident(wi4.shape),
                  _resident((1, D_MODEL)), _resident((1, D_MODEL)), _resident((1, D_MODEL))],
        out_specs=[seq, st8, st1],
        out_shape=[jax.ShapeDtypeStruct((B, T, D_MODEL), BF16),
                   jax.ShapeDtypeStruct((B, SUBLANES, D_MODEL), F32),
                   jax.ShapeDtypeStruct((B, 1, D_MODEL), F32)],
        scratch_shapes=[pltpu.VMEM((tt + SUBLANES, D_MODEL), F32),
                        pltpu.VMEM((1, D_MODEL), F32)],
        compiler_params=_cparams(("arbitrary", "arbitrary")),
        name="rnn",
    )(xb3, gb3, conv0, h0, conv_w, vec(conv_b), wr4, wi4, vec(b_r), vec(b_i), vec(lam))


def _merge_route_tile(x, pa, ob, ga, gg, g1, sh2, sc2, n2_ref, wpb_ref, wo_ref, wr1_ref, wr2_ref, br_ref):
    mix = ga * pa + gg * _dot(ob, wpb_ref[...])
    x1 = x + g1 * _dot(mix.astype(BF16), wo_ref[...])
    h2 = _rms(x1, n2_ref[...]) * (1.0 + sc2) + sh2
    h2a = h2.astype(BF16)
    h2b = (h2 - h2a.astype(F32)).astype(BF16)
    lg = _dot(h2a, wr1_ref[...]) + _dot(h2a, wr2_ref[...]) + _dot(h2b, wr1_ref[...]) + br_ref[...]

    lane = lax.broadcasted_iota(jnp.int32, lg.shape, 1).astype(F32)
    big = float(LANES)
    is_g = lane < N_GROUPS
    gl = jnp.where(is_g, lg, NEG_BIG)
    mg = jnp.max(gl, axis=-1, keepdims=True)
    sg = jnp.sum(jnp.where(is_g, jnp.exp(gl - mg), 0.0), axis=-1, keepdims=True)
    pg = 1.0 / sg
    gidx = jnp.min(jnp.where(is_g & (gl == mg), lane, big), axis=-1, keepdims=True)
    lo = N_GROUPS + EXPERTS_PER_GROUP * gidx
    ing = (lane >= lo) & (lane < lo + EXPERTS_PER_GROUP)
    el = jnp.where(ing, lg, NEG_BIG)
    m1 = jnp.max(el, axis=-1, keepdims=True)
    se = jnp.sum(jnp.where(ing, jnp.exp(el - m1), 0.0), axis=-1, keepdims=True)
    i1 = jnp.min(jnp.where(ing & (el == m1), lane, big), axis=-1, keepdims=True)
    el2 = jnp.where(lane == i1, NEG_BIG, el)
    m2 = jnp.max(el2, axis=-1, keepdims=True)
    i2 = jnp.min(jnp.where(ing & (el2 == m2) & (lane != i1), lane, big), axis=-1, keepdims=True)
    p1 = 1.0 / se
    p2 = jnp.exp(m2 - m1) / se
    den = p1 + p2
    w1 = pg * p1 / den
    w2 = pg * p2 / den
    rt = jnp.where(lane == 0.0, i1 - N_GROUPS,
                   jnp.where(lane == 1.0, i2 - N_GROUPS,
                             jnp.where(lane == 2.0, w1, jnp.where(lane == 3.0, w2, 0.0))))
    return x1, h2, rt


def _post_kernel(x_ref, oa_ref, ob_ref, ga_ref, gg_ref, g1_ref, sh_ref, sc_ref, n2_ref,
                 wpa_ref, wpb_ref, wo_ref, wr1_ref, wr2_ref, br_ref,
                 x1_ref, h2_ref, rt_ref):
    x1, h2, rt = _merge_route_tile(x_ref[...], _dot(oa_ref[...], wpa_ref[...]), ob_ref[...],
                                   ga_ref[...].astype(F32), gg_ref[...].astype(F32), g1_ref[...], sh_ref[...],
                                   sc_ref[...], n2_ref, wpb_ref, wo_ref, wr1_ref, wr2_ref, br_ref)
    x1_ref[...] = x1
    h2_ref[...] = _pack_pairs(h2)
    rt_ref[...] = rt


def _post(x2, oa2, ob2, ga2, gg2, mod3, norm2, wpa, wpb, wo, wr1, wr2, br, *, tiles_per_mod):
    rows = x2.shape[0]
    tm = ROW_TILE
    mr = mod3.shape[1]
    row = pl.BlockSpec((tm, D_MODEL), lambda i: (i, 0))
    modc = lambda c: pl.BlockSpec((None, mr, D_MODEL), lambda i: (i // tiles_per_mod, 0, c))
    return pl.pallas_call(
        _post_kernel,
        grid=(rows // tm,),
        in_specs=[row, row, row, row, row, modc(2), modc(3), modc(4), _resident((1, D_MODEL)),
                  _resident(wpa.shape), _resident(wpb.shape), _resident(wo.shape),
                  _resident(wr1.shape), _resident(wr2.shape), _resident((1, LANES))],
        out_specs=[row, pl.BlockSpec((tm, D_MODEL // 2), lambda i: (i, 0)), pl.BlockSpec((tm, LANES), lambda i: (i, 0))],
        out_shape=[jax.ShapeDtypeStruct((rows, D_MODEL), F32),
                   jax.ShapeDtypeStruct((rows, D_MODEL // 2), jnp.uint32),
                   jax.ShapeDtypeStruct((rows, LANES), F32)],
        compiler_params=_cparams(("arbitrary",)),
        name="post",
    )(x2, oa2, ob2, ga2, gg2, mod3, mod3, mod3, norm2.reshape(1, D_MODEL), wpa, wpb, wo, wr1, wr2, br)


def _mix_kernel(x_ref, oa_ref, sh1_ref, sc1_ref, g1_ref, sh2_ref, sc2_ref, n1_ref, n2_ref, w_ref, bg_ref,
                c0_ref, h0_ref, cw_ref, cb_ref, wr_ref, wi_ref, br_ref, bi_ref, lam_ref,
                wpa_ref, wpb_ref, wo_ref, wr1_ref, wr2_ref, brt_ref,
                x1_ref, h2_ref, rt_ref, cout_ref, hout_ref, xp_ref, hc_ref):
    ti = pl.program_id(1)

    @pl.when(ti == 0)
    def _():
        xp_ref[0:SUBLANES, :] = c0_ref[...]
        hc_ref[...] = h0_ref[...]

    def project(rows):
        x = x_ref[rows, :]
        h = (_rms(x, n1_ref[...]) * (1.0 + sc1_ref[...]) + sh1_ref[...]).astype(BF16)
        mm = lambda c: _dot(h, w_ref[:, c * D_MODEL:(c + 1) * D_MODEL])
        xc, r, i, tail = _rglru_gates(mm(0), xp_ref, cw_ref, cb_ref, wr_ref, wi_ref)
        return dict(x=x, xc=xc, r=r, i=i, tail=tail, gb=mm(1), ga=mm(2), gg=mm(3),
                    pa=_dot(oa_ref[rows, :], wpa_ref[...]))

    def recur(st):
        ob, hc = _rglru_scan(st['xc'], st['r'], st['i'], st['gb'], hc_ref, br_ref, bi_ref, lam_ref)
        return ob.astype(BF16), hc

    def merge(rows, st, ob):
        ga = _sigmoid(st['ga'] + bg_ref[:, :D_MODEL])
        gg = _sigmoid(st['gg'] + bg_ref[:, D_MODEL:])
        x1, h2, rt = _merge_route_tile(st['x'], st['pa'], ob, ga, gg, g1_ref[...], sh2_ref[...], sc2_ref[...],
                                       n2_ref, wpb_ref, wo_ref, wr1_ref, wr2_ref, brt_ref)
        x1_ref[rows, :] = x1
        h2_ref[rows, :] = _pack_pairs(h2)
        rt_ref[rows, :] = rt

    subs = [slice(s, s + RNN_TILE) for s in range(0, x_ref.shape[0], RNN_TILE)]
    states = [project(subs[0])]
    for n, rows in enumerate(subs):
        if n + 1 < len(subs):
            states.append(project(subs[n + 1]))
        ob, hc = recur(states[n])
        merge(rows, states[n], ob)

    @pl.when(ti == pl.num_programs(1) - 1)
    def _():
        cout_ref[...] = states[-1]['tail']
        hout_ref[...] = hc


def _mix(x3, oa3, mod3, conv0, h0, norm1, norm2, w_mix, b_gate, conv_w, conv_b, wr4, wi4, b_r, b_i, lam,
         wpa, wpb, wo, wr1, wr2, br):
    B, T, _ = x3.shape
    tt = MIX_TILE
    vec = lambda a: a.reshape(1, D_MODEL)
    seq = lambda w: pl.BlockSpec((None, tt, w), lambda b, t: (b, t, 0))
    modc = lambda c: pl.BlockSpec((None, 1, D_MODEL), lambda b, t: (b, 0, c))
    st8 = pl.BlockSpec((None, SUBLANES, D_MODEL), lambda b, t: (b, 0, 0))
    st1 = pl.BlockSpec((None, 1, D_MODEL), lambda b, t: (b, 0, 0))
    v1 = _resident((1, D_MODEL))
    return pl.pallas_call(
        _mix_kernel,
        grid=(B, T // tt),
        in_specs=[seq(D_MODEL), seq(D_MODEL), modc(0), modc(1), modc(2), modc(3), modc(4), v1, v1,
                  _resident(w_mix.shape), _resident((1, 2 * D_MODEL)), st8, st1,
                  _resident((CONV_WIDTH, D_MODEL)), v1, _resident(wr4.shape), _resident(wi4.shape), v1, v1, v1,
                  _resident(wpa.shape), _resident(wpb.shape), _resident(wo.shape),
                  _resident(wr1.shape), _resident(wr2.shape), _resident((1, LANES))],
        out_specs=[seq(D_MODEL), seq(D_MODEL // 2), seq(LANES), st8, st1],
        out_shape=[jax.ShapeDtypeStruct((B, T, D_MODEL), F32),
                   jax.ShapeDtypeStruct((B, T, D_MODEL // 2), jnp.uint32),
                   jax.ShapeDtypeStruct((B, T, LANES), F32),
                   jax.ShapeDtypeStruct((B, SUBLANES, D_MODEL), F32),
                   jax.ShapeDtypeStruct((B, 1, D_MODEL), F32)],
        scratch_shapes=[pltpu.VMEM((RNN_TILE + SUBLANES, D_MODEL), F32),
                        pltpu.VMEM((1, D_MODEL), F32)],
        compiler_params=_cparams(("arbitrary", "arbitrary")),
        name="mix",
    )(x3, oa3, mod3, mod3, mod3, mod3, mod3, vec(norm1), vec(norm2), w_mix, b_gate.reshape(1, 2 * D_MODEL),
      conv0, h0, conv_w, vec(conv_b), wr4, wi4, vec(b_r), vec(b_i), vec(lam), wpa, wpb, wo, wr1, wr2, br)


def _dispatch_kernel(dest_ref, x_ref, slots_in_ref, slots_ref, sem):
    del slots_in_ref
    tm = x_ref.shape[0]

    def row_copy(r, slot):
        return pltpu.make_async_copy(x_ref.at[pl.ds(r, 1)], slots_ref.at[pl.ds(slot, 1)], sem)

    def issue(r, carry):
        for k in range(TOP_K):
            row_copy(r, dest_ref[TOP_K * r + k]).start(priority=k % 2)
        return carry

    lax.fori_loop(0, tm, issue, 0, unroll=8)

    for _ in range(TOP_K):
        pltpu.make_async_copy(x_ref, slots_ref.at[pl.ds(0, tm)], sem).wait()


def _dispatch(x, dest, slots, *, row_offset):
    rows, width = x.shape
    tm = min(DISPATCH_TILE, rows)
    assert rows % tm == 0 and row_offset % tm == 0
    tile_offset = row_offset // tm
    return pl.pallas_call(
        _dispatch_kernel,
        grid=(rows // tm,),
        in_specs=[pl.BlockSpec((tm * TOP_K,), lambda i: (i + tile_offset,), memory_space=pltpu.SMEM),
                  pl.BlockSpec((tm, width), lambda i: (i, 0)),
                  pl.BlockSpec(memory_space=pl.ANY)],
        out_specs=pl.BlockSpec(memory_space=pl.ANY),
        out_shape=jax.ShapeDtypeStruct(slots.shape, slots.dtype),
        scratch_shapes=[pltpu.SemaphoreType.DMA(())],
        input_output_aliases={2: 0},
        compiler_params=_cparams(("arbitrary",)),
        name="dispatch",
    )(dest, x, slots)


def _moe_kernel(be_ref, nb_ref, x_ref, w1_ref, w3_ref, w2_ref, o_ref, w1b_ref, w3b_ref, w2b_ref):
    i = pl.program_id(0)

    @pl.when((i == 0) | (be_ref[i] != be_ref[jnp.maximum(i - 1, 0)]))
    def _():
        w1b_ref[...] = w1_ref[...].astype(BF16)
        w3b_ref[...] = w3_ref[...].astype(BF16)
        w2b_ref[...] = w2_ref[...].astype(BF16)

    @pl.when(i < nb_ref[0])
    def _():
        subs = [slice(s, s + MOE_SUB) for s in range(0, x_ref.shape[0], MOE_SUB)]
        ups = []
        for rows in subs:
            x = _unpack_pairs(x_ref[rows, :]).astype(BF16)
            ups.append((_dot(x, w1b_ref[...]), _dot(x, w3b_ref[...])))
        for rows, (a, b) in zip(subs, ups):
            g = a * _sigmoid(a) * b
            o_ref[rows, :] = _pack_pairs(_dot(g.astype(BF16), w2b_ref[...]))

    @pl.when(i >= nb_ref[0])
    def _():
        o_ref[...] = jnp.zeros_like(o_ref)


def _moe(xs, block_e, n_used, w1, w3, w2):
    n_blocks = block_e.shape[0]
    bm = MOE_BLOCK
    wspec = lambda s: pl.BlockSpec((None,) + s, lambda i, be, nb: (be[i], 0, 0))
    gs = pltpu.PrefetchScalarGridSpec(
        num_scalar_prefetch=2,
        grid=(n_blocks,),
        in_specs=[pl.BlockSpec((bm, D_MODEL // 2), lambda i, be, nb: (i, 0)),
                  wspec((D_MODEL, D_EXPERT)), wspec((D_MODEL, D_EXPERT)), wspec((D_EXPERT, D_MODEL))],
        out_specs=pl.BlockSpec((bm, D_MODEL // 2), lambda i, be, nb: (i, 0)),
        scratch_shapes=[pltpu.VMEM((D_MODEL, D_EXPERT), BF16),
                        pltpu.VMEM((D_MODEL, D_EXPERT), BF16),
                        pltpu.VMEM((D_EXPERT, D_MODEL), BF16)],
    )
    return pl.pallas_call(
        _moe_kernel,
        grid_spec=gs,
        out_shape=jax.ShapeDtypeStruct((n_blocks * bm, D_MODEL // 2), jnp.uint32),
        compiler_params=_cparams(("arbitrary",)),
        name="moe",
    )(block_e, n_used, xs, w1, w3, w2)


def _final_kernel(x1_ref, ya_ref, yb_ref, rt_ref, g2_ref, nf_ref, o_ref):
    rt = rt_ref[...]
    moe = (rt[:, TOP_K:TOP_K + 1] * _unpack_pairs(ya_ref[...])
           + rt[:, TOP_K + 1:TOP_K + 2] * _unpack_pairs(yb_ref[...]))
    o_ref[...] = _rms(x1_ref[...] + g2_ref[...] * moe, nf_ref[...])


def _final(x1, ya, yb, rt, mod3, norm_f, *, rows_per_mod, row_offset):
    rows = x1.shape[0]
    tm = min(FINAL_TILE, rows)
    assert row_offset % tm == 0 and rows_per_mod % tm == 0
    tile_offset, tiles_per_mod = row_offset // tm, rows_per_mod // tm
    mr = mod3.shape[1]
    row = pl.BlockSpec((tm, D_MODEL), lambda i: (i, 0))
    tok = lambda w: pl.BlockSpec((tm, w), lambda i: (i + tile_offset, 0))
    return pl.pallas_call(
        _final_kernel,
        grid=(rows // tm,),
        in_specs=[row, tok(D_MODEL // 2), tok(D_MODEL // 2), tok(LANES),
                  pl.BlockSpec((None, mr, D_MODEL), lambda i: (i // tiles_per_mod, 0, 5)),
                  _resident((1, D_MODEL))],
        out_specs=row,
        out_shape=jax.ShapeDtypeStruct((rows, D_MODEL), F32),
        compiler_params=_cparams(("arbitrary",)),
        name="final",
    )(x1, ya, yb, rt, mod3, norm_f.reshape(1, D_MODEL))


def _routing_plan(eid, n_tokens):
    bm = MOE_BLOCK
    n_assign = n_tokens * TOP_K
    n_blocks = -(-n_assign // bm) + N_EXPERTS
    flat_e = eid.reshape(-1)
    hit = flat_e[:, None] == jnp.arange(N_EXPERTS, dtype=jnp.int32)[None, :]
    assert n_assign % ROW_TILE == 0
    runs = hit.astype(BF16).reshape(n_assign // ROW_TILE, ROW_TILE, N_EXPERTS)
    tri = (jnp.arange(ROW_TILE)[:, None] >= jnp.arange(ROW_TILE)[None, :]).astype(BF16)
    within = jnp.einsum('ts,rse->rte', tri, runs, preferred_element_type=F32)
    run_total = within[:, -1, :]
    run_start = jnp.cumsum(run_total, axis=0) - run_total
    csum = (within + run_start[:, None, :]).reshape(n_assign, N_EXPERTS)
    counts = (run_start[-1] + run_total[-1]).astype(jnp.int32)
    padded = (counts + bm - 1) // bm * bm
    pad_end = jnp.cumsum(padded)
    pad_start = pad_end - padded
    dest = jnp.sum(jnp.where(hit, csum - 1.0 + pad_start[None, :].astype(F32), 0.0),
                   axis=1).astype(jnp.int32)
    block_start = jnp.arange(n_blocks, dtype=jnp.int32) * bm
    block_e = jnp.minimum(jnp.sum((pad_end[None, :] <= block_start[:, None]).astype(jnp.int32), axis=1),
                          N_EXPERTS - 1).astype(jnp.int32)
    n_used = (pad_end[-1] // bm).astype(jnp.int32).reshape(1)
    return block_e, n_used, dest.reshape(n_tokens, TOP_K)


def _layer(xp, xs, c_all, cache_k, cache_v, cache_logf, state_conv, state_h, p, norm_f):
    Bp, Tp, _ = xp.shape
    Bs, Ts, _ = xs.shape
    past = cache_k.shape[1]
    np_rows, ns_rows = Bp * Tp, Bs * Ts
    assert Tp % ROW_TILE == 0 and ns_rows == ROW_TILE and Tp % ATTN_BLOCK == 0

    mod = _mod(c_all, p['w_mod'], p['b_mod'])
    mod_p = mod[:Bp].reshape(Bp, 1, 6 * D_MODEL)
    mod_s = jnp.repeat(mod[Bp:], Ts, axis=0).reshape(1, ns_rows, 6 * D_MODEL)

    w_in = p['w_in']
    d3 = 3 * D_MODEL
    wcat = jnp.concatenate([w_in[:, :D_MODEL] * (LOG2E * HEAD_DIM ** -0.5), w_in[:, D_MODEL:d3],
                            w_in[:, d3 + N_HEADS:], p['w_gate']], axis=1).astype(BF16)
    wfl = jnp.pad(w_in[:, d3:d3 + N_HEADS], ((0, 0), (0, LANES - N_HEADS))).astype(BF16)
    wflt = wfl.T
    eye = jnp.eye(GATE_TILE // (D_MODEL // N_RNN_BLOCKS), dtype=F32)

    def gate_tiles(w):
        per = GATE_TILE // w.shape[1]
        w4 = w.reshape(N_RNN_BLOCKS // per, per, w.shape[1], w.shape[2])
        return jnp.einsum('gpcd,pq->gpcqd', w4, eye).reshape(N_RNN_BLOCKS // per, GATE_TILE, GATE_TILE).astype(BF16)

    wr4, wi4 = gate_tiles(p['w_r']), gate_tiles(p['w_i'])
    wpa, wpb, wo = p['w_pa'].astype(BF16), p['w_pb'].astype(BF16), p['w_out'].astype(BF16)
    wr = jnp.pad(jnp.concatenate([p['w_rg'], p['w_re']], axis=1), ((0, 0), (0, LANES - N_GROUPS - N_EXPERTS)))
    wr1 = wr.astype(BF16)
    wr2 = (wr - wr1.astype(F32)).astype(BF16)
    br = jnp.pad(jnp.concatenate([p['b_rg'], p['b_re']]), (0, LANES - N_GROUPS - N_EXPERTS)).reshape(1, LANES)

    in_p = _in_proj(xp.reshape(np_rows, D_MODEL), mod_p, p['norm1'], wcat[:, :d3], wfl, wflt, p['b_gate'], p['b_f'],
                    tiles_per_mod=Tp // IN_TILE_QKV, tiles_per_seq=Tp // IN_TILE_QKV, with_mixer=False)
    in_s = _in_proj(xs.reshape(ns_rows, D_MODEL), mod_s, p['norm1'], wcat, wfl, wflt, p['b_gate'], p['b_f'],
                    tiles_per_mod=1, tiles_per_seq=1, with_mixer=True)
    q_p, kb_p, kt_p, vt_p, lf_p, _, fc_p = in_p
    q_s, kb_s, vb_s, k_s, v_s, xb_s, gb_s, ga_s, gg_s, lf_s, lft_s, _ = in_s
    r3 = lambda a, B, T: a.reshape(B, T, D_MODEL)

    oa_p = _attn_prompt(r3(q_p, Bp, Tp), r3(kb_p, Bp, Tp), vt_p, fc_p.reshape(Bp, Tp, N_HEADS))

    lfp3 = jnp.transpose(cache_logf.astype(F32), (0, 2, 1))
    lfn3 = lft_s.reshape(N_HEADS, Bs, Ts).transpose(1, 0, 2)
    oa_s = _attn_sample(r3(q_s, Bs, Ts), r3(kb_s, Bs, Ts), r3(vb_s, Bs, Ts),
                        jnp.transpose(cache_k, (0, 2, 3, 1)), jnp.transpose(cache_v, (0, 2, 3, 1)), lfp3, lfn3)

    zc = jnp.zeros((Bp, SUBLANES, D_MODEL), F32)
    zh = jnp.zeros((Bp, 1, D_MODEL), F32)
    rnn_w = (p['conv_w'], p['conv_b'], wr4, wi4, p['b_r'], p['b_i'], p['rglru_lambda'])
    post_w = (wpa, wpb, wo, wr1, wr2, br)
    x1_p, h2_p, rt_p, conv_p, hl_p = _mix(xp, oa_p, mod_p, zc, zh, p['norm1'], p['norm2'], wcat[:, d3:], p['b_gate'],
                                          *rnn_w, *post_w)
    x1_p, h2_p, rt_p = (a.reshape(np_rows, a.shape[-1]) for a in (x1_p, h2_p, rt_p))

    sc = jnp.pad(state_conv.astype(F32), ((0, 0), (SUBLANES - (CONV_WIDTH - 1), 0), (0, 0)))
    ob_s, conv_s, hl_s = _rnn(r3(xb_s, Bs, Ts), r3(gb_s, Bs, Ts), sc, state_h.astype(F32).reshape(Bs, 1, D_MODEL),
                              *rnn_w)
    x1_s, h2_s, rt_s = _post(xs.reshape(ns_rows, D_MODEL), oa_s.reshape(ns_rows, D_MODEL),
                             ob_s.reshape(ns_rows, D_MODEL), ga_s, gg_s, mod_s, p['norm2'], *post_w, tiles_per_mod=1)

    n_tok = np_rows + ns_rows
    rt = jnp.concatenate([rt_p, rt_s], axis=0)
    block_e, n_used, dest = _routing_plan(rt[:, :TOP_K].astype(jnp.int32), n_tok)
    slots = jnp.zeros((block_e.shape[0] * MOE_BLOCK, D_MODEL // 2), jnp.uint32)
    slots = _dispatch(h2_p, dest.reshape(-1), slots, row_offset=0)
    slots = _dispatch(h2_s, dest.reshape(-1), slots, row_offset=np_rows)
    take = lambda a, idx: a.at[idx].get(mode='promise_in_bounds')
    yb = _moe(slots, block_e, n_used, p['w1'], p['w3'], p['w2'])
    ya0 = take(yb, dest[:, 0])
    ya1 = take(yb, dest[:, 1])

    y_p = _final(x1_p, ya0, ya1, rt, mod_p, norm_f, rows_per_mod=Tp, row_offset=0)
    y_s = _final(x1_s, ya0, ya1, rt, mod_s, norm_f, rows_per_mod=ns_rows, row_offset=np_rows)

    st = CONV_WIDTH - 1
    state_p = (jnp.transpose(kt_p, (0, 3, 1, 2)), jnp.transpose(vt_p, (0, 3, 1, 2)),
               lf_p.reshape(Bp, Tp, N_HEADS), conv_p[:, SUBLANES - st:], hl_p.reshape(Bp, D_MODEL))
    state_s = (k_s.reshape(Bs, Ts, N_HEADS, HEAD_DIM), v_s.reshape(Bs, Ts, N_HEADS, HEAD_DIM),
               lf_s.reshape(Bs, Ts, N_HEADS), conv_s[:, SUBLANES - st:], hl_s.reshape(Bs, D_MODEL))
    return y_p.reshape(Bp, Tp, D_MODEL), y_s.reshape(Bs, Ts, D_MODEL), state_p, state_s


def kernel(x_prompt, x_sample, cache_k, cache_v, cache_logf, state_conv, state_h, c_prompt, c_sample, w_mod, b_mod, norm1, w_in, b_f, conv_w, conv_b, w_r, b_r, w_i, b_i, rglru_lambda, w_gate, b_gate, w_pa, w_pb, w_out, norm2, w_rg, b_rg, w_re, b_re, w1, w3, w2, norm_f):
    assert w_mod.shape[0] == 1, "single-layer trunk: the final norm is fused into the layer's last stage"
    names = ('w_mod', 'b_mod', 'norm1', 'w_in', 'b_f', 'conv_w', 'conv_b', 'w_r', 'b_r', 'w_i', 'b_i',
             'rglru_lambda', 'w_gate', 'b_gate', 'w_pa', 'w_pb', 'w_out', 'norm2', 'w_rg', 'b_rg', 'w_re', 'b_re',
             'w1', 'w3', 'w2')
    vals = (w_mod, b_mod, norm1, w_in, b_f, conv_w, conv_b, w_r, b_r, w_i, b_i, rglru_lambda, w_gate, b_gate,
            w_pa, w_pb, w_out, norm2, w_rg, b_rg, w_re, b_re, w1, w3, w2)
    p = {n: v[0] for n, v in zip(names, vals)}
    c_all = jnp.concatenate([c_prompt, c_sample], axis=0)
    y_p, y_s, sp, ss = _layer(x_prompt, x_sample, c_all, cache_k[0], cache_v[0], cache_logf[0],
                              state_conv[0], state_h[0], p, norm_f)
    lead = lambda a: a[None]
    return (y_p, y_s, lead(sp[0]), lead(sp[1]), lead(sp[2]), lead(sp[3]), lead(sp[4]),
            lead(ss[0]), lead(ss[1]), lead(ss[2]), lead(ss[3]), lead(ss[4]))
```
